```python
import math
import jax
import jax.numpy as jnp
from jax import lax
import numpy as np

D_MODEL = 1024
BATCH = 16
SEQ = 256
DEPTH = 2
DEC_BATCH = 8
DEC_SEQ = 1024
PAST_LEN = 256

GRID_W = 64
HY_WIDTH = D_MODEL // 4
RG_WIDTH = D_MODEL // 4
HG_WIDTH = D_MODEL // 4
RET_WIDTH = D_MODEL - HY_WIDTH - RG_WIDTH - HG_WIDTH
HY_ORDER = 2
HY_SHORT = 3
HY_EMB = 33
HY_BANDS = (HY_EMB - 1) // 2
HY_FFN = 64
HY_DECAY_TARGET = 1e-2
HY_DECAY_SHORT = 0.3
HY_DECAY_LONG = 1.5
RG_HEADS = 8
RG_HEAD_DIM = RG_WIDTH // RG_HEADS
RG_CONV = 4
RG_C = 8.0
HG_HEADS = 4
HG_DK = HG_WIDTH // HG_HEADS
HG_DV = HG_WIDTH // HG_HEADS
HG_CHUNK = 32
RET_HEADS = 4
RET_DH = RET_WIDTH // RET_HEADS
RET_CHUNK = 64
ROPE_BASE = 10000.0
N_EXPERTS = 16
N_GROUPS = 4
E_PER_GROUP = N_EXPERTS // N_GROUPS
TOP_K = 2
D_EXPERT = 512
N_MOD = 6
EPS = 1e-6
PROJ_SPLITS = ((HY_ORDER + 1) * HY_WIDTH, RG_WIDTH, RG_WIDTH,
               HG_WIDTH, HG_WIDTH, HG_WIDTH, HG_WIDTH, HG_WIDTH,
               RET_WIDTH, RET_WIDTH, RET_WIDTH, RET_WIDTH)
PROJ_WIDTH = sum(PROJ_SPLITS)

kernel_name = "hybrid_flow_backbone_step"


def rmsnorm(x, gain=None):
    xf = x.astype(jnp.float32)
    y = xf * lax.rsqrt(jnp.mean(jnp.square(xf), axis=-1, keepdims=True) + EPS)
    if gain is not None:
        y = y * gain.astype(jnp.float32)
    return y.astype(x.dtype)


def short_conv(u, w, b, pad_left):
    K = w.shape[0]
    L = u.shape[1]
    up = jnp.pad(u, ((0, 0), (pad_left, K - 1 - pad_left), (0, 0)))
    out = b
    for k in range(K):
        out = out + w[k] * up[:, k:k + L]
    return out


def hyena_filters(L, w1, b1, w2, b2, w3):
    f32 = jnp.float32
    t = jnp.arange(L, dtype=f32)
    t_norm = t / max(L - 1, 1)
    bands = jnp.linspace(1e-4, HY_BANDS - 1, HY_BANDS, dtype=f32)
    ang = (2.0 * math.pi / L) * t[:, None] * bands[None, :]
    z = jnp.concatenate([t_norm[:, None], jnp.cos(ang), jnp.sin(ang)], axis=-1)
    h = jnp.sin(z @ w1.astype(f32) + b1.astype(f32))
    h = jnp.sin(h @ w2.astype(f32) + b2.astype(f32))
    h = (h @ w3.astype(f32)).reshape(L, HY_ORDER, 2, HY_WIDTH)
    deltas = jnp.abs(jnp.linspace(math.log(HY_DECAY_TARGET) / HY_DECAY_LONG,
                                  math.log(HY_DECAY_TARGET) / HY_DECAY_SHORT, HY_WIDTH, dtype=f32))
    window = jnp.exp(-t_norm[:, None] * deltas[None, :])
    h = h * window[:, None, None, :]
    h = h * lax.rsqrt(jnp.sum(jnp.square(h), axis=(0, 2), keepdims=True) + EPS)
    h_fwd, h_bwd = h[:, :, 0], h[:, :, 1]
    h_circ = jnp.concatenate([h_fwd, jnp.zeros((1, HY_ORDER, HY_WIDTH), f32), h_bwd[1:][::-1]], axis=0)
    return jnp.fft.rfft(h_circ, axis=0)


def fft_long_conv(z, h_freq):
    L = z.shape[1]
    zf = jnp.fft.rfft(z.astype(jnp.float32), n=2 * L, axis=1)
    y = jnp.fft.irfft(zf * h_freq[None], n=2 * L, axis=1)[:, :L]
    return y.astype(z.dtype)


def hyena_mixer(u, conv_w, conv_b, w1, b1, w2, b2, w3, d_bias):
    L = u.shape[1]
    u = short_conv(u, conv_w, conv_b, (HY_SHORT - 1) // 2)
    v, x1, x2 = jnp.split(u, HY_ORDER + 1, axis=-1)
    hf = hyena_filters(L, w1, b1, w2, b2, w3)
    z = x1 * (fft_long_conv(v, hf[:, 0]) + d_bias[0] * v)
    z = x2 * (fft_long_conv(z, hf[:, 1]) + d_bias[1] * z)
    return z


def linear_scan(a, b, h0):
    def combine(left, right):
        al, bl = left
        ar, br = right
        return al * ar, ar * bl + br
    a_cum, b_cum = lax.associative_scan(combine, (a, b), axis=1)
    return b_cum + a_cum * h0[:, None]


def rglru_mixer(xr, gate, conv_w, conv_b, wa, ba, wx, bx, lam, h0):
    B, L, W = xr.shape
    xc = short_conv(xr, conv_w, conv_b, RG_CONV // 2)
    xh = xc.reshape(B, L, RG_HEADS, RG_HEAD_DIM)
    r = jax.nn.sigmoid(jnp.einsum("blhi,dhij->dblhj", xh, wa).reshape(2, B, L, W) + ba[:, None, None])
    i = jax.nn.sigmoid(jnp.einsum("blhi,dhij->dblhj", xh, wx).reshape(2, B, L, W) + bx[:, None, None])
    log_a = -RG_C * r * jax.nn.softplus(-lam)[:, None, None]
    a = jnp.exp(log_a)
    b = jnp.sqrt(-jnp.expm1(2.0 * log_a)) * (i * xc[None])
    h_f = linear_scan(a[0], b[0], h0[:, 0])
    h_b = jnp.flip(linear_scan(jnp.flip(a[1], 1), jnp.flip(b[1], 1), h0[:, 1]), 1)
    y = (h_f + h_b) * jax.nn.gelu(gate)
    state = jnp.stack([h_f[:, -1], h_b[:, 0]], axis=1)
    return y, state


def chunk_state_scan(decay, U, s0):
    def step(s, xs):
        dec, u = xs
        return (dec * s + u).astype(s.dtype), s
    xs = (jnp.moveaxis(decay.astype(U.dtype), 1, 0), jnp.moveaxis(U, 1, 0))
    s_final, s_prev = lax.scan(step, s0.astype(U.dtype), xs)
    return jnp.moveaxis(s_prev, 0, 1), s_final


def chunk_gla(q, k, v, log_f, s0, chunk, reverse):
    if reverse:
        q, k, v, log_f = (jnp.flip(t, 1) for t in (q, k, v, log_f))
    B, L, H, K = q.shape
    V = v.shape[-1]
    N = L // chunk
    q, k, log_f = (t.reshape(B, N, chunk, H, K) for t in (q, k, log_f))
    v = v.reshape(B, N, chunk, H, V)
    G = jnp.cumsum(log_f, axis=2)
    causal = jnp.tril(jnp.ones((chunk, chunk), dtype=bool))[None, None, :, :, None, None]
    diff = G[:, :, :, None] - G[:, :, None, :]
    decay = jnp.exp(jnp.where(causal, diff, -jnp.inf))
    scores = jnp.einsum("bnthk,bntshk,bnshk->bnhts", q, decay, k)
    o_intra = jnp.einsum("bnhts,bnshv->bnthv", scores, v)
    G_last = G[:, :, -1]
    U = jnp.einsum("bnchk,bnchv->bnhkv", k * jnp.exp(G_last[:, :, None] - G), v)
    s_prev, s_final = chunk_state_scan(jnp.exp(G_last)[..., None], U, s0)
    o_inter = jnp.einsum("bnthk,bnhkv->bnthv", q * jnp.exp(G), s_prev)
    o = (o_intra + o_inter).reshape(B, L, H, V)
    if reverse:
        o = jnp.flip(o, 1)
    return o, s_final


def chunk_retention(q, k, v, log_gamma, s0, chunk, reverse):
    if reverse:
        q, k, v = (jnp.flip(t, 1) for t in (q, k, v))
    B, L, H, K = q.shape
    V = v.shape[-1]
    N = L // chunk
    q, k = (t.reshape(B, N, chunk, H, K) for t in (q, k))
    v = v.reshape(B, N, chunk, H, V)
    pos = jnp.arange(chunk)
    rel = pos[:, None] - pos[None, :]
    lg = log_gamma
    dmat = jnp.exp(jnp.where(rel[None] >= 0, rel[None].astype(lg.dtype) * lg[:, None, None], -jnp.inf))
    scores = jnp.einsum("bnthk,bnshk->bnhts", q, k) * dmat
    o_intra = jnp.einsum("bnhts,bnshv->bnthv", scores, v)
    inner = jnp.exp((pos + 1).astype(lg.dtype)[:, None] * lg[None])
    tail = jnp.exp((chunk - 1 - pos).astype(lg.dtype)[:, None] * lg[None])
    U = jnp.einsum("bnchk,ch,bnchv->bnhkv", k, tail, v)
    dec = jnp.broadcast_to(jnp.exp(chunk * lg)[:, None, None], (B, N, H, 1, 1))
    s_prev, s_final = chunk_state_scan(dec, U, s0)
    o_inter = jnp.einsum("bnthk,th,bnhkv->bnthv", q, inner, s_prev)
    o = (o_intra + o_inter).reshape(B, L, H, V)
    if reverse:
        o = jnp.flip(o, 1)
    return o, s_final


def hgrn2_mixer(q, f_fwd, f_bwd, inp, g, lb, norm_gain, s0):
    B, L, _ = q.shape
    qh = jax.nn.silu(q).reshape(B, L, HG_HEADS, HG_DK)
    vh = inp.reshape(B, L, HG_HEADS, HG_DV)
    outs, states = [], []
    for d, f_pre in enumerate((f_fwd, f_bwd)):
        lbd = lb[d]
        log_f = jnp.logaddexp(jnp.log(lbd), jnp.log1p(-lbd) + jax.nn.log_sigmoid(f_pre))
        k = (1 - lbd) * jax.nn.sigmoid(-f_pre)
        o, s = chunk_gla(qh, k.reshape(B, L, HG_HEADS, HG_DK), vh,
                         log_f.reshape(B, L, HG_HEADS, HG_DK), s0[:, d], HG_CHUNK, d == 1)
        outs.append(o)
        states.append(s)
    o = rmsnorm(outs[0] + outs[1], norm_gain.reshape(HG_HEADS, HG_DV)).reshape(B, L, HG_WIDTH)
    return o * jax.nn.silu(g), jnp.stack(states, axis=1)


def grid_rope(L, dtype):
    rows = L // GRID_W
    row = jnp.repeat(jnp.arange(rows), GRID_W).astype(jnp.float32)
    col = (jnp.arange(L) % GRID_W).astype(jnp.float32)
    n_freq = RET_DH // 4
    inv_freq = ROPE_BASE ** (-jnp.arange(n_freq, dtype=jnp.float32) / n_freq)
    ang = jnp.concatenate([row[:, None] * inv_freq, col[:, None] * inv_freq], axis=-1)
    return jnp.cos(ang).astype(dtype), jnp.sin(ang).astype(dtype)


def apply_rope(x, cos, sin):
    x1, x2 = x[..., 0::2], x[..., 1::2]
    c, s = cos[None, :, None], sin[None, :, None]
    return jnp.stack([x1 * c - x2 * s, x1 * s + x2 * c], axis=-1).reshape(x.shape)


def retention_mixer(q, k, v, g, decay_logit, s0, rope):
    B, L, _ = q.shape
    qh = q.reshape(B, L, RET_HEADS, RET_DH)
    kh = k.reshape(B, L, RET_HEADS, RET_DH) * (RET_DH ** -0.5)
    vh = v.reshape(B, L, RET_HEADS, RET_DH)
    if rope is not None:
        qh = apply_rope(qh, rope[0], rope[1])
        kh = apply_rope(kh, rope[0], rope[1])
    log_gamma = jax.nn.log_sigmoid(decay_logit)
    outs, states = [], []
    for d in range(2):
        o, s = chunk_retention(qh, kh, vh, log_gamma[d], s0[:, d], RET_CHUNK, d == 1)
        outs.append(o)
        states.append(s)
    o = rmsnorm(outs[0] + outs[1]).reshape(B, L, RET_WIDTH)
    return jax.nn.silu(g) * o, jnp.stack(states, axis=1)


def moe_ffn(x, w_router, router_bias, w_gate, w_up, w_down):
    B, L, D = x.shape
    t = x.reshape(-1, D)
    probs = jax.nn.softmax(jnp.dot(t, w_router).astype(jnp.float32), axis=-1)
    sel = probs + router_bias.astype(jnp.float32)
    grp_score = lax.top_k(sel.reshape(-1, N_GROUPS, E_PER_GROUP), TOP_K)[0].sum(-1)
    best = jnp.argmax(grp_score, axis=-1)
    in_group = (jnp.arange(N_EXPERTS) // E_PER_GROUP)[None] == best[:, None]
    _, idx = lax.top_k(jnp.where(in_group, sel, -jnp.inf), TOP_K)
    w = jnp.take_along_axis(probs, idx, axis=-1)
    w = w / jnp.sum(w, axis=-1, keepdims=True)
    gates = jnp.sum(jax.nn.one_hot(idx, N_EXPERTS, dtype=jnp.float32) * w[..., None], axis=1)
    h = jax.nn.silu(jnp.einsum("td,edf->tef", t, w_gate)) * jnp.einsum("td,edf->tef", t, w_up)
    h = h * gates.astype(h.dtype)[..., None]
    return jnp.einsum("tef,efd->td", h, w_down).reshape(B, L, D)


def trunk_layer(P, l, x, cond, rg_h0, hg_s0, ret_s0, lb, rope):
    mod = jnp.dot(jax.nn.silu(cond), P["w_ada"][l]) + P["b_ada"][l]
    sh1, sc1, g1, sh2, sc2, g2 = jnp.split(mod[:, None, :], N_MOD, axis=-1)
    h = rmsnorm(x, P["norm1"][l]) * (1 + sc1) + sh1
    proj = jnp.einsum("bld,dp->blp", h, P["w_in"][l])
    points = [int(s) for s in np.cumsum(PROJ_SPLITS)[:-1]]
    (hy_u, rg_x, rg_g, hg_q, hg_ff, hg_fb, hg_i, hg_g,
     ret_q, ret_k, ret_v, ret_g) = jnp.split(proj, points, axis=-1)
    y_hy = hyena_mixer(hy_u, P["hy_conv_w"][l], P["hy_conv_b"][l], P["hy_w1"][l], P["hy_b1"][l],
                       P["hy_w2"][l], P["hy_b2"][l], P["hy_w3"][l], P["hy_d"][l])
    y_rg, st_rg = rglru_mixer(rg_x, rg_g, P["rg_conv_w"][l], P["rg_conv_b"][l], P["rg_wa"][l],
                              P["rg_ba"][l], P["rg_wx"][l], P["rg_bx"][l], P["rg_lambda"][l], rg_h0)
    y_hg, st_hg = hgrn2_mixer(hg_q, hg_ff, hg_fb, hg_i, hg_g, lb, P["hg_norm"][l], hg_s0)
    y_ret, st_ret = retention_mixer(ret_q, ret_k, ret_v, ret_g, P["ret_decay"][l], ret_s0, rope)
    mixed = jnp.concatenate([y_hy, y_rg, y_hg, y_ret], axis=-1)
    x = x + g1 * jnp.einsum("blm,md->bld", mixed, P["w_out"][l])
    h = rmsnorm(x, P["norm2"][l]) * (1 + sc2) + sh2
    x = x + g2 * moe_ffn(h, P["w_router"], P["router_bias"], P["w_gate"][l], P["w_up"][l], P["w_down"][l])
    return x, st_rg, st_hg, st_ret


def setup_inputs(seed: int = 0) -> dict:
    key = jax.random.key(seed)
    ks = iter(jax.random.split(key, 48))
    f32 = jnp.float32

    def nrm(shape, scale):
        return scale * jax.random.normal(next(ks), shape, f32)

    u = jax.random.uniform(next(ks), (DEPTH, 2, RG_WIDTH), f32, minval=0.9, maxval=0.999)
    a0 = u ** (1.0 / RG_C)
    rg_lambda = jnp.log(a0) - jnp.log1p(-a0)
    gam = 1.0 - 2.0 ** (-5.0 - jnp.arange(RET_HEADS, dtype=f32))
    ret_decay = (jnp.log(gam) - jnp.log1p(-gam))[None, None] + nrm((DEPTH, 2, RET_HEADS), 0.1)
    return {
        "x_prompt": nrm((BATCH, SEQ, D_MODEL), 1.0),
        "x_sample": nrm((DEC_BATCH, DEC_SEQ, D_MODEL), 1.0),
        "state_rglru": nrm((DEC_BATCH, DEPTH, 2, RG_WIDTH), 0.5),
        "state_hgrn": nrm((DEC_BATCH, DEPTH, 2, HG_HEADS, HG_DK, HG_DV), 0.5),
        "state_ret": nrm((DEC_BATCH, DEPTH, 2, RET_HEADS, RET_DH, RET_DH), 0.5),
        "c": nrm((DEC_BATCH, D_MODEL), 1.0),
        "c_ctx": nrm((D_MODEL,), 1.0),
        "norm1": 1.0 + nrm((DEPTH, D_MODEL), 0.02),
        "norm2": 1.0 + nrm((DEPTH, D_MODEL), 0.02),
        "norm_final": 1.0 + nrm((D_MODEL,), 0.02),
        "w_ada": nrm((DEPTH, D_MODEL, N_MOD * D_MODEL), 0.5 * D_MODEL ** -0.5),
        "b_ada": nrm((DEPTH, N_MOD * D_MODEL), 0.02),
        "w_in": nrm((DEPTH, D_MODEL, PROJ_WIDTH), D_MODEL ** -0.5),
        "w_out": nrm((DEPTH, D_MODEL, D_MODEL), D_MODEL ** -0.5),
        "hy_conv_w": nrm((DEPTH, HY_SHORT, (HY_ORDER + 1) * HY_WIDTH), HY_SHORT ** -0.5),
        "hy_conv_b": nrm((DEPTH, (HY_ORDER + 1) * HY_WIDTH), 0.02),
        "hy_w1": nrm((DEPTH, HY_EMB, HY_FFN), 1.0),
        "hy_b1": nrm((DEPTH, HY_FFN), 0.1),
        "hy_w2": nrm((DEPTH, HY_FFN, HY_FFN), HY_FFN ** -0.5),
        "hy_b2": nrm((DEPTH, HY_FFN), 0.1),
        "hy_w3": nrm((DEPTH, HY_FFN, HY_ORDER * 2 * HY_WIDTH), HY_FFN ** -0.5),
        "hy_d": nrm((DEPTH, HY_ORDER, HY_WIDTH), 0.5),
        "rg_conv_w": nrm((DEPTH, RG_CONV, RG_WIDTH), RG_CONV ** -0.5),
        "rg_conv_b": nrm((DEPTH, RG_WIDTH), 0.02),
        "rg_wa": nrm((DEPTH, 2, RG_HEADS, RG_HEAD_DIM, RG_HEAD_DIM), RG_HEAD_DIM ** -0.5),
        "rg_ba": nrm((DEPTH, 2, RG_WIDTH), 0.1),
        "rg_wx": nrm((DEPTH, 2, RG_HEADS, RG_HEAD_DIM, RG_HEAD_DIM), RG_HEAD_DIM ** -0.5),
        "rg_bx": nrm((DEPTH, 2, RG_WIDTH), 0.1),
        "rg_lambda": rg_lambda,
        "hg_lb": nrm((DEPTH, 2, HG_WIDTH), 1.0),
        "hg_norm": 1.0 + nrm((DEPTH, HG_WIDTH), 0.02),
        "ret_decay": ret_decay,
        "w_router": nrm((D_MODEL, N_EXPERTS), D_MODEL ** -0.5),
        "router_bias": nrm((N_EXPERTS,), 0.01),
        "w_gate": nrm((DEPTH, N_EXPERTS, D_MODEL, D_EXPERT), D_MODEL ** -0.5),
        "w_up": nrm((DEPTH, N_EXPERTS, D_MODEL, D_EXPERT), D_MODEL ** -0.5),
        "w_down": nrm((DEPTH, N_EXPERTS, D_EXPERT, D_MODEL), D_EXPERT ** -0.5),
    }


def reference(x_prompt, x_sample, state_rglru, state_hgrn, state_ret, c, c_ctx,
              norm1, norm2, norm_final, w_ada, b_ada, w_in, w_out,
              hy_conv_w, hy_conv_b, hy_w1, hy_b1, hy_w2, hy_b2, hy_w3, hy_d,
              rg_conv_w, rg_conv_b, rg_wa, rg_ba, rg_wx, rg_bx, rg_lambda,
              hg_lb, hg_norm, ret_decay, w_router, router_bias, w_gate, w_up, w_down):
    P = {"norm1": norm1, "norm2": norm2, "w_ada": w_ada, "b_ada": b_ada, "w_in": w_in, "w_out": w_out,
         "hy_conv_w": hy_conv_w, "hy_conv_b": hy_conv_b, "hy_w1": hy_w1, "hy_b1": hy_b1,
         "hy_w2": hy_w2, "hy_b2": hy_b2, "hy_w3": hy_w3, "hy_d": hy_d,
         "rg_conv_w": rg_conv_w, "rg_conv_b": rg_conv_b, "rg_wa": rg_wa, "rg_ba": rg_ba,
         "rg_wx": rg_wx, "rg_bx": rg_bx, "rg_lambda": rg_lambda, "hg_norm": hg_norm,
         "ret_decay": ret_decay, "w_router": w_router, "router_bias": router_bias,
         "w_gate": w_gate, "w_up": w_up, "w_down": w_down}
    lb_cum = jnp.cumsum(jax.nn.softmax(hg_lb.astype(jnp.float32), axis=0), axis=0)
    lb_all = (lb_cum - lb_cum[0:1]).astype(hg_lb.dtype)

    B = x_prompt.shape[0]
    dt = x_prompt.dtype
    z_rg = jnp.zeros((B, 2, RG_WIDTH), dt)
    z_hg = jnp.zeros((B, 2, HG_HEADS, HG_DK, HG_DV), dt)
    z_ret = jnp.zeros((B, 2, RET_HEADS, RET_DH, RET_DH), dt)
    xc = x_prompt
    st_rg, st_hg, st_ret = [], [], []
    for l in range(DEPTH):
        xc, s1, s2, s3 = trunk_layer(P, l, xc, c_ctx[None], z_rg, z_hg, z_ret, lb_all[l], None)
        st_rg.append(s1)
        st_hg.append(s2)
        st_ret.append(s3)
    y_prompt = rmsnorm(xc, norm_final)
    new_state_rglru = jnp.stack(st_rg, axis=1)
    new_state_hgrn = jnp.stack(st_hg, axis=1)
    new_state_ret = jnp.stack(st_ret, axis=1)

    rope = grid_rope(x_sample.shape[1], x_sample.dtype)
    xs = x_sample
    for l in range(DEPTH):
        xs, _, _, _ = trunk_layer(P, l, xs, c, state_rglru[:, l], state_hgrn[:, l], state_ret[:, l],
                                  lb_all[l], rope)
    y_sample = rmsnorm(xs, norm_final)
    return (y_prompt, y_sample, new_state_rglru, new_state_hgrn, new_state_ret)
```

```python
import functools
import math

import numpy as np
import jax
import jax.numpy as jnp
from jax import lax
from jax.experimental import pallas as pl
from jax.experimental.pallas import tpu as pltpu

F32 = jnp.float32
BF16 = jnp.bfloat16

D_MODEL = 1024
DEPTH = 2
GRID_W = 64
HY_WIDTH = 256
RG_WIDTH = 256
HG_WIDTH = 256
RET_WIDTH = 256
MIX_WIDTH = 256
HY_ORDER = 2
HY_EMB = 33
HY_BANDS = 16
HY_FFN = 64
HY_DECAY_TARGET = 1e-2
HY_DECAY_SHORT = 0.3
HY_DECAY_LONG = 1.5
RG_HEADS = 8
RG_HEAD_DIM = 32
RG_C = 8.0
N_HEADS = 4
HEAD_DIM = 64
HG_CHUNK = 64
RET_CHUNK = 256
ROPE_BASE = 10000.0
N_EXPERTS = 16
N_GROUPS = 4
E_PER_GROUP = 4
D_EXPERT = 512
N_MOD = 6
EPS = 1e-6
PROJ_HY = 3 * HY_WIDTH
PROJ_RG = 2 * RG_WIDTH
PROJ_HG = 5 * HG_WIDTH
PROJ_RET = 4 * RET_WIDTH
PROJ_WIDTH = PROJ_HY + PROJ_RG + PROJ_HG + PROJ_RET
COND_ROWS = 16
LANES = 128
VMEM_LIMIT = 56 * 1024 * 1024

_NN = (((1,), (0,)), ((), ()))
_NT = (((1,), (1,)), ((), ()))


def _mm(a, b, dn=_NN):
    return lax.dot_general(a, b, dn, preferred_element_type=F32)


def _split2(x):
    hi = x.astype(BF16)
    lo = (x - hi.astype(F32)).astype(BF16)
    return hi, lo


def _split3(x):
    hi = x.astype(BF16)
    r = x - hi.astype(F32)
    mid = r.astype(BF16)
    lo = (r - mid.astype(F32)).astype(BF16)
    return hi, mid, lo


def _mm1(a, b, dn=_NN):
    return _mm(a.astype(BF16), b.astype(BF16), dn)


def _mm3(a, b, dn=_NN):
    ah, al = _split2(a)
    bh, bl = _split2(b)
    return _mm(ah, bh, dn) + (_mm(ah, bl, dn) + _mm(al, bh, dn))


def _mm_exact_lhs(a_bf16, b):
    b1, b2, b3 = _split3(b)
    return _mm(a_bf16, b1) + (_mm(a_bf16, b2) + _mm(a_bf16, b3))


def _mm_exact_rhs(a, b_bf16):
    a1, a2, a3 = _split3(a)
    return _mm(a1, b_bf16) + (_mm(a2, b_bf16) + _mm(a3, b_bf16))


def _sigmoid(x):
    return 1.0 / (1.0 + jnp.exp(-x))


def _silu(x):
    return x * _sigmoid(x)


def _log_sigmoid(x):
    return jnp.minimum(x, 0.0) - jnp.log(1.0 + jnp.exp(-jnp.abs(x)))


def _gelu_tanh(x):
    return 0.5 * x * (1.0 + jnp.tanh(math.sqrt(2.0 / math.pi) * (x + 0.044715 * (x * x * x))))


def _iota(shape, dim):
    return lax.broadcasted_iota(jnp.int32, shape, dim)


def _shift_rows(u, k, row):
    n = u.shape[0]
    if k == 0:
        return u
    r = pltpu.roll(u, (-k) % n, axis=0)
    if k < 0:
        return jnp.where(row >= -k, r, 0.0)
    return jnp.where(row < n - k, r, 0.0)


def _head_mask(n_rows_per_head):
    shape = (N_HEADS * n_rows_per_head, MIX_WIDTH)
    return (_iota(shape, 0) // n_rows_per_head) == (_iota(shape, 1) // HEAD_DIM)


def _stack_heads(x, mask):
    return jnp.where(mask, jnp.concatenate([x] * N_HEADS, axis=0), 0.0)


def _block_diag_mask():
    shape = (MIX_WIDTH, MIX_WIDTH)
    return (_iota(shape, 0) // HEAD_DIM) == (_iota(shape, 1) // HEAD_DIM)


def _head_rmsnorm(o, ones_bd):
    ms = _mm_exact_rhs(o * o, ones_bd)
    return o * lax.rsqrt(ms + EPS)


def _params(sem, vmem=VMEM_LIMIT):
    return pltpu.CompilerParams(dimension_semantics=sem, vmem_limit_bytes=vmem)


def _const_spec(shape):
    nd = len(shape)
    return pl.BlockSpec(shape, lambda *_: (0,) * nd, pipeline_mode=pl.Buffered(1))


def _mod_kernel(cond_ref, w_ref, b_ref, o_ref):
    o_ref[0] = _mm3(_silu(cond_ref[...]), w_ref[0]) + b_ref[0]


def _modulation(cond, w_ada, b_ada):
    tn = 1536
    n_mod = N_MOD * D_MODEL
    return pl.pallas_call(
        _mod_kernel,
        out_shape=jax.ShapeDtypeStruct((DEPTH, COND_ROWS, n_mod), F32),
        grid=(DEPTH, n_mod // tn),
        in_specs=[
            pl.BlockSpec((COND_ROWS, D_MODEL), lambda l, j: (0, 0)),
            pl.BlockSpec((1, D_MODEL, tn), lambda l, j: (l, 0, j)),
            pl.BlockSpec((1, 1, tn), lambda l, j: (l, 0, j)),
        ],
        out_specs=pl.BlockSpec((1, COND_ROWS, tn), lambda l, j: (l, 0, j)),
        compiler_params=_params(("parallel", "parallel")),
        name="modulation",
    )(cond, w_ada, b_ada.reshape(DEPTH, 1, n_mod))


def _proj_kernel(x_ref, mod_ref, n1_ref, w_ref, hy_ref, rg_ref, hg_ref, ret_ref):
    x = x_ref[...]
    mod = mod_ref[0, 0]
    sh1 = mod[:, 0:D_MODEL]
    sc1 = mod[:, D_MODEL:2 * D_MODEL]
    h = x * lax.rsqrt(jnp.mean(x * x, axis=-1, keepdims=True) + EPS) * n1_ref[...]
    h = (h * (1.0 + sc1) + sh1).astype(BF16)
    c0 = 0
    for ref, width in ((hy_ref, PROJ_HY), (rg_ref, PROJ_RG), (hg_ref, PROJ_HG), (ret_ref, PROJ_RET)):
        ref[...] = _mm(h, w_ref[:, c0:c0 + width])
        c0 += width


def _mod_row_map(tm, n_prompt_tok, dec_seq):
    n_prompt_tiles = n_prompt_tok // tm

    def row(i):
        return jnp.where(i < n_prompt_tiles, 0, 1 + (i * tm - n_prompt_tok) // dec_seq)

    return row


def _in_projection(x_all, mod_l, norm1_l, w_in_l, n_prompt_tok, dec_seq):
    t_all = x_all.shape[0]
    tm = 512
    row = _mod_row_map(tm, n_prompt_tok, dec_seq)
    widths = (PROJ_HY, PROJ_RG, PROJ_HG, PROJ_RET)
    return pl.pallas_call(
        _proj_kernel,
        out_shape=[jax.ShapeDtypeStruct((t_all, w), F32) for w in widths],
        grid=(t_all // tm,),
        in_specs=[
            pl.BlockSpec((tm, D_MODEL), lambda i: (i, 0)),
            pl.BlockSpec((1, 1, 1, N_MOD * D_MODEL), lambda i: (0, row(i), 0, 0)),
            _const_spec((1, D_MODEL)),
            _const_spec((D_MODEL, PROJ_WIDTH)),
        ],
        out_specs=[pl.BlockSpec((tm, w), lambda i: (i, 0)) for w in widths],
        compiler_params=_params(("parallel",)),
        name="in_projection",
    )(x_all, mod_l, norm1_l, w_in_l)


def _dft_tables(seq):
    k = np.arange(seq, dtype=np.int64)
    m = (k[:, None] * k[None, :]) % (2 * seq)
    ang = np.pi * m.astype(np.float64) / seq
    return np.cos(ang), np.sin(ang)


def _hyena_tables(seq):
    cos, sin = _dft_tables(seq)
    sign = np.where(np.arange(seq) % 2 == 0, 1.0, -1.0)
    fwd = np.concatenate([cos, sign[None, :], sin[1:]], axis=0)
    wk = np.full((seq,), 2.0)
    wk[0] = 1.0
    inv_cos = (cos * wk[None, :]) / (2.0 * seq)
    inv_nyq = sign[:, None] / (2.0 * seq)
    inv_sin = 2.0 * sin[:, 1:] / (2.0 * seq)
    inv = np.concatenate([inv_cos, inv_nyq, inv_sin], axis=1)

    def hilo(a):
        a32 = jnp.asarray(a, F32)
        hi = a32.astype(BF16)
        lo = (a32 - hi.astype(F32)).astype(BF16)
        return hi, lo

    return hilo(fwd) + hilo(inv)


def _filter_embedding(seq):
    t = np.arange(seq, dtype=np.float64)
    t_norm = t / max(seq - 1, 1)
    bands = np.linspace(1e-4, HY_BANDS - 1, HY_BANDS)
    ang = (2.0 * np.pi / seq) * t[:, None] * bands[None, :]
    z = np.concatenate([t_norm[:, None], np.cos(ang), np.sin(ang)], axis=-1)
    z = np.pad(z, ((0, 0), (0, LANES - HY_EMB)))
    deltas = np.abs(np.linspace(math.log(HY_DECAY_TARGET) / HY_DECAY_LONG,
                                math.log(HY_DECAY_TARGET) / HY_DECAY_SHORT, HY_WIDTH))
    window = np.exp(-t_norm[:, None] * deltas[None, :])
    return jnp.asarray(z, F32), jnp.asarray(window, F32)


def _filter_kernel(z_ref, win_ref, cos_ref, sin_ref, w1_ref, b1_ref, w2_ref, b2_ref, w3_ref, o_ref):
    seq = z_ref.shape[0]
    h = jnp.sin(_mm3(z_ref[...], w1_ref[0]) + b1_ref[0])
    h = jnp.sin(_mm3(h, w2_ref[0]) + b2_ref[0])
    h = _mm3(h, w3_ref[0])
    win = win_ref[...]
    row = _iota((seq, 1), 0)
    sums, diffs = [], []
    for o in range(HY_ORDER):
        c0 = o * 2 * HY_WIDTH
        hf = h[:, c0:c0 + HY_WIDTH] * win
        hb = h[:, c0 + HY_WIDTH:c0 + 2 * HY_WIDTH] * win
        ssq = jnp.sum(hf * hf + hb * hb, axis=0, keepdims=True)
        inv = lax.rsqrt(ssq + EPS)
        hf = hf * inv
        hb = jnp.where(row == 0, 0.0, hb * inv)
        sums.append(hf + hb)
        diffs.append(hf - hb)
    hsum = jnp.concatenate(sums, axis=1)
    hdiff = jnp.concatenate(diffs, axis=1)
    h_re = _mm3(cos_ref[...], hsum)
    h_im = _mm3(sin_ref[...], hdiff)
    sign = jnp.where(row % 2 == 0, 1.0, -1.0)
    h_nyq = jnp.sum(sign * hsum, axis=0, keepdims=True)
    o_ref[0, 0] = h_re
    o_ref[0, 1] = h_im
    o_ref[0, 2] = jnp.where(row == 0, h_nyq, h_re)


def _hyena_filters(seq, w1, b1, w2, b2, w3):
    z, window = _filter_embedding(seq)
    cos, sin = _dft_tables(seq)
    n_out = HY_ORDER * 2 * HY_WIDTH
    w1p = jnp.pad(w1, ((0, 0), (0, LANES - HY_EMB), (0, LANES - HY_FFN)))
    b1p = jnp.pad(b1, ((0, 0), (0, LANES - HY_FFN))).reshape(DEPTH, 1, LANES)
    w2p = jnp.pad(w2, ((0, 0), (0, LANES - HY_FFN), (0, LANES - HY_FFN)))
    b2p = jnp.pad(b2, ((0, 0), (0, LANES - HY_FFN))).reshape(DEPTH, 1, LANES)
    w3p = jnp.pad(w3, ((0, 0), (0, LANES - HY_FFN), (0, 0)))
    per_layer = lambda shape: pl.BlockSpec((1,) + shape, lambda l: (l,) + (0,) * len(shape))
    return pl.pallas_call(
        _filter_kernel,
        out_shape=jax.ShapeDtypeStruct((DEPTH, 3, seq, HY_ORDER * HY_WIDTH), F32),
        grid=(DEPTH,),
        in_specs=[
            _const_spec((seq, LANES)), _const_spec((seq, HY_WIDTH)),
            _const_spec((seq, seq)), _const_spec((seq, seq)),
            per_layer((LANES, LANES)), per_layer((1, LANES)),
            per_layer((LANES, LANES)), per_layer((1, LANES)),
            per_layer((LANES, n_out)),
        ],
        out_specs=pl.BlockSpec((1, 3, seq, HY_ORDER * HY_WIDTH), lambda l: (l, 0, 0, 0)),
        compiler_params=_params(("parallel",)),
        name=f"hyena_filters_{seq}",
    )(z, window, jnp.asarray(cos, F32), jnp.asarray(sin, F32), w1p, b1p, w2p, b2p, w3p)


def _hyena_kernel(u_ref, cw_ref, cb_ref, d_ref, filt_ref, fh_ref, fl_ref, gh_ref, gl_ref, y_ref):
    seq = u_ref.shape[0]
    u = u_ref[...]
    row = _iota((seq, 1), 0)
    cw = cw_ref[...]
    uc = cb_ref[...] + cw[0:1] * _shift_rows(u, -1, row) + cw[1:2] * u + cw[2:3] * _shift_rows(u, 1, row)
    v = uc[:, 0:HY_WIDTH]
    gates = (uc[:, HY_WIDTH:2 * HY_WIDTH], uc[:, 2 * HY_WIDTH:3 * HY_WIDTH])
    fh, fl, gh, gl = fh_ref[...], fl_ref[...], gh_ref[...], gl_ref[...]
    d = d_ref[...]
    z = v
    for o in range(HY_ORDER):
        c0 = o * HY_WIDTH
        zh, zl = _split2(z)
        spec = _mm(fh, zh) + (_mm(fh, zl) + _mm(fl, zh))
        s_re, s_im = spec[:seq], spec[seq:]
        a = filt_ref[0, :, c0:c0 + HY_WIDTH]
        b = filt_ref[1, :, c0:c0 + HY_WIDTH]
        c = filt_ref[2, :, c0:c0 + HY_WIDTH]
        prod = jnp.concatenate([s_re * a - s_im * b, s_re * b + s_im * c], axis=0)
        ph, plo = _split2(prod)
        conv = _mm(gh, ph) + (_mm(gh, plo) + _mm(gl, ph))
        z = gates[o] * (conv + d[o:o + 1] * z)
    y_ref[...] = z


def _hyena(u_all, tok0, batch, seq, conv_w, conv_b, d_bias, filt_l, tables):
    fh, fl, gh, gl = tables
    blk0 = tok0 // seq
    return pl.pallas_call(
        _hyena_kernel,
        out_shape=jax.ShapeDtypeStruct((batch * seq, HY_WIDTH), F32),
        grid=(batch,),
        in_specs=[
            pl.BlockSpec((seq, PROJ_HY), lambda b: (blk0 + b, 0)),
            _const_spec((3, PROJ_HY)), _const_spec((1, PROJ_HY)), _const_spec((HY_ORDER, HY_WIDTH)),
            _const_spec((3, seq, HY_ORDER * HY_WIDTH)),
            _const_spec((2 * seq, seq)), _const_spec((2 * seq, seq)),
            _const_spec((seq, 2 * seq)), _const_spec((seq, 2 * seq)),
        ],
        out_specs=pl.BlockSpec((seq, HY_WIDTH), lambda b: (b, 0)),
        compiler_params=_params(("parallel",)),
        name=f"hyena_{seq}",
    )(u_all, conv_w, conv_b, d_bias, filt_l, fh, fl, gh, gl)


def _rglru_kernel(rg_ref, cw_ref, cb_ref, wg_ref, bg_ref, sp_ref, h0_ref, y_ref, st_ref):
    seq = rg_ref.shape[0]
    w = RG_WIDTH
    xr = rg_ref[:, 0:w]
    gate = rg_ref[:, w:2 * w]
    row = _iota((seq, 1), 0)
    cw = cw_ref[...]
    xc = (cb_ref[...] + cw[0:1] * _shift_rows(xr, -2, row) + cw[1:2] * _shift_rows(xr, -1, row)
          + cw[2:3] * xr + cw[3:4] * _shift_rows(xr, 1, row))
    g = _sigmoid(_mm3(xc, wg_ref[...]) + bg_ref[...])
    sp = sp_ref[...]
    h0 = h0_ref[0]
    hs = []
    for d in range(2):
        r = g[:, d * w:(d + 1) * w]
        i = g[:, (2 + d) * w:(3 + d) * w]
        log_a = -RG_C * r * sp[d:d + 1]
        a = jnp.exp(log_a)
        b = jnp.sqrt(jnp.tanh(-log_a) * (1.0 + a * a)) * (i * xc)
        step = 1
        while step < seq:
            if d == 0:
                keep = row >= step
                a_s = pltpu.roll(a, step, axis=0)
                b_s = pltpu.roll(b, step, axis=0)
            else:
                keep = row < seq - step
                a_s = pltpu.roll(a, seq - step, axis=0)
                b_s = pltpu.roll(b, seq - step, axis=0)
            b = jnp.where(keep, a * b_s + b, b)
            a = jnp.where(keep, a * a_s, a)
            step *= 2
        hs.append(b + a * h0[d:d + 1])
    y_ref[...] = (hs[0] + hs[1]) * _gelu_tanh(gate)
    st_ref[0, 0:1, :] = hs[0][seq - 1:seq]
    st_ref[0, 1:2, :] = hs[1][0:1]


def _rglru(rg_all, tok0, batch, seq, conv_w, conv_b, w_gates, b_gates, softplus_neg_lam, h0):
    blk0 = tok0 // seq
    return pl.pallas_call(
        _rglru_kernel,
        out_shape=[jax.ShapeDtypeStruct((batch * seq, RG_WIDTH), F32),
                   jax.ShapeDtypeStruct((batch, 2, RG_WIDTH), F32)],
        grid=(batch,),
        in_specs=[
            pl.BlockSpec((seq, PROJ_RG), lambda b: (blk0 + b, 0)),
            _const_spec((4, RG_WIDTH)), _const_spec((1, RG_WIDTH)),
            _const_spec((RG_WIDTH, 4 * RG_WIDTH)), _const_spec((1, 4 * RG_WIDTH)),
            _const_spec((2, RG_WIDTH)),
            pl.BlockSpec((1, 2, RG_WIDTH), lambda b: (b, 0, 0)),
        ],
        out_specs=[pl.BlockSpec((seq, RG_WIDTH), lambda b: (b, 0)),
                   pl.BlockSpec((1, 2, RG_WIDTH), lambda b: (b, 0, 0))],
        compiler_params=_params(("parallel",)),
        name=f"rglru_{seq}",
    )(rg_all, conv_w, conv_b, w_gates, b_gates, softplus_neg_lam, h0)


def _hgrn_kernel(hg_ref, lb_ref, gain_ref, s0_ref, y_ref, st_ref,
                 q_s, v_s, kf_s, lf_s, kb_s, lb_s, of_s, ob_s, stf_s, stb_s):
    seq = hg_ref.shape[0]
    w = HG_WIDTH
    c = HG_CHUNK
    n_chunks = seq // c
    q_s[...] = _silu(hg_ref[:, 0:w])
    v_s[...] = hg_ref[:, 3 * w:4 * w]
    lbv = lb_ref[...]
    for d, (k_s, l_s) in enumerate(((kf_s, lf_s), (kb_s, lb_s))):
        f_pre = hg_ref[:, (1 + d) * w:(2 + d) * w]
        lo = lbv[d:d + 1]
        a1 = jnp.log(lo)
        a2 = jnp.log(1.0 - lo) + _log_sigmoid(f_pre)
        l_s[...] = jnp.maximum(a1, a2) + jnp.log(1.0 + jnp.exp(-jnp.abs(a1 - a2)))
        k_s[...] = (1.0 - lo) * _sigmoid(-f_pre)
    stf_s[...] = s0_ref[0, 0]
    stb_s[...] = s0_ref[0, 1]

    tri_shape = (c, c)
    lower = (_iota(tri_shape, 0) >= _iota(tri_shape, 1))
    upper = (_iota(tri_shape, 0) <= _iota(tri_shape, 1))
    stack_mask = _head_mask(c)
    bd_mask = _block_diag_mask()
    pair_shape = (c, N_HEADS * c)
    t_idx = _iota(pair_shape, 0)
    s_idx = _iota(pair_shape, 1) % c
    mid = c // 2

    def one_direction(n, k_s, l_s, st_s, o_s, forward):
        r0 = pl.multiple_of(n * c, c)
        rows = pl.ds(r0, c)
        q = q_s[rows, :]
        v = v_s[rows, :]
        k = k_s[rows, :]
        lf = l_s[rows, :]
        tri = (lower if forward else upper).astype(BF16)
        g = _mm_exact_lhs(tri, lf)
        g_ref = g[mid:mid + 1]
        g_tot = g[c - 1:c] if forward else g[0:1]
        q_in = q * jnp.exp(g)
        q_sc = q * jnp.exp(g - g_ref)
        k_sc = k * jnp.exp(g_ref - g)
        k_out = k * jnp.exp(g_tot - g)
        pair = _mm1(q_sc, _stack_heads(k_sc, stack_mask), _NT)
        causal = (t_idx >= s_idx) if forward else (t_idx <= s_idx)
        pair = jnp.where(causal, pair, 0.0)
        o = _mm1(pair, _stack_heads(v, stack_mask))
        st = st_s[...]
        o = o + _mm1(q_in, st, _NT)
        o_s[rows, :] = o
        upd = _mm1(v.T, k_out)
        st_s[...] = st * jnp.exp(g_tot) + jnp.where(bd_mask, upd, 0.0)

    def body(i, carry):
        one_direction(i, kf_s, lf_s, stf_s, of_s, True)
        one_direction(n_chunks - 1 - i, kb_s, lb_s, stb_s, ob_s, False)
        return carry

    lax.fori_loop(0, n_chunks, body, 0)

    ones_bd = jnp.where(bd_mask, 1.0 / HEAD_DIM, 0.0).astype(BF16)
    o = _head_rmsnorm(of_s[...] + ob_s[...], ones_bd) * gain_ref[...]
    y_ref[...] = o * _silu(hg_ref[:, 4 * w:5 * w])
    st_ref[0, 0] = stf_s[...]
    st_ref[0, 1] = stb_s[...]


def _hgrn(hg_all, tok0, batch, seq, lb_l, gain_l, s0_t):
    blk0 = tok0 // seq
    w = HG_WIDTH
    return pl.pallas_call(
        _hgrn_kernel,
        out_shape=[jax.ShapeDtypeStruct((batch * seq, w), F32),
                   jax.ShapeDtypeStruct((batch, 2, w, w), F32)],
        grid=(batch,),
        in_specs=[
            pl.BlockSpec((seq, PROJ_HG), lambda b: (blk0 + b, 0)),
            _const_spec((2, w)), _const_spec((1, w)),
            pl.BlockSpec((1, 2, w, w), lambda b: (b, 0, 0, 0)),
        ],
        out_specs=[pl.BlockSpec((seq, w), lambda b: (b, 0)),
                   pl.BlockSpec((1, 2, w, w), lambda b: (b, 0, 0, 0))],
        scratch_shapes=[pltpu.VMEM((seq, w), F32)] * 8 + [pltpu.VMEM((w, w), F32)] * 2,
        compiler_params=_params(("parallel",)),
        name=f"hgrn2_{seq}",
    )(hg_all, lb_l, gain_l, s0_t)


def _ret_kernel(use_rope, ret_ref, cos_ref, sin_ref, dl_ref, dlp_ref, s0_ref, y_ref, st_ref,
                q_s, k_s, v_s, oi_s, ob_s, stf_s, stb_s):
    seq = ret_ref.shape[0]
    w = RET_WIDTH
    c = min(RET_CHUNK, seq)
    n_chunks = seq // c
    q = ret_ref[:, 0:w]
    k = ret_ref[:, w:2 * w] * (HEAD_DIM ** -0.5)
    if use_rope:
        lane = _iota((seq, w), 1)
        even = (lane % 2) == 0
        cos = cos_ref[...]
        sin = sin_ref[...]

        def rope(x):
            nxt = pltpu.roll(x, w - 1, axis=1)
            prv = pltpu.roll(x, 1, axis=1)
            return x * cos + jnp.where(even, nxt, prv) * sin

        q = rope(q)
        k = rope(k)
    q_s[...] = q
    k_s[...] = k
    v_s[...] = ret_ref[:, 2 * w:3 * w]
    stf_s[...] = s0_ref[0, 0]
    stb_s[...] = s0_ref[0, 1]

    lg = _log_sigmoid(dl_ref[...])
    lgp = _log_sigmoid(dlp_ref[...])
    pair_shape = (c, N_HEADS * c)
    t_idx = _iota(pair_shape, 0)
    s_idx = _iota(pair_shape, 1) % c
    dist = (t_idx - s_idx).astype(F32)
    decay = jnp.exp(jnp.where(dist >= 0, dist * lgp[0:1], -dist * lgp[1:2]))
    decay = decay + jnp.where(dist == 0, 1.0, 0.0)
    stack_mask = _head_mask(c)
    bd_mask = _block_diag_mask()
    pos = _iota((c, 1), 0).astype(F32)
    fc = float(c)

    def body(i, carry):
        rows = pl.ds(pl.multiple_of(i * c, c), c)
        qc, kc, vc = q_s[rows, :], k_s[rows, :], v_s[rows, :]
        pair = _mm1(qc, _stack_heads(kc, stack_mask), _NT) * decay
        o = _mm1(pair, _stack_heads(vc, stack_mask))
        st = stf_s[...]
        o = o + _mm1(qc * jnp.exp((pos + 1.0) * lg[0:1]), st, _NT)
        oi_s[rows, :] = o
        upd = _mm1(vc.T, kc * jnp.exp((fc - 1.0 - pos) * lg[0:1]))
        stf_s[...] = st * jnp.exp(fc * lg[0:1]) + jnp.where(bd_mask, upd, 0.0)

        rows_b = pl.ds(pl.multiple_of((n_chunks - 1 - i) * c, c), c)
        qc, kc, vc = q_s[rows_b, :], k_s[rows_b, :], v_s[rows_b, :]
        st = stb_s[...]
        ob_s[rows_b, :] = _mm1(qc * jnp.exp((fc - pos) * lg[1:2]), st, _NT)
        upd = _mm1(vc.T, kc * jnp.exp(pos * lg[1:2]))
        stb_s[...] = st * jnp.exp(fc * lg[1:2]) + jnp.where(bd_mask, upd, 0.0)
        return carry

    lax.fori_loop(0, n_chunks, body, 0)

    ones_bd = jnp.where(bd_mask, 1.0 / HEAD_DIM, 0.0).astype(BF16)
    o = _head_rmsnorm(oi_s[...] + ob_s[...], ones_bd)
    y_ref[...] = _silu(ret_ref[:, 3 * w:4 * w]) * o
    st_ref[0, 0] = stf_s[...]
    st_ref[0, 1] = stb_s[...]


def _rope_tables(seq):
    rows = seq // GRID_W
    row = np.repeat(np.arange(rows), GRID_W).astype(np.float64)
    col = (np.arange(seq) % GRID_W).astype(np.float64)
    n_freq = HEAD_DIM // 4
    inv_freq = ROPE_BASE ** (-np.arange(n_freq, dtype=np.float64) / n_freq)
    ang = np.concatenate([row[:, None] * inv_freq, col[:, None] * inv_freq], axis=-1)
    ang = np.repeat(ang, 2, axis=1)
    cos = np.tile(np.cos(ang), (1, N_HEADS))
    sin = np.tile(np.sin(ang) * np.where(np.arange(HEAD_DIM) % 2 == 0, -1.0, 1.0)[None, :], (1, N_HEADS))
    return jnp.asarray(cos, F32), jnp.asarray(sin, F32)


def _retention(ret_all, tok0, batch, seq, use_rope, decay_l, s0_t):
    blk0 = tok0 // seq
    w = RET_WIDTH
    c = min(RET_CHUNK, seq)
    cos, sin = _rope_tables(seq)
    dl = jnp.repeat(decay_l, HEAD_DIM, axis=-1)
    dlp = jnp.repeat(decay_l, c, axis=-1)
    return pl.pallas_call(
        functools.partial(_ret_kernel, use_rope),
        out_shape=[jax.ShapeDtypeStruct((batch * seq, w), F32),
                   jax.ShapeDtypeStruct((batch, 2, w, w), F32)],
        grid=(batch,),
        in_specs=[
            pl.BlockSpec((seq, PROJ_RET), lambda b: (blk0 + b, 0)),
            _const_spec((seq, w)), _const_spec((seq, w)),
            _const_spec((2, w)), _const_spec((2, N_HEADS * c)),
            pl.BlockSpec((1, 2, w, w), lambda b: (b, 0, 0, 0)),
        ],
        out_specs=[pl.BlockSpec((seq, w), lambda b: (b, 0)),
                   pl.BlockSpec((1, 2, w, w), lambda b: (b, 0, 0, 0))],
        scratch_shapes=[pltpu.VMEM((seq, w), F32)] * 5 + [pltpu.VMEM((w, w), F32)] * 2,
        compiler_params=_params(("parallel",)),
        name=f"retention_{seq}",
    )(ret_all, cos, sin, dl, dlp, s0_t)


def _route(h2, wr_ref, rb_ref):
    logits = _mm3(h2, wr_ref[...])
    tm = logits.shape[0]
    m = jnp.max(logits, axis=-1, keepdims=True)
    e = jnp.exp(logits - m)
    probs = e / jnp.sum(e, axis=-1, keepdims=True)
    sel = probs + rb_ref[...]
    lane = _iota((tm, N_EXPERTS), 1).astype(F32)
    group = (_iota((tm, N_EXPERTS), 1) // E_PER_GROUP).astype(F32)
    neg = -jnp.inf

    def first_argmax(vals):
        mx = jnp.max(vals, axis=-1, keepdims=True)
        idx = jnp.min(jnp.where(vals == mx, lane, float(N_EXPERTS)), axis=-1, keepdims=True)
        return mx, idx

    best_score = None
    best = None
    for g in range(N_GROUPS):
        vals = jnp.where(group == float(g), sel, neg)
        m1, i1 = first_argmax(vals)
        m2, _ = first_argmax(jnp.where(lane == i1, neg, vals))
        score = m1 + m2
        if g == 0:
            best_score, best = score, jnp.zeros_like(score)
        else:
            take = score > best_score
            best = jnp.where(take, float(g), best)
            best_score = jnp.where(take, score, best_score)
    vals = jnp.where(group == best, sel, neg)
    _, i1 = first_argmax(vals)
    _, i2 = first_argmax(jnp.where(lane == i1, neg, vals))
    chosen = (lane == i1) | (lane == i2)
    picked = jnp.where(chosen, probs, 0.0)
    return picked / jnp.sum(picked, axis=-1, keepdims=True)


def _post_kernel(final, x_ref, yhy_ref, yrg_ref, yhg_ref, yret_ref, mod_ref, wo_ref, n2_ref, wr_ref, rb_ref,
                 nf_ref, wg_ref, wu_ref, wd_ref, o_ref, x1_s, h2_s, gates_s, acc_s):
    e = pl.program_id(1)
    d = D_MODEL
    mod = mod_ref[0, 0]

    @pl.when(e == 0)
    def _():
        mixed = None
        for j, ref in enumerate((yhy_ref, yrg_ref, yhg_ref, yret_ref)):
            part = _mm(ref[...].astype(BF16), wo_ref[j * MIX_WIDTH:(j + 1) * MIX_WIDTH, :])
            mixed = part if mixed is None else mixed + part
        x1 = x_ref[...] + mod[:, 2 * d:3 * d] * mixed
        x1_s[...] = x1
        h2 = x1 * lax.rsqrt(jnp.mean(x1 * x1, axis=-1, keepdims=True) + EPS) * n2_ref[...]
        h2 = h2 * (1.0 + mod[:, 4 * d:5 * d]) + mod[:, 3 * d:4 * d]
        h2_s[...] = h2.astype(BF16)
        gates_s[...] = _route(h2, wr_ref, rb_ref)
        acc_s[...] = jnp.zeros_like(acc_s)

    h2 = h2_s[...]
    gates = gates_s[...]
    lane = _iota(gates.shape, 1)
    ge = jnp.sum(jnp.where(lane == e, gates, 0.0), axis=-1, keepdims=True)
    hh = _silu(_mm(h2, wg_ref[0, 0])) * _mm(h2, wu_ref[0, 0]) * ge
    acc_s[...] += _mm(hh.astype(BF16), wd_ref[0, 0])

    @pl.when(e == N_EXPERTS - 1)
    def _():
        x2 = x1_s[...] + mod[:, 5 * d:6 * d] * acc_s[...]
        if final:
            x2 = x2 * lax.rsqrt(jnp.mean(x2 * x2, axis=-1, keepdims=True) + EPS) * nf_ref[...]
        o_ref[...] = x2


def _post(final, layer, x_all, ys, mod_l, w_out_l, norm2_l, w_router, router_bias, norm_final,
          w_gate, w_up, w_down, n_prompt_tok, dec_seq):
    t_all = x_all.shape[0]
    tm = 1024
    row = _mod_row_map(tm, n_prompt_tok, dec_seq)
    tok = lambda width: pl.BlockSpec((tm, width), lambda i, e: (i, 0))
    return pl.pallas_call(
        functools.partial(_post_kernel, final),
        out_shape=jax.ShapeDtypeStruct((t_all, D_MODEL), F32),
        grid=(t_all // tm, N_EXPERTS),
        in_specs=[
            tok(D_MODEL), tok(MIX_WIDTH), tok(MIX_WIDTH), tok(MIX_WIDTH), tok(MIX_WIDTH),
            pl.BlockSpec((1, 1, 1, N_MOD * D_MODEL), lambda i, e: (0, row(i), 0, 0)),
            _const_spec((D_MODEL, D_MODEL)), _const_spec((1, D_MODEL)),
            _const_spec((D_MODEL, N_EXPERTS)), _const_spec((1, N_EXPERTS)), _const_spec((1, D_MODEL)),
            pl.BlockSpec((1, 1, D_MODEL, D_EXPERT), lambda i, e: (layer, e, 0, 0)),
            pl.BlockSpec((1, 1, D_MODEL, D_EXPERT), lambda i, e: (layer, e, 0, 0)),
            pl.BlockSpec((1, 1, D_EXPERT, D_MODEL), lambda i, e: (layer, e, 0, 0)),
        ],
        out_specs=pl.BlockSpec((tm, D_MODEL), lambda i, e: (i, 0)),
        scratch_shapes=[pltpu.VMEM((tm, D_MODEL), F32), pltpu.VMEM((tm, D_MODEL), BF16),
                        pltpu.VMEM((tm, N_EXPERTS), F32), pltpu.VMEM((tm, D_MODEL), F32)],
        compiler_params=_params(("parallel", "arbitrary")),
        name=f"post_{layer}",
    )(x_all, *ys, mod_l, w_out_l, norm2_l, w_router, router_bias, norm_final, w_gate, w_up, w_down)


def _state_to_kernel(s0):
    b = s0.shape[0]
    eye = jnp.eye(N_HEADS, dtype=s0.dtype)
    return jnp.einsum("bdhkv,hg->bdhvgk", s0, eye).reshape(b, 2, MIX_WIDTH, MIX_WIDTH)


def _state_from_kernel(st):
    b = st.shape[0]
    s = st.reshape(b, 2, N_HEADS, HEAD_DIM, N_HEADS, HEAD_DIM)
    s = jnp.stack([s[:, :, h, :, h, :] for h in range(N_HEADS)], axis=2)
    return jnp.swapaxes(s, -1, -2)


def _rg_gate_weights(wa, ba, wx, bx):
    eye = jnp.eye(RG_HEADS, dtype=wa.dtype)

    def dense(wd):
        return jnp.einsum("hij,hg->higj", wd, eye).reshape(RG_WIDTH, RG_WIDTH)

    w = jnp.concatenate([dense(wa[0]), dense(wa[1]), dense(wx[0]), dense(wx[1])], axis=1)
    b = jnp.concatenate([ba[0], ba[1], bx[0], bx[1]], axis=0).reshape(1, 4 * RG_WIDTH)
    return w, b


def kernel(x_prompt, x_sample, state_rglru, state_hgrn, state_ret, c, c_ctx, norm1, norm2, norm_final, w_ada, b_ada, w_in, w_out, hy_conv_w, hy_conv_b, hy_w1, hy_b1, hy_w2, hy_b2, hy_w3, hy_d, rg_conv_w, rg_conv_b, rg_wa, rg_ba, rg_wx, rg_bx, rg_lambda, hg_lb, hg_norm, ret_decay, w_router, router_bias, w_gate, w_up, w_down):
    batch, seq, d = x_prompt.shape
    dec_batch, dec_seq, _ = x_sample.shape
    assert d == D_MODEL and dec_batch + 1 <= COND_ROWS
    n_prompt_tok = batch * seq

    lb_cum = jnp.cumsum(jax.nn.softmax(hg_lb.astype(F32), axis=0), axis=0)
    lb_all = lb_cum - lb_cum[0:1]

    cond = jnp.zeros((COND_ROWS, d), F32).at[0].set(c_ctx).at[1:1 + dec_batch].set(c)
    mod = _modulation(cond, w_ada, b_ada).reshape(DEPTH, COND_ROWS, 1, N_MOD * d)

    w_in_b = w_in.astype(BF16)
    w_out_b = w_out.astype(BF16)
    w_gate_b, w_up_b, w_down_b = w_gate.astype(BF16), w_up.astype(BF16), w_down.astype(BF16)

    passes = (
        dict(tok0=0, batch=batch, seq=seq, rope=False),
        dict(tok0=n_prompt_tok, batch=dec_batch, seq=dec_seq, rope=True),
    )
    filters = {p["seq"]: _hyena_filters(p["seq"], hy_w1, hy_b1, hy_w2, hy_b2, hy_w3) for p in passes}
    tables = {p["seq"]: _hyena_tables(p["seq"]) for p in passes}

    x_all = jnp.concatenate([x_prompt.reshape(-1, d), x_sample.reshape(-1, d)], axis=0)
    zeros_rg = jnp.zeros((batch, 2, RG_WIDTH), F32)
    zeros_st = jnp.zeros((batch, 2, MIX_WIDTH, MIX_WIDTH), F32)
    new_rg, new_hg, new_ret = [], [], []
    for l in range(DEPTH):
        hy_all, rg_all, hg_all, ret_all = _in_projection(
            x_all, mod[l:l + 1], norm1[l].reshape(1, d), w_in_b[l], n_prompt_tok, dec_seq)
        wg, bg = _rg_gate_weights(rg_wa[l], rg_ba[l], rg_wx[l], rg_bx[l])
        sp = jax.nn.softplus(-rg_lambda[l])
        ys = [[], [], [], []]
        for pi, p in enumerate(passes):
            first = pi == 0
            y_hy = _hyena(hy_all, p["tok0"], p["batch"], p["seq"], hy_conv_w[l], hy_conv_b[l].reshape(1, -1),
                          hy_d[l], filters[p["seq"]][l], tables[p["seq"]])
            y_rg, st_rg = _rglru(rg_all, p["tok0"], p["batch"], p["seq"], rg_conv_w[l],
                                 rg_conv_b[l].reshape(1, -1), wg, bg, sp,
                                 zeros_rg if first else state_rglru[:, l])
            y_hg, st_hg = _hgrn(hg_all, p["tok0"], p["batch"], p["seq"], lb_all[l], hg_norm[l].reshape(1, -1),
                                zeros_st if first else _state_to_kernel(state_hgrn[:, l]))
            y_ret, st_ret = _retention(ret_all, p["tok0"], p["batch"], p["seq"], p["rope"], ret_decay[l],
                                       zeros_st if first else _state_to_kernel(state_ret[:, l]))
            for lst, y in zip(ys, (y_hy, y_rg, y_hg, y_ret)):
                lst.append(y)
            if first:
                new_rg.append(st_rg)
                new_hg.append(_state_from_kernel(st_hg))
                new_ret.append(_state_from_kernel(st_ret))
        ys = [jnp.concatenate(lst, axis=0) for lst in ys]
        x_all = _post(l == DEPTH - 1, l, x_all, ys, mod[l:l + 1], w_out_b[l], norm2[l].reshape(1, d),
                      w_router, router_bias.reshape(1, -1), norm_final.reshape(1, d),
                      w_gate_b, w_up_b, w_down_b, n_prompt_tok, dec_seq)

    y_prompt = x_all[:n_prompt_tok].reshape(batch, seq, d)
    y_sample = x_all[n_prompt_tok:].reshape(dec_batch, dec_seq, d)
    return (y_prompt, y_sample, jnp.stack(new_rg, axis=1), jnp.stack(new_hg, axis=1),
            jnp.stack(new_ret, axis=1))
```

```python
import functools
import math

import numpy as np
import jax
import jax.numpy as jnp
from jax import lax
from jax.experimental import pallas as pl
from jax.experimental.pallas import tpu as pltpu

F32 = jnp.float32
BF16 = jnp.bfloat16

D_MODEL = 1024
DEPTH = 2
GRID_W = 64
HY_WIDTH = 256
RG_WIDTH = 256
HG_WIDTH = 256
RET_WIDTH = 256
MIX_WIDTH = 256
HY_ORDER = 2
HY_EMB = 33
HY_BANDS = 16
HY_FFN = 64
HY_DECAY_TARGET = 1e-2
HY_DECAY_SHORT = 0.3
HY_DECAY_LONG = 1.5
RG_HEADS = 8
RG_HEAD_DIM = 32
RG_C = 8.0
N_HEADS = 4
HEAD_DIM = 64
HG_CHUNK = 64
RET_CHUNK = 256
ROPE_BASE = 10000.0
N_EXPERTS = 16
N_GROUPS = 4
E_PER_GROUP = 4
D_EXPERT = 512
N_MOD = 6
EPS = 1e-6
PROJ_HY = 3 * HY_WIDTH
PROJ_RG = 2 * RG_WIDTH
PROJ_HG = 5 * HG_WIDTH
PROJ_RET = 4 * RET_WIDTH
PROJ_WIDTH = PROJ_HY + PROJ_RG + PROJ_HG + PROJ_RET
COND_ROWS = 16
LANES = 128
VMEM_LIMIT = 56 * 1024 * 1024
TM_PROJ = 512
TM_POST = 1024
TN_MOD = 1536

_NN = (((1,), (0,)), ((), ()))
_NT = (((1,), (1,)), ((), ()))


def _mm(a, b, dn=_NN):
    return lax.dot_general(a, b, dn, preferred_element_type=F32)


def _split2(x):
    hi = x.astype(BF16)
    lo = (x - hi.astype(F32)).astype(BF16)
    return hi, lo


def _split3(x):
    hi = x.astype(BF16)
    r = x - hi.astype(F32)
    mid = r.astype(BF16)
    lo = (r - mid.astype(F32)).astype(BF16)
    return hi, mid, lo


def _mm1(a, b, dn=_NN):
    return _mm(a.astype(BF16), b.astype(BF16), dn)


def _mm3(a, b, dn=_NN):
    ah, al = _split2(a)
    bh, bl = _split2(b)
    return _mm(ah, bh, dn) + (_mm(ah, bl, dn) + _mm(al, bh, dn))


def _mm_exact_lhs(a_bf16, b):
    b1, b2, b3 = _split3(b)
    return _mm(a_bf16, b1) + (_mm(a_bf16, b2) + _mm(a_bf16, b3))


def _mm_exact_rhs(a, b_bf16):
    a1, a2, a3 = _split3(a)
    return _mm(a1, b_bf16) + (_mm(a2, b_bf16) + _mm(a3, b_bf16))


def _sigmoid(x):
    return 1.0 / (1.0 + jnp.exp(-x))


def _silu(x):
    return x * _sigmoid(x)


def _log_sigmoid(x):
    return jnp.minimum(x, 0.0) - jnp.log(1.0 + jnp.exp(-jnp.abs(x)))


def _gelu_tanh(x):
    return 0.5 * x * (1.0 + jnp.tanh(math.sqrt(2.0 / math.pi) * (x + 0.044715 * (x * x * x))))


def _iota(shape, dim):
    return lax.broadcasted_iota(jnp.int32, shape, dim)


def _shift_rows(u, k, row):
    n = u.shape[0]
    if k == 0:
        return u
    r = pltpu.roll(u, (-k) % n, axis=0)
    if k < 0:
        return jnp.where(row >= -k, r, 0.0)
    return jnp.where(row < n - k, r, 0.0)


def _head_mask(n_rows_per_head):
    shape = (N_HEADS * n_rows_per_head, MIX_WIDTH)
    return (_iota(shape, 0) // n_rows_per_head) == (_iota(shape, 1) // HEAD_DIM)


def _stack_heads(x, mask):
    return jnp.where(mask, jnp.concatenate([x] * N_HEADS, axis=0), 0.0)


def _block_diag_mask():
    shape = (MIX_WIDTH, MIX_WIDTH)
    return (_iota(shape, 0) // HEAD_DIM) == (_iota(shape, 1) // HEAD_DIM)


def _head_rmsnorm(o, ones_bd):
    ms = _mm_exact_rhs(o * o, ones_bd)
    return o * lax.rsqrt(ms + EPS)


def _load_state(s0_ref, d):
    zero = jnp.zeros((HEAD_DIM, HEAD_DIM), F32)
    rows = []
    for h in range(N_HEADS):
        blk = s0_ref[0, d, h].T
        rows.append(jnp.concatenate([blk if g == h else zero for g in range(N_HEADS)], axis=1))
    return jnp.concatenate(rows, axis=0)


def _store_state(st_ref, d, st):
    for h in range(N_HEADS):
        lo, hi = h * HEAD_DIM, (h + 1) * HEAD_DIM
        st_ref[0, d, h] = st[lo:hi, lo:hi].T


def _params(sem, vmem=VMEM_LIMIT):
    return pltpu.CompilerParams(dimension_semantics=sem, vmem_limit_bytes=vmem)


def _const_spec(shape):
    nd = len(shape)
    return pl.BlockSpec(shape, lambda *_: (0,) * nd, pipeline_mode=pl.Buffered(1))


def _mod_kernel(cond_ref, w_ref, b_ref, o_ref):
    o_ref[0] = _mm3(_silu(cond_ref[...]), w_ref[0]) + b_ref[0]


def _modulation(cond, w_ada, b_ada):
    tn = TN_MOD
    n_mod = N_MOD * D_MODEL
    return pl.pallas_call(
        _mod_kernel,
        out_shape=jax.ShapeDtypeStruct((DEPTH, COND_ROWS, n_mod), F32),
        grid=(DEPTH, n_mod // tn),
        in_specs=[
            pl.BlockSpec((COND_ROWS, D_MODEL), lambda l, j: (0, 0)),
            pl.BlockSpec((1, D_MODEL, tn), lambda l, j: (l, 0, j)),
            pl.BlockSpec((1, 1, tn), lambda l, j: (l, 0, j)),
        ],
        out_specs=pl.BlockSpec((1, COND_ROWS, tn), lambda l, j: (l, 0, j)),
        compiler_params=_params(("parallel", "parallel")),
        name="modulation",
    )(cond, w_ada, b_ada.reshape(DEPTH, 1, n_mod))


def _proj_kernel(x_ref, mod_ref, n1_ref, w_ref, hy_ref, rg_ref, hg_ref, ret_ref):
    x = x_ref[...]
    mod = mod_ref[0, 0]
    sh1 = mod[:, 0:D_MODEL]
    sc1 = mod[:, D_MODEL:2 * D_MODEL]
    h = x * lax.rsqrt(jnp.mean(x * x, axis=-1, keepdims=True) + EPS) * n1_ref[...]
    h = (h * (1.0 + sc1) + sh1).astype(BF16)
    c0 = 0
    for ref, width in ((hy_ref, PROJ_HY), (rg_ref, PROJ_RG), (hg_ref, PROJ_HG), (ret_ref, PROJ_RET)):
        ref[...] = _mm(h, w_ref[:, c0:c0 + width])
        c0 += width


def _mod_row_map(tm, n_prompt_tok, dec_seq):
    n_prompt_tiles = n_prompt_tok // tm

    def row(i):
        return jnp.where(i < n_prompt_tiles, 0, 1 + (i * tm - n_prompt_tok) // dec_seq)

    return row


def _in_projection(x_all, mod_l, norm1_l, w_in_l, n_prompt_tok, dec_seq):
    t_all = x_all.shape[0]
    tm = TM_PROJ
    row = _mod_row_map(tm, n_prompt_tok, dec_seq)
    widths = (PROJ_HY, PROJ_RG, PROJ_HG, PROJ_RET)
    return pl.pallas_call(
        _proj_kernel,
        out_shape=[jax.ShapeDtypeStruct((t_all, w), F32) for w in widths],
        grid=(t_all // tm,),
        in_specs=[
            pl.BlockSpec((tm, D_MODEL), lambda i: (i, 0)),
            pl.BlockSpec((1, 1, 1, N_MOD * D_MODEL), lambda i: (0, row(i), 0, 0)),
            _const_spec((1, D_MODEL)),
            _const_spec((D_MODEL, PROJ_WIDTH)),
        ],
        out_specs=[pl.BlockSpec((tm, w), lambda i: (i, 0)) for w in widths],
        compiler_params=_params(("parallel",)),
        name="in_projection",
    )(x_all, mod_l, norm1_l, w_in_l)


def _dft_tables(seq):
    k = np.arange(seq, dtype=np.int64)
    m = (k[:, None] * k[None, :]) % (2 * seq)
    ang = np.pi * m.astype(np.float64) / seq
    return np.cos(ang), np.sin(ang)


def _hyena_tables(seq):
    cos, sin = _dft_tables(seq)
    sign = np.where(np.arange(seq) % 2 == 0, 1.0, -1.0)
    fwd = np.concatenate([cos, sign[None, :], sin[1:]], axis=0)
    wk = np.full((seq,), 2.0)
    wk[0] = 1.0
    inv_cos = (cos * wk[None, :]) / (2.0 * seq)
    inv_nyq = sign[:, None] / (2.0 * seq)
    inv_sin = 2.0 * sin[:, 1:] / (2.0 * seq)
    inv = np.concatenate([inv_cos, inv_nyq, inv_sin], axis=1)

    return jnp.asarray(fwd, F32).astype(BF16), jnp.asarray(inv, F32).astype(BF16)


def _filter_embedding(seq):
    t = np.arange(seq, dtype=np.float64)
    t_norm = t / max(seq - 1, 1)
    bands = np.linspace(1e-4, HY_BANDS - 1, HY_BANDS)
    ang = (2.0 * np.pi / seq) * t[:, None] * bands[None, :]
    z = np.concatenate([t_norm[:, None], np.cos(ang), np.sin(ang)], axis=-1)
    z = np.pad(z, ((0, 0), (0, LANES - HY_EMB)))
    deltas = np.abs(np.linspace(math.log(HY_DECAY_TARGET) / HY_DECAY_LONG,
                                math.log(HY_DECAY_TARGET) / HY_DECAY_SHORT, HY_WIDTH))
    window = np.exp(-t_norm[:, None] * deltas[None, :])
    return jnp.asarray(z, F32), jnp.asarray(window, F32)


def _filter_kernel(z_ref, win_ref, cos_ref, sin_ref, w1_ref, b1_ref, w2_ref, b2_ref, w3_ref, o_ref):
    seq = z_ref.shape[0]
    h = jnp.sin(_mm3(z_ref[...], w1_ref[0]) + b1_ref[0])
    h = jnp.sin(_mm3(h, w2_ref[0]) + b2_ref[0])
    h = _mm3(h, w3_ref[0])
    win = win_ref[...]
    row = _iota((seq, 1), 0)
    sums, diffs = [], []
    for o in range(HY_ORDER):
        c0 = o * 2 * HY_WIDTH
        hf = h[:, c0:c0 + HY_WIDTH] * win
        hb = h[:, c0 + HY_WIDTH:c0 + 2 * HY_WIDTH] * win
        ssq = jnp.sum(hf * hf + hb * hb, axis=0, keepdims=True)
        inv = lax.rsqrt(ssq + EPS)
        hf = hf * inv
        hb = jnp.where(row == 0, 0.0, hb * inv)
        sums.append(hf + hb)
        diffs.append(hf - hb)
    hsum = jnp.concatenate(sums, axis=1)
    hdiff = jnp.concatenate(diffs, axis=1)
    h_re = _mm3(cos_ref[...], hsum)
    h_im = _mm3(sin_ref[...], hdiff)
    sign = jnp.where(row % 2 == 0, 1.0, -1.0)
    h_nyq = jnp.sum(sign * hsum, axis=0, keepdims=True)
    o_ref[0, 0] = h_re
    o_ref[0, 1] = h_im
    o_ref[0, 2] = jnp.where(row == 0, h_nyq, h_re)


def _hyena_filters(seq, w1, b1, w2, b2, w3):
    z, window = _filter_embedding(seq)
    cos, sin = _dft_tables(seq)
    n_out = HY_ORDER * 2 * HY_WIDTH
    w1p = jnp.pad(w1, ((0, 0), (0, LANES - HY_EMB), (0, LANES - HY_FFN)))
    b1p = jnp.pad(b1, ((0, 0), (0, LANES - HY_FFN))).reshape(DEPTH, 1, LANES)
    w2p = jnp.pad(w2, ((0, 0), (0, LANES - HY_FFN), (0, LANES - HY_FFN)))
    b2p = jnp.pad(b2, ((0, 0), (0, LANES - HY_FFN))).reshape(DEPTH, 1, LANES)
    w3p = jnp.pad(w3, ((0, 0), (0, LANES - HY_FFN), (0, 0)))
    per_layer = lambda shape: pl.BlockSpec((1,) + shape, lambda l: (l,) + (0,) * len(shape))
    return pl.pallas_call(
        _filter_kernel,
        out_shape=jax.ShapeDtypeStruct((DEPTH, 3, seq, HY_ORDER * HY_WIDTH), F32),
        grid=(DEPTH,),
        in_specs=[
            _const_spec((seq, LANES)), _const_spec((seq, HY_WIDTH)),
            _const_spec((seq, seq)), _const_spec((seq, seq)),
            per_layer((LANES, LANES)), per_layer((1, LANES)),
            per_layer((LANES, LANES)), per_layer((1, LANES)),
            per_layer((LANES, n_out)),
        ],
        out_specs=pl.BlockSpec((1, 3, seq, HY_ORDER * HY_WIDTH), lambda l: (l, 0, 0, 0)),
        compiler_params=_params(("parallel",)),
        name=f"hyena_filters_{seq}",
    )(z, window, jnp.asarray(cos, F32), jnp.asarray(sin, F32), w1p, b1p, w2p, b2p, w3p)


def _hyena_kernel(u_ref, cw_ref, cb_ref, d_ref, filt_ref, fwd_ref, inv_ref, y_ref):
    seq = u_ref.shape[0]
    u = u_ref[...]
    row = _iota((seq, 1), 0)
    cw = cw_ref[...]
    uc = cb_ref[...] + cw[0:1] * _shift_rows(u, -1, row) + cw[1:2] * u + cw[2:3] * _shift_rows(u, 1, row)
    v = uc[:, 0:HY_WIDTH]
    gates = (uc[:, HY_WIDTH:2 * HY_WIDTH], uc[:, 2 * HY_WIDTH:3 * HY_WIDTH])
    fwd, inv = fwd_ref[...], inv_ref[...]
    d = d_ref[...]
    z = v
    for o in range(HY_ORDER):
        c0 = o * HY_WIDTH
        spec = _mm(fwd, z.astype(BF16))
        s_re, s_im = spec[:seq], spec[seq:]
        a = filt_ref[0, :, c0:c0 + HY_WIDTH]
        b = filt_ref[1, :, c0:c0 + HY_WIDTH]
        c = filt_ref[2, :, c0:c0 + HY_WIDTH]
        prod = jnp.concatenate([s_re * a - s_im * b, s_re * b + s_im * c], axis=0)
        conv = _mm(inv, prod.astype(BF16))
        z = gates[o] * (conv + d[o:o + 1] * z)
    y_ref[...] = z


def _mixer_call(body, name, proj_all, tok0, batch, seq, consts, batch_ins, y_prev, state_shapes, scratch):
    t_all, proj_width = proj_all.shape
    blk0 = tok0 // seq
    n_in = 1 + len(consts) + len(batch_ins)
    has_prev = y_prev is not None

    def kern(*refs):
        body(*refs[:n_in], *refs[n_in + (1 if has_prev else 0):])

    def batch_spec(shape):
        nd = len(shape)
        return pl.BlockSpec((1,) + tuple(shape[1:]), lambda b: (b,) + (0,) * (nd - 1))

    in_specs = [pl.BlockSpec((seq, proj_width), lambda b: (blk0 + b, 0))]
    in_specs += [_const_spec(a.shape) for a in consts]
    in_specs += [batch_spec(a.shape) for a in batch_ins]
    args = [proj_all, *consts, *batch_ins]
    if has_prev:
        in_specs.append(pl.BlockSpec(memory_space=pl.ANY))
        args.append(y_prev)
    out_shape = [jax.ShapeDtypeStruct((t_all, MIX_WIDTH), F32)]
    out_specs = [pl.BlockSpec((seq, MIX_WIDTH), lambda b: (blk0 + b, 0))]
    for shape in state_shapes:
        out_shape.append(jax.ShapeDtypeStruct(shape, F32))
        out_specs.append(batch_spec(shape))
    return pl.pallas_call(
        kern, out_shape=out_shape, grid=(batch,), in_specs=in_specs, out_specs=out_specs,
        scratch_shapes=scratch,
        input_output_aliases={n_in: 0} if has_prev else {},
        compiler_params=_params(("parallel",)),
        name=f"{name}_{seq}",
    )(*args)


def _hyena(u_all, tok0, batch, seq, y_prev, conv_w, conv_b, d_bias, filt_l, tables):
    consts = [conv_w, conv_b, d_bias, filt_l, *tables]
    return _mixer_call(_hyena_kernel, "hyena", u_all, tok0, batch, seq, consts, [], y_prev, [], [])[0]


def _rglru_kernel(has_s0, emit_state, rg_ref, cw_ref, cb_ref, wg_ref, bg_ref, sp_ref, *refs):
    h0_ref = refs[0] if has_s0 else None
    y_ref = refs[1 if has_s0 else 0]
    st_ref = refs[-1] if emit_state else None
    seq = rg_ref.shape[0]
    w = RG_WIDTH
    xr = rg_ref[:, 0:w]
    gate = rg_ref[:, w:2 * w]
    row = _iota((seq, 1), 0)
    cw = cw_ref[...]
    xc = (cb_ref[...] + cw[0:1] * _shift_rows(xr, -2, row) + cw[1:2] * _shift_rows(xr, -1, row)
          + cw[2:3] * xr + cw[3:4] * _shift_rows(xr, 1, row))
    g = _sigmoid(_mm3(xc, wg_ref[...]) + bg_ref[...])
    sp = sp_ref[...]
    hs = []
    for d in range(2):
        r = g[:, d * w:(d + 1) * w]
        i = g[:, (2 + d) * w:(3 + d) * w]
        log_a = -RG_C * r * sp[d:d + 1]
        a = jnp.exp(log_a)
        b = jnp.sqrt(jnp.tanh(-log_a) * (1.0 + a * a)) * (i * xc)
        step = 1
        while step < seq:
            if d == 0:
                keep = row >= step
                a_s = pltpu.roll(a, step, axis=0)
                b_s = pltpu.roll(b, step, axis=0)
            else:
                keep = row < seq - step
                a_s = pltpu.roll(a, seq - step, axis=0)
                b_s = pltpu.roll(b, seq - step, axis=0)
            b = jnp.where(keep, a * b_s + b, b)
            a = jnp.where(keep, a * a_s, a)
            step *= 2
        hs.append(b + a * h0_ref[0, d:d + 1, :] if has_s0 else b)
    y_ref[...] = (hs[0] + hs[1]) * _gelu_tanh(gate)
    if emit_state:
        st_ref[0, 0:1, :] = hs[0][seq - 1:seq]
        st_ref[0, 1:2, :] = hs[1][0:1]


def _rglru(rg_all, tok0, batch, seq, y_prev, conv_w, conv_b, w_gates, b_gates, softplus_neg_lam, h0, emit_state):
    has_s0 = h0 is not None
    consts = [conv_w, conv_b, w_gates, b_gates, softplus_neg_lam]
    return _mixer_call(functools.partial(_rglru_kernel, has_s0, emit_state), "rglru", rg_all, tok0, batch, seq,
                       consts, [h0] if has_s0 else [], y_prev,
                       [(batch, 2, RG_WIDTH)] if emit_state else [], [])


def _hgrn_kernel(has_s0, emit_state, hg_ref, lb_ref, gain_ref, *refs):
    s0_ref = refs[0] if has_s0 else None
    y_ref = refs[1 if has_s0 else 0]
    st_ref = refs[-11] if emit_state else None
    q_s, v_s, kf_s, lf_s, kb_s, lb_s, of_s, ob_s, stf_s, stb_s = refs[-10:]
    seq = hg_ref.shape[0]
    w = HG_WIDTH
    c = HG_CHUNK
    n_chunks = seq // c
    q_s[...] = _silu(hg_ref[:, 0:w])
    v_s[...] = hg_ref[:, 3 * w:4 * w]
    lbv = lb_ref[...]
    for d, (k_s, l_s) in enumerate(((kf_s, lf_s), (kb_s, lb_s))):
        f_pre = hg_ref[:, (1 + d) * w:(2 + d) * w]
        lo = lbv[d:d + 1]
        a1 = jnp.log(lo)
        a2 = jnp.log(1.0 - lo) + _log_sigmoid(f_pre)
        l_s[...] = jnp.maximum(a1, a2) + jnp.log(1.0 + jnp.exp(-jnp.abs(a1 - a2)))
        k_s[...] = (1.0 - lo) * _sigmoid(-f_pre)
    for d, st_s in enumerate((stf_s, stb_s)):
        st_s[...] = _load_state(s0_ref, d) if has_s0 else jnp.zeros_like(st_s)

    tri_shape = (c, c)
    lower = (_iota(tri_shape, 0) >= _iota(tri_shape, 1))
    upper = (_iota(tri_shape, 0) <= _iota(tri_shape, 1))
    stack_mask = _head_mask(c)
    bd_mask = _block_diag_mask()
    pair_shape = (c, N_HEADS * c)
    t_idx = _iota(pair_shape, 0)
    s_idx = _iota(pair_shape, 1) % c
    mid = c // 2

    def one_direction(n, k_s, l_s, st_s, o_s, forward):
        r0 = pl.multiple_of(n * c, c)
        rows = pl.ds(r0, c)
        q = q_s[rows, :]
        v = v_s[rows, :]
        k = k_s[rows, :]
        lf = l_s[rows, :]
        tri = (lower if forward else upper).astype(BF16)
        g = _mm_exact_lhs(tri, lf)
        g_ref = g[mid:mid + 1]
        g_tot = g[c - 1:c] if forward else g[0:1]
        q_in = q * jnp.exp(g)
        q_sc = q * jnp.exp(g - g_ref)
        k_sc = k * jnp.exp(g_ref - g)
        k_out = k * jnp.exp(g_tot - g)
        pair = _mm1(q_sc, _stack_heads(k_sc, stack_mask), _NT)
        causal = (t_idx >= s_idx) if forward else (t_idx <= s_idx)
        pair = jnp.where(causal, pair, 0.0)
        o = _mm1(pair, _stack_heads(v, stack_mask))
        st = st_s[...]
        o = o + _mm1(q_in, st, _NT)
        o_s[rows, :] = o
        upd = _mm1(v.T, k_out)
        st_s[...] = st * jnp.exp(g_tot) + jnp.where(bd_mask, upd, 0.0)

    def body(i, carry):
        one_direction(i, kf_s, lf_s, stf_s, of_s, True)
        one_direction(n_chunks - 1 - i, kb_s, lb_s, stb_s, ob_s, False)
        return carry

    lax.fori_loop(0, n_chunks, body, 0)

    ones_bd = jnp.where(bd_mask, 1.0 / HEAD_DIM, 0.0).astype(BF16)
    o = _head_rmsnorm(of_s[...] + ob_s[...], ones_bd) * gain_ref[...]
    y_ref[...] = o * _silu(hg_ref[:, 4 * w:5 * w])
    if emit_state:
        _store_state(st_ref, 0, stf_s[...])
        _store_state(st_ref, 1, stb_s[...])


def _state_shape(batch):
    return (batch, 2, N_HEADS, HEAD_DIM, HEAD_DIM)


def _hgrn(hg_all, tok0, batch, seq, y_prev, lb_l, gain_l, s0, emit_state):
    has_s0 = s0 is not None
    w = HG_WIDTH
    scratch = [pltpu.VMEM((seq, w), F32)] * 8 + [pltpu.VMEM((w, w), F32)] * 2
    return _mixer_call(functools.partial(_hgrn_kernel, has_s0, emit_state), "hgrn2", hg_all, tok0, batch, seq,
                       [lb_l, gain_l], [s0] if has_s0 else [], y_prev,
                       [_state_shape(batch)] if emit_state else [], scratch)


def _ret_kernel(use_rope, has_s0, emit_state, ret_ref, cos_ref, sin_ref, dl_ref, dlp_ref, *refs):
    s0_ref = refs[0] if has_s0 else None
    y_ref = refs[1 if has_s0 else 0]
    st_ref = refs[-8] if emit_state else None
    q_s, k_s, v_s, oi_s, ob_s, stf_s, stb_s = refs[-7:]
    seq = ret_ref.shape[0]
    w = RET_WIDTH
    c = min(RET_CHUNK, seq)
    n_chunks = seq // c
    q = ret_ref[:, 0:w]
    k = ret_ref[:, w:2 * w] * (HEAD_DIM ** -0.5)
    if use_rope:
        lane = _iota((seq, w), 1)
        even = (lane % 2) == 0
        cos = cos_ref[...]
        sin = sin_ref[...]

        def rope(x):
            nxt = pltpu.roll(x, w - 1, axis=1)
            prv = pltpu.roll(x, 1, axis=1)
            return x * cos + jnp.where(even, nxt, prv) * sin

        q = rope(q)
        k = rope(k)
    q_s[...] = q
    k_s[...] = k
    v_s[...] = ret_ref[:, 2 * w:3 * w]
    for d, st_s in enumerate((stf_s, stb_s)):
        st_s[...] = _load_state(s0_ref, d) if has_s0 else jnp.zeros_like(st_s)

    lg = _log_sigmoid(dl_ref[...])
    lgp = _log_sigmoid(dlp_ref[...])
    pair_shape = (c, N_HEADS * c)
    t_idx = _iota(pair_shape, 0)
    s_idx = _iota(pair_shape, 1) % c
    dist = (t_idx - s_idx).astype(F32)
    decay = jnp.exp(jnp.where(dist >= 0, dist * lgp[0:1], -dist * lgp[1:2]))
    decay = decay + jnp.where(dist == 0, 1.0, 0.0)
    stack_mask = _head_mask(c)
    bd_mask = _block_diag_mask()
    pos = _iota((c, 1), 0).astype(F32)
    fc = float(c)

    def body(i, carry):
        rows = pl.ds(pl.multiple_of(i * c, c), c)
        qc, kc, vc = q_s[rows, :], k_s[rows, :], v_s[rows, :]
        pair = _mm1(qc, _stack_heads(kc, stack_mask), _NT) * decay
        o = _mm1(pair, _stack_heads(vc, stack_mask))
        st = stf_s[...]
        o = o + _mm1(qc * jnp.exp((pos + 1.0) * lg[0:1]), st, _NT)
        oi_s[rows, :] = o
        upd = _mm1(vc.T, kc * jnp.exp((fc - 1.0 - pos) * lg[0:1]))
        stf_s[...] = st * jnp.exp(fc * lg[0:1]) + jnp.where(bd_mask, upd, 0.0)

        rows_b = pl.ds(pl.multiple_of((n_chunks - 1 - i) * c, c), c)
        qc, kc, vc = q_s[rows_b, :], k_s[rows_b, :], v_s[rows_b, :]
        st = stb_s[...]
        ob_s[rows_b, :] = _mm1(qc * jnp.exp((fc - pos) * lg[1:2]), st, _NT)
        upd = _mm1(vc.T, kc * jnp.exp(pos * lg[1:2]))
        stb_s[...] = st * jnp.exp(fc * lg[1:2]) + jnp.where(bd_mask, upd, 0.0)
        return carry

    lax.fori_loop(0, n_chunks, body, 0)

    ones_bd = jnp.where(bd_mask, 1.0 / HEAD_DIM, 0.0).astype(BF16)
    o = _head_rmsnorm(oi_s[...] + ob_s[...], ones_bd)
    y_ref[...] = _silu(ret_ref[:, 3 * w:4 * w]) * o
    if emit_state:
        _store_state(st_ref, 0, stf_s[...])
        _store_state(st_ref, 1, stb_s[...])


def _rope_tables(seq):
    rows = seq // GRID_W
    row = np.repeat(np.arange(rows), GRID_W).astype(np.float64)
    col = (np.arange(seq) % GRID_W).astype(np.float64)
    n_freq = HEAD_DIM // 4
    inv_freq = ROPE_BASE ** (-np.arange(n_freq, dtype=np.float64) / n_freq)
    ang = np.concatenate([row[:, None] * inv_freq, col[:, None] * inv_freq], axis=-1)
    ang = np.repeat(ang, 2, axis=1)
    cos = np.tile(np.cos(ang), (1, N_HEADS))
    sin = np.tile(np.sin(ang) * np.where(np.arange(HEAD_DIM) % 2 == 0, -1.0, 1.0)[None, :], (1, N_HEADS))
    return jnp.asarray(cos, F32), jnp.asarray(sin, F32)


def _retention(ret_all, tok0, batch, seq, y_prev, use_rope, decay_l, s0, emit_state):
    has_s0 = s0 is not None
    w = RET_WIDTH
    c = min(RET_CHUNK, seq)
    cos, sin = _rope_tables(seq)
    dl = jnp.repeat(decay_l, HEAD_DIM, axis=-1)
    dlp = jnp.repeat(decay_l, c, axis=-1)
    scratch = [pltpu.VMEM((seq, w), F32)] * 5 + [pltpu.VMEM((w, w), F32)] * 2
    return _mixer_call(functools.partial(_ret_kernel, use_rope, has_s0, emit_state), "retention", ret_all, tok0,
                       batch, seq, [cos, sin, dl, dlp], [s0] if has_s0 else [], y_prev,
                       [_state_shape(batch)] if emit_state else [], scratch)


def _route(h2, wr_ref, rb_ref):
    logits = _mm3(h2, wr_ref[...])
    tm = logits.shape[0]
    m = jnp.max(logits, axis=-1, keepdims=True)
    e = jnp.exp(logits - m)
    probs = e / jnp.sum(e, axis=-1, keepdims=True)
    sel = probs + rb_ref[...]
    lane = _iota((tm, N_EXPERTS), 1).astype(F32)
    group = (_iota((tm, N_EXPERTS), 1) // E_PER_GROUP).astype(F32)
    neg = -jnp.inf

    def first_argmax(vals):
        mx = jnp.max(vals, axis=-1, keepdims=True)
        idx = jnp.min(jnp.where(vals == mx, lane, float(N_EXPERTS)), axis=-1, keepdims=True)
        return mx, idx

    best_score = None
    best = None
    for g in range(N_GROUPS):
        vals = jnp.where(group == float(g), sel, neg)
        m1, i1 = first_argmax(vals)
        m2, _ = first_argmax(jnp.where(lane == i1, neg, vals))
        score = m1 + m2
        if g == 0:
            best_score, best = score, jnp.zeros_like(score)
        else:
            take = score > best_score
            best = jnp.where(take, float(g), best)
            best_score = jnp.where(take, score, best_score)
    vals = jnp.where(group == best, sel, neg)
    _, i1 = first_argmax(vals)
    _, i2 = first_argmax(jnp.where(lane == i1, neg, vals))
    chosen = (lane == i1) | (lane == i2)
    picked = jnp.where(chosen, probs, 0.0)
    return picked / jnp.sum(picked, axis=-1, keepdims=True)


def _post_kernel(final, x_ref, yhy_ref, yrg_ref, yhg_ref, yret_ref, mod_ref, wo_ref, n2_ref, wr_ref, rb_ref,
                 nf_ref, wg_ref, wu_ref, wd_ref, o_ref, x1_s, h2_s, gates_s, acc_s):
    e = pl.program_id(1)
    d = D_MODEL
    mod = mod_ref[0, 0]

    @pl.when(e == 0)
    def _():
        mixed = None
        for j, ref in enumerate((yhy_ref, yrg_ref, yhg_ref, yret_ref)):
            part = _mm(ref[...].astype(BF16), wo_ref[j * MIX_WIDTH:(j + 1) * MIX_WIDTH, :])
            mixed = part if mixed is None else mixed + part
        x1 = x_ref[...] + mod[:, 2 * d:3 * d] * mixed
        x1_s[...] = x1
        h2 = x1 * lax.rsqrt(jnp.mean(x1 * x1, axis=-1, keepdims=True) + EPS) * n2_ref[...]
        h2 = h2 * (1.0 + mod[:, 4 * d:5 * d]) + mod[:, 3 * d:4 * d]
        h2_s[...] = h2.astype(BF16)
        gates_s[...] = _route(h2, wr_ref, rb_ref)
        acc_s[...] = jnp.zeros_like(acc_s)

    h2 = h2_s[...]
    gates = gates_s[...]
    lane = _iota(gates.shape, 1)
    ge = jnp.sum(jnp.where(lane == e, gates, 0.0), axis=-1, keepdims=True)
    hh = _silu(_mm(h2, wg_ref[0, 0])) * _mm(h2, wu_ref[0, 0]) * ge
    acc_s[...] += _mm(hh.astype(BF16), wd_ref[0, 0])

    @pl.when(e == N_EXPERTS - 1)
    def _():
        x2 = x1_s[...] + mod[:, 5 * d:6 * d] * acc_s[...]
        if final:
            x2 = x2 * lax.rsqrt(jnp.mean(x2 * x2, axis=-1, keepdims=True) + EPS) * nf_ref[...]
        o_ref[...] = x2


def _post(final, layer, x_all, ys, mod_l, w_out_l, norm2_l, w_router, router_bias, norm_final,
          w_gate, w_up, w_down, n_prompt_tok, dec_seq):
    t_all = x_all.shape[0]
    tm = TM_POST
    row = _mod_row_map(tm, n_prompt_tok, dec_seq)
    tok = lambda width: pl.BlockSpec((tm, width), lambda i, e: (i, 0))
    return pl.pallas_call(
        functools.partial(_post_kernel, final),
        out_shape=jax.ShapeDtypeStruct((t_all, D_MODEL), F32),
        grid=(t_all // tm, N_EXPERTS),
        in_specs=[
            tok(D_MODEL), tok(MIX_WIDTH), tok(MIX_WIDTH), tok(MIX_WIDTH), tok(MIX_WIDTH),
            pl.BlockSpec((1, 1, 1, N_MOD * D_MODEL), lambda i, e: (0, row(i), 0, 0)),
            _const_spec((D_MODEL, D_MODEL)), _const_spec((1, D_MODEL)),
            _const_spec((D_MODEL, N_EXPERTS)), _const_spec((1, N_EXPERTS)), _const_spec((1, D_MODEL)),
            pl.BlockSpec((1, 1, D_MODEL, D_EXPERT), lambda i, e: (layer, e, 0, 0)),
            pl.BlockSpec((1, 1, D_MODEL, D_EXPERT), lambda i, e: (layer, e, 0, 0)),
            pl.BlockSpec((1, 1, D_EXPERT, D_MODEL), lambda i, e: (layer, e, 0, 0)),
        ],
        out_specs=pl.BlockSpec((tm, D_MODEL), lambda i, e: (i, 0)),
        scratch_shapes=[pltpu.VMEM((tm, D_MODEL), F32), pltpu.VMEM((tm, D_MODEL), BF16),
                        pltpu.VMEM((tm, N_EXPERTS), F32), pltpu.VMEM((tm, D_MODEL), F32)],
        compiler_params=_params(("parallel", "arbitrary")),
        name=f"post_{layer}",
    )(x_all, *ys, mod_l, w_out_l, norm2_l, w_router, router_bias, norm_final, w_gate, w_up, w_down)


def _rg_gate_weights(wa, ba, wx, bx):
    eye = jnp.eye(RG_HEADS, dtype=wa.dtype)

    def dense(wd):
        return jnp.einsum("hij,hg->higj", wd, eye).reshape(RG_WIDTH, RG_WIDTH)

    w = jnp.concatenate([dense(wa[0]), dense(wa[1]), dense(wx[0]), dense(wx[1])], axis=1)
    b = jnp.concatenate([ba[0], ba[1], bx[0], bx[1]], axis=0).reshape(1, 4 * RG_WIDTH)
    return w, b


def kernel(x_prompt, x_sample, state_rglru, state_hgrn, state_ret, c, c_ctx, norm1, norm2, norm_final, w_ada, b_ada, w_in, w_out, hy_conv_w, hy_conv_b, hy_w1, hy_b1, hy_w2, hy_b2, hy_w3, hy_d, rg_conv_w, rg_conv_b, rg_wa, rg_ba, rg_wx, rg_bx, rg_lambda, hg_lb, hg_norm, ret_decay, w_router, router_bias, w_gate, w_up, w_down):
    batch, seq, d = x_prompt.shape
    dec_batch, dec_seq, _ = x_sample.shape
    assert d == D_MODEL and dec_batch + 1 <= COND_ROWS
    n_prompt_tok = batch * seq

    lb_cum = jnp.cumsum(jax.nn.softmax(hg_lb.astype(F32), axis=0), axis=0)
    lb_all = lb_cum - lb_cum[0:1]

    cond = jnp.zeros((COND_ROWS, d), F32).at[0].set(c_ctx).at[1:1 + dec_batch].set(c)
    mod = _modulation(cond, w_ada, b_ada).reshape(DEPTH, COND_ROWS, 1, N_MOD * d)

    w_in_b = w_in.astype(BF16)
    w_out_b = w_out.astype(BF16)
    w_gate_b, w_up_b, w_down_b = w_gate.astype(BF16), w_up.astype(BF16), w_down.astype(BF16)

    passes = (
        dict(tok0=0, batch=batch, seq=seq, rope=False),
        dict(tok0=n_prompt_tok, batch=dec_batch, seq=dec_seq, rope=True),
    )
    filters = {p["seq"]: _hyena_filters(p["seq"], hy_w1, hy_b1, hy_w2, hy_b2, hy_w3) for p in passes}
    tables = {p["seq"]: _hyena_tables(p["seq"]) for p in passes}

    x_all = jnp.concatenate([x_prompt.reshape(-1, d), x_sample.reshape(-1, d)], axis=0)
    new_rg, new_hg, new_ret = [], [], []
    for l in range(DEPTH):
        hy_all, rg_all, hg_all, ret_all = _in_projection(
            x_all, mod[l:l + 1], norm1[l].reshape(1, d), w_in_b[l], n_prompt_tok, dec_seq)
        wg, bg = _rg_gate_weights(rg_wa[l], rg_ba[l], rg_wx[l], rg_bx[l])
        sp = jax.nn.softplus(-rg_lambda[l])
        ys = [None] * 4
        for pi, p in enumerate(passes):
            first = pi == 0
            geom = (p["tok0"], p["batch"], p["seq"])
            ys[0] = _hyena(hy_all, *geom, ys[0], hy_conv_w[l], hy_conv_b[l].reshape(1, -1), hy_d[l],
                           filters[p["seq"]][l], tables[p["seq"]])
            ys[1], *st_rg = _rglru(rg_all, *geom, ys[1], rg_conv_w[l], rg_conv_b[l].reshape(1, -1), wg, bg, sp,
                                   None if first else state_rglru[:, l], first)
            ys[2], *st_hg = _hgrn(hg_all, *geom, ys[2], lb_all[l], hg_norm[l].reshape(1, -1),
                                  None if first else state_hgrn[:, l], first)
            ys[3], *st_ret = _retention(ret_all, *geom, ys[3], p["rope"], ret_decay[l],
                                        None if first else state_ret[:, l], first)
            if first:
                new_rg.append(st_rg[0])
                new_hg.append(st_hg[0])
                new_ret.append(st_ret[0])
        x_all = _post(l == DEPTH - 1, l, x_all, ys, mod[l:l + 1], w_out_b[l], norm2[l].reshape(1, d),
                      w_router, router_bias.reshape(1, -1), norm_final.reshape(1, d),
                      w_gate_b, w_up_b, w_down_b, n_prompt_tok, dec_seq)

    y_prompt = x_all[:n_prompt_tok].reshape(batch, seq, d)
    y_sample = x_all[n_prompt_tok:].reshape(dec_batch, dec_seq, d)
    return (y_prompt, y_sample, jnp.stack(new_rg, axis=1), jnp.stack(new_hg, axis=1),
            jnp.stack(new_ret, axis=1))
```

```python
import functools
import math

import numpy as np
import jax
import jax.numpy as jnp
from jax import lax
from jax.experimental import pallas as pl
from jax.experimental.pallas import tpu as pltpu

F32 = jnp.float32
BF16 = jnp.bfloat16

D_MODEL = 1024
DEPTH = 2
GRID_W = 64
HY_WIDTH = 256
RG_WIDTH = 256
HG_WIDTH = 256
RET_WIDTH = 256
MIX_WIDTH = 256
HY_ORDER = 2
HY_EMB = 33
HY_BANDS = 16
HY_FFN = 64
HY_DECAY_TARGET = 1e-2
HY_DECAY_SHORT = 0.3
HY_DECAY_LONG = 1.5
RG_HEADS = 8
RG_HEAD_DIM = 32
RG_C = 8.0
N_HEADS = 4
HEAD_DIM = 64
HG_CHUNK = 64
RET_CHUNK = 256
ROPE_BASE = 10000.0
N_EXPERTS = 16
N_GROUPS = 4
E_PER_GROUP = 4
D_EXPERT = 512
N_MOD = 6
EPS = 1e-6
PROJ_HY = 3 * HY_WIDTH
PROJ_RG = 2 * RG_WIDTH
PROJ_HG = 5 * HG_WIDTH
PROJ_RET = 4 * RET_WIDTH
PROJ_WIDTH = PROJ_HY + PROJ_RG + PROJ_HG + PROJ_RET
COND_ROWS = 16
LANES = 128
VMEM_LIMIT = 56 * 1024 * 1024
TM_PROJ = 512
TB_MOE = 256
TS_MOE = 512
TN_MOD = 1536

_NN = (((1,), (0,)), ((), ()))
_NT = (((1,), (1,)), ((), ()))


def _mm(a, b, dn=_NN):
    return lax.dot_general(a, b, dn, preferred_element_type=F32)


def _split2(x):
    hi = x.astype(BF16)
    lo = (x - hi.astype(F32)).astype(BF16)
    return hi, lo


def _split3(x):
    hi = x.astype(BF16)
    r = x - hi.astype(F32)
    mid = r.astype(BF16)
    lo = (r - mid.astype(F32)).astype(BF16)
    return hi, mid, lo


def _mm1(a, b, dn=_NN):
    return _mm(a.astype(BF16), b.astype(BF16), dn)


def _mm3(a, b, dn=_NN):
    ah, al = _split2(a)
    bh, bl = _split2(b)
    return _mm(ah, bh, dn) + (_mm(ah, bl, dn) + _mm(al, bh, dn))


def _mm_exact_lhs(a_bf16, b):
    b1, b2, b3 = _split3(b)
    return _mm(a_bf16, b1) + (_mm(a_bf16, b2) + _mm(a_bf16, b3))


def _mm_exact_rhs(a, b_bf16):
    a1, a2, a3 = _split3(a)
    return _mm(a1, b_bf16) + (_mm(a2, b_bf16) + _mm(a3, b_bf16))


def _sigmoid(x):
    return 1.0 / (1.0 + jnp.exp(-x))


def _silu(x):
    return x * _sigmoid(x)


def _log_sigmoid(x):
    return jnp.minimum(x, 0.0) - jnp.log(1.0 + jnp.exp(-jnp.abs(x)))


def _gelu_tanh(x):
    return 0.5 * x * (1.0 + jnp.tanh(math.sqrt(2.0 / math.pi) * (x + 0.044715 * (x * x * x))))


def _iota(shape, dim):
    return lax.broadcasted_iota(jnp.int32, shape, dim)


def _shift_rows(u, k, row):
    n = u.shape[0]
    if k == 0:
        return u
    r = pltpu.roll(u, (-k) % n, axis=0)
    if k < 0:
        return jnp.where(row >= -k, r, 0.0)
    return jnp.where(row < n - k, r, 0.0)


def _head_mask(n_rows_per_head):
    shape = (N_HEADS * n_rows_per_head, MIX_WIDTH)
    return (_iota(shape, 0) // n_rows_per_head) == (_iota(shape, 1) // HEAD_DIM)


def _stack_heads(x, mask):
    return jnp.where(mask, jnp.concatenate([x] * N_HEADS, axis=0), 0.0)


def _block_diag_mask():
    shape = (MIX_WIDTH, MIX_WIDTH)
    return (_iota(shape, 0) // HEAD_DIM) == (_iota(shape, 1) // HEAD_DIM)


def _head_rmsnorm(o, ones_bd):
    ms = _mm_exact_rhs(o * o, ones_bd)
    return o * lax.rsqrt(ms + EPS)


def _load_state(s0_ref, d):
    zero = jnp.zeros((HEAD_DIM, HEAD_DIM), F32)
    rows = []
    for h in range(N_HEADS):
        blk = s0_ref[0, d, h].T
        rows.append(jnp.concatenate([blk if g == h else zero for g in range(N_HEADS)], axis=1))
    return jnp.concatenate(rows, axis=0)


def _store_state(st_ref, d, st):
    for h in range(N_HEADS):
        lo, hi = h * HEAD_DIM, (h + 1) * HEAD_DIM
        st_ref[0, d, h] = st[lo:hi, lo:hi].T


def _params(sem, vmem=VMEM_LIMIT):
    return pltpu.CompilerParams(dimension_semantics=sem, vmem_limit_bytes=vmem)


def _const_spec(shape):
    nd = len(shape)
    return pl.BlockSpec(shape, lambda *_: (0,) * nd, pipeline_mode=pl.Buffered(1))


def _mod_kernel(cond_ref, w_ref, b_ref, o_ref):
    o_ref[0] = _mm3(_silu(cond_ref[...]), w_ref[0]) + b_ref[0]


def _modulation(cond, w_ada, b_ada):
    tn = TN_MOD
    n_mod = N_MOD * D_MODEL
    return pl.pallas_call(
        _mod_kernel,
        out_shape=jax.ShapeDtypeStruct((DEPTH, COND_ROWS, n_mod), F32),
        grid=(DEPTH, n_mod // tn),
        in_specs=[
            pl.BlockSpec((COND_ROWS, D_MODEL), lambda l, j: (0, 0)),
            pl.BlockSpec((1, D_MODEL, tn), lambda l, j: (l, 0, j)),
            pl.BlockSpec((1, 1, tn), lambda l, j: (l, 0, j)),
        ],
        out_specs=pl.BlockSpec((1, COND_ROWS, tn), lambda l, j: (l, 0, j)),
        compiler_params=_params(("parallel", "parallel")),
        name="modulation",
    )(cond, w_ada, b_ada.reshape(DEPTH, 1, n_mod))


def _pair_specs(tm, width, n0_tiles):
    return [pl.BlockSpec((tm, width), lambda i, *_: (jnp.minimum(i, n0_tiles - 1), 0)),
            pl.BlockSpec((tm, width), lambda i, *_: (jnp.maximum(i - n0_tiles, 0), 0))]


def _proj_kernel(n0_tiles, xp_ref, xs_ref, mod_ref, n1_ref, w_ref, hy_ref, rg_ref, hg_ref, ret_ref):
    x = jnp.where(pl.program_id(0) < n0_tiles, xp_ref[...], xs_ref[...])
    mod = mod_ref[0, 0]
    sh1 = mod[:, 0:D_MODEL]
    sc1 = mod[:, D_MODEL:2 * D_MODEL]
    h = x * lax.rsqrt(jnp.mean(x * x, axis=-1, keepdims=True) + EPS) * n1_ref[...]
    h = (h * (1.0 + sc1) + sh1).astype(BF16)
    c0 = 0
    for ref, width in ((hy_ref, PROJ_HY), (rg_ref, PROJ_RG), (hg_ref, PROJ_HG), (ret_ref, PROJ_RET)):
        ref[...] = _mm(h, w_ref[:, c0:c0 + width])
        c0 += width


def _mod_row_map(tm, n_prompt_tok, dec_seq):
    n_prompt_tiles = n_prompt_tok // tm

    def row(i):
        return jnp.where(i < n_prompt_tiles, 0, 1 + (i * tm - n_prompt_tok) // dec_seq)

    return row


def _in_projection(xp, xs, mod_l, norm1_l, w_in_l, dec_seq):
    n_prompt_tok = xp.shape[0]
    t_all = n_prompt_tok + xs.shape[0]
    tm = TM_PROJ
    n0_tiles = n_prompt_tok // tm
    row = _mod_row_map(tm, n_prompt_tok, dec_seq)
    widths = (PROJ_HY, PROJ_RG, PROJ_HG, PROJ_RET)
    return pl.pallas_call(
        functools.partial(_proj_kernel, n0_tiles),
        out_shape=[jax.ShapeDtypeStruct((t_all, w), F32) for w in widths],
        grid=(t_all // tm,),
        in_specs=_pair_specs(tm, D_MODEL, n0_tiles) + [
            pl.BlockSpec((1, 1, 1, N_MOD * D_MODEL), lambda i: (0, row(i), 0, 0)),
            _const_spec((1, D_MODEL)),
            _const_spec((D_MODEL, PROJ_WIDTH)),
        ],
        out_specs=[pl.BlockSpec((tm, w), lambda i: (i, 0)) for w in widths],
        compiler_params=_params(("parallel",)),
        name="in_projection",
    )(xp, xs, mod_l, norm1_l, w_in_l)


def _dft_tables(seq):
    k = np.arange(seq, dtype=np.int64)
    m = (k[:, None] * k[None, :]) % (2 * seq)
    ang = np.pi * m.astype(np.float64) / seq
    return np.cos(ang), np.sin(ang)


def _hyena_tables(seq):
    cos, sin = _dft_tables(seq)
    sign = np.where(np.arange(seq) % 2 == 0, 1.0, -1.0)
    fwd = np.concatenate([cos, sign[None, :], sin[1:]], axis=0)
    wk = np.full((seq,), 2.0)
    wk[0] = 1.0
    inv_cos = (cos * wk[None, :]) / (2.0 * seq)
    inv_nyq = sign[:, None] / (2.0 * seq)
    inv_sin = 2.0 * sin[:, 1:] / (2.0 * seq)
    inv = np.concatenate([inv_cos, inv_nyq, inv_sin], axis=1)

    return jnp.asarray(fwd, F32).astype(BF16), jnp.asarray(inv, F32).astype(BF16)


def _filter_embedding(seq):
    t = np.arange(seq, dtype=np.float64)
    t_norm = t / max(seq - 1, 1)
    bands = np.linspace(1e-4, HY_BANDS - 1, HY_BANDS)
    ang = (2.0 * np.pi / seq) * t[:, None] * bands[None, :]
    z = np.concatenate([t_norm[:, None], np.cos(ang), np.sin(ang)], axis=-1)
    z = np.pad(z, ((0, 0), (0, LANES - HY_EMB)))
    deltas = np.abs(np.linspace(math.log(HY_DECAY_TARGET) / HY_DECAY_LONG,
                                math.log(HY_DECAY_TARGET) / HY_DECAY_SHORT, HY_WIDTH))
    window = np.exp(-t_norm[:, None] * deltas[None, :])
    return jnp.asarray(z, F32), jnp.asarray(window, F32)


def _filter_kernel(z_ref, win_ref, cos_ref, sin_ref, w1_ref, b1_ref, w2_ref, b2_ref, w3_ref, o_ref):
    seq = z_ref.shape[0]
    h = jnp.sin(_mm3(z_ref[...], w1_ref[0]) + b1_ref[0])
    h = jnp.sin(_mm3(h, w2_ref[0]) + b2_ref[0])
    h = _mm3(h, w3_ref[0])
    win = win_ref[...]
    row = _iota((seq, 1), 0)
    sums, diffs = [], []
    for o in range(HY_ORDER):
        c0 = o * 2 * HY_WIDTH
        hf = h[:, c0:c0 + HY_WIDTH] * win
        hb = h[:, c0 + HY_WIDTH:c0 + 2 * HY_WIDTH] * win
        ssq = jnp.sum(hf * hf + hb * hb, axis=0, keepdims=True)
        inv = lax.rsqrt(ssq + EPS)
        hf = hf * inv
        hb = jnp.where(row == 0, 0.0, hb * inv)
        sums.append(hf + hb)
        diffs.append(hf - hb)
    hsum = jnp.concatenate(sums, axis=1)
    hdiff = jnp.concatenate(diffs, axis=1)
    h_re = _mm3(cos_ref[...], hsum)
    h_im = _mm3(sin_ref[...], hdiff)
    sign = jnp.where(row % 2 == 0, 1.0, -1.0)
    h_nyq = jnp.sum(sign * hsum, axis=0, keepdims=True)
    o_ref[0, 0] = h_re
    o_ref[0, 1] = h_im
    o_ref[0, 2] = jnp.where(row == 0, h_nyq, h_re)


def _hyena_filters(seq, w1, b1, w2, b2, w3):
    z, window = _filter_embedding(seq)
    cos, sin = _dft_tables(seq)
    n_out = HY_ORDER * 2 * HY_WIDTH
    w1p = jnp.pad(w1, ((0, 0), (0, LANES - HY_EMB), (0, LANES - HY_FFN)))
    b1p = jnp.pad(b1, ((0, 0), (0, LANES - HY_FFN))).reshape(DEPTH, 1, LANES)
    w2p = jnp.pad(w2, ((0, 0), (0, LANES - HY_FFN), (0, LANES - HY_FFN)))
    b2p = jnp.pad(b2, ((0, 0), (0, LANES - HY_FFN))).reshape(DEPTH, 1, LANES)
    w3p = jnp.pad(w3, ((0, 0), (0, LANES - HY_FFN), (0, 0)))
    per_layer = lambda shape: pl.BlockSpec((1,) + shape, lambda l: (l,) + (0,) * len(shape))
    return pl.pallas_call(
        _filter_kernel,
        out_shape=jax.ShapeDtypeStruct((DEPTH, 3, seq, HY_ORDER * HY_WIDTH), F32),
        grid=(DEPTH,),
        in_specs=[
            _const_spec((seq, LANES)), _const_spec((seq, HY_WIDTH)),
            _const_spec((seq, seq)), _const_spec((seq, seq)),
            per_layer((LANES, LANES)), per_layer((1, LANES)),
            per_layer((LANES, LANES)), per_layer((1, LANES)),
            per_layer((LANES, n_out)),
        ],
        out_specs=pl.BlockSpec((1, 3, seq, HY_ORDER * HY_WIDTH), lambda l: (l, 0, 0, 0)),
        compiler_params=_params(("parallel",)),
        name=f"hyena_filters_{seq}",
    )(z, window, jnp.asarray(cos, F32), jnp.asarray(sin, F32), w1p, b1p, w2p, b2p, w3p)


def _hyena_kernel(u_ref, cw_ref, cb_ref, d_ref, filt_ref, fwd_ref, inv_ref, y_ref):
    seq = u_ref.shape[0]
    u = u_ref[...]
    row = _iota((seq, 1), 0)
    cw = cw_ref[...]
    uc = cb_ref[...] + cw[0:1] * _shift_rows(u, -1, row) + cw[1:2] * u + cw[2:3] * _shift_rows(u, 1, row)
    v = uc[:, 0:HY_WIDTH]
    gates = (uc[:, HY_WIDTH:2 * HY_WIDTH], uc[:, 2 * HY_WIDTH:3 * HY_WIDTH])
    fwd, inv = fwd_ref[...], inv_ref[...]
    d = d_ref[...]
    z = v
    for o in range(HY_ORDER):
        c0 = o * HY_WIDTH
        spec = _mm(fwd, z.astype(BF16))
        s_re, s_im = spec[:seq], spec[seq:]
        a = filt_ref[0, :, c0:c0 + HY_WIDTH]
        b = filt_ref[1, :, c0:c0 + HY_WIDTH]
        c = filt_ref[2, :, c0:c0 + HY_WIDTH]
        prod = jnp.concatenate([s_re * a - s_im * b, s_re * b + s_im * c], axis=0)
        conv = _mm(inv, prod.astype(BF16))
        z = gates[o] * (conv + d[o:o + 1] * z)
    y_ref[...] = z


def _mixer_call(body, name, proj_all, tok0, batch, seq, consts, batch_ins, y_prev, state_shapes, scratch):
    t_all, proj_width = proj_all.shape
    blk0 = tok0 // seq
    n_in = 1 + len(consts) + len(batch_ins)
    has_prev = y_prev is not None

    def kern(*refs):
        body(*refs[:n_in], *refs[n_in + (1 if has_prev else 0):])

    def batch_spec(shape):
        nd = len(shape)
        return pl.BlockSpec((1,) + tuple(shape[1:]), lambda b: (b,) + (0,) * (nd - 1))

    in_specs = [pl.BlockSpec((seq, proj_width), lambda b: (blk0 + b, 0))]
    in_specs += [_const_spec(a.shape) for a in consts]
    in_specs += [batch_spec(a.shape) for a in batch_ins]
    args = [proj_all, *consts, *batch_ins]
    if has_prev:
        in_specs.append(pl.BlockSpec(memory_space=pl.ANY))
        args.append(y_prev)
    out_shape = [jax.ShapeDtypeStruct((t_all, MIX_WIDTH), F32)]
    out_specs = [pl.BlockSpec((seq, MIX_WIDTH), lambda b: (blk0 + b, 0))]
    for shape in state_shapes:
        out_shape.append(jax.ShapeDtypeStruct(shape, F32))
        out_specs.append(batch_spec(shape))
    return pl.pallas_call(
        kern, out_shape=out_shape, grid=(batch,), in_specs=in_specs, out_specs=out_specs,
        scratch_shapes=scratch,
        input_output_aliases={n_in: 0} if has_prev else {},
        compiler_params=_params(("parallel",)),
        name=f"{name}_{seq}",
    )(*args)


def _hyena(u_all, tok0, batch, seq, y_prev, conv_w, conv_b, d_bias, filt_l, tables):
    consts = [conv_w, conv_b, d_bias, filt_l, *tables]
    return _mixer_call(_hyena_kernel, "hyena", u_all, tok0, batch, seq, consts, [], y_prev, [], [])[0]


def _rglru_kernel(has_s0, emit_state, rg_ref, cw_ref, cb_ref, wg_ref, bg_ref, sp_ref, *refs):
    h0_ref = refs[0] if has_s0 else None
    y_ref = refs[1 if has_s0 else 0]
    st_ref = refs[-1] if emit_state else None
    seq = rg_ref.shape[0]
    w = RG_WIDTH
    xr = rg_ref[:, 0:w]
    gate = rg_ref[:, w:2 * w]
    row = _iota((seq, 1), 0)
    cw = cw_ref[...]
    xc = (cb_ref[...] + cw[0:1] * _shift_rows(xr, -2, row) + cw[1:2] * _shift_rows(xr, -1, row)
          + cw[2:3] * xr + cw[3:4] * _shift_rows(xr, 1, row))
    g = _sigmoid(_mm3(xc, wg_ref[...]) + bg_ref[...])
    sp = sp_ref[...]
    hs = []
    for d in range(2):
        r = g[:, d * w:(d + 1) * w]
        i = g[:, (2 + d) * w:(3 + d) * w]
        log_a = -RG_C * r * sp[d:d + 1]
        a = jnp.exp(log_a)
        b = jnp.sqrt(jnp.tanh(-log_a) * (1.0 + a * a)) * (i * xc)
        step = 1
        while step < seq:
            if d == 0:
                keep = row >= step
                a_s = pltpu.roll(a, step, axis=0)
                b_s = pltpu.roll(b, step, axis=0)
            else:
                keep = row < seq - step
                a_s = pltpu.roll(a, seq - step, axis=0)
                b_s = pltpu.roll(b, seq - step, axis=0)
            b = jnp.where(keep, a * b_s + b, b)
            a = jnp.where(keep, a * a_s, a)
            step *= 2
        hs.append(b + a * h0_ref[0, d:d + 1, :] if has_s0 else b)
    y_ref[...] = (hs[0] + hs[1]) * _gelu_tanh(gate)
    if emit_state:
        st_ref[0, 0:1, :] = hs[0][seq - 1:seq]
        st_ref[0, 1:2, :] = hs[1][0:1]


def _rglru(rg_all, tok0, batch, seq, y_prev, conv_w, conv_b, w_gates, b_gates, softplus_neg_lam, h0, emit_state):
    has_s0 = h0 is not None
    consts = [conv_w, conv_b, w_gates, b_gates, softplus_neg_lam]
    return _mixer_call(functools.partial(_rglru_kernel, has_s0, emit_state), "rglru", rg_all, tok0, batch, seq,
                       consts, [h0] if has_s0 else [], y_prev,
                       [(batch, 2, RG_WIDTH)] if emit_state else [], [])


def _hgrn_kernel(has_s0, emit_state, hg_ref, lb_ref, gain_ref, *refs):
    s0_ref = refs[0] if has_s0 else None
    y_ref = refs[1 if has_s0 else 0]
    st_ref = refs[-11] if emit_state else None
    q_s, v_s, kf_s, lf_s, kb_s, lb_s, of_s, ob_s, stf_s, stb_s = refs[-10:]
    seq = hg_ref.shape[0]
    w = HG_WIDTH
    c = HG_CHUNK
    n_chunks = seq // c
    q_s[...] = _silu(hg_ref[:, 0:w])
    v_s[...] = hg_ref[:, 3 * w:4 * w]
    lbv = lb_ref[...]
    for d, (k_s, l_s) in enumerate(((kf_s, lf_s), (kb_s, lb_s))):
        f_pre = hg_ref[:, (1 + d) * w:(2 + d) * w]
        lo = lbv[d:d + 1]
        a1 = jnp.log(lo)
        a2 = jnp.log(1.0 - lo) + _log_sigmoid(f_pre)
        l_s[...] = jnp.maximum(a1, a2) + jnp.log(1.0 + jnp.exp(-jnp.abs(a1 - a2)))
        k_s[...] = (1.0 - lo) * _sigmoid(-f_pre)
    for d, st_s in enumerate((stf_s, stb_s)):
        st_s[...] = _load_state(s0_ref, d) if has_s0 else jnp.zeros_like(st_s)

    tri_shape = (c, c)
    lower = (_iota(tri_shape, 0) >= _iota(tri_shape, 1))
    upper = (_iota(tri_shape, 0) <= _iota(tri_shape, 1))
    stack_mask = _head_mask(c)
    bd_mask = _block_diag_mask()
    pair_shape = (c, N_HEADS * c)
    t_idx = _iota(pair_shape, 0)
    s_idx = _iota(pair_shape, 1) % c
    mid = c // 2

    def one_direction(n, k_s, l_s, st_s, o_s, forward):
        r0 = pl.multiple_of(n * c, c)
        rows = pl.ds(r0, c)
        q = q_s[rows, :]
        v = v_s[rows, :]
        k = k_s[rows, :]
        lf = l_s[rows, :]
        tri = (lower if forward else upper).astype(BF16)
        g = _mm_exact_lhs(tri, lf)
        g_ref = g[mid:mid + 1]
        g_tot = g[c - 1:c] if forward else g[0:1]
        q_in = q * jnp.exp(g)
        q_sc = q * jnp.exp(g - g_ref)
        k_sc = k * jnp.exp(g_ref - g)
        k_out = k * jnp.exp(g_tot - g)
        pair = _mm1(q_sc, _stack_heads(k_sc, stack_mask), _NT)
        causal = (t_idx >= s_idx) if forward else (t_idx <= s_idx)
        pair = jnp.where(causal, pair, 0.0)
        o = _mm1(pair, _stack_heads(v, stack_mask))
        st = st_s[...]
        o = o + _mm1(q_in, st, _NT)
        o_s[rows, :] = o
        upd = _mm1(v.T, k_out)
        st_s[...] = st * jnp.exp(g_tot) + jnp.where(bd_mask, upd, 0.0)

    def body(i, carry):
        one_direction(i, kf_s, lf_s, stf_s, of_s, True)
        one_direction(n_chunks - 1 - i, kb_s, lb_s, stb_s, ob_s, False)
        return carry

    lax.fori_loop(0, n_chunks, body, 0)

    ones_bd = jnp.where(bd_mask, 1.0 / HEAD_DIM, 0.0).astype(BF16)
    o = _head_rmsnorm(of_s[...] + ob_s[...], ones_bd) * gain_ref[...]
    y_ref[...] = o * _silu(hg_ref[:, 4 * w:5 * w])
    if emit_state:
        _store_state(st_ref, 0, stf_s[...])
        _store_state(st_ref, 1, stb_s[...])


def _state_shape(batch):
    return (batch, 2, N_HEADS, HEAD_DIM, HEAD_DIM)


def _hgrn(hg_all, tok0, batch, seq, y_prev, lb_l, gain_l, s0, emit_state):
    has_s0 = s0 is not None
    w = HG_WIDTH
    scratch = [pltpu.VMEM((seq, w), F32)] * 8 + [pltpu.VMEM((w, w), F32)] * 2
    return _mixer_call(functools.partial(_hgrn_kernel, has_s0, emit_state), "hgrn2", hg_all, tok0, batch, seq,
                       [lb_l, gain_l], [s0] if has_s0 else [], y_prev,
                       [_state_shape(batch)] if emit_state else [], scratch)


def _ret_kernel(use_rope, has_s0, emit_state, ret_ref, cos_ref, sin_ref, dl_ref, dlp_ref, *refs):
    s0_ref = refs[0] if has_s0 else None
    y_ref = refs[1 if has_s0 else 0]
    st_ref = refs[-8] if emit_state else None
    q_s, k_s, v_s, oi_s, ob_s, stf_s, stb_s = refs[-7:]
    seq = ret_ref.shape[0]
    w = RET_WIDTH
    c = min(RET_CHUNK, seq)
    n_chunks = seq // c
    q = ret_ref[:, 0:w]
    k = ret_ref[:, w:2 * w] * (HEAD_DIM ** -0.5)
    if use_rope:
        lane = _iota((seq, w), 1)
        even = (lane % 2) == 0
        cos = cos_ref[...]
        sin = sin_ref[...]

        def rope(x):
            nxt = pltpu.roll(x, w - 1, axis=1)
            prv = pltpu.roll(x, 1, axis=1)
            return x * cos + jnp.where(even, nxt, prv) * sin

        q = rope(q)
        k = rope(k)
    q_s[...] = q
    k_s[...] = k
    v_s[...] = ret_ref[:, 2 * w:3 * w]
    for d, st_s in enumerate((stf_s, stb_s)):
        st_s[...] = _load_state(s0_ref, d) if has_s0 else jnp.zeros_like(st_s)

    lg = _log_sigmoid(dl_ref[...])
    lgp = _log_sigmoid(dlp_ref[...])
    pair_shape = (c, N_HEADS * c)
    t_idx = _iota(pair_shape, 0)
    s_idx = _iota(pair_shape, 1) % c
    dist = (t_idx - s_idx).astype(F32)
    decay = jnp.exp(jnp.where(dist >= 0, dist * lgp[0:1], -dist * lgp[1:2]))
    decay = decay + jnp.where(dist == 0, 1.0, 0.0)
    stack_mask = _head_mask(c)
    bd_mask = _block_diag_mask()
    pos = _iota((c, 1), 0).astype(F32)
    fc = float(c)

    def body(i, carry):
        rows = pl.ds(pl.multiple_of(i * c, c), c)
        qc, kc, vc = q_s[rows, :], k_s[rows, :], v_s[rows, :]
        pair = _mm1(qc, _stack_heads(kc, stack_mask), _NT) * decay
        o = _mm1(pair, _stack_heads(vc, stack_mask))
        st = stf_s[...]
        o = o + _mm1(qc * jnp.exp((pos + 1.0) * lg[0:1]), st, _NT)
        oi_s[rows, :] = o
        upd = _mm1(vc.T, kc * jnp.exp((fc - 1.0 - pos) * lg[0:1]))
        stf_s[...] = st * jnp.exp(fc * lg[0:1]) + jnp.where(bd_mask, upd, 0.0)

        rows_b = pl.ds(pl.multiple_of((n_chunks - 1 - i) * c, c), c)
        qc, kc, vc = q_s[rows_b, :], k_s[rows_b, :], v_s[rows_b, :]
        st = stb_s[...]
        ob_s[rows_b, :] = _mm1(qc * jnp.exp((fc - pos) * lg[1:2]), st, _NT)
        upd = _mm1(vc.T, kc * jnp.exp(pos * lg[1:2]))
        stb_s[...] = st * jnp.exp(fc * lg[1:2]) + jnp.where(bd_mask, upd, 0.0)
        return carry

    lax.fori_loop(0, n_chunks, body, 0)

    ones_bd = jnp.where(bd_mask, 1.0 / HEAD_DIM, 0.0).astype(BF16)
    o = _head_rmsnorm(oi_s[...] + ob_s[...], ones_bd)
    y_ref[...] = _silu(ret_ref[:, 3 * w:4 * w]) * o
    if emit_state:
        _store_state(st_ref, 0, stf_s[...])
        _store_state(st_ref, 1, stb_s[...])


def _rope_tables(seq):
    rows = seq // GRID_W
    row = np.repeat(np.arange(rows), GRID_W).astype(np.float64)
    col = (np.arange(seq) % GRID_W).astype(np.float64)
    n_freq = HEAD_DIM // 4
    inv_freq = ROPE_BASE ** (-np.arange(n_freq, dtype=np.float64) / n_freq)
    ang = np.concatenate([row[:, None] * inv_freq, col[:, None] * inv_freq], axis=-1)
    ang = np.repeat(ang, 2, axis=1)
    cos = np.tile(np.cos(ang), (1, N_HEADS))
    sin = np.tile(np.sin(ang) * np.where(np.arange(HEAD_DIM) % 2 == 0, -1.0, 1.0)[None, :], (1, N_HEADS))
    return jnp.asarray(cos, F32), jnp.asarray(sin, F32)


def _retention(ret_all, tok0, batch, seq, y_prev, use_rope, decay_l, s0, emit_state):
    has_s0 = s0 is not None
    w = RET_WIDTH
    c = min(RET_CHUNK, seq)
    cos, sin = _rope_tables(seq)
    dl = jnp.repeat(decay_l, HEAD_DIM, axis=-1)
    dlp = jnp.repeat(decay_l, c, axis=-1)
    scratch = [pltpu.VMEM((seq, w), F32)] * 5 + [pltpu.VMEM((w, w), F32)] * 2
    return _mixer_call(functools.partial(_ret_kernel, use_rope, has_s0, emit_state), "retention", ret_all, tok0,
                       batch, seq, [cos, sin, dl, dlp], [s0] if has_s0 else [], y_prev,
                       [_state_shape(batch)] if emit_state else [], scratch)


def _route(h2, wr_ref, rb_ref):
    logits = _mm3(h2, wr_ref[...])
    tm = logits.shape[0]
    m = jnp.max(logits, axis=-1, keepdims=True)
    e = jnp.exp(logits - m)
    probs = e / jnp.sum(e, axis=-1, keepdims=True)
    sel = probs + rb_ref[...]
    lane = _iota((tm, N_EXPERTS), 1).astype(F32)
    group = (_iota((tm, N_EXPERTS), 1) // E_PER_GROUP).astype(F32)
    neg = -jnp.inf

    def first_argmax(vals):
        mx = jnp.max(vals, axis=-1, keepdims=True)
        idx = jnp.min(jnp.where(vals == mx, lane, float(N_EXPERTS)), axis=-1, keepdims=True)
        return mx, idx

    best_score = None
    best = None
    for g in range(N_GROUPS):
        vals = jnp.where(group == float(g), sel, neg)
        m1, i1 = first_argmax(vals)
        m2, _ = first_argmax(jnp.where(lane == i1, neg, vals))
        score = m1 + m2
        if g == 0:
            best_score, best = score, jnp.zeros_like(score)
        else:
            take = score > best_score
            best = jnp.where(take, float(g), best)
            best_score = jnp.where(take, score, best_score)
    vals = jnp.where(group == best, sel, neg)
    _, i1 = first_argmax(vals)
    _, i2 = first_argmax(jnp.where(lane == i1, neg, vals))
    chosen = (lane == i1) | (lane == i2)
    picked = jnp.where(chosen, probs, 0.0)
    return picked / jnp.sum(picked, axis=-1, keepdims=True), best


INFO_GROUP = 4
INFO_RANK = 5


def _route_kernel(n0_tiles, xp_ref, xs_ref, yhy_ref, yrg_ref, yhg_ref, yret_ref, mod_ref, wo_ref, n2_ref,
                  wr_ref, rb_ref, x1_ref, h2_ref, info_ref, cnt_ref, carry_s):
    i = pl.program_id(0)
    d = D_MODEL
    tb = TB_MOE
    mod = mod_ref[0, 0]

    @pl.when(i == 0)
    def _():
        carry_s[...] = jnp.zeros_like(carry_s)

    mixed = None
    for j, ref in enumerate((yhy_ref, yrg_ref, yhg_ref, yret_ref)):
        part = _mm(ref[...].astype(BF16), wo_ref[j * MIX_WIDTH:(j + 1) * MIX_WIDTH, :])
        mixed = part if mixed is None else mixed + part
    x = jnp.where(i < n0_tiles, xp_ref[...], xs_ref[...])
    x1 = x + mod[:, 2 * d:3 * d] * mixed
    x1_ref[...] = x1
    h2 = x1 * lax.rsqrt(jnp.mean(x1 * x1, axis=-1, keepdims=True) + EPS) * n2_ref[...]
    h2 = h2 * (1.0 + mod[:, 4 * d:5 * d]) + mod[:, 3 * d:4 * d]
    h2_ref[...] = h2.astype(BF16)

    gates, best = _route(h2, wr_ref, rb_ref)
    lane16 = _iota((tb, N_EXPERTS), 1).astype(F32)
    lane = _iota((tb, LANES), 1)
    info = jnp.zeros((tb, LANES), F32)
    for j in range(E_PER_GROUP):
        gj = jnp.sum(jnp.where(lane16 == E_PER_GROUP * best + j, gates, 0.0), axis=-1, keepdims=True)
        info = jnp.where(lane == j, gj, info)
    onehot = jnp.where(lane.astype(F32) == best, 1.0, 0.0)
    lower = (_iota((tb, tb), 0) >= _iota((tb, tb), 1)).astype(BF16)
    incl = _mm(lower, onehot.astype(BF16))
    carry = carry_s[...]
    rank = jnp.sum(onehot * (incl + carry), axis=-1, keepdims=True) - 1.0
    info = jnp.where(lane == INFO_GROUP, best, info)
    info_ref[...] = jnp.where(lane == INFO_RANK, rank, info)
    cnt = incl[tb - 1:tb]
    cnt_ref[0] = cnt
    carry_s[...] = carry + cnt


def _route_call(xp, xs, ys, mod_l, w_out_l, norm2_l, w_router, router_bias, dec_seq):
    n_prompt_tok = xp.shape[0]
    t_all = n_prompt_tok + xs.shape[0]
    tb = TB_MOE
    n0_tiles = n_prompt_tok // tb
    row = _mod_row_map(tb, n_prompt_tok, dec_seq)
    tok = lambda width: pl.BlockSpec((tb, width), lambda i: (i, 0))
    return pl.pallas_call(
        functools.partial(_route_kernel, n0_tiles),
        out_shape=[jax.ShapeDtypeStruct((t_all, D_MODEL), F32), jax.ShapeDtypeStruct((t_all, D_MODEL), BF16),
                   jax.ShapeDtypeStruct((t_all, LANES), F32), jax.ShapeDtypeStruct((t_all // tb, 1, LANES), F32)],
        grid=(t_all // tb,),
        in_specs=_pair_specs(tb, D_MODEL, n0_tiles) + [
            tok(MIX_WIDTH), tok(MIX_WIDTH), tok(MIX_WIDTH), tok(MIX_WIDTH),
            pl.BlockSpec((1, 1, 1, N_MOD * D_MODEL), lambda i: (0, row(i), 0, 0)),
            _const_spec((D_MODEL, D_MODEL)), _const_spec((1, D_MODEL)),
            _const_spec((D_MODEL, N_EXPERTS)), _const_spec((1, N_EXPERTS)),
        ],
        out_specs=[tok(D_MODEL), tok(D_MODEL), tok(LANES), pl.BlockSpec((1, 1, LANES), lambda i: (i, 0, 0))],
        scratch_shapes=[pltpu.VMEM((1, LANES), F32)],
        compiler_params=_params(("arbitrary",)),
        name="route",
    )(xp, xs, *ys, mod_l, w_out_l, norm2_l, w_router, router_bias)


def _moe_plan(cnt, info):
    tb, ts = TB_MOE, TS_MOE
    n_blocks = cnt.shape[0]
    t_all = n_blocks * tb
    n_tiles = t_all // ts + N_GROUPS
    sub_per_tile = ts // tb
    n_sub = n_tiles * sub_per_tile
    cnt = cnt[:, 0, :N_GROUPS].astype(jnp.int32)
    grp = info[:, INFO_GROUP].astype(jnp.int32)
    rank = info[:, INFO_RANK].astype(jnp.int32)
    tot = jnp.sum(cnt, axis=0)
    padded = ((tot + ts - 1) // ts) * ts
    off = jnp.cumsum(padded) - padded
    pos = off[grp] + rank
    cend = jnp.cumsum(cnt, axis=0)
    cum = cend - cnt

    sub_start = jnp.arange(n_sub, dtype=jnp.int32) * tb
    in_g = (sub_start[:, None] >= off[None, :]) & (sub_start[:, None] < (off + padded)[None, :])
    g_of = jnp.argmax(in_g, axis=1).astype(jnp.int32)
    r0 = sub_start - off[g_of]
    r1 = jnp.minimum(r0 + tb, tot[g_of])
    live = jnp.any(in_g, axis=1) & (r1 > r0)
    cend_g = cend[:, g_of]
    blo = jnp.where(live, jnp.sum(cend_g <= r0[None, :], axis=0), 0).astype(jnp.int32)
    bhi = jnp.where(live, jnp.sum(cend_g <= (r1 - 1)[None, :], axis=0), -1).astype(jnp.int32)

    used = jnp.any(in_g, axis=1)[::sub_per_tile]
    tile_group = g_of[::sub_per_tile]
    eidx = E_PER_GROUP * tile_group[:, None] + jnp.arange(E_PER_GROUP, dtype=jnp.int32)[None, :]
    n_used = jnp.sum(used.astype(jnp.int32))
    last = eidx[jnp.maximum(n_used - 1, 0), E_PER_GROUP - 1]
    eidx = jnp.where(used[:, None], eidx, last).reshape(-1).astype(jnp.int32)

    start = off[None, :] + cum
    end = off[None, :] + cend
    has = cnt > 0
    slo = jnp.where(has, start // tb, 0).reshape(-1).astype(jnp.int32)
    shi = jnp.where(has, (end - 1) // tb, -1).reshape(-1).astype(jnp.int32)
    return dict(pos=pos, eidx=eidx, used=used.astype(jnp.int32), blo=blo, bhi=bhi, slo=slo, shi=shi,
                n_tiles=n_tiles)


def _expert_kernel(eidx_ref, used_ref, blo_ref, bhi_ref, h2_ref, pos_ref, info_ref, wg_ref, wu_ref, wd_ref,
                   ys_ref, xs_s, gs_s, acc_s, ax_s, ag_s):
    i = pl.program_id(0)
    j = pl.program_id(1)
    used = used_ref[i] > 0
    tb = TB_MOE
    sub_per_tile = TS_MOE // tb

    @pl.when(used & (j == 0))
    def _():
        r_idx = _iota((tb, tb), 0)
        for half in range(sub_per_tile):
            s = i * sub_per_tile + half
            base = s * tb
            ax_s[...] = jnp.zeros_like(ax_s)
            ag_s[...] = jnp.zeros_like(ag_s)

            def gather(b, carry):
                sel = jnp.where(pos_ref[b] - base == r_idx, 1.0, 0.0).astype(BF16)
                tok = pl.ds(pl.multiple_of(b * tb, tb), tb)
                ax_s[...] += _mm(sel, h2_ref[tok, :])
                ag_s[...] += _mm_exact_lhs(sel, info_ref[tok, :])
                return carry

            lax.fori_loop(blo_ref[s], bhi_ref[s] + 1, gather, 0)
            xs_s[half * tb:(half + 1) * tb, :] = ax_s[...].astype(BF16)
            gs_s[half * tb:(half + 1) * tb, :] = ag_s[...]
        acc_s[...] = jnp.zeros_like(acc_s)

    @pl.when(used)
    def _():
        x = xs_s[...]
        gs = gs_s[...]
        ge = jnp.sum(jnp.where(_iota(gs.shape, 1) == j, gs, 0.0), axis=-1, keepdims=True)
        hh = _silu(_mm(x, wg_ref[0, 0])) * _mm(x, wu_ref[0, 0]) * ge
        acc_s[...] += _mm(hh.astype(BF16), wd_ref[0, 0])

    @pl.when(j == E_PER_GROUP - 1)
    def _():
        ys_ref[...] = jnp.where(used, acc_s[...], 0.0).astype(BF16)


def _expert_call(layer, plan, h2, info, w_gate, w_up, w_down):
    t_all = h2.shape[0]
    tb, ts = TB_MOE, TS_MOE
    n_tiles = plan["n_tiles"]
    pos_rows = plan["pos"].reshape(t_all // tb, 1, tb)
    weight = lambda shape: pl.BlockSpec((1, 1) + shape, lambda i, j, eidx, *_: (layer, eidx[i * E_PER_GROUP + j], 0, 0))
    grid_spec = pltpu.PrefetchScalarGridSpec(
        num_scalar_prefetch=4,
        grid=(n_tiles, E_PER_GROUP),
        in_specs=[
            _const_spec((t_all, D_MODEL)), _const_spec(pos_rows.shape), _const_spec((t_all, LANES)),
            weight((D_MODEL, D_EXPERT)), weight((D_MODEL, D_EXPERT)), weight((D_EXPERT, D_MODEL)),
        ],
        out_specs=pl.BlockSpec((ts, D_MODEL), lambda i, j, *_: (i, 0)),
        scratch_shapes=[pltpu.VMEM((ts, D_MODEL), BF16), pltpu.VMEM((ts, LANES), F32),
                        pltpu.VMEM((ts, D_MODEL), F32), pltpu.VMEM((tb, D_MODEL), F32),
                        pltpu.VMEM((tb, LANES), F32)],
    )
    return pl.pallas_call(
        _expert_kernel,
        out_shape=jax.ShapeDtypeStruct((n_tiles * ts, D_MODEL), BF16),
        grid_spec=grid_spec,
        compiler_params=_params(("arbitrary", "arbitrary")),
        name=f"experts_{layer}",
    )(plan["eidx"], plan["used"], plan["blo"], plan["bhi"], h2, pos_rows, info, w_gate, w_up, w_down)


def _combine_kernel(final, n0_tiles, slo_ref, shi_ref, ys_ref, pos_ref, x1_ref, mod_ref, nf_ref,
                    op_ref, os_ref, acc_s):
    b = pl.program_id(0)
    d = D_MODEL
    tb = TB_MOE
    acc_s[...] = jnp.zeros_like(acc_s)
    c_idx = _iota((tb, tb), 1)
    pos = pos_ref[...]
    for g in range(N_GROUPS):
        def scatter(s, carry):
            sel = jnp.where(pos - s * tb == c_idx, 1.0, 0.0).astype(BF16)
            acc_s[...] += _mm(sel, ys_ref[pl.ds(pl.multiple_of(s * tb, tb), tb), :])
            return carry

        lax.fori_loop(slo_ref[b * N_GROUPS + g], shi_ref[b * N_GROUPS + g] + 1, scatter, 0)

    def result():
        x2 = x1_ref[...] + mod_ref[0, 0][:, 5 * d:6 * d] * acc_s[...]
        if final:
            x2 = x2 * lax.rsqrt(jnp.mean(x2 * x2, axis=-1, keepdims=True) + EPS) * nf_ref[...]
        return x2

    @pl.when(b < n0_tiles)
    def _():
        op_ref[...] = result()

    @pl.when(b >= n0_tiles)
    def _():
        os_ref[...] = result()


def _combine_call(final, plan, ys_sorted, x1, mod_l, norm_final, n_prompt_tok, dec_seq):
    t_all = x1.shape[0]
    tb = TB_MOE
    n0_tiles = n_prompt_tok // tb
    row = _mod_row_map(tb, n_prompt_tok, dec_seq)
    grid_spec = pltpu.PrefetchScalarGridSpec(
        num_scalar_prefetch=2,
        grid=(t_all // tb,),
        in_specs=[
            _const_spec(ys_sorted.shape),
            pl.BlockSpec((tb, 1), lambda b, *_: (b, 0)),
            pl.BlockSpec((tb, D_MODEL), lambda b, *_: (b, 0)),
            pl.BlockSpec((1, 1, 1, N_MOD * D_MODEL), lambda b, *_: (0, row(b), 0, 0)),
            _const_spec((1, D_MODEL)),
        ],
        out_specs=_pair_specs(tb, D_MODEL, n0_tiles),
        scratch_shapes=[pltpu.VMEM((tb, D_MODEL), F32)],
    )
    return pl.pallas_call(
        functools.partial(_combine_kernel, final, n0_tiles),
        out_shape=[jax.ShapeDtypeStruct((n_prompt_tok, D_MODEL), F32),
                   jax.ShapeDtypeStruct((t_all - n_prompt_tok, D_MODEL), F32)],
        grid_spec=grid_spec,
        compiler_params=_params(("arbitrary",)),
        name="combine",
    )(plan["slo"], plan["shi"], ys_sorted, plan["pos"].reshape(t_all, 1), x1, mod_l, norm_final)


def _rg_gate_weights(wa, ba, wx, bx):
    eye = jnp.eye(RG_HEADS, dtype=wa.dtype)

    def dense(wd):
        return jnp.einsum("hij,hg->higj", wd, eye).reshape(RG_WIDTH, RG_WIDTH)

    w = jnp.concatenate([dense(wa[0]), dense(wa[1]), dense(wx[0]), dense(wx[1])], axis=1)
    b = jnp.concatenate([ba[0], ba[1], bx[0], bx[1]], axis=0).reshape(1, 4 * RG_WIDTH)
    return w, b


def kernel(x_prompt, x_sample, state_rglru, state_hgrn, state_ret, c, c_ctx, norm1, norm2, norm_final, w_ada, b_ada, w_in, w_out, hy_conv_w, hy_conv_b, hy_w1, hy_b1, hy_w2, hy_b2, hy_w3, hy_d, rg_conv_w, rg_conv_b, rg_wa, rg_ba, rg_wx, rg_bx, rg_lambda, hg_lb, hg_norm, ret_decay, w_router, router_bias, w_gate, w_up, w_down):
    batch, seq, d = x_prompt.shape
    dec_batch, dec_seq, _ = x_sample.shape
    assert d == D_MODEL and dec_batch + 1 <= COND_ROWS
    n_prompt_tok = batch * seq

    lb_cum = jnp.cumsum(jax.nn.softmax(hg_lb.astype(F32), axis=0), axis=0)
    lb_all = lb_cum - lb_cum[0:1]

    cond = jnp.zeros((COND_ROWS, d), F32).at[0].set(c_ctx).at[1:1 + dec_batch].set(c)
    mod = _modulation(cond, w_ada, b_ada).reshape(DEPTH, COND_ROWS, 1, N_MOD * d)

    w_in_b = w_in.astype(BF16)
    w_out_b = w_out.astype(BF16)
    w_gate_b, w_up_b, w_down_b = w_gate.astype(BF16), w_up.astype(BF16), w_down.astype(BF16)

    passes = (
        dict(tok0=0, batch=batch, seq=seq, rope=False),
        dict(tok0=n_prompt_tok, batch=dec_batch, seq=dec_seq, rope=True),
    )
    filters = {p["seq"]: _hyena_filters(p["seq"], hy_w1, hy_b1, hy_w2, hy_b2, hy_w3) for p in passes}
    tables = {p["seq"]: _hyena_tables(p["seq"]) for p in passes}

    xp = x_prompt.reshape(-1, d)
    xs = x_sample.reshape(-1, d)
    new_rg, new_hg, new_ret = [], [], []
    for l in range(DEPTH):
        hy_all, rg_all, hg_all, ret_all = _in_projection(
            xp, xs, mod[l:l + 1], norm1[l].reshape(1, d), w_in_b[l], dec_seq)
        wg, bg = _rg_gate_weights(rg_wa[l], rg_ba[l], rg_wx[l], rg_bx[l])
        sp = jax.nn.softplus(-rg_lambda[l])
        ys = [None] * 4
        for pi, p in enumerate(passes):
            first = pi == 0
            geom = (p["tok0"], p["batch"], p["seq"])
            ys[0] = _hyena(hy_all, *geom, ys[0], hy_conv_w[l], hy_conv_b[l].reshape(1, -1), hy_d[l],
                           filters[p["seq"]][l], tables[p["seq"]])
            ys[1], *st_rg = _rglru(rg_all, *geom, ys[1], rg_conv_w[l], rg_conv_b[l].reshape(1, -1), wg, bg, sp,
                                   None if first else state_rglru[:, l], first)
            ys[2], *st_hg = _hgrn(hg_all, *geom, ys[2], lb_all[l], hg_norm[l].reshape(1, -1),
                                  None if first else state_hgrn[:, l], first)
            ys[3], *st_ret = _retention(ret_all, *geom, ys[3], p["rope"], ret_decay[l],
                                        None if first else state_ret[:, l], first)
            if first:
                new_rg.append(st_rg[0])
                new_hg.append(st_hg[0])
                new_ret.append(st_ret[0])
        x1, h2, info, cnt = _route_call(xp, xs, ys, mod[l:l + 1], w_out_b[l], norm2[l].reshape(1, d),
                                        w_router, router_bias.reshape(1, -1), dec_seq)
        plan = _moe_plan(cnt, info)
        ys_sorted = _expert_call(l, plan, h2, info, w_gate_b, w_up_b, w_down_b)
        xp, xs = _combine_call(l == DEPTH - 1, plan, ys_sorted, x1, mod[l:l + 1], norm_final.reshape(1, d),
                               n_prompt_tok, dec_seq)

    return (xp.reshape(batch, seq, d), xs.reshape(dec_batch, dec_seq, d), jnp.stack(new_rg, axis=1),
            jnp.stack(new_hg, axis=1), jnp.stack(new_ret, axis=1))
```

```python
import functools
import math

import numpy as np
import jax
import jax.numpy as jnp
from jax import lax
from jax.experimental import pallas as pl
from jax.experimental.pallas import tpu as pltpu

F32 = jnp.float32
BF16 = jnp.bfloat16

D_MODEL = 1024
DEPTH = 2
GRID_W = 64
HY_WIDTH = 256
RG_WIDTH = 256
HG_WIDTH = 256
RET_WIDTH = 256
MIX_WIDTH = 256
HY_ORDER = 2
HY_EMB = 33
HY_BANDS = 16
HY_FFN = 64
HY_DECAY_TARGET = 1e-2
HY_DECAY_SHORT = 0.3
HY_DECAY_LONG = 1.5
RG_HEADS = 8
RG_HEAD_DIM = 32
RG_C = 8.0
N_HEADS = 4
HEAD_DIM = 64
HG_CHUNK = 64
RET_CHUNK = 256
ROPE_BASE = 10000.0
N_EXPERTS = 16
N_GROUPS = 4
E_PER_GROUP = 4
D_EXPERT = 512
N_MOD = 6
EPS = 1e-6
PROJ_HY = 3 * HY_WIDTH
PROJ_RG = 2 * RG_WIDTH
PROJ_HG = 5 * HG_WIDTH
PROJ_RET = 4 * RET_WIDTH
PROJ_WIDTH = PROJ_HY + PROJ_RG + PROJ_HG + PROJ_RET
COND_ROWS = 16
LANES = 128
VMEM_LIMIT = 56 * 1024 * 1024
TM_PROJ = 512
TB_MOE = 256
TM_ROUTE = 512
TS_MOE = 512
TN_MOD = 1536

_NN = (((1,), (0,)), ((), ()))
_NT = (((1,), (1,)), ((), ()))


def _mm(a, b, dn=_NN):
    return lax.dot_general(a, b, dn, preferred_element_type=F32)


def _split2(x):
    hi = x.astype(BF16)
    lo = (x - hi.astype(F32)).astype(BF16)
    return hi, lo


def _split3(x):
    hi = x.astype(BF16)
    r = x - hi.astype(F32)
    mid = r.astype(BF16)
    lo = (r - mid.astype(F32)).astype(BF16)
    return hi, mid, lo


def _mm1(a, b, dn=_NN):
    return _mm(a.astype(BF16), b.astype(BF16), dn)


def _mm3(a, b, dn=_NN):
    ah, al = _split2(a)
    bh, bl = _split2(b)
    return _mm(ah, bh, dn) + (_mm(ah, bl, dn) + _mm(al, bh, dn))


def _mm_exact_lhs(a_bf16, b):
    b1, b2, b3 = _split3(b)
    return _mm(a_bf16, b1) + (_mm(a_bf16, b2) + _mm(a_bf16, b3))


def _mm_exact_rhs(a, b_bf16):
    a1, a2, a3 = _split3(a)
    return _mm(a1, b_bf16) + (_mm(a2, b_bf16) + _mm(a3, b_bf16))


def _sigmoid(x):
    return 1.0 / (1.0 + jnp.exp(-x))


def _silu(x):
    return x * _sigmoid(x)


def _log_sigmoid(x):
    return jnp.minimum(x, 0.0) - jnp.log(1.0 + jnp.exp(-jnp.abs(x)))


def _gelu_tanh(x):
    return 0.5 * x * (1.0 + jnp.tanh(math.sqrt(2.0 / math.pi) * (x + 0.044715 * (x * x * x))))


def _iota(shape, dim):
    return lax.broadcasted_iota(jnp.int32, shape, dim)


def _shift_rows(u, k, row):
    n = u.shape[0]
    if k == 0:
        return u
    r = pltpu.roll(u, (-k) % n, axis=0)
    if k < 0:
        return jnp.where(row >= -k, r, 0.0)
    return jnp.where(row < n - k, r, 0.0)


def _head_mask(n_rows_per_head):
    shape = (N_HEADS * n_rows_per_head, MIX_WIDTH)
    return (_iota(shape, 0) // n_rows_per_head) == (_iota(shape, 1) // HEAD_DIM)


def _stack_heads(x, mask):
    return jnp.where(mask, jnp.concatenate([x] * N_HEADS, axis=0), 0.0)


def _block_diag_mask():
    shape = (MIX_WIDTH, MIX_WIDTH)
    return (_iota(shape, 0) // HEAD_DIM) == (_iota(shape, 1) // HEAD_DIM)


def _head_rmsnorm(o, ones_bd):
    ms = _mm_exact_rhs(o * o, ones_bd)
    return o * lax.rsqrt(ms + EPS)


def _load_state(s0_ref, d):
    zero = jnp.zeros((HEAD_DIM, HEAD_DIM), F32)
    rows = []
    for h in range(N_HEADS):
        blk = s0_ref[0, d, h].T
        rows.append(jnp.concatenate([blk if g == h else zero for g in range(N_HEADS)], axis=1))
    return jnp.concatenate(rows, axis=0)


def _store_state(st_ref, d, st):
    for h in range(N_HEADS):
        lo, hi = h * HEAD_DIM, (h + 1) * HEAD_DIM
        st_ref[0, d, h] = st[lo:hi, lo:hi].T


def _params(sem, vmem=VMEM_LIMIT):
    return pltpu.CompilerParams(dimension_semantics=sem, vmem_limit_bytes=vmem)


def _const_spec(shape):
    nd = len(shape)
    return pl.BlockSpec(shape, lambda *_: (0,) * nd, pipeline_mode=pl.Buffered(1))


def _mod_kernel(cond_ref, w_ref, b_ref, o_ref):
    o_ref[0] = _mm3(_silu(cond_ref[...]), w_ref[0]) + b_ref[0]


def _modulation(cond, w_ada, b_ada):
    tn = TN_MOD
    n_mod = N_MOD * D_MODEL
    return pl.pallas_call(
        _mod_kernel,
        out_shape=jax.ShapeDtypeStruct((DEPTH, COND_ROWS, n_mod), F32),
        grid=(DEPTH, n_mod // tn),
        in_specs=[
            pl.BlockSpec((COND_ROWS, D_MODEL), lambda l, j: (0, 0)),
            pl.BlockSpec((1, D_MODEL, tn), lambda l, j: (l, 0, j)),
            pl.BlockSpec((1, 1, tn), lambda l, j: (l, 0, j)),
        ],
        out_specs=pl.BlockSpec((1, COND_ROWS, tn), lambda l, j: (l, 0, j)),
        compiler_params=_params(("parallel", "parallel")),
        name="modulation",
    )(cond, w_ada, b_ada.reshape(DEPTH, 1, n_mod))


def _pair_specs(tm, width, n0_tiles):
    return [pl.BlockSpec((tm, width), lambda i, *_: (jnp.minimum(i, n0_tiles - 1), 0)),
            pl.BlockSpec((tm, width), lambda i, *_: (jnp.maximum(i - n0_tiles, 0), 0))]


def _proj_kernel(n0_tiles, xp_ref, xs_ref, mod_ref, n1_ref, w_ref, hy_ref, rg_ref, hg_ref, ret_ref):
    x = jnp.where(pl.program_id(0) < n0_tiles, xp_ref[...], xs_ref[...])
    mod = mod_ref[0, 0]
    sh1 = mod[:, 0:D_MODEL]
    sc1 = mod[:, D_MODEL:2 * D_MODEL]
    h = x * lax.rsqrt(jnp.mean(x * x, axis=-1, keepdims=True) + EPS) * n1_ref[...]
    h = (h * (1.0 + sc1) + sh1).astype(BF16)
    c0 = 0
    for ref, width in ((hy_ref, PROJ_HY), (rg_ref, PROJ_RG), (hg_ref, PROJ_HG), (ret_ref, PROJ_RET)):
        ref[...] = _mm(h, w_ref[:, c0:c0 + width])
        c0 += width


def _mod_row_map(tm, n_prompt_tok, dec_seq):
    n_prompt_tiles = n_prompt_tok // tm

    def row(i):
        return jnp.where(i < n_prompt_tiles, 0, 1 + (i * tm - n_prompt_tok) // dec_seq)

    return row


def _in_projection(xp, xs, mod_l, norm1_l, w_in_l, dec_seq):
    n_prompt_tok = xp.shape[0]
    t_all = n_prompt_tok + xs.shape[0]
    tm = TM_PROJ
    n0_tiles = n_prompt_tok // tm
    row = _mod_row_map(tm, n_prompt_tok, dec_seq)
    widths = (PROJ_HY, PROJ_RG, PROJ_HG, PROJ_RET)
    return pl.pallas_call(
        functools.partial(_proj_kernel, n0_tiles),
        out_shape=[jax.ShapeDtypeStruct((t_all, w), F32) for w in widths],
        grid=(t_all // tm,),
        in_specs=_pair_specs(tm, D_MODEL, n0_tiles) + [
            pl.BlockSpec((1, 1, 1, N_MOD * D_MODEL), lambda i: (0, row(i), 0, 0)),
            _const_spec((1, D_MODEL)),
            _const_spec((D_MODEL, PROJ_WIDTH)),
        ],
        out_specs=[pl.BlockSpec((tm, w), lambda i: (i, 0)) for w in widths],
        compiler_params=_params(("parallel",)),
        name="in_projection",
    )(xp, xs, mod_l, norm1_l, w_in_l)


def _dft_tables(seq):
    k = np.arange(seq, dtype=np.int64)
    m = (k[:, None] * k[None, :]) % (2 * seq)
    ang = np.pi * m.astype(np.float64) / seq
    return np.cos(ang), np.sin(ang)


def _hyena_tables(seq):
    cos, sin = _dft_tables(seq)
    sign = np.where(np.arange(seq) % 2 == 0, 1.0, -1.0)
    fwd = np.concatenate([cos, sign[None, :], sin[1:]], axis=0)
    wk = np.full((seq,), 2.0)
    wk[0] = 1.0
    inv_cos = (cos * wk[None, :]) / (2.0 * seq)
    inv_nyq = sign[:, None] / (2.0 * seq)
    inv_sin = 2.0 * sin[:, 1:] / (2.0 * seq)
    inv = np.concatenate([inv_cos, inv_nyq, inv_sin], axis=1)

    return jnp.asarray(fwd, F32).astype(BF16), jnp.asarray(inv, F32).astype(BF16)


def _filter_embedding(seq):
    t = np.arange(seq, dtype=np.float64)
    t_norm = t / max(seq - 1, 1)
    bands = np.linspace(1e-4, HY_BANDS - 1, HY_BANDS)
    ang = (2.0 * np.pi / seq) * t[:, None] * bands[None, :]
    z = np.concatenate([t_norm[:, None], np.cos(ang), np.sin(ang)], axis=-1)
    z = np.pad(z, ((0, 0), (0, LANES - HY_EMB)))
    deltas = np.abs(np.linspace(math.log(HY_DECAY_TARGET) / HY_DECAY_LONG,
                                math.log(HY_DECAY_TARGET) / HY_DECAY_SHORT, HY_WIDTH))
    window = np.exp(-t_norm[:, None] * deltas[None, :])
    return jnp.asarray(z, F32), jnp.asarray(window, F32)


def _filter_kernel(z_ref, win_ref, cos_ref, sin_ref, w1_ref, b1_ref, w2_ref, b2_ref, w3_ref, o_ref):
    seq = z_ref.shape[0]
    h = jnp.sin(_mm3(z_ref[...], w1_ref[0]) + b1_ref[0])
    h = jnp.sin(_mm3(h, w2_ref[0]) + b2_ref[0])
    h = _mm3(h, w3_ref[0])
    win = win_ref[...]
    row = _iota((seq, 1), 0)
    sums, diffs = [], []
    for o in range(HY_ORDER):
        c0 = o * 2 * HY_WIDTH
        hf = h[:, c0:c0 + HY_WIDTH] * win
        hb = h[:, c0 + HY_WIDTH:c0 + 2 * HY_WIDTH] * win
        ssq = jnp.sum(hf * hf + hb * hb, axis=0, keepdims=True)
        inv = lax.rsqrt(ssq + EPS)
        hf = hf * inv
        hb = jnp.where(row == 0, 0.0, hb * inv)
        sums.append(hf + hb)
        diffs.append(hf - hb)
    hsum = jnp.concatenate(sums, axis=1)
    hdiff = jnp.concatenate(diffs, axis=1)
    h_re = _mm3(cos_ref[...], hsum)
    h_im = _mm3(sin_ref[...], hdiff)
    sign = jnp.where(row % 2 == 0, 1.0, -1.0)
    h_nyq = jnp.sum(sign * hsum, axis=0, keepdims=True)
    o_ref[0, 0] = h_re
    o_ref[0, 1] = h_im
    o_ref[0, 2] = jnp.where(row == 0, h_nyq, h_re)


def _hyena_filters(seq, w1, b1, w2, b2, w3):
    z, window = _filter_embedding(seq)
    cos, sin = _dft_tables(seq)
    n_out = HY_ORDER * 2 * HY_WIDTH
    w1p = jnp.pad(w1, ((0, 0), (0, LANES - HY_EMB), (0, LANES - HY_FFN)))
    b1p = jnp.pad(b1, ((0, 0), (0, LANES - HY_FFN))).reshape(DEPTH, 1, LANES)
    w2p = jnp.pad(w2, ((0, 0), (0, LANES - HY_FFN), (0, LANES - HY_FFN)))
    b2p = jnp.pad(b2, ((0, 0), (0, LANES - HY_FFN))).reshape(DEPTH, 1, LANES)
    w3p = jnp.pad(w3, ((0, 0), (0, LANES - HY_FFN), (0, 0)))
    per_layer = lambda shape: pl.BlockSpec((1,) + shape, lambda l: (l,) + (0,) * len(shape))
    return pl.pallas_call(
        _filter_kernel,
        out_shape=jax.ShapeDtypeStruct((DEPTH, 3, seq, HY_ORDER * HY_WIDTH), F32),
        grid=(DEPTH,),
        in_specs=[
            _const_spec((seq, LANES)), _const_spec((seq, HY_WIDTH)),
            _const_spec((seq, seq)), _const_spec((seq, seq)),
            per_layer((LANES, LANES)), per_layer((1, LANES)),
            per_layer((LANES, LANES)), per_layer((1, LANES)),
            per_layer((LANES, n_out)),
        ],
        out_specs=pl.BlockSpec((1, 3, seq, HY_ORDER * HY_WIDTH), lambda l: (l, 0, 0, 0)),
        compiler_params=_params(("parallel",)),
        name=f"hyena_filters_{seq}",
    )(z, window, jnp.asarray(cos, F32), jnp.asarray(sin, F32), w1p, b1p, w2p, b2p, w3p)


def _hyena_kernel(u_ref, cw_ref, cb_ref, d_ref, filt_ref, fwd_ref, inv_ref, y_ref):
    seq = u_ref.shape[0]
    u = u_ref[...]
    row = _iota((seq, 1), 0)
    cw = cw_ref[...]
    uc = cb_ref[...] + cw[0:1] * _shift_rows(u, -1, row) + cw[1:2] * u + cw[2:3] * _shift_rows(u, 1, row)
    v = uc[:, 0:HY_WIDTH]
    gates = (uc[:, HY_WIDTH:2 * HY_WIDTH], uc[:, 2 * HY_WIDTH:3 * HY_WIDTH])
    fwd, inv = fwd_ref[...], inv_ref[...]
    d = d_ref[...]
    z = v
    for o in range(HY_ORDER):
        c0 = o * HY_WIDTH
        spec = _mm(fwd, z.astype(BF16))
        s_re, s_im = spec[:seq], spec[seq:]
        a = filt_ref[0, :, c0:c0 + HY_WIDTH]
        b = filt_ref[1, :, c0:c0 + HY_WIDTH]
        c = filt_ref[2, :, c0:c0 + HY_WIDTH]
        prod = jnp.concatenate([s_re * a - s_im * b, s_re * b + s_im * c], axis=0)
        conv = _mm(inv, prod.astype(BF16))
        z = gates[o] * (conv + d[o:o + 1] * z)
    y_ref[...] = z


def _mixer_call(body, name, proj_all, tok0, batch, seq, consts, batch_ins, state_shapes, scratch):
    proj_width = proj_all.shape[1]
    blk0 = tok0 // seq

    def batch_spec(shape):
        nd = len(shape)
        return pl.BlockSpec((1,) + tuple(shape[1:]), lambda b: (b,) + (0,) * (nd - 1))

    in_specs = [pl.BlockSpec((seq, proj_width), lambda b: (blk0 + b, 0))]
    in_specs += [_const_spec(a.shape) for a in consts]
    in_specs += [batch_spec(a.shape) for a in batch_ins]
    out_shape = [jax.ShapeDtypeStruct((batch * seq, MIX_WIDTH), F32)]
    out_specs = [pl.BlockSpec((seq, MIX_WIDTH), lambda b: (b, 0))]
    for shape in state_shapes:
        out_shape.append(jax.ShapeDtypeStruct(shape, F32))
        out_specs.append(batch_spec(shape))
    return pl.pallas_call(
        body, out_shape=out_shape, grid=(batch,), in_specs=in_specs, out_specs=out_specs,
        scratch_shapes=scratch,
        compiler_params=_params(("parallel",)),
        name=f"{name}_{seq}",
    )(proj_all, *consts, *batch_ins)


def _hyena(u_all, tok0, batch, seq, conv_w, conv_b, d_bias, filt_l, tables):
    consts = [conv_w, conv_b, d_bias, filt_l, *tables]
    return _mixer_call(_hyena_kernel, "hyena", u_all, tok0, batch, seq, consts, [], [], [])[0]


def _rglru_kernel(has_s0, emit_state, rg_ref, cw_ref, cb_ref, wg_ref, bg_ref, sp_ref, *refs):
    h0_ref = refs[0] if has_s0 else None
    y_ref = refs[1 if has_s0 else 0]
    st_ref = refs[-1] if emit_state else None
    seq = rg_ref.shape[0]
    w = RG_WIDTH
    xr = rg_ref[:, 0:w]
    gate = rg_ref[:, w:2 * w]
    row = _iota((seq, 1), 0)
    cw = cw_ref[...]
    xc = (cb_ref[...] + cw[0:1] * _shift_rows(xr, -2, row) + cw[1:2] * _shift_rows(xr, -1, row)
          + cw[2:3] * xr + cw[3:4] * _shift_rows(xr, 1, row))
    g = _sigmoid(_mm3(xc, wg_ref[...]) + bg_ref[...])
    sp = sp_ref[...]
    hs = []
    for d in range(2):
        r = g[:, d * w:(d + 1) * w]
        i = g[:, (2 + d) * w:(3 + d) * w]
        log_a = -RG_C * r * sp[d:d + 1]
        a = jnp.exp(log_a)
        b = jnp.sqrt(jnp.tanh(-log_a) * (1.0 + a * a)) * (i * xc)
        step = 1
        while step < seq:
            if d == 0:
                keep = row >= step
                a_s = pltpu.roll(a, step, axis=0)
                b_s = pltpu.roll(b, step, axis=0)
            else:
                keep = row < seq - step
                a_s = pltpu.roll(a, seq - step, axis=0)
                b_s = pltpu.roll(b, seq - step, axis=0)
            b = jnp.where(keep, a * b_s + b, b)
            a = jnp.where(keep, a * a_s, a)
            step *= 2
        hs.append(b + a * h0_ref[0, d:d + 1, :] if has_s0 else b)
    y_ref[...] = (hs[0] + hs[1]) * _gelu_tanh(gate)
    if emit_state:
        st_ref[0, 0:1, :] = hs[0][seq - 1:seq]
        st_ref[0, 1:2, :] = hs[1][0:1]


def _rglru(rg_all, tok0, batch, seq, conv_w, conv_b, w_gates, b_gates, softplus_neg_lam, h0, emit_state):
    has_s0 = h0 is not None
    consts = [conv_w, conv_b, w_gates, b_gates, softplus_neg_lam]
    return _mixer_call(functools.partial(_rglru_kernel, has_s0, emit_state), "rglru", rg_all, tok0, batch, seq,
                       consts, [h0] if has_s0 else [],
                       [(batch, 2, RG_WIDTH)] if emit_state else [], [])


def _hgrn_kernel(has_s0, emit_state, hg_ref, lb_ref, gain_ref, *refs):
    s0_ref = refs[0] if has_s0 else None
    y_ref = refs[1 if has_s0 else 0]
    st_ref = refs[-11] if emit_state else None
    q_s, v_s, kf_s, lf_s, kb_s, lb_s, of_s, ob_s, stf_s, stb_s = refs[-10:]
    seq = hg_ref.shape[0]
    w = HG_WIDTH
    c = HG_CHUNK
    n_chunks = seq // c
    q_s[...] = _silu(hg_ref[:, 0:w])
    v_s[...] = hg_ref[:, 3 * w:4 * w]
    lbv = lb_ref[...]
    for d, (k_s, l_s) in enumerate(((kf_s, lf_s), (kb_s, lb_s))):
        f_pre = hg_ref[:, (1 + d) * w:(2 + d) * w]
        lo = lbv[d:d + 1]
        a1 = jnp.log(lo)
        a2 = jnp.log(1.0 - lo) + _log_sigmoid(f_pre)
        l_s[...] = jnp.maximum(a1, a2) + jnp.log(1.0 + jnp.exp(-jnp.abs(a1 - a2)))
        k_s[...] = (1.0 - lo) * _sigmoid(-f_pre)
    for d, st_s in enumerate((stf_s, stb_s)):
        st_s[...] = _load_state(s0_ref, d) if has_s0 else jnp.zeros_like(st_s)

    tri_shape = (c, c)
    lower = (_iota(tri_shape, 0) >= _iota(tri_shape, 1))
    upper = (_iota(tri_shape, 0) <= _iota(tri_shape, 1))
    stack_mask = _head_mask(c)
    bd_mask = _block_diag_mask()
    pair_shape = (c, N_HEADS * c)
    t_idx = _iota(pair_shape, 0)
    s_idx = _iota(pair_shape, 1) % c
    mid = c // 2

    def one_direction(n, k_s, l_s, st_s, o_s, forward):
        r0 = pl.multiple_of(n * c, c)
        rows = pl.ds(r0, c)
        q = q_s[rows, :]
        v = v_s[rows, :]
        k = k_s[rows, :]
        lf = l_s[rows, :]
        tri = (lower if forward else upper).astype(BF16)
        g = _mm_exact_lhs(tri, lf)
        g_ref = g[mid:mid + 1]
        g_tot = g[c - 1:c] if forward else g[0:1]
        q_in = q * jnp.exp(g)
        q_sc = q * jnp.exp(g - g_ref)
        k_sc = k * jnp.exp(g_ref - g)
        k_out = k * jnp.exp(g_tot - g)
        pair = _mm1(q_sc, _stack_heads(k_sc, stack_mask), _NT)
        causal = (t_idx >= s_idx) if forward else (t_idx <= s_idx)
        pair = jnp.where(causal, pair, 0.0)
        o = _mm1(pair, _stack_heads(v, stack_mask))
        st = st_s[...]
        o = o + _mm1(q_in, st, _NT)
        o_s[rows, :] = o
        upd = _mm1(v.T, k_out)
        st_s[...] = st * jnp.exp(g_tot) + jnp.where(bd_mask, upd, 0.0)

    def body(i, carry):
        one_direction(i, kf_s, lf_s, stf_s, of_s, True)
        one_direction(n_chunks - 1 - i, kb_s, lb_s, stb_s, ob_s, False)
        return carry

    lax.fori_loop(0, n_chunks, body, 0)

    ones_bd = jnp.where(bd_mask, 1.0 / HEAD_DIM, 0.0).astype(BF16)
    o = _head_rmsnorm(of_s[...] + ob_s[...], ones_bd) * gain_ref[...]
    y_ref[...] = o * _silu(hg_ref[:, 4 * w:5 * w])
    if emit_state:
        _store_state(st_ref, 0, stf_s[...])
        _store_state(st_ref, 1, stb_s[...])


def _state_shape(batch):
    return (batch, 2, N_HEADS, HEAD_DIM, HEAD_DIM)


def _hgrn(hg_all, tok0, batch, seq, lb_l, gain_l, s0, emit_state):
    has_s0 = s0 is not None
    w = HG_WIDTH
    scratch = [pltpu.VMEM((seq, w), F32)] * 8 + [pltpu.VMEM((w, w), F32)] * 2
    return _mixer_call(functools.partial(_hgrn_kernel, has_s0, emit_state), "hgrn2", hg_all, tok0, batch, seq,
                       [lb_l, gain_l], [s0] if has_s0 else [],
                       [_state_shape(batch)] if emit_state else [], scratch)


def _ret_kernel(use_rope, has_s0, emit_state, ret_ref, cos_ref, sin_ref, dl_ref, dlp_ref, *refs):
    s0_ref = refs[0] if has_s0 else None
    y_ref = refs[1 if has_s0 else 0]
    st_ref = refs[-8] if emit_state else None
    q_s, k_s, v_s, oi_s, ob_s, stf_s, stb_s = refs[-7:]
    seq = ret_ref.shape[0]
    w = RET_WIDTH
    c = min(RET_CHUNK, seq)
    n_chunks = seq // c
    q = ret_ref[:, 0:w]
    k = ret_ref[:, w:2 * w] * (HEAD_DIM ** -0.5)
    if use_rope:
        lane = _iota((seq, w), 1)
        even = (lane % 2) == 0
        cos = cos_ref[...]
        sin = sin_ref[...]

        def rope(x):
            nxt = pltpu.roll(x, w - 1, axis=1)
            prv = pltpu.roll(x, 1, axis=1)
            return x * cos + jnp.where(even, nxt, prv) * sin

        q = rope(q)
        k = rope(k)
    q_s[...] = q
    k_s[...] = k
    v_s[...] = ret_ref[:, 2 * w:3 * w]
    for d, st_s in enumerate((stf_s, stb_s)):
        st_s[...] = _load_state(s0_ref, d) if has_s0 else jnp.zeros_like(st_s)

    lg = _log_sigmoid(dl_ref[...])
    lgp = _log_sigmoid(dlp_ref[...])
    pair_shape = (c, N_HEADS * c)
    t_idx = _iota(pair_shape, 0)
    s_idx = _iota(pair_shape, 1) % c
    dist = (t_idx - s_idx).astype(F32)
    decay = jnp.exp(jnp.where(dist >= 0, dist * lgp[0:1], -dist * lgp[1:2]))
    decay = decay + jnp.where(dist == 0, 1.0, 0.0)
    stack_mask = _head_mask(c)
    bd_mask = _block_diag_mask()
    pos = _iota((c, 1), 0).astype(F32)
    fc = float(c)

    def body(i, carry):
        rows = pl.ds(pl.multiple_of(i * c, c), c)
        qc, kc, vc = q_s[rows, :], k_s[rows, :], v_s[rows, :]
        pair = _mm1(qc, _stack_heads(kc, stack_mask), _NT) * decay
        o = _mm1(pair, _stack_heads(vc, stack_mask))
        st = stf_s[...]
        o = o + _mm1(qc * jnp.exp((pos + 1.0) * lg[0:1]), st, _NT)
        oi_s[rows, :] = o
        upd = _mm1(vc.T, kc * jnp.exp((fc - 1.0 - pos) * lg[0:1]))
        stf_s[...] = st * jnp.exp(fc * lg[0:1]) + jnp.where(bd_mask, upd, 0.0)

        rows_b = pl.ds(pl.multiple_of((n_chunks - 1 - i) * c, c), c)
        qc, kc, vc = q_s[rows_b, :], k_s[rows_b, :], v_s[rows_b, :]
        st = stb_s[...]
        ob_s[rows_b, :] = _mm1(qc * jnp.exp((fc - pos) * lg[1:2]), st, _NT)
        upd = _mm1(vc.T, kc * jnp.exp(pos * lg[1:2]))
        stb_s[...] = st * jnp.exp(fc * lg[1:2]) + jnp.where(bd_mask, upd, 0.0)
        return carry

    lax.fori_loop(0, n_chunks, body, 0)

    ones_bd = jnp.where(bd_mask, 1.0 / HEAD_DIM, 0.0).astype(BF16)
    o = _head_rmsnorm(oi_s[...] + ob_s[...], ones_bd)
    y_ref[...] = _silu(ret_ref[:, 3 * w:4 * w]) * o
    if emit_state:
        _store_state(st_ref, 0, stf_s[...])
        _store_state(st_ref, 1, stb_s[...])


def _rope_tables(seq):
    rows = seq // GRID_W
    row = np.repeat(np.arange(rows), GRID_W).astype(np.float64)
    col = (np.arange(seq) % GRID_W).astype(np.float64)
    n_freq = HEAD_DIM // 4
    inv_freq = ROPE_BASE ** (-np.arange(n_freq, dtype=np.float64) / n_freq)
    ang = np.concatenate([row[:, None] * inv_freq, col[:, None] * inv_freq], axis=-1)
    ang = np.repeat(ang, 2, axis=1)
    cos = np.tile(np.cos(ang), (1, N_HEADS))
    sin = np.tile(np.sin(ang) * np.where(np.arange(HEAD_DIM) % 2 == 0, -1.0, 1.0)[None, :], (1, N_HEADS))
    return jnp.asarray(cos, F32), jnp.asarray(sin, F32)


def _retention(ret_all, tok0, batch, seq, use_rope, decay_l, s0, emit_state):
    has_s0 = s0 is not None
    w = RET_WIDTH
    c = min(RET_CHUNK, seq)
    cos, sin = _rope_tables(seq)
    dl = jnp.repeat(decay_l, HEAD_DIM, axis=-1)
    dlp = jnp.repeat(decay_l, c, axis=-1)
    scratch = [pltpu.VMEM((seq, w), F32)] * 5 + [pltpu.VMEM((w, w), F32)] * 2
    return _mixer_call(functools.partial(_ret_kernel, use_rope, has_s0, emit_state), "retention", ret_all, tok0,
                       batch, seq, [cos, sin, dl, dlp], [s0] if has_s0 else [],
                       [_state_shape(batch)] if emit_state else [], scratch)


def _route(h2, wrt_ref, rb_ref):
    logits = _mm3(wrt_ref[...], h2, _NT)
    tm = logits.shape[1]
    m = jnp.max(logits, axis=0, keepdims=True)
    e = jnp.exp(logits - m)
    probs = e / jnp.sum(e, axis=0, keepdims=True)
    sel = probs + rb_ref[...]
    lane = _iota((N_EXPERTS, tm), 0).astype(F32)
    group = (_iota((N_EXPERTS, tm), 0) // E_PER_GROUP).astype(F32)
    neg = -jnp.inf

    def first_argmax(vals):
        mx = jnp.max(vals, axis=0, keepdims=True)
        idx = jnp.min(jnp.where(vals == mx, lane, float(N_EXPERTS)), axis=0, keepdims=True)
        return mx, idx

    best_score = None
    best = None
    for g in range(N_GROUPS):
        vals = jnp.where(group == float(g), sel, neg)
        m1, i1 = first_argmax(vals)
        m2, _ = first_argmax(jnp.where(lane == i1, neg, vals))
        score = m1 + m2
        if g == 0:
            best_score, best = score, jnp.zeros_like(score)
        else:
            take = score > best_score
            best = jnp.where(take, float(g), best)
            best_score = jnp.where(take, score, best_score)
    vals = jnp.where(group == best, sel, neg)
    _, i1 = first_argmax(vals)
    _, i2 = first_argmax(jnp.where(lane == i1, neg, vals))
    chosen = (lane == i1) | (lane == i2)
    picked = jnp.where(chosen, probs, 0.0)
    return picked / jnp.sum(picked, axis=0, keepdims=True), best


INFO_ROWS = 8
INFO_GROUP = 4
INFO_RANK = 5
H2_WIDTH = D_MODEL + LANES


def _route_kernel(n0_tiles, xp_ref, xs_ref, *refs):
    y_refs = refs[:8]
    mod_ref, wo_ref, n2_ref, wrt_ref, rb_ref, x1_ref, h2_ref, info_ref, cend_ref, carry_s = refs[8:]
    i = pl.program_id(0)
    first = i < n0_tiles
    d = D_MODEL
    tb = TB_MOE
    mod = mod_ref[0, 0]

    @pl.when(i == 0)
    def _():
        carry_s[...] = jnp.zeros_like(carry_s)

    tm = TM_ROUTE
    mixed = None
    for j in range(4):
        y = jnp.where(first, y_refs[2 * j][...], y_refs[2 * j + 1][...])
        part = _mm(y.astype(BF16), wo_ref[j * MIX_WIDTH:(j + 1) * MIX_WIDTH, :])
        mixed = part if mixed is None else mixed + part
    x = jnp.where(first, xp_ref[...], xs_ref[...])
    x1 = x + mod[:, 2 * d:3 * d] * mixed
    x1_ref[...] = x1
    h2 = x1 * lax.rsqrt(jnp.mean(x1 * x1, axis=-1, keepdims=True) + EPS) * n2_ref[...]
    h2 = h2 * (1.0 + mod[:, 4 * d:5 * d]) + mod[:, 3 * d:4 * d]
    h2_ref[:, 0:d] = h2.astype(BF16)

    gates, best = _route(h2, wrt_ref, rb_ref)
    expert = _iota((N_EXPERTS, tm), 0).astype(F32)
    g4 = [jnp.sum(jnp.where(expert == E_PER_GROUP * best + j, gates, 0.0), axis=0, keepdims=True)
          for j in range(E_PER_GROUP)]
    onehot = jnp.where(expert == best, 1.0, 0.0)
    upper = (_iota((tm, tm), 0) <= _iota((tm, tm), 1)).astype(BF16)
    carry = carry_s[...]
    incl = _mm(onehot.astype(BF16), upper) + carry[:, 0:1]
    rank = jnp.sum(onehot * incl, axis=0, keepdims=True) - 1.0
    info_ref[...] = jnp.concatenate(g4 + [best, rank, jnp.zeros((INFO_ROWS - 6, tm), F32)], axis=0)
    for h in range(tm // tb):
        end = (h + 1) * tb
        cend_ref[h] = jnp.broadcast_to(incl[:, end - 1:end], (N_EXPERTS, LANES))
    carry_s[...] = jnp.broadcast_to(incl[:, tm - 1:tm], (N_EXPERTS, LANES))

    pieces = _split3(jnp.concatenate(g4 + [jnp.zeros((N_EXPERTS - E_PER_GROUP, tm), F32)], axis=0))
    pieces = jnp.concatenate(list(pieces) + [jnp.zeros((LANES - 3 * N_EXPERTS, tm), BF16)], axis=0)
    eye = (_iota((tm, tm), 0) == _iota((tm, tm), 1)).astype(BF16)
    h2_ref[:, d:d + LANES] = _mm(eye, pieces, _NT).astype(BF16)


def _route_call(xp, xs, ys, mod_l, w_out_l, norm2_l, w_router, router_bias, dec_seq):
    n_prompt_tok = xp.shape[0]
    t_all = n_prompt_tok + xs.shape[0]
    tm = TM_ROUTE
    sub = tm // TB_MOE
    n0_tiles = n_prompt_tok // tm
    row = _mod_row_map(tm, n_prompt_tok, dec_seq)
    tok = lambda width: pl.BlockSpec((tm, width), lambda i: (i, 0))
    y_specs, y_args = [], []
    for yp, ysm in ys:
        y_specs += _pair_specs(tm, MIX_WIDTH, n0_tiles)
        y_args += [yp, ysm]
    return pl.pallas_call(
        functools.partial(_route_kernel, n0_tiles),
        out_shape=[jax.ShapeDtypeStruct((t_all, D_MODEL), F32), jax.ShapeDtypeStruct((t_all, H2_WIDTH), BF16),
                   jax.ShapeDtypeStruct((INFO_ROWS, t_all), F32),
                   jax.ShapeDtypeStruct((t_all // TB_MOE, N_EXPERTS, LANES), F32)],
        grid=(t_all // tm,),
        in_specs=_pair_specs(tm, D_MODEL, n0_tiles) + y_specs + [
            pl.BlockSpec((1, 1, 1, N_MOD * D_MODEL), lambda i: (0, row(i), 0, 0)),
            _const_spec((D_MODEL, D_MODEL)), _const_spec((1, D_MODEL)),
            _const_spec((N_EXPERTS, D_MODEL)), _const_spec((N_EXPERTS, 1)),
        ],
        out_specs=[tok(D_MODEL), tok(H2_WIDTH), pl.BlockSpec((INFO_ROWS, tm), lambda i: (0, i)),
                   pl.BlockSpec((sub, N_EXPERTS, LANES), lambda i: (i, 0, 0))],
        scratch_shapes=[pltpu.VMEM((N_EXPERTS, LANES), F32)],
        compiler_params=_params(("arbitrary",)),
        name="route",
    )(xp, xs, *y_args, mod_l, w_out_l, norm2_l, w_router.T, router_bias.reshape(N_EXPERTS, 1))


def _moe_plan(cend, info):
    tb, ts = TB_MOE, TS_MOE
    n_blocks = cend.shape[0]
    t_all = n_blocks * tb
    n_tiles = t_all // ts + N_GROUPS
    sub_per_tile = ts // tb
    n_sub = n_tiles * sub_per_tile
    cend = cend[:, :N_GROUPS, 0].astype(jnp.int32)
    cum = jnp.concatenate([jnp.zeros((1, N_GROUPS), jnp.int32), cend[:-1]], axis=0)
    cnt = cend - cum
    grp = info[INFO_GROUP].astype(jnp.int32)
    rank = info[INFO_RANK].astype(jnp.int32)
    tot = cend[-1]
    padded = ((tot + ts - 1) // ts) * ts
    off = jnp.cumsum(padded) - padded
    pos = off[grp] + rank

    sub_start = jnp.arange(n_sub, dtype=jnp.int32) * tb
    in_g = (sub_start[:, None] >= off[None, :]) & (sub_start[:, None] < (off + padded)[None, :])
    g_of = jnp.argmax(in_g, axis=1).astype(jnp.int32)
    r0 = sub_start - off[g_of]
    r1 = jnp.minimum(r0 + tb, tot[g_of])
    live = jnp.any(in_g, axis=1) & (r1 > r0)
    cend_g = cend[:, g_of]
    blo = jnp.where(live, jnp.sum(cend_g <= r0[None, :], axis=0), 0).astype(jnp.int32)
    bhi = jnp.where(live, jnp.sum(cend_g <= (r1 - 1)[None, :], axis=0), -1).astype(jnp.int32)

    used = jnp.any(in_g, axis=1)[::sub_per_tile]
    tile_group = g_of[::sub_per_tile]
    eidx = E_PER_GROUP * tile_group[:, None] + jnp.arange(E_PER_GROUP, dtype=jnp.int32)[None, :]
    n_used = jnp.sum(used.astype(jnp.int32))
    last = eidx[jnp.maximum(n_used - 1, 0), E_PER_GROUP - 1]
    eidx = jnp.where(used[:, None], eidx, last).reshape(-1).astype(jnp.int32)

    start = off[None, :] + cum
    end = off[None, :] + cend
    has = cnt > 0
    slo = jnp.where(has, start // tb, 0).reshape(-1).astype(jnp.int32)
    shi = jnp.where(has, (end - 1) // tb, -1).reshape(-1).astype(jnp.int32)
    return dict(pos=pos, eidx=eidx, used=used.astype(jnp.int32), blo=blo, bhi=bhi, slo=slo, shi=shi,
                n_tiles=n_tiles)


def _expert_kernel(eidx_ref, used_ref, blo_ref, bhi_ref, h2_ref, pos_ref, wg_ref, wu_ref, wd_ref,
                   ys_ref, xs_s, acc_s, ax_s):
    i = pl.program_id(0)
    j = pl.program_id(1)
    used = used_ref[i] > 0
    tb = TB_MOE
    sub_per_tile = TS_MOE // tb

    @pl.when(used & (j == 0))
    def _():
        r_idx = _iota((tb, tb), 0)
        for half in range(sub_per_tile):
            s = i * sub_per_tile + half
            base = s * tb
            ax_s[...] = jnp.zeros_like(ax_s)

            def gather(b, carry):
                sel = jnp.where(pos_ref[b] - base == r_idx, 1.0, 0.0).astype(BF16)
                tok = pl.ds(pl.multiple_of(b * tb, tb), tb)
                ax_s[...] += _mm(sel, h2_ref[tok, :])
                return carry

            lax.fori_loop(blo_ref[s], bhi_ref[s] + 1, gather, 0)
            xs_s[half * tb:(half + 1) * tb, :] = ax_s[...].astype(BF16)
        acc_s[...] = jnp.zeros_like(acc_s)

    @pl.when(used)
    def _():
        x = xs_s[:, 0:D_MODEL]
        pieces = xs_s[:, D_MODEL:H2_WIDTH].astype(F32)
        ge = jnp.sum(jnp.where(_iota(pieces.shape, 1) % N_EXPERTS == j, pieces, 0.0), axis=-1, keepdims=True)
        hh = _silu(_mm(x, wg_ref[0, 0].astype(BF16))) * _mm(x, wu_ref[0, 0].astype(BF16)) * ge
        acc_s[...] += _mm(hh.astype(BF16), wd_ref[0, 0].astype(BF16))

    @pl.when(j == E_PER_GROUP - 1)
    def _():
        ys_ref[...] = jnp.where(used, acc_s[...], 0.0).astype(BF16)


def _expert_call(layer, plan, h2, w_gate, w_up, w_down):
    t_all = h2.shape[0]
    tb, ts = TB_MOE, TS_MOE
    n_tiles = plan["n_tiles"]
    pos_rows = plan["pos"].reshape(t_all // tb, 1, tb)
    weight = lambda shape: pl.BlockSpec((1, 1) + shape, lambda i, j, eidx, *_: (layer, eidx[i * E_PER_GROUP + j], 0, 0))
    grid_spec = pltpu.PrefetchScalarGridSpec(
        num_scalar_prefetch=4,
        grid=(n_tiles, E_PER_GROUP),
        in_specs=[
            _const_spec((t_all, H2_WIDTH)), _const_spec(pos_rows.shape),
            weight((D_MODEL, D_EXPERT)), weight((D_MODEL, D_EXPERT)), weight((D_EXPERT, D_MODEL)),
        ],
        out_specs=pl.BlockSpec((ts, D_MODEL), lambda i, j, *_: (i, 0)),
        scratch_shapes=[pltpu.VMEM((ts, H2_WIDTH), BF16), pltpu.VMEM((ts, D_MODEL), F32),
                        pltpu.VMEM((tb, H2_WIDTH), F32)],
    )
    return pl.pallas_call(
        _expert_kernel,
        out_shape=jax.ShapeDtypeStruct((n_tiles * ts, D_MODEL), BF16),
        grid_spec=grid_spec,
        compiler_params=_params(("arbitrary", "arbitrary")),
        name=f"experts_{layer}",
    )(plan["eidx"], plan["used"], plan["blo"], plan["bhi"], h2, pos_rows, w_gate, w_up, w_down)


def _combine_kernel(final, n0_tiles, slo_ref, shi_ref, ys_ref, pos_ref, x1_ref, mod_ref, nf_ref,
                    op_ref, os_ref, acc_s):
    b = pl.program_id(0)
    d = D_MODEL
    tb = TB_MOE
    acc_s[...] = jnp.zeros_like(acc_s)
    c_idx = _iota((tb, tb), 1)
    pos = pos_ref[...]
    for g in range(N_GROUPS):
        def scatter(s, carry):
            sel = jnp.where(pos - s * tb == c_idx, 1.0, 0.0).astype(BF16)
            acc_s[...] += _mm(sel, ys_ref[pl.ds(pl.multiple_of(s * tb, tb), tb), :])
            return carry

        lax.fori_loop(slo_ref[b * N_GROUPS + g], shi_ref[b * N_GROUPS + g] + 1, scatter, 0)

    def result():
        x2 = x1_ref[...] + mod_ref[0, 0][:, 5 * d:6 * d] * acc_s[...]
        if final:
            x2 = x2 * lax.rsqrt(jnp.mean(x2 * x2, axis=-1, keepdims=True) + EPS) * nf_ref[...]
        return x2

    @pl.when(b < n0_tiles)
    def _():
        op_ref[...] = result()

    @pl.when(b >= n0_tiles)
    def _():
        os_ref[...] = result()


def _combine_call(final, plan, ys_sorted, x1, mod_l, norm_final, n_prompt_tok, dec_seq):
    t_all = x1.shape[0]
    tb = TB_MOE
    n0_tiles = n_prompt_tok // tb
    row = _mod_row_map(tb, n_prompt_tok, dec_seq)
    grid_spec = pltpu.PrefetchScalarGridSpec(
        num_scalar_prefetch=2,
        grid=(t_all // tb,),
        in_specs=[
            _const_spec(ys_sorted.shape),
            pl.BlockSpec((tb, 1), lambda b, *_: (b, 0)),
            pl.BlockSpec((tb, D_MODEL), lambda b, *_: (b, 0)),
            pl.BlockSpec((1, 1, 1, N_MOD * D_MODEL), lambda b, *_: (0, row(b), 0, 0)),
            _const_spec((1, D_MODEL)),
        ],
        out_specs=_pair_specs(tb, D_MODEL, n0_tiles),
        scratch_shapes=[pltpu.VMEM((tb, D_MODEL), F32)],
    )
    return pl.pallas_call(
        functools.partial(_combine_kernel, final, n0_tiles),
        out_shape=[jax.ShapeDtypeStruct((n_prompt_tok, D_MODEL), F32),
                   jax.ShapeDtypeStruct((t_all - n_prompt_tok, D_MODEL), F32)],
        grid_spec=grid_spec,
        compiler_params=_params(("arbitrary",)),
        name="combine",
    )(plan["slo"], plan["shi"], ys_sorted, plan["pos"].reshape(t_all, 1), x1, mod_l, norm_final)


def _rg_gate_weights(wa, ba, wx, bx):
    eye = jnp.eye(RG_HEADS, dtype=wa.dtype)

    def dense(wd):
        return jnp.einsum("hij,hg->higj", wd, eye).reshape(RG_WIDTH, RG_WIDTH)

    w = jnp.concatenate([dense(wa[0]), dense(wa[1]), dense(wx[0]), dense(wx[1])], axis=1)
    b = jnp.concatenate([ba[0], ba[1], bx[0], bx[1]], axis=0).reshape(1, 4 * RG_WIDTH)
    return w, b


def kernel(x_prompt, x_sample, state_rglru, state_hgrn, state_ret, c, c_ctx, norm1, norm2, norm_final, w_ada, b_ada, w_in, w_out, hy_conv_w, hy_conv_b, hy_w1, hy_b1, hy_w2, hy_b2, hy_w3, hy_d, rg_conv_w, rg_conv_b, rg_wa, rg_ba, rg_wx, rg_bx, rg_lambda, hg_lb, hg_norm, ret_decay, w_router, router_bias, w_gate, w_up, w_down):
    batch, seq, d = x_prompt.shape
    dec_batch, dec_seq, _ = x_sample.shape
    assert d == D_MODEL and dec_batch + 1 <= COND_ROWS
    n_prompt_tok = batch * seq

    lb_cum = jnp.cumsum(jax.nn.softmax(hg_lb.astype(F32), axis=0), axis=0)
    lb_all = lb_cum - lb_cum[0:1]

    cond = jnp.zeros((COND_ROWS, d), F32).at[0].set(c_ctx).at[1:1 + dec_batch].set(c)
    mod = _modulation(cond, w_ada, b_ada).reshape(DEPTH, COND_ROWS, 1, N_MOD * d)

    w_in_b = w_in.astype(BF16)
    w_out_b = w_out.astype(BF16)

    passes = (
        dict(tok0=0, batch=batch, seq=seq, rope=False),
        dict(tok0=n_prompt_tok, batch=dec_batch, seq=dec_seq, rope=True),
    )
    filters = {p["seq"]: _hyena_filters(p["seq"], hy_w1, hy_b1, hy_w2, hy_b2, hy_w3) for p in passes}
    tables = {p["seq"]: _hyena_tables(p["seq"]) for p in passes}

    xp = x_prompt.reshape(-1, d)
    xs = x_sample.reshape(-1, d)
    new_rg, new_hg, new_ret = [], [], []
    for l in range(DEPTH):
        hy_all, rg_all, hg_all, ret_all = _in_projection(
            xp, xs, mod[l:l + 1], norm1[l].reshape(1, d), w_in_b[l], dec_seq)
        wg, bg = _rg_gate_weights(rg_wa[l], rg_ba[l], rg_wx[l], rg_bx[l])
        sp = jax.nn.softplus(-rg_lambda[l])
        ys = [[], [], [], []]
        for pi, p in enumerate(passes):
            first = pi == 0
            geom = (p["tok0"], p["batch"], p["seq"])
            y_hy = _hyena(hy_all, *geom, hy_conv_w[l], hy_conv_b[l].reshape(1, -1), hy_d[l],
                          filters[p["seq"]][l], tables[p["seq"]])
            y_rg, *st_rg = _rglru(rg_all, *geom, rg_conv_w[l], rg_conv_b[l].reshape(1, -1), wg, bg, sp,
                                  None if first else state_rglru[:, l], first)
            y_hg, *st_hg = _hgrn(hg_all, *geom, lb_all[l], hg_norm[l].reshape(1, -1),
                                 None if first else state_hgrn[:, l], first)
            y_ret, *st_ret = _retention(ret_all, *geom, p["rope"], ret_decay[l],
                                        None if first else state_ret[:, l], first)
            for lst, y in zip(ys, (y_hy, y_rg, y_hg, y_ret)):
                lst.append(y)
            if first:
                new_rg.append(st_rg[0])
                new_hg.append(st_hg[0])
                new_ret.append(st_ret[0])
        x1, h2, info, cend = _route_call(xp, xs, ys, mod[l:l + 1], w_out_b[l], norm2[l].reshape(1, d),
                                         w_router, router_bias, dec_seq)
        plan = _moe_plan(cend, info)
        ys_sorted = _expert_call(l, plan, h2, w_gate, w_up, w_down)
        xp, xs = _combine_call(l == DEPTH - 1, plan, ys_sorted, x1, mod[l:l + 1], norm_final.reshape(1, d),
                               n_prompt_tok, dec_seq)

    return (xp.reshape(batch, seq, d), xs.reshape(dec_batch, dec_seq, d), jnp.stack(new_rg, axis=1),
            jnp.stack(new_hg, axis=1), jnp.stack(new_ret, axis=1))
```

```python
import functools
import math

import numpy as np
import jax
import jax.numpy as jnp
from jax import lax
from jax.experimental import pallas as pl
from jax.experimental.pallas import tpu as pltpu

F32 = jnp.float32
BF16 = jnp.bfloat16

D_MODEL = 1024
DEPTH = 2
GRID_W = 64
HY_WIDTH = 256
RG_WIDTH = 256
HG_WIDTH = 256
RET_WIDTH = 256
MIX_WIDTH = 256
HY_ORDER = 2
HY_EMB = 33
HY_BANDS = 16
HY_FFN = 64
HY_DECAY_TARGET = 1e-2
HY_DECAY_SHORT = 0.3
HY_DECAY_LONG = 1.5
RG_HEADS = 8
RG_HEAD_DIM = 32
RG_C = 8.0
RG_CHUNK = 16
N_HEADS = 4
HEAD_DIM = 64
HG_CHUNK = 64
RET_CHUNK = 256
ROPE_BASE = 10000.0
N_EXPERTS = 16
N_GROUPS = 4
E_PER_GROUP = 4
D_EXPERT = 512
N_MOD = 6
EPS = 1e-6
PROJ_HY = 3 * HY_WIDTH
PROJ_RG = 2 * RG_WIDTH
PROJ_HG = 5 * HG_WIDTH
PROJ_RET = 4 * RET_WIDTH
PROJ_WIDTH = PROJ_HY + PROJ_RG + PROJ_HG + PROJ_RET
COND_ROWS = 16
LANES = 128
VMEM_LIMIT = 56 * 1024 * 1024
TM_PROJ = 512
TB_MOE = 256
TM_ROUTE = 512
TS_MOE = 512
TN_MOD = 1536

_NN = (((1,), (0,)), ((), ()))
_NT = (((1,), (1,)), ((), ()))


def _mm(a, b, dn=_NN):
    return lax.dot_general(a, b, dn, preferred_element_type=F32)


def _split2(x):
    hi = x.astype(BF16)
    lo = (x - hi.astype(F32)).astype(BF16)
    return hi, lo


def _split3(x):
    hi = x.astype(BF16)
    r = x - hi.astype(F32)
    mid = r.astype(BF16)
    lo = (r - mid.astype(F32)).astype(BF16)
    return hi, mid, lo


def _mm1(a, b, dn=_NN):
    return _mm(a.astype(BF16), b.astype(BF16), dn)


def _mm3(a, b, dn=_NN):
    ah, al = _split2(a)
    bh, bl = _split2(b)
    return _mm(ah, bh, dn) + (_mm(ah, bl, dn) + _mm(al, bh, dn))


def _mm_exact_lhs(a_bf16, b):
    b1, b2, b3 = _split3(b)
    return _mm(a_bf16, b1) + (_mm(a_bf16, b2) + _mm(a_bf16, b3))


def _mm_exact_rhs(a, b_bf16):
    a1, a2, a3 = _split3(a)
    return _mm(a1, b_bf16) + (_mm(a2, b_bf16) + _mm(a3, b_bf16))


def _sigmoid(x):
    return 1.0 / (1.0 + jnp.exp(-x))


def _silu(x):
    return x * _sigmoid(x)


def _log_sigmoid(x):
    return jnp.minimum(x, 0.0) - jnp.log(1.0 + jnp.exp(-jnp.abs(x)))


def _gelu_tanh(x):
    return 0.5 * x * (1.0 + jnp.tanh(math.sqrt(2.0 / math.pi) * (x + 0.044715 * (x * x * x))))


def _iota(shape, dim):
    return lax.broadcasted_iota(jnp.int32, shape, dim)


def _shift_rows(u, k, row):
    n = u.shape[0]
    if k == 0:
        return u
    r = pltpu.roll(u, (-k) % n, axis=0)
    if k < 0:
        return jnp.where(row >= -k, r, 0.0)
    return jnp.where(row < n - k, r, 0.0)


def _head_mask(n_rows_per_head):
    shape = (N_HEADS * n_rows_per_head, MIX_WIDTH)
    return (_iota(shape, 0) // n_rows_per_head) == (_iota(shape, 1) // HEAD_DIM)


def _stack_heads(x, mask):
    return jnp.where(mask, jnp.concatenate([x] * N_HEADS, axis=0), 0.0)


def _block_diag_mask():
    shape = (MIX_WIDTH, MIX_WIDTH)
    return (_iota(shape, 0) // HEAD_DIM) == (_iota(shape, 1) // HEAD_DIM)


def _head_rmsnorm(o, ones_bd):
    ms = _mm_exact_rhs(o * o, ones_bd)
    return o * lax.rsqrt(ms + EPS)


def _load_state(s0_ref, d):
    zero = jnp.zeros((HEAD_DIM, HEAD_DIM), F32)
    rows = []
    for h in range(N_HEADS):
        blk = s0_ref[0, d, h].T
        rows.append(jnp.concatenate([blk if g == h else zero for g in range(N_HEADS)], axis=1))
    return jnp.concatenate(rows, axis=0)


def _store_state(st_ref, d, st):
    for h in range(N_HEADS):
        lo, hi = h * HEAD_DIM, (h + 1) * HEAD_DIM
        st_ref[0, d, h] = st[lo:hi, lo:hi].T


def _params(sem, vmem=VMEM_LIMIT):
    return pltpu.CompilerParams(dimension_semantics=sem, vmem_limit_bytes=vmem)


def _const_spec(shape):
    nd = len(shape)
    return pl.BlockSpec(shape, lambda *_: (0,) * nd, pipeline_mode=pl.Buffered(1))


def _mod_kernel(cond_ref, w_ref, b_ref, o_ref):
    o_ref[0] = _mm3(_silu(cond_ref[...]), w_ref[0]) + b_ref[0]


def _modulation(cond, w_ada, b_ada):
    tn = TN_MOD
    n_mod = N_MOD * D_MODEL
    return pl.pallas_call(
        _mod_kernel,
        out_shape=jax.ShapeDtypeStruct((DEPTH, COND_ROWS, n_mod), F32),
        grid=(DEPTH, n_mod // tn),
        in_specs=[
            pl.BlockSpec((COND_ROWS, D_MODEL), lambda l, j: (0, 0)),
            pl.BlockSpec((1, D_MODEL, tn), lambda l, j: (l, 0, j)),
            pl.BlockSpec((1, 1, tn), lambda l, j: (l, 0, j)),
        ],
        out_specs=pl.BlockSpec((1, COND_ROWS, tn), lambda l, j: (l, 0, j)),
        compiler_params=_params(("parallel", "parallel")),
        name="modulation",
    )(cond, w_ada, b_ada.reshape(DEPTH, 1, n_mod))


def _pair_specs(tm, width, n0_tiles):
    return [pl.BlockSpec((tm, width), lambda i, *_: (jnp.minimum(i, n0_tiles - 1), 0)),
            pl.BlockSpec((tm, width), lambda i, *_: (jnp.maximum(i - n0_tiles, 0), 0))]


def _proj_kernel(n0_tiles, xp_ref, xs_ref, mod_ref, n1_ref, w_ref, hy_ref, rg_ref, hg_ref, ret_ref):
    x = jnp.where(pl.program_id(0) < n0_tiles, xp_ref[...], xs_ref[...])
    mod = mod_ref[0, 0]
    sh1 = mod[:, 0:D_MODEL]
    sc1 = mod[:, D_MODEL:2 * D_MODEL]
    h = x * lax.rsqrt(jnp.mean(x * x, axis=-1, keepdims=True) + EPS) * n1_ref[...]
    h = (h * (1.0 + sc1) + sh1).astype(BF16)
    c0 = 0
    for ref, width in ((hy_ref, PROJ_HY), (rg_ref, PROJ_RG), (hg_ref, PROJ_HG), (ret_ref, PROJ_RET)):
        ref[...] = _mm(h, w_ref[:, c0:c0 + width])
        c0 += width


def _mod_row_map(tm, n_prompt_tok, dec_seq):
    n_prompt_tiles = n_prompt_tok // tm

    def row(i):
        return jnp.where(i < n_prompt_tiles, 0, 1 + (i * tm - n_prompt_tok) // dec_seq)

    return row


def _in_projection(xp, xs, mod_l, norm1_l, w_in_l, dec_seq):
    n_prompt_tok = xp.shape[0]
    t_all = n_prompt_tok + xs.shape[0]
    tm = TM_PROJ
    n0_tiles = n_prompt_tok // tm
    row = _mod_row_map(tm, n_prompt_tok, dec_seq)
    widths = (PROJ_HY, PROJ_RG, PROJ_HG, PROJ_RET)
    return pl.pallas_call(
        functools.partial(_proj_kernel, n0_tiles),
        out_shape=[jax.ShapeDtypeStruct((t_all, w), F32) for w in widths],
        grid=(t_all // tm,),
        in_specs=_pair_specs(tm, D_MODEL, n0_tiles) + [
            pl.BlockSpec((1, 1, 1, N_MOD * D_MODEL), lambda i: (0, row(i), 0, 0)),
            _const_spec((1, D_MODEL)),
            _const_spec((D_MODEL, PROJ_WIDTH)),
        ],
        out_specs=[pl.BlockSpec((tm, w), lambda i: (i, 0)) for w in widths],
        compiler_params=_params(("parallel",)),
        name="in_projection",
    )(xp, xs, mod_l, norm1_l, w_in_l)


def _dft_tables(seq):
    k = np.arange(seq, dtype=np.int64)
    m = (k[:, None] * k[None, :]) % (2 * seq)
    ang = np.pi * m.astype(np.float64) / seq
    return np.cos(ang), np.sin(ang)


def _hyena_tables(seq):
    cos, sin = _dft_tables(seq)
    sign = np.where(np.arange(seq) % 2 == 0, 1.0, -1.0)
    fwd = np.concatenate([cos, sign[None, :], sin[1:]], axis=0)
    wk = np.full((seq,), 2.0)
    wk[0] = 1.0
    inv_cos = (cos * wk[None, :]) / (2.0 * seq)
    inv_nyq = sign[:, None] / (2.0 * seq)
    inv_sin = 2.0 * sin[:, 1:] / (2.0 * seq)
    inv = np.concatenate([inv_cos, inv_nyq, inv_sin], axis=1)

    return jnp.asarray(fwd, F32).astype(BF16), jnp.asarray(inv, F32).astype(BF16)


def _filter_embedding(seq):
    t = np.arange(seq, dtype=np.float64)
    t_norm = t / max(seq - 1, 1)
    bands = np.linspace(1e-4, HY_BANDS - 1, HY_BANDS)
    ang = (2.0 * np.pi / seq) * t[:, None] * bands[None, :]
    z = np.concatenate([t_norm[:, None], np.cos(ang), np.sin(ang)], axis=-1)
    z = np.pad(z, ((0, 0), (0, LANES - HY_EMB)))
    deltas = np.abs(np.linspace(math.log(HY_DECAY_TARGET) / HY_DECAY_LONG,
                                math.log(HY_DECAY_TARGET) / HY_DECAY_SHORT, HY_WIDTH))
    window = np.exp(-t_norm[:, None] * deltas[None, :])
    return jnp.asarray(z, F32), jnp.asarray(window, F32)


def _filter_kernel(z_ref, win_ref, cos_ref, sin_ref, w1_ref, b1_ref, w2_ref, b2_ref, w3_ref, o_ref):
    seq = z_ref.shape[0]
    h = jnp.sin(_mm3(z_ref[...], w1_ref[0]) + b1_ref[0])
    h = jnp.sin(_mm3(h, w2_ref[0]) + b2_ref[0])
    h = _mm3(h, w3_ref[0])
    win = win_ref[...]
    row = _iota((seq, 1), 0)
    sums, diffs = [], []
    for o in range(HY_ORDER):
        c0 = o * 2 * HY_WIDTH
        hf = h[:, c0:c0 + HY_WIDTH] * win
        hb = h[:, c0 + HY_WIDTH:c0 + 2 * HY_WIDTH] * win
        ssq = jnp.sum(hf * hf + hb * hb, axis=0, keepdims=True)
        inv = lax.rsqrt(ssq + EPS)
        hf = hf * inv
        hb = jnp.where(row == 0, 0.0, hb * inv)
        sums.append(hf + hb)
        diffs.append(hf - hb)
    hsum = jnp.concatenate(sums, axis=1)
    hdiff = jnp.concatenate(diffs, axis=1)
    h_re = _mm3(cos_ref[...], hsum)
    h_im = _mm3(sin_ref[...], hdiff)
    sign = jnp.where(row % 2 == 0, 1.0, -1.0)
    h_nyq = jnp.sum(sign * hsum, axis=0, keepdims=True)
    o_ref[0, 0] = h_re
    o_ref[0, 1] = h_im
    o_ref[0, 2] = jnp.where(row == 0, h_nyq, h_re)


def _hyena_filters(seq, w1, b1, w2, b2, w3):
    z, window = _filter_embedding(seq)
    cos, sin = _dft_tables(seq)
    n_out = HY_ORDER * 2 * HY_WIDTH
    w1p = jnp.pad(w1, ((0, 0), (0, LANES - HY_EMB), (0, LANES - HY_FFN)))
    b1p = jnp.pad(b1, ((0, 0), (0, LANES - HY_FFN))).reshape(DEPTH, 1, LANES)
    w2p = jnp.pad(w2, ((0, 0), (0, LANES - HY_FFN), (0, LANES - HY_FFN)))
    b2p = jnp.pad(b2, ((0, 0), (0, LANES - HY_FFN))).reshape(DEPTH, 1, LANES)
    w3p = jnp.pad(w3, ((0, 0), (0, LANES - HY_FFN), (0, 0)))
    per_layer = lambda shape: pl.BlockSpec((1,) + shape, lambda l: (l,) + (0,) * len(shape))
    return pl.pallas_call(
        _filter_kernel,
        out_shape=jax.ShapeDtypeStruct((DEPTH, 3, seq, HY_ORDER * HY_WIDTH), F32),
        grid=(DEPTH,),
        in_specs=[
            _const_spec((seq, LANES)), _const_spec((seq, HY_WIDTH)),
            _const_spec((seq, seq)), _const_spec((seq, seq)),
            per_layer((LANES, LANES)), per_layer((1, LANES)),
            per_layer((LANES, LANES)), per_layer((1, LANES)),
            per_layer((LANES, n_out)),
        ],
        out_specs=pl.BlockSpec((1, 3, seq, HY_ORDER * HY_WIDTH), lambda l: (l, 0, 0, 0)),
        compiler_params=_params(("parallel",)),
        name=f"hyena_filters_{seq}",
    )(z, window, jnp.asarray(cos, F32), jnp.asarray(sin, F32), w1p, b1p, w2p, b2p, w3p)


def _hyena_kernel(u_ref, cw_ref, cb_ref, d_ref, filt_ref, fwd_ref, inv_ref, y_ref):
    seq = u_ref.shape[0]
    u = u_ref[...]
    row = _iota((seq, 1), 0)
    cw = cw_ref[...]
    uc = cb_ref[...] + cw[0:1] * _shift_rows(u, -1, row) + cw[1:2] * u + cw[2:3] * _shift_rows(u, 1, row)
    v = uc[:, 0:HY_WIDTH]
    gates = (uc[:, HY_WIDTH:2 * HY_WIDTH], uc[:, 2 * HY_WIDTH:3 * HY_WIDTH])
    fwd, inv = fwd_ref[...], inv_ref[...]
    d = d_ref[...]
    z = v
    for o in range(HY_ORDER):
        c0 = o * HY_WIDTH
        spec = _mm(fwd, z.astype(BF16))
        s_re, s_im = spec[:seq], spec[seq:]
        a = filt_ref[0, :, c0:c0 + HY_WIDTH]
        b = filt_ref[1, :, c0:c0 + HY_WIDTH]
        c = filt_ref[2, :, c0:c0 + HY_WIDTH]
        prod = jnp.concatenate([s_re * a - s_im * b, s_re * b + s_im * c], axis=0)
        conv = _mm(inv, prod.astype(BF16))
        z = gates[o] * (conv + d[o:o + 1] * z)
    y_ref[...] = z


def _mixer_call(body, name, proj_all, tok0, batch, seq, consts, batch_ins, state_shapes, scratch):
    proj_width = proj_all.shape[1]
    blk0 = tok0 // seq

    def batch_spec(shape):
        nd = len(shape)
        return pl.BlockSpec((1,) + tuple(shape[1:]), lambda b: (b,) + (0,) * (nd - 1))

    in_specs = [pl.BlockSpec((seq, proj_width), lambda b: (blk0 + b, 0))]
    in_specs += [_const_spec(a.shape) for a in consts]
    in_specs += [batch_spec(a.shape) for a in batch_ins]
    out_shape = [jax.ShapeDtypeStruct((batch * seq, MIX_WIDTH), F32)]
    out_specs = [pl.BlockSpec((seq, MIX_WIDTH), lambda b: (b, 0))]
    for shape in state_shapes:
        out_shape.append(jax.ShapeDtypeStruct(shape, F32))
        out_specs.append(batch_spec(shape))
    return pl.pallas_call(
        body, out_shape=out_shape, grid=(batch,), in_specs=in_specs, out_specs=out_specs,
        scratch_shapes=scratch,
        compiler_params=_params(("parallel",)),
        name=f"{name}_{seq}",
    )(proj_all, *consts, *batch_ins)


def _hyena(u_all, tok0, batch, seq, conv_w, conv_b, d_bias, filt_l, tables):
    consts = [conv_w, conv_b, d_bias, filt_l, *tables]
    return _mixer_call(_hyena_kernel, "hyena", u_all, tok0, batch, seq, consts, [], [], [])[0]


def _rglru_kernel(has_s0, emit_state, rg_ref, cw_ref, cb_ref, wg_ref, bg_ref, sp_ref, *refs):
    h0_ref = refs[0] if has_s0 else None
    y_ref = refs[1 if has_s0 else 0]
    st_ref = refs[-1] if emit_state else None
    seq = rg_ref.shape[0]
    w = RG_WIDTH
    xr = rg_ref[:, 0:w]
    gate = rg_ref[:, w:2 * w]
    row = _iota((seq, 1), 0)
    cw = cw_ref[...]
    xc = (cb_ref[...] + cw[0:1] * _shift_rows(xr, -2, row) + cw[1:2] * _shift_rows(xr, -1, row)
          + cw[2:3] * xr + cw[3:4] * _shift_rows(xr, 1, row))
    g = _sigmoid(_mm3(xc, wg_ref[...]) + bg_ref[...])
    sp = sp_ref[...]
    c = RG_CHUNK
    n = seq // c
    pos = row % c
    hs = []
    for d in range(2):
        forward = d == 0
        r = g[:, d * w:(d + 1) * w]
        i = g[:, (2 + d) * w:(3 + d) * w]
        log_a = -RG_C * r * sp[d:d + 1]
        a = jnp.exp(log_a)
        b = jnp.sqrt(jnp.tanh(-log_a) * (1.0 + a * a)) * (i * xc)
        step = 1
        while step < c:
            keep = (pos >= step) if forward else (pos < c - step)
            shift = step if forward else seq - step
            a_s = pltpu.roll(a, shift, axis=0)
            b_s = pltpu.roll(b, shift, axis=0)
            b = jnp.where(keep, a * b_s + b, b)
            a = jnp.where(keep, a * a_s, a)
            step *= 2
        a3 = a.reshape(n, c, w)
        b3 = b.reshape(n, c, w)
        edge = c - 1 if forward else 0
        a_end = a3[:, edge:edge + 1, :]
        b_end = b3[:, edge:edge + 1, :]
        h = h0_ref[0, d:d + 1, :] if has_s0 else jnp.zeros((1, w), F32)
        h_in = [None] * n
        for ci in (range(n) if forward else range(n - 1, -1, -1)):
            h_in[ci] = h
            h = b_end[ci] + a_end[ci] * h
        hs.append((b3 + a3 * jnp.stack(h_in, axis=0)).reshape(seq, w))
    y_ref[...] = (hs[0] + hs[1]) * _gelu_tanh(gate)
    if emit_state:
        st_ref[0, 0:1, :] = hs[0][seq - 1:seq]
        st_ref[0, 1:2, :] = hs[1][0:1]


def _rglru(rg_all, tok0, batch, seq, conv_w, conv_b, w_gates, b_gates, softplus_neg_lam, h0, emit_state):
    has_s0 = h0 is not None
    consts = [conv_w, conv_b, w_gates, b_gates, softplus_neg_lam]
    return _mixer_call(functools.partial(_rglru_kernel, has_s0, emit_state), "rglru", rg_all, tok0, batch, seq,
                       consts, [h0] if has_s0 else [],
                       [(batch, 2, RG_WIDTH)] if emit_state else [], [])


def _hgrn_kernel(has_s0, emit_state, hg_ref, lb_ref, gain_ref, *refs):
    s0_ref = refs[0] if has_s0 else None
    y_ref = refs[1 if has_s0 else 0]
    st_ref = refs[-1] if emit_state else None
    seq = hg_ref.shape[0]
    w = HG_WIDTH
    c = HG_CHUNK
    n = seq // c
    mid = c // 2
    chunks = lambda a: a.reshape(n, c, w)
    pos = _iota((seq, 1), 0) % c
    stack_mask = _head_mask(c)[None]
    bd_mask = _block_diag_mask()
    pair_shape = (c, N_HEADS * c)
    t_idx = _iota(pair_shape, 0)
    s_idx = _iota(pair_shape, 1) % c

    def stack_heads(a3):
        return jnp.where(stack_mask, jnp.concatenate([a3] * N_HEADS, axis=1), 0.0).astype(BF16)

    q3 = chunks(_silu(hg_ref[:, 0:w]))
    v3 = chunks(hg_ref[:, 3 * w:4 * w])
    v_stack = stack_heads(v3)
    v_t = jnp.swapaxes(v3, 1, 2).astype(BF16)
    lbv = lb_ref[...]
    o_sum = None
    finals = []
    for d in range(2):
        forward = d == 0
        f_pre = hg_ref[:, (1 + d) * w:(2 + d) * w]
        lo = lbv[d:d + 1]
        a1 = jnp.log(lo)
        a2 = jnp.log(1.0 - lo) + _log_sigmoid(f_pre)
        g = jnp.maximum(a1, a2) + jnp.log(1.0 + jnp.exp(-jnp.abs(a1 - a2)))
        k3 = chunks((1.0 - lo) * _sigmoid(-f_pre))
        step = 1
        while step < c:
            if forward:
                g = jnp.where(pos >= step, g + pltpu.roll(g, step, axis=0), g)
            else:
                g = jnp.where(pos < c - step, g + pltpu.roll(g, seq - step, axis=0), g)
            step *= 2
        g3 = chunks(g)
        g_ref = g3[:, mid:mid + 1, :]
        g_tot = g3[:, c - 1:c, :] if forward else g3[:, 0:1, :]
        q_in = (q3 * jnp.exp(g3)).astype(BF16)
        q_sc = (q3 * jnp.exp(g3 - g_ref)).astype(BF16)
        k_stack = stack_heads(k3 * jnp.exp(g_ref - g3))
        k_out = (k3 * jnp.exp(g_tot - g3)).astype(BF16)
        pair = jnp.einsum("ntl,nrl->ntr", q_sc, k_stack, preferred_element_type=F32)
        causal = (t_idx >= s_idx) if forward else (t_idx <= s_idx)
        pair = jnp.where(causal[None], pair, 0.0).astype(BF16)
        o_intra = jnp.einsum("ntr,nrv->ntv", pair, v_stack, preferred_element_type=F32)
        upd = jnp.einsum("nvs,nsk->nvk", v_t, k_out, preferred_element_type=F32)
        decay = jnp.exp(g_tot)
        st = _load_state(s0_ref, d) if has_s0 else jnp.zeros((w, w), F32)
        o_inter = [None] * n
        for ci in (range(n) if forward else range(n - 1, -1, -1)):
            o_inter[ci] = _mm(q_in[ci], st.astype(BF16), _NT)
            st = st * decay[ci] + jnp.where(bd_mask, upd[ci], 0.0)
        o_dir = o_intra + jnp.stack(o_inter, axis=0)
        o_sum = o_dir if o_sum is None else o_sum + o_dir
        finals.append(st)

    ones_bd = jnp.where(bd_mask, 1.0 / HEAD_DIM, 0.0).astype(BF16)
    o = _head_rmsnorm(o_sum.reshape(seq, w), ones_bd) * gain_ref[...]
    y_ref[...] = o * _silu(hg_ref[:, 4 * w:5 * w])
    if emit_state:
        _store_state(st_ref, 0, finals[0])
        _store_state(st_ref, 1, finals[1])


def _state_shape(batch):
    return (batch, 2, N_HEADS, HEAD_DIM, HEAD_DIM)


def _hgrn(hg_all, tok0, batch, seq, lb_l, gain_l, s0, emit_state):
    has_s0 = s0 is not None
    return _mixer_call(functools.partial(_hgrn_kernel, has_s0, emit_state), "hgrn2", hg_all, tok0, batch, seq,
                       [lb_l, gain_l], [s0] if has_s0 else [],
                       [_state_shape(batch)] if emit_state else [], [])


def _ret_kernel(use_rope, has_s0, emit_state, ret_ref, cos_ref, sin_ref, dl_ref, dlp_ref, *refs):
    s0_ref = refs[0] if has_s0 else None
    y_ref = refs[1 if has_s0 else 0]
    st_ref = refs[-8] if emit_state else None
    q_s, k_s, v_s, oi_s, ob_s, stf_s, stb_s = refs[-7:]
    seq = ret_ref.shape[0]
    w = RET_WIDTH
    c = min(RET_CHUNK, seq)
    n_chunks = seq // c
    q = ret_ref[:, 0:w]
    k = ret_ref[:, w:2 * w] * (HEAD_DIM ** -0.5)
    if use_rope:
        lane = _iota((seq, w), 1)
        even = (lane % 2) == 0
        cos = cos_ref[...]
        sin = sin_ref[...]

        def rope(x):
            nxt = pltpu.roll(x, w - 1, axis=1)
            prv = pltpu.roll(x, 1, axis=1)
            return x * cos + jnp.where(even, nxt, prv) * sin

        q = rope(q)
        k = rope(k)
    q_s[...] = q
    k_s[...] = k
    v_s[...] = ret_ref[:, 2 * w:3 * w]
    for d, st_s in enumerate((stf_s, stb_s)):
        st_s[...] = _load_state(s0_ref, d) if has_s0 else jnp.zeros_like(st_s)

    lg = _log_sigmoid(dl_ref[...])
    lgp = _log_sigmoid(dlp_ref[...])
    pair_shape = (c, N_HEADS * c)
    t_idx = _iota(pair_shape, 0)
    s_idx = _iota(pair_shape, 1) % c
    dist = (t_idx - s_idx).astype(F32)
    decay = jnp.exp(jnp.where(dist >= 0, dist * lgp[0:1], -dist * lgp[1:2]))
    decay = decay + jnp.where(dist == 0, 1.0, 0.0)
    stack_mask = _head_mask(c)
    bd_mask = _block_diag_mask()
    pos = _iota((c, 1), 0).astype(F32)
    fc = float(c)

    def body(i, carry):
        rows = pl.ds(pl.multiple_of(i * c, c), c)
        qc, kc, vc = q_s[rows, :], k_s[rows, :], v_s[rows, :]
        pair = _mm1(qc, _stack_heads(kc, stack_mask), _NT) * decay
        o = _mm1(pair, _stack_heads(vc, stack_mask))
        st = stf_s[...]
        o = o + _mm1(qc * jnp.exp((pos + 1.0) * lg[0:1]), st, _NT)
        oi_s[rows, :] = o
        upd = _mm1(vc.T, kc * jnp.exp((fc - 1.0 - pos) * lg[0:1]))
        stf_s[...] = st * jnp.exp(fc * lg[0:1]) + jnp.where(bd_mask, upd, 0.0)

        rows_b = pl.ds(pl.multiple_of((n_chunks - 1 - i) * c, c), c)
        qc, kc, vc = q_s[rows_b, :], k_s[rows_b, :], v_s[rows_b, :]
        st = stb_s[...]
        ob_s[rows_b, :] = _mm1(qc * jnp.exp((fc - pos) * lg[1:2]), st, _NT)
        upd = _mm1(vc.T, kc * jnp.exp(pos * lg[1:2]))
        stb_s[...] = st * jnp.exp(fc * lg[1:2]) + jnp.where(bd_mask, upd, 0.0)
        return carry

    lax.fori_loop(0, n_chunks, body, 0)

    ones_bd = jnp.where(bd_mask, 1.0 / HEAD_DIM, 0.0).astype(BF16)
    o = _head_rmsnorm(oi_s[...] + ob_s[...], ones_bd)
    y_ref[...] = _silu(ret_ref[:, 3 * w:4 * w]) * o
    if emit_state:
        _store_state(st_ref, 0, stf_s[...])
        _store_state(st_ref, 1, stb_s[...])


def _rope_tables(seq):
    rows = seq // GRID_W
    row = np.repeat(np.arange(rows), GRID_W).astype(np.float64)
    col = (np.arange(seq) % GRID_W).astype(np.float64)
    n_freq = HEAD_DIM // 4
    inv_freq = ROPE_BASE ** (-np.arange(n_freq, dtype=np.float64) / n_freq)
    ang = np.concatenate([row[:, None] * inv_freq, col[:, None] * inv_freq], axis=-1)
    ang = np.repeat(ang, 2, axis=1)
    cos = np.tile(np.cos(ang), (1, N_HEADS))
    sin = np.tile(np.sin(ang) * np.where(np.arange(HEAD_DIM) % 2 == 0, -1.0, 1.0)[None, :], (1, N_HEADS))
    return jnp.asarray(cos, F32), jnp.asarray(sin, F32)


def _retention(ret_all, tok0, batch, seq, use_rope, decay_l, s0, emit_state):
    has_s0 = s0 is not None
    w = RET_WIDTH
    c = min(RET_CHUNK, seq)
    cos, sin = _rope_tables(seq)
    dl = jnp.repeat(decay_l, HEAD_DIM, axis=-1)
    dlp = jnp.repeat(decay_l, c, axis=-1)
    scratch = [pltpu.VMEM((seq, w), F32)] * 5 + [pltpu.VMEM((w, w), F32)] * 2
    return _mixer_call(functools.partial(_ret_kernel, use_rope, has_s0, emit_state), "retention", ret_all, tok0,
                       batch, seq, [cos, sin, dl, dlp], [s0] if has_s0 else [],
                       [_state_shape(batch)] if emit_state else [], scratch)


def _route(h2, wrt_ref, rb_ref):
    logits = _mm3(wrt_ref[...], h2, _NT)
    tm = logits.shape[1]
    m = jnp.max(logits, axis=0, keepdims=True)
    e = jnp.exp(logits - m)
    probs = e / jnp.sum(e, axis=0, keepdims=True)
    sel = probs + rb_ref[...]
    lane = _iota((N_EXPERTS, tm), 0).astype(F32)
    group = (_iota((N_EXPERTS, tm), 0) // E_PER_GROUP).astype(F32)
    neg = -jnp.inf

    def first_argmax(vals):
        mx = jnp.max(vals, axis=0, keepdims=True)
        idx = jnp.min(jnp.where(vals == mx, lane, float(N_EXPERTS)), axis=0, keepdims=True)
        return mx, idx

    best_score = None
    best = None
    for g in range(N_GROUPS):
        vals = jnp.where(group == float(g), sel, neg)
        m1, i1 = first_argmax(vals)
        m2, _ = first_argmax(jnp.where(lane == i1, neg, vals))
        score = m1 + m2
        if g == 0:
            best_score, best = score, jnp.zeros_like(score)
        else:
            take = score > best_score
            best = jnp.where(take, float(g), best)
            best_score = jnp.where(take, score, best_score)
    vals = jnp.where(group == best, sel, neg)
    _, i1 = first_argmax(vals)
    _, i2 = first_argmax(jnp.where(lane == i1, neg, vals))
    chosen = (lane == i1) | (lane == i2)
    picked = jnp.where(chosen, probs, 0.0)
    return picked / jnp.sum(picked, axis=0, keepdims=True), best


INFO_ROWS = 8
INFO_GROUP = 4
INFO_RANK = 5
H2_WIDTH = D_MODEL + LANES


def _route_kernel(n0_tiles, xp_ref, xs_ref, *refs):
    y_refs = refs[:8]
    mod_ref, wo_ref, n2_ref, wrt_ref, rb_ref, x1_ref, h2_ref, info_ref, cend_ref, carry_s = refs[8:]
    i = pl.program_id(0)
    first = i < n0_tiles
    d = D_MODEL
    tb = TB_MOE
    mod = mod_ref[0, 0]

    @pl.when(i == 0)
    def _():
        carry_s[...] = jnp.zeros_like(carry_s)

    tm = TM_ROUTE
    mixed = None
    for j in range(4):
        y = jnp.where(first, y_refs[2 * j][...], y_refs[2 * j + 1][...])
        part = _mm(y.astype(BF16), wo_ref[j * MIX_WIDTH:(j + 1) * MIX_WIDTH, :])
        mixed = part if mixed is None else mixed + part
    x = jnp.where(first, xp_ref[...], xs_ref[...])
    x1 = x + mod[:, 2 * d:3 * d] * mixed
    x1_ref[...] = x1
    h2 = x1 * lax.rsqrt(jnp.mean(x1 * x1, axis=-1, keepdims=True) + EPS) * n2_ref[...]
    h2 = h2 * (1.0 + mod[:, 4 * d:5 * d]) + mod[:, 3 * d:4 * d]
    h2_ref[:, 0:d] = h2.astype(BF16)

    gates, best = _route(h2, wrt_ref, rb_ref)
    expert = _iota((N_EXPERTS, tm), 0).astype(F32)
    g4 = [jnp.sum(jnp.where(expert == E_PER_GROUP * best + j, gates, 0.0), axis=0, keepdims=True)
          for j in range(E_PER_GROUP)]
    onehot = jnp.where(expert == best, 1.0, 0.0)
    upper = (_iota((tm, tm), 0) <= _iota((tm, tm), 1)).astype(BF16)
    carry = carry_s[...]
    incl = _mm(onehot.astype(BF16), upper) + carry[:, 0:1]
    rank = jnp.sum(onehot * incl, axis=0, keepdims=True) - 1.0
    info_ref[...] = jnp.concatenate(g4 + [best, rank, jnp.zeros((INFO_ROWS - 6, tm), F32)], axis=0)
    for h in range(tm // tb):
        end = (h + 1) * tb
        cend_ref[h] = jnp.broadcast_to(incl[:, end - 1:end], (N_EXPERTS, LANES))
    carry_s[...] = jnp.broadcast_to(incl[:, tm - 1:tm], (N_EXPERTS, LANES))

    pieces = _split3(jnp.concatenate(g4 + [jnp.zeros((N_EXPERTS - E_PER_GROUP, tm), F32)], axis=0))
    pieces = jnp.concatenate(list(pieces) + [jnp.zeros((LANES - 3 * N_EXPERTS, tm), BF16)], axis=0)
    eye = (_iota((tm, tm), 0) == _iota((tm, tm), 1)).astype(BF16)
    h2_ref[:, d:d + LANES] = _mm(eye, pieces, _NT).astype(BF16)


def _route_call(xp, xs, ys, mod_l, w_out_l, norm2_l, w_router, router_bias, dec_seq):
    n_prompt_tok = xp.shape[0]
    t_all = n_prompt_tok + xs.shape[0]
    tm = TM_ROUTE
    sub = tm // TB_MOE
    n0_tiles = n_prompt_tok // tm
    row = _mod_row_map(tm, n_prompt_tok, dec_seq)
    tok = lambda width: pl.BlockSpec((tm, width), lambda i: (i, 0))
    y_specs, y_args = [], []
    for yp, ysm in ys:
        y_specs += _pair_specs(tm, MIX_WIDTH, n0_tiles)
        y_args += [yp, ysm]
    return pl.pallas_call(
        functools.partial(_route_kernel, n0_tiles),
        out_shape=[jax.ShapeDtypeStruct((t_all, D_MODEL), F32), jax.ShapeDtypeStruct((t_all, H2_WIDTH), BF16),
                   jax.ShapeDtypeStruct((INFO_ROWS, t_all), F32),
                   jax.ShapeDtypeStruct((t_all // TB_MOE, N_EXPERTS, LANES), F32)],
        grid=(t_all // tm,),
        in_specs=_pair_specs(tm, D_MODEL, n0_tiles) + y_specs + [
            pl.BlockSpec((1, 1, 1, N_MOD * D_MODEL), lambda i: (0, row(i), 0, 0)),
            _const_spec((D_MODEL, D_MODEL)), _const_spec((1, D_MODEL)),
            _const_spec((N_EXPERTS, D_MODEL)), _const_spec((N_EXPERTS, 1)),
        ],
        out_specs=[tok(D_MODEL), tok(H2_WIDTH), pl.BlockSpec((INFO_ROWS, tm), lambda i: (0, i)),
                   pl.BlockSpec((sub, N_EXPERTS, LANES), lambda i: (i, 0, 0))],
        scratch_shapes=[pltpu.VMEM((N_EXPERTS, LANES), F32)],
        compiler_params=_params(("arbitrary",)),
        name="route",
    )(xp, xs, *y_args, mod_l, w_out_l, norm2_l, w_router.T, router_bias.reshape(N_EXPERTS, 1))


def _moe_plan(cend, info):
    tb, ts = TB_MOE, TS_MOE
    n_blocks = cend.shape[0]
    t_all = n_blocks * tb
    n_tiles = t_all // ts + N_GROUPS
    sub_per_tile = ts // tb
    n_sub = n_tiles * sub_per_tile
    cend = cend[:, :N_GROUPS, 0].astype(jnp.int32)
    cum = jnp.concatenate([jnp.zeros((1, N_GROUPS), jnp.int32), cend[:-1]], axis=0)
    cnt = cend - cum
    grp = info[INFO_GROUP].astype(jnp.int32)
    rank = info[INFO_RANK].astype(jnp.int32)
    tot = cend[-1]
    padded = ((tot + ts - 1) // ts) * ts
    off = jnp.cumsum(padded) - padded
    pos = off[grp] + rank

    sub_start = jnp.arange(n_sub, dtype=jnp.int32) * tb
    in_g = (sub_start[:, None] >= off[None, :]) & (sub_start[:, None] < (off + padded)[None, :])
    g_of = jnp.argmax(in_g, axis=1).astype(jnp.int32)
    r0 = sub_start - off[g_of]
    r1 = jnp.minimum(r0 + tb, tot[g_of])
    live = jnp.any(in_g, axis=1) & (r1 > r0)
    cend_g = cend[:, g_of]
    blo = jnp.where(live, jnp.sum(cend_g <= r0[None, :], axis=0), 0).astype(jnp.int32)
    bhi = jnp.where(live, jnp.sum(cend_g <= (r1 - 1)[None, :], axis=0), -1).astype(jnp.int32)

    used = jnp.any(in_g, axis=1)[::sub_per_tile]
    tile_group = g_of[::sub_per_tile]
    eidx = E_PER_GROUP * tile_group[:, None] + jnp.arange(E_PER_GROUP, dtype=jnp.int32)[None, :]
    n_used = jnp.sum(used.astype(jnp.int32))
    last = eidx[jnp.maximum(n_used - 1, 0), E_PER_GROUP - 1]
    eidx = jnp.where(used[:, None], eidx, last).reshape(-1).astype(jnp.int32)

    start = off[None, :] + cum
    end = off[None, :] + cend
    has = cnt > 0
    slo = jnp.where(has, start // tb, 0).reshape(-1).astype(jnp.int32)
    shi = jnp.where(has, (end - 1) // tb, -1).reshape(-1).astype(jnp.int32)
    return dict(pos=pos, eidx=eidx, used=used.astype(jnp.int32), blo=blo, bhi=bhi, slo=slo, shi=shi,
                n_tiles=n_tiles)


def _expert_kernel(eidx_ref, used_ref, blo_ref, bhi_ref, h2_ref, pos_ref, wg_ref, wu_ref, wd_ref,
                   ys_ref, xs_s, acc_s, ax_s):
    i = pl.program_id(0)
    j = pl.program_id(1)
    used = used_ref[i] > 0
    tb = TB_MOE
    sub_per_tile = TS_MOE // tb

    @pl.when(used & (j == 0))
    def _():
        r_idx = _iota((tb, tb), 0)
        for half in range(sub_per_tile):
            s = i * sub_per_tile + half
            base = s * tb
            ax_s[...] = jnp.zeros_like(ax_s)

            def gather(b, carry):
                sel = jnp.where(pos_ref[b] - base == r_idx, 1.0, 0.0).astype(BF16)
                tok = pl.ds(pl.multiple_of(b * tb, tb), tb)
                ax_s[...] += _mm(sel, h2_ref[tok, :])
                return carry

            lax.fori_loop(blo_ref[s], bhi_ref[s] + 1, gather, 0)
            xs_s[half * tb:(half + 1) * tb, :] = ax_s[...].astype(BF16)
        acc_s[...] = jnp.zeros_like(acc_s)

    @pl.when(used)
    def _():
        x = xs_s[:, 0:D_MODEL]
        pieces = xs_s[:, D_MODEL:H2_WIDTH].astype(F32)
        ge = jnp.sum(jnp.where(_iota(pieces.shape, 1) % N_EXPERTS == j, pieces, 0.0), axis=-1, keepdims=True)
        hh = _silu(_mm(x, wg_ref[0, 0].astype(BF16))) * _mm(x, wu_ref[0, 0].astype(BF16)) * ge
        acc_s[...] += _mm(hh.astype(BF16), wd_ref[0, 0].astype(BF16))

    @pl.when(j == E_PER_GROUP - 1)
    def _():
        ys_ref[...] = jnp.where(used, acc_s[...], 0.0).astype(BF16)


def _expert_call(layer, plan, h2, w_gate, w_up, w_down):
    t_all = h2.shape[0]
    tb, ts = TB_MOE, TS_MOE
    n_tiles = plan["n_tiles"]
    pos_rows = plan["pos"].reshape(t_all // tb, 1, tb)
    weight = lambda shape: pl.BlockSpec((1, 1) + shape, lambda i, j, eidx, *_: (layer, eidx[i * E_PER_GROUP + j], 0, 0))
    grid_spec = pltpu.PrefetchScalarGridSpec(
        num_scalar_prefetch=4,
        grid=(n_tiles, E_PER_GROUP),
        in_specs=[
            _const_spec((t_all, H2_WIDTH)), _const_spec(pos_rows.shape),
            weight((D_MODEL, D_EXPERT)), weight((D_MODEL, D_EXPERT)), weight((D_EXPERT, D_MODEL)),
        ],
        out_specs=pl.BlockSpec((ts, D_MODEL), lambda i, j, *_: (i, 0)),
        scratch_shapes=[pltpu.VMEM((ts, H2_WIDTH), BF16), pltpu.VMEM((ts, D_MODEL), F32),
                        pltpu.VMEM((tb, H2_WIDTH), F32)],
    )
    return pl.pallas_call(
        _expert_kernel,
        out_shape=jax.ShapeDtypeStruct((n_tiles * ts, D_MODEL), BF16),
        grid_spec=grid_spec,
        compiler_params=_params(("arbitrary", "arbitrary")),
        name=f"experts_{layer}",
    )(plan["eidx"], plan["used"], plan["blo"], plan["bhi"], h2, pos_rows, w_gate, w_up, w_down)


def _combine_kernel(final, n0_tiles, slo_ref, shi_ref, ys_ref, pos_ref, x1_ref, mod_ref, nf_ref,
                    op_ref, os_ref, acc_s):
    b = pl.program_id(0)
    d = D_MODEL
    tb = TB_MOE
    acc_s[...] = jnp.zeros_like(acc_s)
    c_idx = _iota((tb, tb), 1)
    pos = pos_ref[...]
    for g in range(N_GROUPS):
        def scatter(s, carry):
            sel = jnp.where(pos - s * tb == c_idx, 1.0, 0.0).astype(BF16)
            acc_s[...] += _mm(sel, ys_ref[pl.ds(pl.multiple_of(s * tb, tb), tb), :])
            return carry

        lax.fori_loop(slo_ref[b * N_GROUPS + g], shi_ref[b * N_GROUPS + g] + 1, scatter, 0)

    def result():
        x2 = x1_ref[...] + mod_ref[0, 0][:, 5 * d:6 * d] * acc_s[...]
        if final:
            x2 = x2 * lax.rsqrt(jnp.mean(x2 * x2, axis=-1, keepdims=True) + EPS) * nf_ref[...]
        return x2

    @pl.when(b < n0_tiles)
    def _():
        op_ref[...] = result()

    @pl.when(b >= n0_tiles)
    def _():
        os_ref[...] = result()


def _combine_call(final, plan, ys_sorted, x1, mod_l, norm_final, n_prompt_tok, dec_seq):
    t_all = x1.shape[0]
    tb = TB_MOE
    n0_tiles = n_prompt_tok // tb
    row = _mod_row_map(tb, n_prompt_tok, dec_seq)
    grid_spec = pltpu.PrefetchScalarGridSpec(
        num_scalar_prefetch=2,
        grid=(t_all // tb,),
        in_specs=[
            _const_spec(ys_sorted.shape),
            pl.BlockSpec((tb, 1), lambda b, *_: (b, 0)),
            pl.BlockSpec((tb, D_MODEL), lambda b, *_: (b, 0)),
            pl.BlockSpec((1, 1, 1, N_MOD * D_MODEL), lambda b, *_: (0, row(b), 0, 0)),
            _const_spec((1, D_MODEL)),
        ],
        out_specs=_pair_specs(tb, D_MODEL, n0_tiles),
        scratch_shapes=[pltpu.VMEM((tb, D_MODEL), F32)],
    )
    return pl.pallas_call(
        functools.partial(_combine_kernel, final, n0_tiles),
        out_shape=[jax.ShapeDtypeStruct((n_prompt_tok, D_MODEL), F32),
                   jax.ShapeDtypeStruct((t_all - n_prompt_tok, D_MODEL), F32)],
        grid_spec=grid_spec,
        compiler_params=_params(("arbitrary",)),
        name="combine",
    )(plan["slo"], plan["shi"], ys_sorted, plan["pos"].reshape(t_all, 1), x1, mod_l, norm_final)


def _rg_gate_weights(wa, ba, wx, bx):
    eye = jnp.eye(RG_HEADS, dtype=wa.dtype)

    def dense(wd):
        return jnp.einsum("hij,hg->higj", wd, eye).reshape(RG_WIDTH, RG_WIDTH)

    w = jnp.concatenate([dense(wa[0]), dense(wa[1]), dense(wx[0]), dense(wx[1])], axis=1)
    b = jnp.concatenate([ba[0], ba[1], bx[0], bx[1]], axis=0).reshape(1, 4 * RG_WIDTH)
    return w, b


def kernel(x_prompt, x_sample, state_rglru, state_hgrn, state_ret, c, c_ctx, norm1, norm2, norm_final, w_ada, b_ada, w_in, w_out, hy_conv_w, hy_conv_b, hy_w1, hy_b1, hy_w2, hy_b2, hy_w3, hy_d, rg_conv_w, rg_conv_b, rg_wa, rg_ba, rg_wx, rg_bx, rg_lambda, hg_lb, hg_norm, ret_decay, w_router, router_bias, w_gate, w_up, w_down):
    batch, seq, d = x_prompt.shape
    dec_batch, dec_seq, _ = x_sample.shape
    assert d == D_MODEL and dec_batch + 1 <= COND_ROWS
    n_prompt_tok = batch * seq

    lb_cum = jnp.cumsum(jax.nn.softmax(hg_lb.astype(F32), axis=0), axis=0)
    lb_all = lb_cum - lb_cum[0:1]

    cond = jnp.zeros((COND_ROWS, d), F32).at[0].set(c_ctx).at[1:1 + dec_batch].set(c)
    mod = _modulation(cond, w_ada, b_ada).reshape(DEPTH, COND_ROWS, 1, N_MOD * d)

    w_in_b = w_in.astype(BF16)
    w_out_b = w_out.astype(BF16)

    passes = (
        dict(tok0=0, batch=batch, seq=seq, rope=False),
        dict(tok0=n_prompt_tok, batch=dec_batch, seq=dec_seq, rope=True),
    )
    filters = {p["seq"]: _hyena_filters(p["seq"], hy_w1, hy_b1, hy_w2, hy_b2, hy_w3) for p in passes}
    tables = {p["seq"]: _hyena_tables(p["seq"]) for p in passes}

    xp = x_prompt.reshape(-1, d)
    xs = x_sample.reshape(-1, d)
    new_rg, new_hg, new_ret = [], [], []
    for l in range(DEPTH):
        hy_all, rg_all, hg_all, ret_all = _in_projection(
            xp, xs, mod[l:l + 1], norm1[l].reshape(1, d), w_in_b[l], dec_seq)
        wg, bg = _rg_gate_weights(rg_wa[l], rg_ba[l], rg_wx[l], rg_bx[l])
        sp = jax.nn.softplus(-rg_lambda[l])
        ys = [[], [], [], []]
        for pi, p in enumerate(passes):
            first = pi == 0
            geom = (p["tok0"], p["batch"], p["seq"])
            y_hy = _hyena(hy_all, *geom, hy_conv_w[l], hy_conv_b[l].reshape(1, -1), hy_d[l],
                          filters[p["seq"]][l], tables[p["seq"]])
            y_rg, *st_rg = _rglru(rg_all, *geom, rg_conv_w[l], rg_conv_b[l].reshape(1, -1), wg, bg, sp,
                                  None if first else state_rglru[:, l], first)
            y_hg, *st_hg = _hgrn(hg_all, *geom, lb_all[l], hg_norm[l].reshape(1, -1),
                                 None if first else state_hgrn[:, l], first)
            y_ret, *st_ret = _retention(ret_all, *geom, p["rope"], ret_decay[l],
                                        None if first else state_ret[:, l], first)
            for lst, y in zip(ys, (y_hy, y_rg, y_hg, y_ret)):
                lst.append(y)
            if first:
                new_rg.append(st_rg[0])
                new_hg.append(st_hg[0])
                new_ret.append(st_ret[0])
        x1, h2, info, cend = _route_call(xp, xs, ys, mod[l:l + 1], w_out_b[l], norm2[l].reshape(1, d),
                                         w_router, router_bias, dec_seq)
        plan = _moe_plan(cend, info)
        ys_sorted = _expert_call(l, plan, h2, w_gate, w_up, w_down)
        xp, xs = _combine_call(l == DEPTH - 1, plan, ys_sorted, x1, mod[l:l + 1], norm_final.reshape(1, d),
                               n_prompt_tok, dec_seq)

    return (xp.reshape(batch, seq, d), xs.reshape(dec_batch, dec_seq, d), jnp.stack(new_rg, axis=1),
            jnp.stack(new_hg, axis=1), jnp.stack(new_ret, axis=1))
```

```python
import functools
import math

import numpy as np
import jax
import jax.numpy as jnp
from jax import lax
from jax.experimental import pallas as pl
from jax.experimental.pallas import tpu as pltpu

F32 = jnp.float32
BF16 = jnp.bfloat16

D_MODEL = 1024
DEPTH = 2
GRID_W = 64
HY_WIDTH = 256
RG_WIDTH = 256
HG_WIDTH = 256
RET_WIDTH = 256
MIX_WIDTH = 256
HY_ORDER = 2
HY_EMB = 33
HY_BANDS = 16
HY_FFN = 64
HY_DECAY_TARGET = 1e-2
HY_DECAY_SHORT = 0.3
HY_DECAY_LONG = 1.5
RG_HEADS = 8
RG_HEAD_DIM = 32
RG_C = 8.0
RG_CHUNK = 16
N_HEADS = 4
HEAD_DIM = 64
HG_CHUNK = 64
RET_CHUNK = 256
ROPE_BASE = 10000.0
N_EXPERTS = 16
N_GROUPS = 4
E_PER_GROUP = 4
D_EXPERT = 512
N_MOD = 6
EPS = 1e-6
PROJ_HY = 3 * HY_WIDTH
PROJ_RG = 2 * RG_WIDTH
PROJ_HG = 5 * HG_WIDTH
PROJ_RET = 4 * RET_WIDTH
PROJ_WIDTH = PROJ_HY + PROJ_RG + PROJ_HG + PROJ_RET
COND_ROWS = 16
LANES = 128
VMEM_LIMIT = 56 * 1024 * 1024
TM_PROJ = 512
TB_MOE = 256
TM_ROUTE = 512
TS_MOE = 1024
TN_MOD = 1536

_NN = (((1,), (0,)), ((), ()))
_NT = (((1,), (1,)), ((), ()))


def _mm(a, b, dn=_NN):
    return lax.dot_general(a, b, dn, preferred_element_type=F32)


def _split2(x):
    hi = x.astype(BF16)
    lo = (x - hi.astype(F32)).astype(BF16)
    return hi, lo


def _split3(x):
    hi = x.astype(BF16)
    r = x - hi.astype(F32)
    mid = r.astype(BF16)
    lo = (r - mid.astype(F32)).astype(BF16)
    return hi, mid, lo


def _mm3(a, b, dn=_NN):
    ah, al = _split2(a)
    bh, bl = _split2(b)
    return _mm(ah, bh, dn) + (_mm(ah, bl, dn) + _mm(al, bh, dn))


def _mm_exact_rhs(a, b_bf16):
    a1, a2, a3 = _split3(a)
    return _mm(a1, b_bf16) + (_mm(a2, b_bf16) + _mm(a3, b_bf16))


def _sigmoid(x):
    return 1.0 / (1.0 + jnp.exp(-x))


def _silu(x):
    return x * _sigmoid(x)


def _log_sigmoid(x):
    return jnp.minimum(x, 0.0) - jnp.log(1.0 + jnp.exp(-jnp.abs(x)))


def _gelu_tanh(x):
    return 0.5 * x * (1.0 + jnp.tanh(math.sqrt(2.0 / math.pi) * (x + 0.044715 * (x * x * x))))


def _iota(shape, dim):
    return lax.broadcasted_iota(jnp.int32, shape, dim)


def _shift_rows(u, k, row):
    n = u.shape[0]
    if k == 0:
        return u
    r = pltpu.roll(u, (-k) % n, axis=0)
    if k < 0:
        return jnp.where(row >= -k, r, 0.0)
    return jnp.where(row < n - k, r, 0.0)


def _head_mask(n_rows_per_head):
    shape = (N_HEADS * n_rows_per_head, MIX_WIDTH)
    return (_iota(shape, 0) // n_rows_per_head) == (_iota(shape, 1) // HEAD_DIM)


def _block_diag_mask():
    shape = (MIX_WIDTH, MIX_WIDTH)
    return (_iota(shape, 0) // HEAD_DIM) == (_iota(shape, 1) // HEAD_DIM)


def _head_rmsnorm(o, ones_bd):
    ms = _mm_exact_rhs(o * o, ones_bd)
    return o * lax.rsqrt(ms + EPS)


def _load_state(s0_ref, d):
    zero = jnp.zeros((HEAD_DIM, HEAD_DIM), F32)
    rows = []
    for h in range(N_HEADS):
        blk = s0_ref[0, d, h].T
        rows.append(jnp.concatenate([blk if g == h else zero for g in range(N_HEADS)], axis=1))
    return jnp.concatenate(rows, axis=0)


def _store_state(st_ref, d, st):
    for h in range(N_HEADS):
        lo, hi = h * HEAD_DIM, (h + 1) * HEAD_DIM
        st_ref[0, d, h] = st[lo:hi, lo:hi].T


def _params(sem, vmem=VMEM_LIMIT):
    return pltpu.CompilerParams(dimension_semantics=sem, vmem_limit_bytes=vmem)


def _const_spec(shape):
    nd = len(shape)
    return pl.BlockSpec(shape, lambda *_: (0,) * nd, pipeline_mode=pl.Buffered(1))


def _mod_kernel(cond_ref, w_ref, b_ref, o_ref):
    o_ref[0] = _mm3(_silu(cond_ref[...]), w_ref[0]) + b_ref[0]


def _modulation(cond, w_ada, b_ada):
    tn = TN_MOD
    n_mod = N_MOD * D_MODEL
    return pl.pallas_call(
        _mod_kernel,
        out_shape=jax.ShapeDtypeStruct((DEPTH, COND_ROWS, n_mod), F32),
        grid=(DEPTH, n_mod // tn),
        in_specs=[
            pl.BlockSpec((COND_ROWS, D_MODEL), lambda l, j: (0, 0)),
            pl.BlockSpec((1, D_MODEL, tn), lambda l, j: (l, 0, j)),
            pl.BlockSpec((1, 1, tn), lambda l, j: (l, 0, j)),
        ],
        out_specs=pl.BlockSpec((1, COND_ROWS, tn), lambda l, j: (l, 0, j)),
        compiler_params=_params(("parallel", "parallel")),
        name="modulation",
    )(cond, w_ada, b_ada.reshape(DEPTH, 1, n_mod))


def _pair_specs(tm, width, n0_tiles):
    return [pl.BlockSpec((tm, width), lambda i, *_: (jnp.minimum(i, n0_tiles - 1), 0)),
            pl.BlockSpec((tm, width), lambda i, *_: (jnp.maximum(i - n0_tiles, 0), 0))]


def _proj_kernel(n0_tiles, xp_ref, xs_ref, mod_ref, n1_ref, w_ref, hy_ref, rg_ref, hg_ref, ret_ref):
    x = jnp.where(pl.program_id(0) < n0_tiles, xp_ref[...], xs_ref[...])
    mod = mod_ref[0, 0]
    sh1 = mod[:, 0:D_MODEL]
    sc1 = mod[:, D_MODEL:2 * D_MODEL]
    h = x * lax.rsqrt(jnp.mean(x * x, axis=-1, keepdims=True) + EPS) * n1_ref[...]
    h = (h * (1.0 + sc1) + sh1).astype(BF16)
    c0 = 0
    for ref, width in ((hy_ref, PROJ_HY), (rg_ref, PROJ_RG), (hg_ref, PROJ_HG), (ret_ref, PROJ_RET)):
        ref[...] = _mm(h, w_ref[:, c0:c0 + width])
        c0 += width


def _mod_row_map(tm, n_prompt_tok, dec_seq):
    n_prompt_tiles = n_prompt_tok // tm

    def row(i):
        return jnp.where(i < n_prompt_tiles, 0, 1 + (i * tm - n_prompt_tok) // dec_seq)

    return row


def _in_projection(xp, xs, mod_l, norm1_l, w_in_l, dec_seq):
    n_prompt_tok = xp.shape[0]
    t_all = n_prompt_tok + xs.shape[0]
    tm = TM_PROJ
    n0_tiles = n_prompt_tok // tm
    row = _mod_row_map(tm, n_prompt_tok, dec_seq)
    widths = (PROJ_HY, PROJ_RG, PROJ_HG, PROJ_RET)
    return pl.pallas_call(
        functools.partial(_proj_kernel, n0_tiles),
        out_shape=[jax.ShapeDtypeStruct((t_all, w), F32) for w in widths],
        grid=(t_all // tm,),
        in_specs=_pair_specs(tm, D_MODEL, n0_tiles) + [
            pl.BlockSpec((1, 1, 1, N_MOD * D_MODEL), lambda i: (0, row(i), 0, 0)),
            _const_spec((1, D_MODEL)),
            _const_spec((D_MODEL, PROJ_WIDTH)),
        ],
        out_specs=[pl.BlockSpec((tm, w), lambda i: (i, 0)) for w in widths],
        compiler_params=_params(("parallel",)),
        name="in_projection",
    )(xp, xs, mod_l, norm1_l, w_in_l)


def _dft_tables(seq):
    k = np.arange(seq, dtype=np.int64)
    m = (k[:, None] * k[None, :]) % (2 * seq)
    ang = np.pi * m.astype(np.float64) / seq
    return np.cos(ang), np.sin(ang)


def _hyena_tables(seq):
    cos, sin = _dft_tables(seq)
    sign = np.where(np.arange(seq) % 2 == 0, 1.0, -1.0)
    fwd = np.concatenate([cos, sign[None, :], sin[1:]], axis=0)
    wk = np.full((seq,), 2.0)
    wk[0] = 1.0
    inv_cos = (cos * wk[None, :]) / (2.0 * seq)
    inv_nyq = sign[:, None] / (2.0 * seq)
    inv_sin = 2.0 * sin[:, 1:] / (2.0 * seq)
    inv = np.concatenate([inv_cos, inv_nyq, inv_sin], axis=1)

    return jnp.asarray(fwd, F32).astype(BF16), jnp.asarray(inv, F32).astype(BF16)


def _filter_embedding(seq):
    t = np.arange(seq, dtype=np.float64)
    t_norm = t / max(seq - 1, 1)
    bands = np.linspace(1e-4, HY_BANDS - 1, HY_BANDS)
    ang = (2.0 * np.pi / seq) * t[:, None] * bands[None, :]
    z = np.concatenate([t_norm[:, None], np.cos(ang), np.sin(ang)], axis=-1)
    z = np.pad(z, ((0, 0), (0, LANES - HY_EMB)))
    deltas = np.abs(np.linspace(math.log(HY_DECAY_TARGET) / HY_DECAY_LONG,
                                math.log(HY_DECAY_TARGET) / HY_DECAY_SHORT, HY_WIDTH))
    window = np.exp(-t_norm[:, None] * deltas[None, :])
    return jnp.asarray(z, F32), jnp.asarray(window, F32)


def _filter_kernel(z_ref, win_ref, cos_ref, sin_ref, w1_ref, b1_ref, w2_ref, b2_ref, w3_ref, o_ref):
    seq = z_ref.shape[0]
    h = jnp.sin(_mm3(z_ref[...], w1_ref[0]) + b1_ref[0])
    h = jnp.sin(_mm3(h, w2_ref[0]) + b2_ref[0])
    h = _mm3(h, w3_ref[0])
    win = win_ref[...]
    row = _iota((seq, 1), 0)
    sums, diffs = [], []
    for o in range(HY_ORDER):
        c0 = o * 2 * HY_WIDTH
        hf = h[:, c0:c0 + HY_WIDTH] * win
        hb = h[:, c0 + HY_WIDTH:c0 + 2 * HY_WIDTH] * win
        ssq = jnp.sum(hf * hf + hb * hb, axis=0, keepdims=True)
        inv = lax.rsqrt(ssq + EPS)
        hf = hf * inv
        hb = jnp.where(row == 0, 0.0, hb * inv)
        sums.append(hf + hb)
        diffs.append(hf - hb)
    hsum = jnp.concatenate(sums, axis=1)
    hdiff = jnp.concatenate(diffs, axis=1)
    h_re = _mm3(cos_ref[...], hsum)
    h_im = _mm3(sin_ref[...], hdiff)
    sign = jnp.where(row % 2 == 0, 1.0, -1.0)
    h_nyq = jnp.sum(sign * hsum, axis=0, keepdims=True)
    o_ref[0, 0] = h_re
    o_ref[0, 1] = h_im
    o_ref[0, 2] = jnp.where(row == 0, h_nyq, h_re)


def _hyena_filters(seq, w1, b1, w2, b2, w3):
    z, window = _filter_embedding(seq)
    cos, sin = _dft_tables(seq)
    n_out = HY_ORDER * 2 * HY_WIDTH
    w1p = jnp.pad(w1, ((0, 0), (0, LANES - HY_EMB), (0, LANES - HY_FFN)))
    b1p = jnp.pad(b1, ((0, 0), (0, LANES - HY_FFN))).reshape(DEPTH, 1, LANES)
    w2p = jnp.pad(w2, ((0, 0), (0, LANES - HY_FFN), (0, LANES - HY_FFN)))
    b2p = jnp.pad(b2, ((0, 0), (0, LANES - HY_FFN))).reshape(DEPTH, 1, LANES)
    w3p = jnp.pad(w3, ((0, 0), (0, LANES - HY_FFN), (0, 0)))
    per_layer = lambda shape: pl.BlockSpec((1,) + shape, lambda l: (l,) + (0,) * len(shape))
    return pl.pallas_call(
        _filter_kernel,
        out_shape=jax.ShapeDtypeStruct((DEPTH, 3, seq, HY_ORDER * HY_WIDTH), F32),
        grid=(DEPTH,),
        in_specs=[
            _const_spec((seq, LANES)), _const_spec((seq, HY_WIDTH)),
            _const_spec((seq, seq)), _const_spec((seq, seq)),
            per_layer((LANES, LANES)), per_layer((1, LANES)),
            per_layer((LANES, LANES)), per_layer((1, LANES)),
            per_layer((LANES, n_out)),
        ],
        out_specs=pl.BlockSpec((1, 3, seq, HY_ORDER * HY_WIDTH), lambda l: (l, 0, 0, 0)),
        compiler_params=_params(("parallel",)),
        name=f"hyena_filters_{seq}",
    )(z, window, jnp.asarray(cos, F32), jnp.asarray(sin, F32), w1p, b1p, w2p, b2p, w3p)


def _hyena_kernel(u_ref, cw_ref, cb_ref, d_ref, filt_ref, fwd_ref, inv_ref, y_ref):
    seq = u_ref.shape[0]
    u = u_ref[...]
    row = _iota((seq, 1), 0)
    cw = cw_ref[...]
    uc = cb_ref[...] + cw[0:1] * _shift_rows(u, -1, row) + cw[1:2] * u + cw[2:3] * _shift_rows(u, 1, row)
    v = uc[:, 0:HY_WIDTH]
    gates = (uc[:, HY_WIDTH:2 * HY_WIDTH], uc[:, 2 * HY_WIDTH:3 * HY_WIDTH])
    fwd, inv = fwd_ref[...], inv_ref[...]
    d = d_ref[...]
    z = v
    for o in range(HY_ORDER):
        c0 = o * HY_WIDTH
        spec = _mm(fwd, z.astype(BF16))
        s_re, s_im = spec[:seq], spec[seq:]
        a = filt_ref[0, :, c0:c0 + HY_WIDTH]
        b = filt_ref[1, :, c0:c0 + HY_WIDTH]
        c = filt_ref[2, :, c0:c0 + HY_WIDTH]
        prod = jnp.concatenate([s_re * a - s_im * b, s_re * b + s_im * c], axis=0)
        conv = _mm(inv, prod.astype(BF16))
        z = gates[o] * (conv + d[o:o + 1] * z)
    y_ref[...] = z


def _mixer_call(body, name, proj_all, tok0, batch, seq, consts, batch_ins, state_shapes, scratch):
    proj_width = proj_all.shape[1]
    blk0 = tok0 // seq

    def batch_spec(shape):
        nd = len(shape)
        return pl.BlockSpec((1,) + tuple(shape[1:]), lambda b: (b,) + (0,) * (nd - 1))

    in_specs = [pl.BlockSpec((seq, proj_width), lambda b: (blk0 + b, 0))]
    in_specs += [_const_spec(a.shape) for a in consts]
    in_specs += [batch_spec(a.shape) for a in batch_ins]
    out_shape = [jax.ShapeDtypeStruct((batch * seq, MIX_WIDTH), F32)]
    out_specs = [pl.BlockSpec((seq, MIX_WIDTH), lambda b: (b, 0))]
    for shape in state_shapes:
        out_shape.append(jax.ShapeDtypeStruct(shape, F32))
        out_specs.append(batch_spec(shape))
    return pl.pallas_call(
        body, out_shape=out_shape, grid=(batch,), in_specs=in_specs, out_specs=out_specs,
        scratch_shapes=scratch,
        compiler_params=_params(("parallel",)),
        name=f"{name}_{seq}",
    )(proj_all, *consts, *batch_ins)


def _hyena(u_all, tok0, batch, seq, conv_w, conv_b, d_bias, filt_l, tables):
    consts = [conv_w, conv_b, d_bias, filt_l, *tables]
    return _mixer_call(_hyena_kernel, "hyena", u_all, tok0, batch, seq, consts, [], [], [])[0]


def _rglru_kernel(has_s0, emit_state, rg_ref, cw_ref, cb_ref, wg_ref, bg_ref, sp_ref, *refs):
    h0_ref = refs[0] if has_s0 else None
    y_ref = refs[1 if has_s0 else 0]
    st_ref = refs[-1] if emit_state else None
    seq = rg_ref.shape[0]
    w = RG_WIDTH
    xr = rg_ref[:, 0:w]
    gate = rg_ref[:, w:2 * w]
    row = _iota((seq, 1), 0)
    cw = cw_ref[...]
    xc = (cb_ref[...] + cw[0:1] * _shift_rows(xr, -2, row) + cw[1:2] * _shift_rows(xr, -1, row)
          + cw[2:3] * xr + cw[3:4] * _shift_rows(xr, 1, row))
    g = _sigmoid(_mm3(xc, wg_ref[...]) + bg_ref[...])
    sp = sp_ref[...]
    c = RG_CHUNK
    n = seq // c
    pos = row % c
    hs = []
    for d in range(2):
        forward = d == 0
        r = g[:, d * w:(d + 1) * w]
        i = g[:, (2 + d) * w:(3 + d) * w]
        log_a = -RG_C * r * sp[d:d + 1]
        a = jnp.exp(log_a)
        b = jnp.sqrt(jnp.tanh(-log_a) * (1.0 + a * a)) * (i * xc)
        step = 1
        while step < c:
            keep = (pos >= step) if forward else (pos < c - step)
            shift = step if forward else seq - step
            a_s = pltpu.roll(a, shift, axis=0)
            b_s = pltpu.roll(b, shift, axis=0)
            b = jnp.where(keep, a * b_s + b, b)
            a = jnp.where(keep, a * a_s, a)
            step *= 2
        a3 = a.reshape(n, c, w)
        b3 = b.reshape(n, c, w)
        edge = c - 1 if forward else 0
        a_end = a3[:, edge:edge + 1, :]
        b_end = b3[:, edge:edge + 1, :]
        h = h0_ref[0, d:d + 1, :] if has_s0 else jnp.zeros((1, w), F32)
        h_in = [None] * n
        for ci in (range(n) if forward else range(n - 1, -1, -1)):
            h_in[ci] = h
            h = b_end[ci] + a_end[ci] * h
        hs.append((b3 + a3 * jnp.stack(h_in, axis=0)).reshape(seq, w))
    y_ref[...] = (hs[0] + hs[1]) * _gelu_tanh(gate)
    if emit_state:
        st_ref[0, 0:1, :] = hs[0][seq - 1:seq]
        st_ref[0, 1:2, :] = hs[1][0:1]


def _rglru(rg_all, tok0, batch, seq, conv_w, conv_b, w_gates, b_gates, softplus_neg_lam, h0, emit_state):
    has_s0 = h0 is not None
    consts = [conv_w, conv_b, w_gates, b_gates, softplus_neg_lam]
    return _mixer_call(functools.partial(_rglru_kernel, has_s0, emit_state), "rglru", rg_all, tok0, batch, seq,
                       consts, [h0] if has_s0 else [],
                       [(batch, 2, RG_WIDTH)] if emit_state else [], [])


def _hgrn_kernel(has_s0, emit_state, hg_ref, lb_ref, gain_ref, *refs):
    s0_ref = refs[0] if has_s0 else None
    y_ref = refs[1 if has_s0 else 0]
    st_ref = refs[-1] if emit_state else None
    seq = hg_ref.shape[0]
    w = HG_WIDTH
    c = HG_CHUNK
    n = seq // c
    mid = c // 2
    chunks = lambda a: a.reshape(n, c, w)
    pos = _iota((seq, 1), 0) % c
    stack_mask = _head_mask(c)[None]
    bd_mask = _block_diag_mask()
    pair_shape = (c, N_HEADS * c)
    t_idx = _iota(pair_shape, 0)
    s_idx = _iota(pair_shape, 1) % c

    def stack_heads(a3):
        return jnp.where(stack_mask, jnp.concatenate([a3] * N_HEADS, axis=1), 0.0).astype(BF16)

    q3 = chunks(_silu(hg_ref[:, 0:w]))
    v3 = chunks(hg_ref[:, 3 * w:4 * w])
    v_stack = stack_heads(v3)
    v_t = jnp.swapaxes(v3, 1, 2).astype(BF16)
    lbv = lb_ref[...]
    o_sum = None
    finals = []
    for d in range(2):
        forward = d == 0
        f_pre = hg_ref[:, (1 + d) * w:(2 + d) * w]
        lo = lbv[d:d + 1]
        a1 = jnp.log(lo)
        a2 = jnp.log(1.0 - lo) + _log_sigmoid(f_pre)
        g = jnp.maximum(a1, a2) + jnp.log(1.0 + jnp.exp(-jnp.abs(a1 - a2)))
        k3 = chunks((1.0 - lo) * _sigmoid(-f_pre))
        step = 1
        while step < c:
            if forward:
                g = jnp.where(pos >= step, g + pltpu.roll(g, step, axis=0), g)
            else:
                g = jnp.where(pos < c - step, g + pltpu.roll(g, seq - step, axis=0), g)
            step *= 2
        g3 = chunks(g)
        g_ref = g3[:, mid:mid + 1, :]
        g_tot = g3[:, c - 1:c, :] if forward else g3[:, 0:1, :]
        q_in = (q3 * jnp.exp(g3)).astype(BF16)
        q_sc = (q3 * jnp.exp(g3 - g_ref)).astype(BF16)
        k_stack = stack_heads(k3 * jnp.exp(g_ref - g3))
        k_out = (k3 * jnp.exp(g_tot - g3)).astype(BF16)
        pair = jnp.einsum("ntl,nrl->ntr", q_sc, k_stack, preferred_element_type=F32)
        causal = (t_idx >= s_idx) if forward else (t_idx <= s_idx)
        pair = jnp.where(causal[None], pair, 0.0).astype(BF16)
        o_intra = jnp.einsum("ntr,nrv->ntv", pair, v_stack, preferred_element_type=F32)
        upd = jnp.einsum("nvs,nsk->nvk", v_t, k_out, preferred_element_type=F32)
        decay = jnp.exp(g_tot)
        st = _load_state(s0_ref, d) if has_s0 else jnp.zeros((w, w), F32)
        o_inter = [None] * n
        for ci in (range(n) if forward else range(n - 1, -1, -1)):
            o_inter[ci] = _mm(q_in[ci], st.astype(BF16), _NT)
            st = st * decay[ci] + jnp.where(bd_mask, upd[ci], 0.0)
        o_dir = o_intra + jnp.stack(o_inter, axis=0)
        o_sum = o_dir if o_sum is None else o_sum + o_dir
        finals.append(st)

    ones_bd = jnp.where(bd_mask, 1.0 / HEAD_DIM, 0.0).astype(BF16)
    o = _head_rmsnorm(o_sum.reshape(seq, w), ones_bd) * gain_ref[...]
    y_ref[...] = o * _silu(hg_ref[:, 4 * w:5 * w])
    if emit_state:
        _store_state(st_ref, 0, finals[0])
        _store_state(st_ref, 1, finals[1])


def _state_shape(batch):
    return (batch, 2, N_HEADS, HEAD_DIM, HEAD_DIM)


def _hgrn(hg_all, tok0, batch, seq, lb_l, gain_l, s0, emit_state):
    has_s0 = s0 is not None
    return _mixer_call(functools.partial(_hgrn_kernel, has_s0, emit_state), "hgrn2", hg_all, tok0, batch, seq,
                       [lb_l, gain_l], [s0] if has_s0 else [],
                       [_state_shape(batch)] if emit_state else [], [])


def _ret_kernel(use_rope, has_s0, emit_state, ret_ref, cos_ref, sin_ref, dl_ref, dlp_ref, *refs):
    s0_ref = refs[0] if has_s0 else None
    y_ref = refs[1 if has_s0 else 0]
    st_ref = refs[-1] if emit_state else None
    seq = ret_ref.shape[0]
    w = RET_WIDTH
    c = min(RET_CHUNK, seq)
    n = seq // c
    chunks = lambda a: a.reshape(n, c, w)
    q = ret_ref[:, 0:w]
    k = ret_ref[:, w:2 * w] * (HEAD_DIM ** -0.5)
    if use_rope:
        lane = _iota((seq, w), 1)
        even = (lane % 2) == 0
        cos = cos_ref[...]
        sin = sin_ref[...]

        def rope(x):
            nxt = pltpu.roll(x, w - 1, axis=1)
            prv = pltpu.roll(x, 1, axis=1)
            return x * cos + jnp.where(even, nxt, prv) * sin

        q = rope(q)
        k = rope(k)
    q3, k3, v3 = chunks(q), chunks(k), chunks(ret_ref[:, 2 * w:3 * w])

    lg = _log_sigmoid(dl_ref[...])
    lgp = _log_sigmoid(dlp_ref[...])
    pair_shape = (c, N_HEADS * c)
    t_idx = _iota(pair_shape, 0)
    s_idx = _iota(pair_shape, 1) % c
    dist = (t_idx - s_idx).astype(F32)
    decay = jnp.exp(jnp.where(dist >= 0, dist * lgp[0:1], -dist * lgp[1:2]))
    decay = decay + jnp.where(dist == 0, 1.0, 0.0)
    stack_mask = _head_mask(c)[None]
    bd_mask = _block_diag_mask()
    pos = _iota((c, 1), 0).astype(F32)
    fc = float(c)

    def stack_heads(a3):
        return jnp.where(stack_mask, jnp.concatenate([a3] * N_HEADS, axis=1), 0.0).astype(BF16)

    pair = jnp.einsum("ntl,nrl->ntr", q3.astype(BF16), stack_heads(k3), preferred_element_type=F32)
    pair = (pair * decay[None]).astype(BF16)
    o_sum = jnp.einsum("ntr,nrv->ntv", pair, stack_heads(v3), preferred_element_type=F32)
    v_t = jnp.swapaxes(v3, 1, 2).astype(BF16)
    finals = []
    for d in range(2):
        forward = d == 0
        lgd = lg[d:d + 1]
        q_in = (q3 * jnp.exp(((pos + 1.0) if forward else (fc - pos)) * lgd)).astype(BF16)
        k_out = (k3 * jnp.exp(((fc - 1.0 - pos) if forward else pos) * lgd)).astype(BF16)
        upd = jnp.einsum("nvs,nsk->nvk", v_t, k_out, preferred_element_type=F32)
        chunk_decay = jnp.exp(fc * lgd)
        st = _load_state(s0_ref, d) if has_s0 else jnp.zeros((w, w), F32)
        o_inter = [None] * n
        for ci in (range(n) if forward else range(n - 1, -1, -1)):
            o_inter[ci] = _mm(q_in[ci], st.astype(BF16), _NT)
            st = st * chunk_decay + jnp.where(bd_mask, upd[ci], 0.0)
        o_sum = o_sum + jnp.stack(o_inter, axis=0)
        finals.append(st)

    ones_bd = jnp.where(bd_mask, 1.0 / HEAD_DIM, 0.0).astype(BF16)
    o = _head_rmsnorm(o_sum.reshape(seq, w), ones_bd)
    y_ref[...] = _silu(ret_ref[:, 3 * w:4 * w]) * o
    if emit_state:
        _store_state(st_ref, 0, finals[0])
        _store_state(st_ref, 1, finals[1])


def _rope_tables(seq):
    rows = seq // GRID_W
    row = np.repeat(np.arange(rows), GRID_W).astype(np.float64)
    col = (np.arange(seq) % GRID_W).astype(np.float64)
    n_freq = HEAD_DIM // 4
    inv_freq = ROPE_BASE ** (-np.arange(n_freq, dtype=np.float64) / n_freq)
    ang = np.concatenate([row[:, None] * inv_freq, col[:, None] * inv_freq], axis=-1)
    ang = np.repeat(ang, 2, axis=1)
    cos = np.tile(np.cos(ang), (1, N_HEADS))
    sin = np.tile(np.sin(ang) * np.where(np.arange(HEAD_DIM) % 2 == 0, -1.0, 1.0)[None, :], (1, N_HEADS))
    return jnp.asarray(cos, F32), jnp.asarray(sin, F32)


def _retention(ret_all, tok0, batch, seq, use_rope, decay_l, s0, emit_state):
    has_s0 = s0 is not None
    w = RET_WIDTH
    c = min(RET_CHUNK, seq)
    cos, sin = _rope_tables(seq)
    dl = jnp.repeat(decay_l, HEAD_DIM, axis=-1)
    dlp = jnp.repeat(decay_l, c, axis=-1)
    return _mixer_call(functools.partial(_ret_kernel, use_rope, has_s0, emit_state), "retention", ret_all, tok0,
                       batch, seq, [cos, sin, dl, dlp], [s0] if has_s0 else [],
                       [_state_shape(batch)] if emit_state else [], [])


def _route(h2, wrt_ref, rb_ref):
    logits = _mm3(wrt_ref[...], h2, _NT)
    tm = logits.shape[1]
    m = jnp.max(logits, axis=0, keepdims=True)
    e = jnp.exp(logits - m)
    probs = e / jnp.sum(e, axis=0, keepdims=True)
    sel = probs + rb_ref[...]
    lane = _iota((N_EXPERTS, tm), 0).astype(F32)
    group = (_iota((N_EXPERTS, tm), 0) // E_PER_GROUP).astype(F32)
    neg = -jnp.inf

    def first_argmax(vals):
        mx = jnp.max(vals, axis=0, keepdims=True)
        idx = jnp.min(jnp.where(vals == mx, lane, float(N_EXPERTS)), axis=0, keepdims=True)
        return mx, idx

    best_score = None
    best = None
    for g in range(N_GROUPS):
        vals = jnp.where(group == float(g), sel, neg)
        m1, i1 = first_argmax(vals)
        m2, _ = first_argmax(jnp.where(lane == i1, neg, vals))
        score = m1 + m2
        if g == 0:
            best_score, best = score, jnp.zeros_like(score)
        else:
            take = score > best_score
            best = jnp.where(take, float(g), best)
            best_score = jnp.where(take, score, best_score)
    vals = jnp.where(group == best, sel, neg)
    _, i1 = first_argmax(vals)
    _, i2 = first_argmax(jnp.where(lane == i1, neg, vals))
    chosen = (lane == i1) | (lane == i2)
    picked = jnp.where(chosen, probs, 0.0)
    return picked / jnp.sum(picked, axis=0, keepdims=True), best


INFO_ROWS = 8
INFO_GROUP = 4
INFO_RANK = 5
H2_WIDTH = D_MODEL + LANES


def _route_kernel(n0_tiles, xp_ref, xs_ref, *refs):
    y_refs = refs[:8]
    mod_ref, wo_ref, n2_ref, wrt_ref, rb_ref, x1_ref, h2_ref, info_ref, cend_ref, carry_s = refs[8:]
    i = pl.program_id(0)
    first = i < n0_tiles
    d = D_MODEL
    tb = TB_MOE
    mod = mod_ref[0, 0]

    @pl.when(i == 0)
    def _():
        carry_s[...] = jnp.zeros_like(carry_s)

    tm = TM_ROUTE
    mixed = None
    for j in range(4):
        y = jnp.where(first, y_refs[2 * j][...], y_refs[2 * j + 1][...])
        part = _mm(y.astype(BF16), wo_ref[j * MIX_WIDTH:(j + 1) * MIX_WIDTH, :])
        mixed = part if mixed is None else mixed + part
    x = jnp.where(first, xp_ref[...], xs_ref[...])
    x1 = x + mod[:, 2 * d:3 * d] * mixed
    x1_ref[...] = x1
    h2 = x1 * lax.rsqrt(jnp.mean(x1 * x1, axis=-1, keepdims=True) + EPS) * n2_ref[...]
    h2 = h2 * (1.0 + mod[:, 4 * d:5 * d]) + mod[:, 3 * d:4 * d]
    h2_ref[:, 0:d] = h2.astype(BF16)

    gates, best = _route(h2, wrt_ref, rb_ref)
    expert = _iota((N_EXPERTS, tm), 0).astype(F32)
    g4 = [jnp.sum(jnp.where(expert == E_PER_GROUP * best + j, gates, 0.0), axis=0, keepdims=True)
          for j in range(E_PER_GROUP)]
    onehot = jnp.where(expert == best, 1.0, 0.0)
    upper = (_iota((tm, tm), 0) <= _iota((tm, tm), 1)).astype(BF16)
    carry = carry_s[...]
    incl = _mm(onehot.astype(BF16), upper) + carry[:, 0:1]
    rank = jnp.sum(onehot * incl, axis=0, keepdims=True) - 1.0
    info_ref[...] = jnp.concatenate(g4 + [best, rank, jnp.zeros((INFO_ROWS - 6, tm), F32)], axis=0)
    for h in range(tm // tb):
        end = (h + 1) * tb
        cend_ref[h] = jnp.broadcast_to(incl[:, end - 1:end], (N_EXPERTS, LANES))
    carry_s[...] = jnp.broadcast_to(incl[:, tm - 1:tm], (N_EXPERTS, LANES))

    pieces = _split3(jnp.concatenate(g4 + [jnp.zeros((N_EXPERTS - E_PER_GROUP, tm), F32)], axis=0))
    pieces = jnp.concatenate(list(pieces) + [jnp.zeros((LANES - 3 * N_EXPERTS, tm), BF16)], axis=0)
    eye = (_iota((tm, tm), 0) == _iota((tm, tm), 1)).astype(BF16)
    h2_ref[:, d:d + LANES] = _mm(eye, pieces, _NT).astype(BF16)


def _route_call(xp, xs, ys, mod_l, w_out_l, norm2_l, w_router, router_bias, dec_seq):
    n_prompt_tok = xp.shape[0]
    t_all = n_prompt_tok + xs.shape[0]
    tm = TM_ROUTE
    sub = tm // TB_MOE
    n0_tiles = n_prompt_tok // tm
    row = _mod_row_map(tm, n_prompt_tok, dec_seq)
    tok = lambda width: pl.BlockSpec((tm, width), lambda i: (i, 0))
    y_specs, y_args = [], []
    for yp, ysm in ys:
        y_specs += _pair_specs(tm, MIX_WIDTH, n0_tiles)
        y_args += [yp, ysm]
    return pl.pallas_call(
        functools.partial(_route_kernel, n0_tiles),
        out_shape=[jax.ShapeDtypeStruct((t_all, D_MODEL), F32), jax.ShapeDtypeStruct((t_all, H2_WIDTH), BF16),
                   jax.ShapeDtypeStruct((INFO_ROWS, t_all), F32),
                   jax.ShapeDtypeStruct((t_all // TB_MOE, N_EXPERTS, LANES), F32)],
        grid=(t_all // tm,),
        in_specs=_pair_specs(tm, D_MODEL, n0_tiles) + y_specs + [
            pl.BlockSpec((1, 1, 1, N_MOD * D_MODEL), lambda i: (0, row(i), 0, 0)),
            _const_spec((D_MODEL, D_MODEL)), _const_spec((1, D_MODEL)),
            _const_spec((N_EXPERTS, D_MODEL)), _const_spec((N_EXPERTS, 1)),
        ],
        out_specs=[tok(D_MODEL), tok(H2_WIDTH), pl.BlockSpec((INFO_ROWS, tm), lambda i: (0, i)),
                   pl.BlockSpec((sub, N_EXPERTS, LANES), lambda i: (i, 0, 0))],
        scratch_shapes=[pltpu.VMEM((N_EXPERTS, LANES), F32)],
        compiler_params=_params(("arbitrary",)),
        name="route",
    )(xp, xs, *y_args, mod_l, w_out_l, norm2_l, w_router.T, router_bias.reshape(N_EXPERTS, 1))


def _moe_plan(cend, info):
    tb, ts = TB_MOE, TS_MOE
    n_blocks = cend.shape[0]
    t_all = n_blocks * tb
    n_tiles = t_all // ts + N_GROUPS
    sub_per_tile = ts // tb
    n_sub = n_tiles * sub_per_tile
    cend = cend[:, :N_GROUPS, 0].astype(jnp.int32)
    cum = jnp.concatenate([jnp.zeros((1, N_GROUPS), jnp.int32), cend[:-1]], axis=0)
    cnt = cend - cum
    grp = info[INFO_GROUP].astype(jnp.int32)
    rank = info[INFO_RANK].astype(jnp.int32)
    tot = cend[-1]
    padded = ((tot + ts - 1) // ts) * ts
    off = jnp.cumsum(padded) - padded
    pos = off[grp] + rank

    sub_start = jnp.arange(n_sub, dtype=jnp.int32) * tb
    in_g = (sub_start[:, None] >= off[None, :]) & (sub_start[:, None] < (off + padded)[None, :])
    g_of = jnp.argmax(in_g, axis=1).astype(jnp.int32)
    r0 = sub_start - off[g_of]
    r1 = jnp.minimum(r0 + tb, tot[g_of])
    live = jnp.any(in_g, axis=1) & (r1 > r0)
    cend_g = cend[:, g_of]
    blo = jnp.where(live, jnp.sum(cend_g <= r0[None, :], axis=0), 0).astype(jnp.int32)
    bhi = jnp.where(live, jnp.sum(cend_g <= (r1 - 1)[None, :], axis=0), -1).astype(jnp.int32)

    used = jnp.any(in_g, axis=1)[::sub_per_tile]
    tile_group = g_of[::sub_per_tile]
    eidx = E_PER_GROUP * tile_group[:, None] + jnp.arange(E_PER_GROUP, dtype=jnp.int32)[None, :]
    n_used = jnp.sum(used.astype(jnp.int32))
    last = eidx[jnp.maximum(n_used - 1, 0), E_PER_GROUP - 1]
    eidx = jnp.where(used[:, None], eidx, last).reshape(-1).astype(jnp.int32)

    start = off[None, :] + cum
    end = off[None, :] + cend
    has = cnt > 0
    slo = jnp.where(has, start // tb, 0).reshape(-1).astype(jnp.int32)
    shi = jnp.where(has, (end - 1) // tb, -1).reshape(-1).astype(jnp.int32)
    return dict(pos=pos, eidx=eidx, used=used.astype(jnp.int32), blo=blo, bhi=bhi, slo=slo, shi=shi,
                n_tiles=n_tiles)


def _expert_kernel(eidx_ref, used_ref, blo_ref, bhi_ref, h2_ref, pos_ref, wg_ref, wu_ref, wd_ref,
                   ys_ref, xs_s, acc_s, ax_s):
    i = pl.program_id(0)
    j = pl.program_id(1)
    used = used_ref[i] > 0
    tb = TB_MOE
    sub_per_tile = TS_MOE // tb

    @pl.when(used & (j == 0))
    def _():
        r_idx = _iota((tb, tb), 0)
        for half in range(sub_per_tile):
            s = i * sub_per_tile + half
            base = s * tb
            ax_s[...] = jnp.zeros_like(ax_s)

            def gather(b, carry):
                sel = jnp.where(pos_ref[b] - base == r_idx, 1.0, 0.0).astype(BF16)
                tok = pl.ds(pl.multiple_of(b * tb, tb), tb)
                ax_s[...] += _mm(sel, h2_ref[tok, :])
                return carry

            lax.fori_loop(blo_ref[s], bhi_ref[s] + 1, gather, 0)
            xs_s[half * tb:(half + 1) * tb, :] = ax_s[...].astype(BF16)
        acc_s[...] = jnp.zeros_like(acc_s)

    @pl.when(used)
    def _():
        x = xs_s[:, 0:D_MODEL]
        pieces = xs_s[:, D_MODEL:H2_WIDTH].astype(F32)
        ge = jnp.sum(jnp.where(_iota(pieces.shape, 1) % N_EXPERTS == j, pieces, 0.0), axis=-1, keepdims=True)
        hh = _silu(_mm(x, wg_ref[0, 0].astype(BF16))) * _mm(x, wu_ref[0, 0].astype(BF16)) * ge
        acc_s[...] += _mm(hh.astype(BF16), wd_ref[0, 0].astype(BF16))

    @pl.when(j == E_PER_GROUP - 1)
    def _():
        ys_ref[...] = jnp.where(used, acc_s[...], 0.0).astype(BF16)


def _expert_call(layer, plan, h2, w_gate, w_up, w_down):
    t_all = h2.shape[0]
    tb, ts = TB_MOE, TS_MOE
    n_tiles = plan["n_tiles"]
    pos_rows = plan["pos"].reshape(t_all // tb, 1, tb)
    weight = lambda shape: pl.BlockSpec((1, 1) + shape, lambda i, j, eidx, *_: (layer, eidx[i * E_PER_GROUP + j], 0, 0))
    grid_spec = pltpu.PrefetchScalarGridSpec(
        num_scalar_prefetch=4,
        grid=(n_tiles, E_PER_GROUP),
        in_specs=[
            _const_spec((t_all, H2_WIDTH)), _const_spec(pos_rows.shape),
            weight((D_MODEL, D_EXPERT)), weight((D_MODEL, D_EXPERT)), weight((D_EXPERT, D_MODEL)),
        ],
        out_specs=pl.BlockSpec((ts, D_MODEL), lambda i, j, *_: (i, 0)),
        scratch_shapes=[pltpu.VMEM((ts, H2_WIDTH), BF16), pltpu.VMEM((ts, D_MODEL), F32),
                        pltpu.VMEM((tb, H2_WIDTH), F32)],
    )
    return pl.pallas_call(
        _expert_kernel,
        out_shape=jax.ShapeDtypeStruct((n_tiles * ts, D_MODEL), BF16),
        grid_spec=grid_spec,
        compiler_params=_params(("arbitrary", "arbitrary")),
        name=f"experts_{layer}",
    )(plan["eidx"], plan["used"], plan["blo"], plan["bhi"], h2, pos_rows, w_gate, w_up, w_down)


def _combine_kernel(final, n0_tiles, slo_ref, shi_ref, ys_ref, pos_ref, x1_ref, mod_ref, nf_ref,
                    op_ref, os_ref, acc_s):
    b = pl.program_id(0)
    d = D_MODEL
    tb = TB_MOE
    acc_s[...] = jnp.zeros_like(acc_s)
    c_idx = _iota((tb, tb), 1)
    pos = pos_ref[...]
    for g in range(N_GROUPS):
        def scatter(s, carry):
            sel = jnp.where(pos - s * tb == c_idx, 1.0, 0.0).astype(BF16)
            acc_s[...] += _mm(sel, ys_ref[pl.ds(pl.multiple_of(s * tb, tb), tb), :])
            return carry

        lax.fori_loop(slo_ref[b * N_GROUPS + g], shi_ref[b * N_GROUPS + g] + 1, scatter, 0)

    def result():
        x2 = x1_ref[...] + mod_ref[0, 0][:, 5 * d:6 * d] * acc_s[...]
        if final:
            x2 = x2 * lax.rsqrt(jnp.mean(x2 * x2, axis=-1, keepdims=True) + EPS) * nf_ref[...]
        return x2

    @pl.when(b < n0_tiles)
    def _():
        op_ref[...] = result()

    @pl.when(b >= n0_tiles)
    def _():
        os_ref[...] = result()


def _combine_call(final, plan, ys_sorted, x1, mod_l, norm_final, n_prompt_tok, dec_seq):
    t_all = x1.shape[0]
    tb = TB_MOE
    n0_tiles = n_prompt_tok // tb
    row = _mod_row_map(tb, n_prompt_tok, dec_seq)
    grid_spec = pltpu.PrefetchScalarGridSpec(
        num_scalar_prefetch=2,
        grid=(t_all // tb,),
        in_specs=[
            _const_spec(ys_sorted.shape),
            pl.BlockSpec((tb, 1), lambda b, *_: (b, 0)),
            pl.BlockSpec((tb, D_MODEL), lambda b, *_: (b, 0)),
            pl.BlockSpec((1, 1, 1, N_MOD * D_MODEL), lambda b, *_: (0, row(b), 0, 0)),
            _const_spec((1, D_MODEL)),
        ],
        out_specs=_pair_specs(tb, D_MODEL, n0_tiles),
        scratch_shapes=[pltpu.VMEM((tb, D_MODEL), F32)],
    )
    return pl.pallas_call(
        functools.partial(_combine_kernel, final, n0_tiles),
        out_shape=[jax.ShapeDtypeStruct((n_prompt_tok, D_MODEL), F32),
                   jax.ShapeDtypeStruct((t_all - n_prompt_tok, D_MODEL), F32)],
        grid_spec=grid_spec,
        compiler_params=_params(("arbitrary",)),
        name="combine",
    )(plan["slo"], plan["shi"], ys_sorted, plan["pos"].reshape(t_all, 1), x1, mod_l, norm_final)


def _rg_gate_weights(wa, ba, wx, bx):
    eye = jnp.eye(RG_HEADS, dtype=wa.dtype)

    def dense(wd):
        return jnp.einsum("hij,hg->higj", wd, eye).reshape(RG_WIDTH, RG_WIDTH)

    w = jnp.concatenate([dense(wa[0]), dense(wa[1]), dense(wx[0]), dense(wx[1])], axis=1)
    b = jnp.concatenate([ba[0], ba[1], bx[0], bx[1]], axis=0).reshape(1, 4 * RG_WIDTH)
    return w, b


def kernel(x_prompt, x_sample, state_rglru, state_hgrn, state_ret, c, c_ctx, norm1, norm2, norm_final, w_ada, b_ada, w_in, w_out, hy_conv_w, hy_conv_b, hy_w1, hy_b1, hy_w2, hy_b2, hy_w3, hy_d, rg_conv_w, rg_conv_b, rg_wa, rg_ba, rg_wx, rg_bx, rg_lambda, hg_lb, hg_norm, ret_decay, w_router, router_bias, w_gate, w_up, w_down):
    batch, seq, d = x_prompt.shape
    dec_batch, dec_seq, _ = x_sample.shape
    assert d == D_MODEL and dec_batch + 1 <= COND_ROWS
    n_prompt_tok = batch * seq

    lb_cum = jnp.cumsum(jax.nn.softmax(hg_lb.astype(F32), axis=0), axis=0)
    lb_all = lb_cum - lb_cum[0:1]

    cond = jnp.zeros((COND_ROWS, d), F32).at[0].set(c_ctx).at[1:1 + dec_batch].set(c)
    mod = _modulation(cond, w_ada, b_ada).reshape(DEPTH, COND_ROWS, 1, N_MOD * d)

    w_in_b = w_in.astype(BF16)
    w_out_b = w_out.astype(BF16)

    passes = (
        dict(tok0=0, batch=batch, seq=seq, rope=False),
        dict(tok0=n_prompt_tok, batch=dec_batch, seq=dec_seq, rope=True),
    )
    filters = {p["seq"]: _hyena_filters(p["seq"], hy_w1, hy_b1, hy_w2, hy_b2, hy_w3) for p in passes}
    tables = {p["seq"]: _hyena_tables(p["seq"]) for p in passes}

    xp = x_prompt.reshape(-1, d)
    xs = x_sample.reshape(-1, d)
    new_rg, new_hg, new_ret = [], [], []
    for l in range(DEPTH):
        hy_all, rg_all, hg_all, ret_all = _in_projection(
            xp, xs, mod[l:l + 1], norm1[l].reshape(1, d), w_in_b[l], dec_seq)
        wg, bg = _rg_gate_weights(rg_wa[l], rg_ba[l], rg_wx[l], rg_bx[l])
        sp = jax.nn.softplus(-rg_lambda[l])
        ys = [[], [], [], []]
        for pi, p in enumerate(passes):
            first = pi == 0
            geom = (p["tok0"], p["batch"], p["seq"])
            y_hy = _hyena(hy_all, *geom, hy_conv_w[l], hy_conv_b[l].reshape(1, -1), hy_d[l],
                          filters[p["seq"]][l], tables[p["seq"]])
            y_rg, *st_rg = _rglru(rg_all, *geom, rg_conv_w[l], rg_conv_b[l].reshape(1, -1), wg, bg, sp,
                                  None if first else state_rglru[:, l], first)
            y_hg, *st_hg = _hgrn(hg_all, *geom, lb_all[l], hg_norm[l].reshape(1, -1),
                                 None if first else state_hgrn[:, l], first)
            y_ret, *st_ret = _retention(ret_all, *geom, p["rope"], ret_decay[l],
                                        None if first else state_ret[:, l], first)
            for lst, y in zip(ys, (y_hy, y_rg, y_hg, y_ret)):
                lst.append(y)
            if first:
                new_rg.append(st_rg[0])
                new_hg.append(st_hg[0])
                new_ret.append(st_ret[0])
        x1, h2, info, cend = _route_call(xp, xs, ys, mod[l:l + 1], w_out_b[l], norm2[l].reshape(1, d),
                                         w_router, router_bias, dec_seq)
        plan = _moe_plan(cend, info)
        ys_sorted = _expert_call(l, plan, h2, w_gate, w_up, w_down)
        xp, xs = _combine_call(l == DEPTH - 1, plan, ys_sorted, x1, mod[l:l + 1], norm_final.reshape(1, d),
                               n_prompt_tok, dec_seq)

    return (xp.reshape(batch, seq, d), xs.reshape(dec_batch, dec_seq, d), jnp.stack(new_rg, axis=1),
            jnp.stack(new_hg, axis=1), jnp.stack(new_ret, axis=1))
```

```python
import functools
import math

import numpy as np
import jax
import jax.numpy as jnp
from jax import lax
from jax.experimental import pallas as pl
from jax.experimental.pallas import tpu as pltpu

F32 = jnp.float32
BF16 = jnp.bfloat16

D_MODEL = 1024
DEPTH = 2
GRID_W = 64
HY_WIDTH = 256
RG_WIDTH = 256
HG_WIDTH = 256
RET_WIDTH = 256
MIX_WIDTH = 256
HY_ORDER = 2
HY_EMB = 33
HY_BANDS = 16
HY_FFN = 64
HY_DECAY_TARGET = 1e-2
HY_DECAY_SHORT = 0.3
HY_DECAY_LONG = 1.5
HY_GROUP_COLS = 512
RG_HEADS = 8
RG_HEAD_DIM = 32
RG_C = 8.0
RG_CHUNK = 16
N_HEADS = 4
HEAD_DIM = 64
HG_CHUNK = 64
RET_CHUNK = 256
ROPE_BASE = 10000.0
N_EXPERTS = 16
N_GROUPS = 4
E_PER_GROUP = 4
D_EXPERT = 512
N_MOD = 6
EPS = 1e-6
PROJ_HY = 3 * HY_WIDTH
PROJ_RG = 2 * RG_WIDTH
PROJ_HG = 5 * HG_WIDTH
PROJ_RET = 4 * RET_WIDTH
PROJ_WIDTH = PROJ_HY + PROJ_RG + PROJ_HG + PROJ_RET
COND_ROWS = 16
LANES = 128
VMEM_LIMIT = 56 * 1024 * 1024
TM_PROJ = 512
TB_MOE = 256
TM_ROUTE = 512
TS_MOE = 1024
TN_MOD = 1536

_NN = (((1,), (0,)), ((), ()))
_NT = (((1,), (1,)), ((), ()))


def _mm(a, b, dn=_NN):
    return lax.dot_general(a, b, dn, preferred_element_type=F32)


def _split2(x):
    hi = x.astype(BF16)
    lo = (x - hi.astype(F32)).astype(BF16)
    return hi, lo


def _split3(x):
    hi = x.astype(BF16)
    r = x - hi.astype(F32)
    mid = r.astype(BF16)
    lo = (r - mid.astype(F32)).astype(BF16)
    return hi, mid, lo


def _mm3(a, b, dn=_NN):
    ah, al = _split2(a)
    bh, bl = _split2(b)
    return _mm(ah, bh, dn) + (_mm(ah, bl, dn) + _mm(al, bh, dn))


def _mm_exact_rhs(a, b_bf16):
    a1, a2, a3 = _split3(a)
    return _mm(a1, b_bf16) + (_mm(a2, b_bf16) + _mm(a3, b_bf16))


def _sigmoid(x):
    return 1.0 / (1.0 + jnp.exp(-x))


def _silu(x):
    return x * _sigmoid(x)


def _log_sigmoid(x):
    return jnp.minimum(x, 0.0) - jnp.log(1.0 + jnp.exp(-jnp.abs(x)))


def _gelu_tanh(x):
    return 0.5 * x * (1.0 + jnp.tanh(math.sqrt(2.0 / math.pi) * (x + 0.044715 * (x * x * x))))


def _iota(shape, dim):
    return lax.broadcasted_iota(jnp.int32, shape, dim)


def _shift_rows(u, k, row):
    n = u.shape[0]
    if k == 0:
        return u
    r = pltpu.roll(u, (-k) % n, axis=0)
    if k < 0:
        return jnp.where(row >= -k, r, 0.0)
    return jnp.where(row < n - k, r, 0.0)


def _head_mask(n_rows_per_head):
    shape = (N_HEADS * n_rows_per_head, MIX_WIDTH)
    return (_iota(shape, 0) // n_rows_per_head) == (_iota(shape, 1) // HEAD_DIM)


def _block_diag_mask():
    shape = (MIX_WIDTH, MIX_WIDTH)
    return (_iota(shape, 0) // HEAD_DIM) == (_iota(shape, 1) // HEAD_DIM)


def _head_rmsnorm(o, ones_bd):
    ms = _mm_exact_rhs(o * o, ones_bd)
    return o * lax.rsqrt(ms + EPS)


def _load_state(s0_ref, d):
    zero = jnp.zeros((HEAD_DIM, HEAD_DIM), F32)
    rows = []
    for h in range(N_HEADS):
        blk = s0_ref[0, d, h].T
        rows.append(jnp.concatenate([blk if g == h else zero for g in range(N_HEADS)], axis=1))
    return jnp.concatenate(rows, axis=0)


def _store_state(st_ref, d, st):
    for h in range(N_HEADS):
        lo, hi = h * HEAD_DIM, (h + 1) * HEAD_DIM
        st_ref[0, d, h] = st[lo:hi, lo:hi].T


def _params(sem, vmem=VMEM_LIMIT):
    return pltpu.CompilerParams(dimension_semantics=sem, vmem_limit_bytes=vmem)


def _const_spec(shape):
    nd = len(shape)
    return pl.BlockSpec(shape, lambda *_: (0,) * nd, pipeline_mode=pl.Buffered(1))


def _mod_kernel(cond_ref, w_ref, b_ref, o_ref):
    o_ref[0] = _mm3(_silu(cond_ref[...]), w_ref[0]) + b_ref[0]


def _modulation(cond, w_ada, b_ada):
    tn = TN_MOD
    n_mod = N_MOD * D_MODEL
    return pl.pallas_call(
        _mod_kernel,
        out_shape=jax.ShapeDtypeStruct((DEPTH, COND_ROWS, n_mod), F32),
        grid=(DEPTH, n_mod // tn),
        in_specs=[
            pl.BlockSpec((COND_ROWS, D_MODEL), lambda l, j: (0, 0)),
            pl.BlockSpec((1, D_MODEL, tn), lambda l, j: (l, 0, j)),
            pl.BlockSpec((1, 1, tn), lambda l, j: (l, 0, j)),
        ],
        out_specs=pl.BlockSpec((1, COND_ROWS, tn), lambda l, j: (l, 0, j)),
        compiler_params=_params(("parallel", "parallel")),
        name="modulation",
    )(cond, w_ada, b_ada.reshape(DEPTH, 1, n_mod))


def _pair_specs(tm, width, n0_tiles):
    return [pl.BlockSpec((tm, width), lambda i, *_: (jnp.minimum(i, n0_tiles - 1), 0)),
            pl.BlockSpec((tm, width), lambda i, *_: (jnp.maximum(i - n0_tiles, 0), 0))]


def _proj_kernel(n0_tiles, xp_ref, xs_ref, mod_ref, n1_ref, w_ref, hy_ref, rg_ref, hg_ref, ret_ref):
    x = jnp.where(pl.program_id(0) < n0_tiles, xp_ref[...], xs_ref[...])
    mod = mod_ref[0, 0]
    sh1 = mod[:, 0:D_MODEL]
    sc1 = mod[:, D_MODEL:2 * D_MODEL]
    h = x * lax.rsqrt(jnp.mean(x * x, axis=-1, keepdims=True) + EPS) * n1_ref[...]
    h = (h * (1.0 + sc1) + sh1).astype(BF16)
    c0 = 0
    for ref, width in ((hy_ref, PROJ_HY), (rg_ref, PROJ_RG), (hg_ref, PROJ_HG), (ret_ref, PROJ_RET)):
        ref[...] = _mm(h, w_ref[:, c0:c0 + width])
        c0 += width


def _mod_row_map(tm, n_prompt_tok, dec_seq):
    n_prompt_tiles = n_prompt_tok // tm

    def row(i):
        return jnp.where(i < n_prompt_tiles, 0, 1 + (i * tm - n_prompt_tok) // dec_seq)

    return row


def _in_projection(xp, xs, mod_l, norm1_l, w_in_l, dec_seq):
    n_prompt_tok = xp.shape[0]
    t_all = n_prompt_tok + xs.shape[0]
    tm = TM_PROJ
    n0_tiles = n_prompt_tok // tm
    row = _mod_row_map(tm, n_prompt_tok, dec_seq)
    widths = (PROJ_HY, PROJ_RG, PROJ_HG, PROJ_RET)
    return pl.pallas_call(
        functools.partial(_proj_kernel, n0_tiles),
        out_shape=[jax.ShapeDtypeStruct((t_all, w), F32) for w in widths],
        grid=(t_all // tm,),
        in_specs=_pair_specs(tm, D_MODEL, n0_tiles) + [
            pl.BlockSpec((1, 1, 1, N_MOD * D_MODEL), lambda i: (0, row(i), 0, 0)),
            _const_spec((1, D_MODEL)),
            _const_spec((D_MODEL, PROJ_WIDTH)),
        ],
        out_specs=[pl.BlockSpec((tm, w), lambda i: (i, 0)) for w in widths],
        compiler_params=_params(("parallel",)),
        name="in_projection",
    )(xp, xs, mod_l, norm1_l, w_in_l)


def _dft_tables(seq):
    k = np.arange(seq, dtype=np.int64)
    m = (k[:, None] * k[None, :]) % (2 * seq)
    ang = np.pi * m.astype(np.float64) / seq
    return np.cos(ang), np.sin(ang)


def _hyena_tables(seq):
    cos, sin = _dft_tables(seq)
    sign = np.where(np.arange(seq) % 2 == 0, 1.0, -1.0)
    fwd = np.concatenate([cos, sign[None, :], sin[1:]], axis=0)
    wk = np.full((seq,), 2.0)
    wk[0] = 1.0
    inv_cos = (cos * wk[None, :]) / (2.0 * seq)
    inv_nyq = sign[:, None] / (2.0 * seq)
    inv_sin = 2.0 * sin[:, 1:] / (2.0 * seq)
    inv = np.concatenate([inv_cos, inv_nyq, inv_sin], axis=1)

    return jnp.asarray(fwd, F32).astype(BF16), jnp.asarray(inv, F32).astype(BF16)


def _filter_embedding(seq):
    t = np.arange(seq, dtype=np.float64)
    t_norm = t / max(seq - 1, 1)
    bands = np.linspace(1e-4, HY_BANDS - 1, HY_BANDS)
    ang = (2.0 * np.pi / seq) * t[:, None] * bands[None, :]
    z = np.concatenate([t_norm[:, None], np.cos(ang), np.sin(ang)], axis=-1)
    z = np.pad(z, ((0, 0), (0, LANES - HY_EMB)))
    deltas = np.abs(np.linspace(math.log(HY_DECAY_TARGET) / HY_DECAY_LONG,
                                math.log(HY_DECAY_TARGET) / HY_DECAY_SHORT, HY_WIDTH))
    window = np.exp(-t_norm[:, None] * deltas[None, :])
    return jnp.asarray(z, F32), jnp.asarray(window, F32)


def _filter_kernel(z_ref, win_ref, cos_ref, sin_ref, w1_ref, b1_ref, w2_ref, b2_ref, w3_ref, o_ref):
    seq = z_ref.shape[0]
    h = jnp.sin(_mm3(z_ref[...], w1_ref[0]) + b1_ref[0])
    h = jnp.sin(_mm3(h, w2_ref[0]) + b2_ref[0])
    h = _mm3(h, w3_ref[0])
    win = win_ref[...]
    row = _iota((seq, 1), 0)
    sums, diffs = [], []
    for o in range(HY_ORDER):
        c0 = o * 2 * HY_WIDTH
        hf = h[:, c0:c0 + HY_WIDTH] * win
        hb = h[:, c0 + HY_WIDTH:c0 + 2 * HY_WIDTH] * win
        ssq = jnp.sum(hf * hf + hb * hb, axis=0, keepdims=True)
        inv = lax.rsqrt(ssq + EPS)
        hf = hf * inv
        hb = jnp.where(row == 0, 0.0, hb * inv)
        sums.append(hf + hb)
        diffs.append(hf - hb)
    hsum = jnp.concatenate(sums, axis=1)
    hdiff = jnp.concatenate(diffs, axis=1)
    h_re = _mm3(cos_ref[...], hsum)
    h_im = _mm3(sin_ref[...], hdiff)
    sign = jnp.where(row % 2 == 0, 1.0, -1.0)
    h_nyq = jnp.sum(sign * hsum, axis=0, keepdims=True)
    o_ref[0, 0] = h_re
    o_ref[0, 1] = h_im
    o_ref[0, 2] = jnp.where(row == 0, h_nyq, h_re)


def _hyena_filters(seq, w1, b1, w2, b2, w3):
    z, window = _filter_embedding(seq)
    cos, sin = _dft_tables(seq)
    n_out = HY_ORDER * 2 * HY_WIDTH
    w1p = jnp.pad(w1, ((0, 0), (0, LANES - HY_EMB), (0, LANES - HY_FFN)))
    b1p = jnp.pad(b1, ((0, 0), (0, LANES - HY_FFN))).reshape(DEPTH, 1, LANES)
    w2p = jnp.pad(w2, ((0, 0), (0, LANES - HY_FFN), (0, LANES - HY_FFN)))
    b2p = jnp.pad(b2, ((0, 0), (0, LANES - HY_FFN))).reshape(DEPTH, 1, LANES)
    w3p = jnp.pad(w3, ((0, 0), (0, LANES - HY_FFN), (0, 0)))
    per_layer = lambda shape: pl.BlockSpec((1,) + shape, lambda l: (l,) + (0,) * len(shape))
    return pl.pallas_call(
        _filter_kernel,
        out_shape=jax.ShapeDtypeStruct((DEPTH, 3, seq, HY_ORDER * HY_WIDTH), F32),
        grid=(DEPTH,),
        in_specs=[
            _const_spec((seq, LANES)), _const_spec((seq, HY_WIDTH)),
            _const_spec((seq, seq)), _const_spec((seq, seq)),
            per_layer((LANES, LANES)), per_layer((1, LANES)),
            per_layer((LANES, LANES)), per_layer((1, LANES)),
            per_layer((LANES, n_out)),
        ],
        out_specs=pl.BlockSpec((1, 3, seq, HY_ORDER * HY_WIDTH), lambda l: (l, 0, 0, 0)),
        compiler_params=_params(("parallel",)),
        name=f"hyena_filters_{seq}",
    )(z, window, jnp.asarray(cos, F32), jnp.asarray(sin, F32), w1p, b1p, w2p, b2p, w3p)


def _hyena_kernel(group, u_ref, cw_ref, cb_ref, d_ref, filt_ref, fwd_ref, inv_ref, y_ref):
    seq = u_ref.shape[0] // group
    cw = cw_ref[...]
    row = _iota((seq, 1), 0)
    ucs = []
    for s in range(group):
        u = u_ref[s * seq:(s + 1) * seq, :]
        ucs.append(cb_ref[...] + cw[0:1] * _shift_rows(u, -1, row) + cw[1:2] * u + cw[2:3] * _shift_rows(u, 1, row))
    side = lambda c0: jnp.concatenate([uc[:, c0:c0 + HY_WIDTH] for uc in ucs], axis=1)
    tile = lambda a: jnp.concatenate([a] * group, axis=1)
    gates = (side(HY_WIDTH), side(2 * HY_WIDTH))
    d = d_ref[...]
    z = side(0)
    for o in range(HY_ORDER):
        c0 = o * HY_WIDTH
        spec = _mm(fwd_ref[...], z.astype(BF16))
        s_re, s_im = spec[:seq], spec[seq:]
        a = tile(filt_ref[0, :, c0:c0 + HY_WIDTH])
        b = tile(filt_ref[1, :, c0:c0 + HY_WIDTH])
        c = tile(filt_ref[2, :, c0:c0 + HY_WIDTH])
        prod = jnp.concatenate([s_re * a - s_im * b, s_re * b + s_im * c], axis=0)
        conv = _mm(inv_ref[...], prod.astype(BF16))
        z = gates[o] * (conv + tile(d[o:o + 1]) * z)
    for s in range(group):
        y_ref[s * seq:(s + 1) * seq, :] = z[:, s * HY_WIDTH:(s + 1) * HY_WIDTH]


def _mixer_call(body, name, proj_all, tok0, batch, seq, consts, batch_ins, state_shapes, scratch, group=1):
    proj_width = proj_all.shape[1]
    rows = group * seq
    blk0 = tok0 // rows

    def batch_spec(shape):
        nd = len(shape)
        return pl.BlockSpec((group,) + tuple(shape[1:]), lambda b: (b,) + (0,) * (nd - 1))

    in_specs = [pl.BlockSpec((rows, proj_width), lambda b: (blk0 + b, 0))]
    in_specs += [_const_spec(a.shape) for a in consts]
    in_specs += [batch_spec(a.shape) for a in batch_ins]
    out_shape = [jax.ShapeDtypeStruct((batch * seq, MIX_WIDTH), F32)]
    out_specs = [pl.BlockSpec((rows, MIX_WIDTH), lambda b: (b, 0))]
    for shape in state_shapes:
        out_shape.append(jax.ShapeDtypeStruct(shape, F32))
        out_specs.append(batch_spec(shape))
    return pl.pallas_call(
        body, out_shape=out_shape, grid=(batch // group,), in_specs=in_specs, out_specs=out_specs,
        scratch_shapes=scratch,
        compiler_params=_params(("parallel",)),
        name=f"{name}_{seq}",
    )(proj_all, *consts, *batch_ins)


def _hyena(u_all, tok0, batch, seq, conv_w, conv_b, d_bias, filt_l, tables):
    consts = [conv_w, conv_b, d_bias, filt_l, *tables]
    group = HY_GROUP_COLS // HY_WIDTH
    return _mixer_call(functools.partial(_hyena_kernel, group), "hyena", u_all, tok0, batch, seq, consts,
                       [], [], [], group=group)[0]


def _rglru_kernel(has_s0, emit_state, rg_ref, cw_ref, cb_ref, wg_ref, bg_ref, sp_ref, *refs):
    h0_ref = refs[0] if has_s0 else None
    y_ref = refs[1 if has_s0 else 0]
    st_ref = refs[-1] if emit_state else None
    seq = rg_ref.shape[0]
    w = RG_WIDTH
    xr = rg_ref[:, 0:w]
    gate = rg_ref[:, w:2 * w]
    row = _iota((seq, 1), 0)
    cw = cw_ref[...]
    xc = (cb_ref[...] + cw[0:1] * _shift_rows(xr, -2, row) + cw[1:2] * _shift_rows(xr, -1, row)
          + cw[2:3] * xr + cw[3:4] * _shift_rows(xr, 1, row))
    g = _sigmoid(_mm3(xc, wg_ref[...]) + bg_ref[...])
    sp = sp_ref[...]
    c = RG_CHUNK
    n = seq // c
    pos = row % c
    hs = []
    for d in range(2):
        forward = d == 0
        r = g[:, d * w:(d + 1) * w]
        i = g[:, (2 + d) * w:(3 + d) * w]
        log_a = -RG_C * r * sp[d:d + 1]
        a = jnp.exp(log_a)
        b = jnp.sqrt(jnp.tanh(-log_a) * (1.0 + a * a)) * (i * xc)
        step = 1
        while step < c:
            keep = (pos >= step) if forward else (pos < c - step)
            shift = step if forward else seq - step
            a_s = pltpu.roll(a, shift, axis=0)
            b_s = pltpu.roll(b, shift, axis=0)
            b = jnp.where(keep, a * b_s + b, b)
            a = jnp.where(keep, a * a_s, a)
            step *= 2
        a3 = a.reshape(n, c, w)
        b3 = b.reshape(n, c, w)
        edge = c - 1 if forward else 0
        a_end = a3[:, edge:edge + 1, :]
        b_end = b3[:, edge:edge + 1, :]
        h = h0_ref[0, d:d + 1, :] if has_s0 else jnp.zeros((1, w), F32)
        h_in = [None] * n
        for ci in (range(n) if forward else range(n - 1, -1, -1)):
            h_in[ci] = h
            h = b_end[ci] + a_end[ci] * h
        hs.append((b3 + a3 * jnp.stack(h_in, axis=0)).reshape(seq, w))
    y_ref[...] = (hs[0] + hs[1]) * _gelu_tanh(gate)
    if emit_state:
        st_ref[0, 0:1, :] = hs[0][seq - 1:seq]
        st_ref[0, 1:2, :] = hs[1][0:1]


def _rglru(rg_all, tok0, batch, seq, conv_w, conv_b, w_gates, b_gates, softplus_neg_lam, h0, emit_state):
    has_s0 = h0 is not None
    consts = [conv_w, conv_b, w_gates, b_gates, softplus_neg_lam]
    return _mixer_call(functools.partial(_rglru_kernel, has_s0, emit_state), "rglru", rg_all, tok0, batch, seq,
                       consts, [h0] if has_s0 else [],
                       [(batch, 2, RG_WIDTH)] if emit_state else [], [])


def _hgrn_kernel(has_s0, emit_state, hg_ref, lb_ref, gain_ref, *refs):
    s0_ref = refs[0] if has_s0 else None
    y_ref = refs[1 if has_s0 else 0]
    st_ref = refs[-1] if emit_state else None
    seq = hg_ref.shape[0]
    w = HG_WIDTH
    c = HG_CHUNK
    n = seq // c
    mid = c // 2
    chunks = lambda a: a.reshape(n, c, w)
    pos = _iota((seq, 1), 0) % c
    stack_mask = _head_mask(c)[None]
    bd_mask = _block_diag_mask()
    pair_shape = (c, N_HEADS * c)
    t_idx = _iota(pair_shape, 0)
    s_idx = _iota(pair_shape, 1) % c

    def stack_heads(a3):
        return jnp.where(stack_mask, jnp.concatenate([a3] * N_HEADS, axis=1), 0.0).astype(BF16)

    q3 = chunks(_silu(hg_ref[:, 0:w]))
    v3 = chunks(hg_ref[:, 3 * w:4 * w])
    v_stack = stack_heads(v3)
    v_t = jnp.swapaxes(v3, 1, 2).astype(BF16)
    lbv = lb_ref[...]
    o_sum = None
    finals = []
    for d in range(2):
        forward = d == 0
        f_pre = hg_ref[:, (1 + d) * w:(2 + d) * w]
        lo = lbv[d:d + 1]
        a1 = jnp.log(lo)
        a2 = jnp.log(1.0 - lo) + _log_sigmoid(f_pre)
        g = jnp.maximum(a1, a2) + jnp.log(1.0 + jnp.exp(-jnp.abs(a1 - a2)))
        k3 = chunks((1.0 - lo) * _sigmoid(-f_pre))
        step = 1
        while step < c:
            if forward:
                g = jnp.where(pos >= step, g + pltpu.roll(g, step, axis=0), g)
            else:
                g = jnp.where(pos < c - step, g + pltpu.roll(g, seq - step, axis=0), g)
            step *= 2
        g3 = chunks(g)
        g_ref = g3[:, mid:mid + 1, :]
        g_tot = g3[:, c - 1:c, :] if forward else g3[:, 0:1, :]
        q_in = (q3 * jnp.exp(g3)).astype(BF16)
        q_sc = (q3 * jnp.exp(g3 - g_ref)).astype(BF16)
        k_stack = stack_heads(k3 * jnp.exp(g_ref - g3))
        k_out = (k3 * jnp.exp(g_tot - g3)).astype(BF16)
        pair = jnp.einsum("ntl,nrl->ntr", q_sc, k_stack, preferred_element_type=F32)
        causal = (t_idx >= s_idx) if forward else (t_idx <= s_idx)
        pair = jnp.where(causal[None], pair, 0.0).astype(BF16)
        o_intra = jnp.einsum("ntr,nrv->ntv", pair, v_stack, preferred_element_type=F32)
        upd = jnp.einsum("nvs,nsk->nvk", v_t, k_out, preferred_element_type=F32)
        decay = jnp.exp(g_tot)
        st = _load_state(s0_ref, d) if has_s0 else jnp.zeros((w, w), F32)
        o_inter = [None] * n
        for ci in (range(n) if forward else range(n - 1, -1, -1)):
            o_inter[ci] = _mm(q_in[ci], st.astype(BF16), _NT)
            st = st * decay[ci] + jnp.where(bd_mask, upd[ci], 0.0)
        o_dir = o_intra + jnp.stack(o_inter, axis=0)
        o_sum = o_dir if o_sum is None else o_sum + o_dir
        finals.append(st)

    ones_bd = jnp.where(bd_mask, 1.0 / HEAD_DIM, 0.0).astype(BF16)
    o = _head_rmsnorm(o_sum.reshape(seq, w), ones_bd) * gain_ref[...]
    y_ref[...] = o * _silu(hg_ref[:, 4 * w:5 * w])
    if emit_state:
        _store_state(st_ref, 0, finals[0])
        _store_state(st_ref, 1, finals[1])


def _state_shape(batch):
    return (batch, 2, N_HEADS, HEAD_DIM, HEAD_DIM)


def _hgrn(hg_all, tok0, batch, seq, lb_l, gain_l, s0, emit_state):
    has_s0 = s0 is not None
    return _mixer_call(functools.partial(_hgrn_kernel, has_s0, emit_state), "hgrn2", hg_all, tok0, batch, seq,
                       [lb_l, gain_l], [s0] if has_s0 else [],
                       [_state_shape(batch)] if emit_state else [], [])


def _ret_kernel(use_rope, has_s0, emit_state, ret_ref, cos_ref, sin_ref, dl_ref, dlp_ref, *refs):
    s0_ref = refs[0] if has_s0 else None
    y_ref = refs[1 if has_s0 else 0]
    st_ref = refs[-1] if emit_state else None
    seq = ret_ref.shape[0]
    w = RET_WIDTH
    c = min(RET_CHUNK, seq)
    n = seq // c
    chunks = lambda a: a.reshape(n, c, w)
    q = ret_ref[:, 0:w]
    k = ret_ref[:, w:2 * w] * (HEAD_DIM ** -0.5)
    if use_rope:
        lane = _iota((seq, w), 1)
        even = (lane % 2) == 0
        cos = cos_ref[...]
        sin = sin_ref[...]

        def rope(x):
            nxt = pltpu.roll(x, w - 1, axis=1)
            prv = pltpu.roll(x, 1, axis=1)
            return x * cos + jnp.where(even, nxt, prv) * sin

        q = rope(q)
        k = rope(k)
    q3, k3, v3 = chunks(q), chunks(k), chunks(ret_ref[:, 2 * w:3 * w])

    lg = _log_sigmoid(dl_ref[...])
    lgp = _log_sigmoid(dlp_ref[...])
    pair_shape = (c, N_HEADS * c)
    t_idx = _iota(pair_shape, 0)
    s_idx = _iota(pair_shape, 1) % c
    dist = (t_idx - s_idx).astype(F32)
    decay = jnp.exp(jnp.where(dist >= 0, dist * lgp[0:1], -dist * lgp[1:2]))
    decay = decay + jnp.where(dist == 0, 1.0, 0.0)
    stack_mask = _head_mask(c)[None]
    bd_mask = _block_diag_mask()
    pos = _iota((c, 1), 0).astype(F32)
    fc = float(c)

    def stack_heads(a3):
        return jnp.where(stack_mask, jnp.concatenate([a3] * N_HEADS, axis=1), 0.0).astype(BF16)

    pair = jnp.einsum("ntl,nrl->ntr", q3.astype(BF16), stack_heads(k3), preferred_element_type=F32)
    pair = (pair * decay[None]).astype(BF16)
    o_sum = jnp.einsum("ntr,nrv->ntv", pair, stack_heads(v3), preferred_element_type=F32)
    v_t = jnp.swapaxes(v3, 1, 2).astype(BF16)
    finals = []
    for d in range(2):
        forward = d == 0
        lgd = lg[d:d + 1]
        q_in = (q3 * jnp.exp(((pos + 1.0) if forward else (fc - pos)) * lgd)).astype(BF16)
        k_out = (k3 * jnp.exp(((fc - 1.0 - pos) if forward else pos) * lgd)).astype(BF16)
        upd = jnp.einsum("nvs,nsk->nvk", v_t, k_out, preferred_element_type=F32)
        chunk_decay = jnp.exp(fc * lgd)
        st = _load_state(s0_ref, d) if has_s0 else jnp.zeros((w, w), F32)
        o_inter = [None] * n
        for ci in (range(n) if forward else range(n - 1, -1, -1)):
            o_inter[ci] = _mm(q_in[ci], st.astype(BF16), _NT)
            st = st * chunk_decay + jnp.where(bd_mask, upd[ci], 0.0)
        o_sum = o_sum + jnp.stack(o_inter, axis=0)
        finals.append(st)

    ones_bd = jnp.where(bd_mask, 1.0 / HEAD_DIM, 0.0).astype(BF16)
    o = _head_rmsnorm(o_sum.reshape(seq, w), ones_bd)
    y_ref[...] = _silu(ret_ref[:, 3 * w:4 * w]) * o
    if emit_state:
        _store_state(st_ref, 0, finals[0])
        _store_state(st_ref, 1, finals[1])


def _rope_tables(seq):
    rows = seq // GRID_W
    row = np.repeat(np.arange(rows), GRID_W).astype(np.float64)
    col = (np.arange(seq) % GRID_W).astype(np.float64)
    n_freq = HEAD_DIM // 4
    inv_freq = ROPE_BASE ** (-np.arange(n_freq, dtype=np.float64) / n_freq)
    ang = np.concatenate([row[:, None] * inv_freq, col[:, None] * inv_freq], axis=-1)
    ang = np.repeat(ang, 2, axis=1)
    cos = np.tile(np.cos(ang), (1, N_HEADS))
    sin = np.tile(np.sin(ang) * np.where(np.arange(HEAD_DIM) % 2 == 0, -1.0, 1.0)[None, :], (1, N_HEADS))
    return jnp.asarray(cos, F32), jnp.asarray(sin, F32)


def _retention(ret_all, tok0, batch, seq, use_rope, decay_l, s0, emit_state):
    has_s0 = s0 is not None
    w = RET_WIDTH
    c = min(RET_CHUNK, seq)
    cos, sin = _rope_tables(seq)
    dl = jnp.repeat(decay_l, HEAD_DIM, axis=-1)
    dlp = jnp.repeat(decay_l, c, axis=-1)
    return _mixer_call(functools.partial(_ret_kernel, use_rope, has_s0, emit_state), "retention", ret_all, tok0,
                       batch, seq, [cos, sin, dl, dlp], [s0] if has_s0 else [],
                       [_state_shape(batch)] if emit_state else [], [])


def _route(h2, wrt_ref, rb_ref):
    logits = _mm3(wrt_ref[...], h2, _NT)
    tm = logits.shape[1]
    m = jnp.max(logits, axis=0, keepdims=True)
    e = jnp.exp(logits - m)
    probs = e / jnp.sum(e, axis=0, keepdims=True)
    sel = probs + rb_ref[...]
    lane = _iota((N_EXPERTS, tm), 0).astype(F32)
    group = (_iota((N_EXPERTS, tm), 0) // E_PER_GROUP).astype(F32)
    neg = -jnp.inf

    def first_argmax(vals):
        mx = jnp.max(vals, axis=0, keepdims=True)
        idx = jnp.min(jnp.where(vals == mx, lane, float(N_EXPERTS)), axis=0, keepdims=True)
        return mx, idx

    best_score = None
    best = None
    for g in range(N_GROUPS):
        vals = jnp.where(group == float(g), sel, neg)
        m1, i1 = first_argmax(vals)
        m2, _ = first_argmax(jnp.where(lane == i1, neg, vals))
        score = m1 + m2
        if g == 0:
            best_score, best = score, jnp.zeros_like(score)
        else:
            take = score > best_score
            best = jnp.where(take, float(g), best)
            best_score = jnp.where(take, score, best_score)
    vals = jnp.where(group == best, sel, neg)
    _, i1 = first_argmax(vals)
    _, i2 = first_argmax(jnp.where(lane == i1, neg, vals))
    chosen = (lane == i1) | (lane == i2)
    picked = jnp.where(chosen, probs, 0.0)
    return picked / jnp.sum(picked, axis=0, keepdims=True), best


INFO_ROWS = 8
INFO_GROUP = 4
INFO_RANK = 5
H2_WIDTH = D_MODEL + LANES


def _route_kernel(n0_tiles, xp_ref, xs_ref, *refs):
    y_refs = refs[:8]
    mod_ref, wo_ref, n2_ref, wrt_ref, rb_ref, x1_ref, h2_ref, info_ref, cend_ref, carry_s = refs[8:]
    i = pl.program_id(0)
    first = i < n0_tiles
    d = D_MODEL
    tb = TB_MOE
    mod = mod_ref[0, 0]

    @pl.when(i == 0)
    def _():
        carry_s[...] = jnp.zeros_like(carry_s)

    tm = TM_ROUTE
    mixed = None
    for j in range(4):
        y = jnp.where(first, y_refs[2 * j][...], y_refs[2 * j + 1][...])
        part = _mm(y.astype(BF16), wo_ref[j * MIX_WIDTH:(j + 1) * MIX_WIDTH, :])
        mixed = part if mixed is None else mixed + part
    x = jnp.where(first, xp_ref[...], xs_ref[...])
    x1 = x + mod[:, 2 * d:3 * d] * mixed
    x1_ref[...] = x1
    h2 = x1 * lax.rsqrt(jnp.mean(x1 * x1, axis=-1, keepdims=True) + EPS) * n2_ref[...]
    h2 = h2 * (1.0 + mod[:, 4 * d:5 * d]) + mod[:, 3 * d:4 * d]
    h2_ref[:, 0:d] = h2.astype(BF16)

    gates, best = _route(h2, wrt_ref, rb_ref)
    expert = _iota((N_EXPERTS, tm), 0).astype(F32)
    g4 = [jnp.sum(jnp.where(expert == E_PER_GROUP * best + j, gates, 0.0), axis=0, keepdims=True)
          for j in range(E_PER_GROUP)]
    onehot = jnp.where(expert == best, 1.0, 0.0)
    upper = (_iota((tm, tm), 0) <= _iota((tm, tm), 1)).astype(BF16)
    carry = carry_s[...]
    incl = _mm(onehot.astype(BF16), upper) + carry[:, 0:1]
    rank = jnp.sum(onehot * incl, axis=0, keepdims=True) - 1.0
    info_ref[...] = jnp.concatenate(g4 + [best, rank, jnp.zeros((INFO_ROWS - 6, tm), F32)], axis=0)
    for h in range(tm // tb):
        end = (h + 1) * tb
        cend_ref[h] = jnp.broadcast_to(incl[:, end - 1:end], (N_EXPERTS, LANES))
    carry_s[...] = jnp.broadcast_to(incl[:, tm - 1:tm], (N_EXPERTS, LANES))

    pieces = _split3(jnp.concatenate(g4 + [jnp.zeros((N_EXPERTS - E_PER_GROUP, tm), F32)], axis=0))
    pieces = jnp.concatenate(list(pieces) + [jnp.zeros((LANES - 3 * N_EXPERTS, tm), BF16)], axis=0)
    eye = (_iota((tm, tm), 0) == _iota((tm, tm), 1)).astype(BF16)
    h2_ref[:, d:d + LANES] = _mm(eye, pieces, _NT).astype(BF16)


def _route_call(xp, xs, ys, mod_l, w_out_l, norm2_l, w_router, router_bias, dec_seq):
    n_prompt_tok = xp.shape[0]
    t_all = n_prompt_tok + xs.shape[0]
    tm = TM_ROUTE
    sub = tm // TB_MOE
    n0_tiles = n_prompt_tok // tm
    row = _mod_row_map(tm, n_prompt_tok, dec_seq)
    tok = lambda width: pl.BlockSpec((tm, width), lambda i: (i, 0))
    y_specs, y_args = [], []
    for yp, ysm in ys:
        y_specs += _pair_specs(tm, MIX_WIDTH, n0_tiles)
        y_args += [yp, ysm]
    return pl.pallas_call(
        functools.partial(_route_kernel, n0_tiles),
        out_shape=[jax.ShapeDtypeStruct((t_all, D_MODEL), F32), jax.ShapeDtypeStruct((t_all, H2_WIDTH), BF16),
                   jax.ShapeDtypeStruct((INFO_ROWS, t_all), F32),
                   jax.ShapeDtypeStruct((t_all // TB_MOE, N_EXPERTS, LANES), F32)],
        grid=(t_all // tm,),
        in_specs=_pair_specs(tm, D_MODEL, n0_tiles) + y_specs + [
            pl.BlockSpec((1, 1, 1, N_MOD * D_MODEL), lambda i: (0, row(i), 0, 0)),
            _const_spec((D_MODEL, D_MODEL)), _const_spec((1, D_MODEL)),
            _const_spec((N_EXPERTS, D_MODEL)), _const_spec((N_EXPERTS, 1)),
        ],
        out_specs=[tok(D_MODEL), tok(H2_WIDTH), pl.BlockSpec((INFO_ROWS, tm), lambda i: (0, i)),
                   pl.BlockSpec((sub, N_EXPERTS, LANES), lambda i: (i, 0, 0))],
        scratch_shapes=[pltpu.VMEM((N_EXPERTS, LANES), F32)],
        compiler_params=_params(("arbitrary",)),
        name="route",
    )(xp, xs, *y_args, mod_l, w_out_l, norm2_l, w_router.T, router_bias.reshape(N_EXPERTS, 1))


def _moe_plan(cend, info):
    tb, ts = TB_MOE, TS_MOE
    n_blocks = cend.shape[0]
    t_all = n_blocks * tb
    n_tiles = t_all // ts + N_GROUPS
    sub_per_tile = ts // tb
    n_sub = n_tiles * sub_per_tile
    cend = cend[:, :N_GROUPS, 0].astype(jnp.int32)
    cum = jnp.concatenate([jnp.zeros((1, N_GROUPS), jnp.int32), cend[:-1]], axis=0)
    cnt = cend - cum
    grp = info[INFO_GROUP].astype(jnp.int32)
    rank = info[INFO_RANK].astype(jnp.int32)
    tot = cend[-1]
    padded = ((tot + ts - 1) // ts) * ts
    off = jnp.cumsum(padded) - padded
    pos = off[grp] + rank

    sub_start = jnp.arange(n_sub, dtype=jnp.int32) * tb
    in_g = (sub_start[:, None] >= off[None, :]) & (sub_start[:, None] < (off + padded)[None, :])
    g_of = jnp.argmax(in_g, axis=1).astype(jnp.int32)
    r0 = sub_start - off[g_of]
    r1 = jnp.minimum(r0 + tb, tot[g_of])
    live = jnp.any(in_g, axis=1) & (r1 > r0)
    cend_g = cend[:, g_of]
    blo = jnp.where(live, jnp.sum(cend_g <= r0[None, :], axis=0), 0).astype(jnp.int32)
    bhi = jnp.where(live, jnp.sum(cend_g <= (r1 - 1)[None, :], axis=0), -1).astype(jnp.int32)

    used = jnp.any(in_g, axis=1)[::sub_per_tile]
    tile_group = g_of[::sub_per_tile]
    eidx = E_PER_GROUP * tile_group[:, None] + jnp.arange(E_PER_GROUP, dtype=jnp.int32)[None, :]
    n_used = jnp.sum(used.astype(jnp.int32))
    last = eidx[jnp.maximum(n_used - 1, 0), E_PER_GROUP - 1]
    eidx = jnp.where(used[:, None], eidx, last).reshape(-1).astype(jnp.int32)

    start = off[None, :] + cum
    end = off[None, :] + cend
    has = cnt > 0
    slo = jnp.where(has, start // tb, 0).reshape(-1).astype(jnp.int32)
    shi = jnp.where(has, (end - 1) // tb, -1).reshape(-1).astype(jnp.int32)
    return dict(pos=pos, eidx=eidx, used=used.astype(jnp.int32), blo=blo, bhi=bhi, slo=slo, shi=shi,
                n_tiles=n_tiles)


def _expert_kernel(eidx_ref, used_ref, blo_ref, bhi_ref, h2_ref, pos_ref, wg_ref, wu_ref, wd_ref,
                   ys_ref, xs_s, acc_s, ax_s):
    i = pl.program_id(0)
    j = pl.program_id(1)
    used = used_ref[i] > 0
    tb = TB_MOE
    sub_per_tile = TS_MOE // tb

    @pl.when(used & (j == 0))
    def _():
        r_idx = _iota((tb, tb), 0)
        for half in range(sub_per_tile):
            s = i * sub_per_tile + half
            base = s * tb
            ax_s[...] = jnp.zeros_like(ax_s)

            def gather(b, carry):
                sel = jnp.where(pos_ref[b] - base == r_idx, 1.0, 0.0).astype(BF16)
                tok = pl.ds(pl.multiple_of(b * tb, tb), tb)
                ax_s[...] += _mm(sel, h2_ref[tok, :])
                return carry

            lax.fori_loop(blo_ref[s], bhi_ref[s] + 1, gather, 0)
            xs_s[half * tb:(half + 1) * tb, :] = ax_s[...].astype(BF16)
        acc_s[...] = jnp.zeros_like(acc_s)

    @pl.when(used)
    def _():
        x = xs_s[:, 0:D_MODEL]
        pieces = xs_s[:, D_MODEL:H2_WIDTH].astype(F32)
        ge = jnp.sum(jnp.where(_iota(pieces.shape, 1) % N_EXPERTS == j, pieces, 0.0), axis=-1, keepdims=True)
        hh = _silu(_mm(x, wg_ref[0, 0].astype(BF16))) * _mm(x, wu_ref[0, 0].astype(BF16)) * ge
        acc_s[...] += _mm(hh.astype(BF16), wd_ref[0, 0].astype(BF16))

    @pl.when(j == E_PER_GROUP - 1)
    def _():
        ys_ref[...] = jnp.where(used, acc_s[...], 0.0).astype(BF16)


def _expert_call(layer, plan, h2, w_gate, w_up, w_down):
    t_all = h2.shape[0]
    tb, ts = TB_MOE, TS_MOE
    n_tiles = plan["n_tiles"]
    pos_rows = plan["pos"].reshape(t_all // tb, 1, tb)
    weight = lambda shape: pl.BlockSpec((1, 1) + shape, lambda i, j, eidx, *_: (layer, eidx[i * E_PER_GROUP + j], 0, 0))
    grid_spec = pltpu.PrefetchScalarGridSpec(
        num_scalar_prefetch=4,
        grid=(n_tiles, E_PER_GROUP),
        in_specs=[
            _const_spec((t_all, H2_WIDTH)), _const_spec(pos_rows.shape),
            weight((D_MODEL, D_EXPERT)), weight((D_MODEL, D_EXPERT)), weight((D_EXPERT, D_MODEL)),
        ],
        out_specs=pl.BlockSpec((ts, D_MODEL), lambda i, j, *_: (i, 0)),
        scratch_shapes=[pltpu.VMEM((ts, H2_WIDTH), BF16), pltpu.VMEM((ts, D_MODEL), F32),
                        pltpu.VMEM((tb, H2_WIDTH), F32)],
    )
    return pl.pallas_call(
        _expert_kernel,
        out_shape=jax.ShapeDtypeStruct((n_tiles * ts, D_MODEL), BF16),
        grid_spec=grid_spec,
        compiler_params=_params(("arbitrary", "arbitrary")),
        name=f"experts_{layer}",
    )(plan["eidx"], plan["used"], plan["blo"], plan["bhi"], h2, pos_rows, w_gate, w_up, w_down)


def _combine_kernel(final, n0_tiles, slo_ref, shi_ref, ys_ref, pos_ref, x1_ref, mod_ref, nf_ref,
                    op_ref, os_ref, acc_s):
    b = pl.program_id(0)
    d = D_MODEL
    tb = TB_MOE
    acc_s[...] = jnp.zeros_like(acc_s)
    c_idx = _iota((tb, tb), 1)
    pos = pos_ref[...]
    for g in range(N_GROUPS):
        def scatter(s, carry):
            sel = jnp.where(pos - s * tb == c_idx, 1.0, 0.0).astype(BF16)
            acc_s[...] += _mm(sel, ys_ref[pl.ds(pl.multiple_of(s * tb, tb), tb), :])
            return carry

        lax.fori_loop(slo_ref[b * N_GROUPS + g], shi_ref[b * N_GROUPS + g] + 1, scatter, 0)

    def result():
        x2 = x1_ref[...] + mod_ref[0, 0][:, 5 * d:6 * d] * acc_s[...]
        if final:
            x2 = x2 * lax.rsqrt(jnp.mean(x2 * x2, axis=-1, keepdims=True) + EPS) * nf_ref[...]
        return x2

    @pl.when(b < n0_tiles)
    def _():
        op_ref[...] = result()

    @pl.when(b >= n0_tiles)
    def _():
        os_ref[...] = result()


def _combine_call(final, plan, ys_sorted, x1, mod_l, norm_final, n_prompt_tok, dec_seq):
    t_all = x1.shape[0]
    tb = TB_MOE
    n0_tiles = n_prompt_tok // tb
    row = _mod_row_map(tb, n_prompt_tok, dec_seq)
    grid_spec = pltpu.PrefetchScalarGridSpec(
        num_scalar_prefetch=2,
        grid=(t_all // tb,),
        in_specs=[
            _const_spec(ys_sorted.shape),
            pl.BlockSpec((tb, 1), lambda b, *_: (b, 0)),
            pl.BlockSpec((tb, D_MODEL), lambda b, *_: (b, 0)),
            pl.BlockSpec((1, 1, 1, N_MOD * D_MODEL), lambda b, *_: (0, row(b), 0, 0)),
            _const_spec((1, D_MODEL)),
        ],
        out_specs=_pair_specs(tb, D_MODEL, n0_tiles),
        scratch_shapes=[pltpu.VMEM((tb, D_MODEL), F32)],
    )
    return pl.pallas_call(
        functools.partial(_combine_kernel, final, n0_tiles),
        out_shape=[jax.ShapeDtypeStruct((n_prompt_tok, D_MODEL), F32),
                   jax.ShapeDtypeStruct((t_all - n_prompt_tok, D_MODEL), F32)],
        grid_spec=grid_spec,
        compiler_params=_params(("arbitrary",)),
        name="combine",
    )(plan["slo"], plan["shi"], ys_sorted, plan["pos"].reshape(t_all, 1), x1, mod_l, norm_final)


def _rg_gate_weights(wa, ba, wx, bx):
    eye = jnp.eye(RG_HEADS, dtype=wa.dtype)

    def dense(wd):
        return jnp.einsum("hij,hg->higj", wd, eye).reshape(RG_WIDTH, RG_WIDTH)

    w = jnp.concatenate([dense(wa[0]), dense(wa[1]), dense(wx[0]), dense(wx[1])], axis=1)
    b = jnp.concatenate([ba[0], ba[1], bx[0], bx[1]], axis=0).reshape(1, 4 * RG_WIDTH)
    return w, b


def kernel(x_prompt, x_sample, state_rglru, state_hgrn, state_ret, c, c_ctx, norm1, norm2, norm_final, w_ada, b_ada, w_in, w_out, hy_conv_w, hy_conv_b, hy_w1, hy_b1, hy_w2, hy_b2, hy_w3, hy_d, rg_conv_w, rg_conv_b, rg_wa, rg_ba, rg_wx, rg_bx, rg_lambda, hg_lb, hg_norm, ret_decay, w_router, router_bias, w_gate, w_up, w_down):
    batch, seq, d = x_prompt.shape
    dec_batch, dec_seq, _ = x_sample.shape
    assert d == D_MODEL and dec_batch + 1 <= COND_ROWS
    n_prompt_tok = batch * seq

    lb_cum = jnp.cumsum(jax.nn.softmax(hg_lb.astype(F32), axis=0), axis=0)
    lb_all = lb_cum - lb_cum[0:1]

    cond = jnp.zeros((COND_ROWS, d), F32).at[0].set(c_ctx).at[1:1 + dec_batch].set(c)
    mod = _modulation(cond, w_ada, b_ada).reshape(DEPTH, COND_ROWS, 1, N_MOD * d)

    w_in_b = w_in.astype(BF16)
    w_out_b = w_out.astype(BF16)

    passes = (
        dict(tok0=0, batch=batch, seq=seq, rope=False),
        dict(tok0=n_prompt_tok, batch=dec_batch, seq=dec_seq, rope=True),
    )
    filters = {p["seq"]: _hyena_filters(p["seq"], hy_w1, hy_b1, hy_w2, hy_b2, hy_w3) for p in passes}
    tables = {p["seq"]: _hyena_tables(p["seq"]) for p in passes}

    xp = x_prompt.reshape(-1, d)
    xs = x_sample.reshape(-1, d)
    new_rg, new_hg, new_ret = [], [], []
    for l in range(DEPTH):
        hy_all, rg_all, hg_all, ret_all = _in_projection(
            xp, xs, mod[l:l + 1], norm1[l].reshape(1, d), w_in_b[l], dec_seq)
        wg, bg = _rg_gate_weights(rg_wa[l], rg_ba[l], rg_wx[l], rg_bx[l])
        sp = jax.nn.softplus(-rg_lambda[l])
        ys = [[], [], [], []]
        for pi, p in enumerate(passes):
            first = pi == 0
            geom = (p["tok0"], p["batch"], p["seq"])
            y_hy = _hyena(hy_all, *geom, hy_conv_w[l], hy_conv_b[l].reshape(1, -1), hy_d[l],
                          filters[p["seq"]][l], tables[p["seq"]])
            y_rg, *st_rg = _rglru(rg_all, *geom, rg_conv_w[l], rg_conv_b[l].reshape(1, -1), wg, bg, sp,
                                  None if first else state_rglru[:, l], first)
            y_hg, *st_hg = _hgrn(hg_all, *geom, lb_all[l], hg_norm[l].reshape(1, -1),
                                 None if first else state_hgrn[:, l], first)
            y_ret, *st_ret = _retention(ret_all, *geom, p["rope"], ret_decay[l],
                                        None if first else state_ret[:, l], first)
            for lst, y in zip(ys, (y_hy, y_rg, y_hg, y_ret)):
                lst.append(y)
            if first:
                new_rg.append(st_rg[0])
                new_hg.append(st_hg[0])
                new_ret.append(st_ret[0])
        x1, h2, info, cend = _route_call(xp, xs, ys, mod[l:l + 1], w_out_b[l], norm2[l].reshape(1, d),
                                         w_router, router_bias, dec_seq)
        plan = _moe_plan(cend, info)
        ys_sorted = _expert_call(l, plan, h2, w_gate, w_up, w_down)
        xp, xs = _combine_call(l == DEPTH - 1, plan, ys_sorted, x1, mod[l:l + 1], norm_final.reshape(1, d),
                               n_prompt_tok, dec_seq)

    return (xp.reshape(batch, seq, d), xs.reshape(dec_batch, dec_seq, d), jnp.stack(new_rg, axis=1),
            jnp.stack(new_hg, axis=1), jnp.stack(new_ret, axis=1))
```

```python
import functools
import math

import numpy as np
import jax
import jax.numpy as jnp
from jax import lax
from jax.experimental import pallas as pl
from jax.experimental.pallas import tpu as pltpu

F32 = jnp.float32
BF16 = jnp.bfloat16

D_MODEL = 1024
DEPTH = 2
GRID_W = 64
HY_WIDTH = 256
RG_WIDTH = 256
HG_WIDTH = 256
RET_WIDTH = 256
MIX_WIDTH = 256
HY_ORDER = 2
HY_EMB = 33
HY_BANDS = 16
HY_FFN = 64
HY_DECAY_TARGET = 1e-2
HY_DECAY_SHORT = 0.3
HY_DECAY_LONG = 1.5
HY_GROUP_COLS = 512
RG_HEADS = 8
RG_HEAD_DIM = 32
RG_C = 8.0
RG_CHUNK = 16
N_HEADS = 4
HEAD_DIM = 64
HG_CHUNK = 64
RET_CHUNK = 256
ROPE_BASE = 10000.0
N_EXPERTS = 16
N_GROUPS = 4
E_PER_GROUP = 4
D_EXPERT = 512
N_MOD = 6
EPS = 1e-6
PROJ_HY = 3 * HY_WIDTH
PROJ_RG = 2 * RG_WIDTH
PROJ_HG = 5 * HG_WIDTH
PROJ_RET = 4 * RET_WIDTH
PROJ_WIDTH = PROJ_HY + PROJ_RG + PROJ_HG + PROJ_RET
COND_ROWS = 16
LANES = 128
VMEM_LIMIT = 56 * 1024 * 1024
TM_PROJ = 512
TB_MOE = 512
TR_MOE = 256
TM_ROUTE = TB_MOE
TS_MOE = 1024
TN_MOD = 1536

_NN = (((1,), (0,)), ((), ()))
_NT = (((1,), (1,)), ((), ()))


def _mm(a, b, dn=_NN):
    return lax.dot_general(a, b, dn, preferred_element_type=F32)


def _split2(x):
    hi = x.astype(BF16)
    lo = (x - hi.astype(F32)).astype(BF16)
    return hi, lo


def _split3(x):
    hi = x.astype(BF16)
    r = x - hi.astype(F32)
    mid = r.astype(BF16)
    lo = (r - mid.astype(F32)).astype(BF16)
    return hi, mid, lo


def _mm3(a, b, dn=_NN):
    ah, al = _split2(a)
    bh, bl = _split2(b)
    return _mm(ah, bh, dn) + (_mm(ah, bl, dn) + _mm(al, bh, dn))


def _mm_exact_rhs(a, b_bf16):
    a1, a2, a3 = _split3(a)
    return _mm(a1, b_bf16) + (_mm(a2, b_bf16) + _mm(a3, b_bf16))


def _sigmoid(x):
    return 1.0 / (1.0 + jnp.exp(-x))


def _silu(x):
    return x * _sigmoid(x)


def _log_sigmoid(x):
    return jnp.minimum(x, 0.0) - jnp.log(1.0 + jnp.exp(-jnp.abs(x)))


def _gelu_tanh(x):
    return 0.5 * x * (1.0 + jnp.tanh(math.sqrt(2.0 / math.pi) * (x + 0.044715 * (x * x * x))))


def _iota(shape, dim):
    return lax.broadcasted_iota(jnp.int32, shape, dim)


def _shift_rows(u, k, row):
    n = u.shape[0]
    if k == 0:
        return u
    r = pltpu.roll(u, (-k) % n, axis=0)
    if k < 0:
        return jnp.where(row >= -k, r, 0.0)
    return jnp.where(row < n - k, r, 0.0)


def _head_mask(n_rows_per_head):
    shape = (N_HEADS * n_rows_per_head, MIX_WIDTH)
    return (_iota(shape, 0) // n_rows_per_head) == (_iota(shape, 1) // HEAD_DIM)


def _block_diag_mask():
    shape = (MIX_WIDTH, MIX_WIDTH)
    return (_iota(shape, 0) // HEAD_DIM) == (_iota(shape, 1) // HEAD_DIM)


def _head_rmsnorm(o, ones_bd):
    ms = _mm_exact_rhs(o * o, ones_bd)
    return o * lax.rsqrt(ms + EPS)


def _load_state(s0_ref, d):
    zero = jnp.zeros((HEAD_DIM, HEAD_DIM), F32)
    rows = []
    for h in range(N_HEADS):
        blk = s0_ref[0, d, h].T
        rows.append(jnp.concatenate([blk if g == h else zero for g in range(N_HEADS)], axis=1))
    return jnp.concatenate(rows, axis=0)


def _store_state(st_ref, d, st):
    for h in range(N_HEADS):
        lo, hi = h * HEAD_DIM, (h + 1) * HEAD_DIM
        st_ref[0, d, h] = st[lo:hi, lo:hi].T


def _params(sem, vmem=VMEM_LIMIT):
    return pltpu.CompilerParams(dimension_semantics=sem, vmem_limit_bytes=vmem)


def _const_spec(shape):
    nd = len(shape)
    return pl.BlockSpec(shape, lambda *_: (0,) * nd, pipeline_mode=pl.Buffered(1))


def _mod_kernel(cond_ref, w_ref, b_ref, o_ref):
    o_ref[0] = _mm3(_silu(cond_ref[...]), w_ref[0]) + b_ref[0]


def _modulation(cond, w_ada, b_ada):
    tn = TN_MOD
    n_mod = N_MOD * D_MODEL
    return pl.pallas_call(
        _mod_kernel,
        out_shape=jax.ShapeDtypeStruct((DEPTH, COND_ROWS, n_mod), F32),
        grid=(DEPTH, n_mod // tn),
        in_specs=[
            pl.BlockSpec((COND_ROWS, D_MODEL), lambda l, j: (0, 0)),
            pl.BlockSpec((1, D_MODEL, tn), lambda l, j: (l, 0, j)),
            pl.BlockSpec((1, 1, tn), lambda l, j: (l, 0, j)),
        ],
        out_specs=pl.BlockSpec((1, COND_ROWS, tn), lambda l, j: (l, 0, j)),
        compiler_params=_params(("parallel", "parallel")),
        name="modulation",
    )(cond, w_ada, b_ada.reshape(DEPTH, 1, n_mod))


def _pair_specs(tm, width, n0_tiles):
    return [pl.BlockSpec((tm, width), lambda i, *_: (jnp.minimum(i, n0_tiles - 1), 0)),
            pl.BlockSpec((tm, width), lambda i, *_: (jnp.maximum(i - n0_tiles, 0), 0))]


def _proj_kernel(n0_tiles, xp_ref, xs_ref, mod_ref, n1_ref, w_ref, hy_ref, rg_ref, hg_ref, ret_ref):
    x = jnp.where(pl.program_id(0) < n0_tiles, xp_ref[...], xs_ref[...])
    mod = mod_ref[0, 0]
    sh1 = mod[:, 0:D_MODEL]
    sc1 = mod[:, D_MODEL:2 * D_MODEL]
    h = x * lax.rsqrt(jnp.mean(x * x, axis=-1, keepdims=True) + EPS) * n1_ref[...]
    h = (h * (1.0 + sc1) + sh1).astype(BF16)
    c0 = 0
    for ref, width in ((hy_ref, PROJ_HY), (rg_ref, PROJ_RG), (hg_ref, PROJ_HG), (ret_ref, PROJ_RET)):
        ref[...] = _mm(h, w_ref[:, c0:c0 + width])
        c0 += width


def _mod_row_map(tm, n_prompt_tok, dec_seq):
    n_prompt_tiles = n_prompt_tok // tm

    def row(i):
        return jnp.where(i < n_prompt_tiles, 0, 1 + (i * tm - n_prompt_tok) // dec_seq)

    return row


def _in_projection(xp, xs, mod_l, norm1_l, w_in_l, dec_seq):
    n_prompt_tok = xp.shape[0]
    t_all = n_prompt_tok + xs.shape[0]
    tm = TM_PROJ
    n0_tiles = n_prompt_tok // tm
    row = _mod_row_map(tm, n_prompt_tok, dec_seq)
    widths = (PROJ_HY, PROJ_RG, PROJ_HG, PROJ_RET)
    return pl.pallas_call(
        functools.partial(_proj_kernel, n0_tiles),
        out_shape=[jax.ShapeDtypeStruct((t_all, w), F32) for w in widths],
        grid=(t_all // tm,),
        in_specs=_pair_specs(tm, D_MODEL, n0_tiles) + [
            pl.BlockSpec((1, 1, 1, N_MOD * D_MODEL), lambda i: (0, row(i), 0, 0)),
            _const_spec((1, D_MODEL)),
            _const_spec((D_MODEL, PROJ_WIDTH)),
        ],
        out_specs=[pl.BlockSpec((tm, w), lambda i: (i, 0)) for w in widths],
        compiler_params=_params(("parallel",)),
        name="in_projection",
    )(xp, xs, mod_l, norm1_l, w_in_l)


def _dft_tables(seq):
    k = np.arange(seq, dtype=np.int64)
    m = (k[:, None] * k[None, :]) % (2 * seq)
    ang = np.pi * m.astype(np.float64) / seq
    return np.cos(ang), np.sin(ang)


def _hyena_tables(seq):
    cos, sin = _dft_tables(seq)
    sign = np.where(np.arange(seq) % 2 == 0, 1.0, -1.0)
    fwd = np.concatenate([cos, sign[None, :], sin[1:]], axis=0)
    wk = np.full((seq,), 2.0)
    wk[0] = 1.0
    inv_cos = (cos * wk[None, :]) / (2.0 * seq)
    inv_nyq = sign[:, None] / (2.0 * seq)
    inv_sin = 2.0 * sin[:, 1:] / (2.0 * seq)
    inv = np.concatenate([inv_cos, inv_nyq, inv_sin], axis=1)

    return jnp.asarray(fwd, F32).astype(BF16), jnp.asarray(inv, F32).astype(BF16)


def _filter_embedding(seq):
    t = np.arange(seq, dtype=np.float64)
    t_norm = t / max(seq - 1, 1)
    bands = np.linspace(1e-4, HY_BANDS - 1, HY_BANDS)
    ang = (2.0 * np.pi / seq) * t[:, None] * bands[None, :]
    z = np.concatenate([t_norm[:, None], np.cos(ang), np.sin(ang)], axis=-1)
    z = np.pad(z, ((0, 0), (0, LANES - HY_EMB)))
    deltas = np.abs(np.linspace(math.log(HY_DECAY_TARGET) / HY_DECAY_LONG,
                                math.log(HY_DECAY_TARGET) / HY_DECAY_SHORT, HY_WIDTH))
    window = np.exp(-t_norm[:, None] * deltas[None, :])
    return jnp.asarray(z, F32), jnp.asarray(window, F32)


def _filter_kernel(z_ref, win_ref, cos_ref, sin_ref, w1_ref, b1_ref, w2_ref, b2_ref, w3_ref, o_ref):
    seq = z_ref.shape[0]
    h = jnp.sin(_mm3(z_ref[...], w1_ref[0]) + b1_ref[0])
    h = jnp.sin(_mm3(h, w2_ref[0]) + b2_ref[0])
    h = _mm3(h, w3_ref[0])
    win = win_ref[...]
    row = _iota((seq, 1), 0)
    sums, diffs = [], []
    for o in range(HY_ORDER):
        c0 = o * 2 * HY_WIDTH
        hf = h[:, c0:c0 + HY_WIDTH] * win
        hb = h[:, c0 + HY_WIDTH:c0 + 2 * HY_WIDTH] * win
        ssq = jnp.sum(hf * hf + hb * hb, axis=0, keepdims=True)
        inv = lax.rsqrt(ssq + EPS)
        hf = hf * inv
        hb = jnp.where(row == 0, 0.0, hb * inv)
        sums.append(hf + hb)
        diffs.append(hf - hb)
    hsum = jnp.concatenate(sums, axis=1)
    hdiff = jnp.concatenate(diffs, axis=1)
    h_re = _mm3(cos_ref[...], hsum)
    h_im = _mm3(sin_ref[...], hdiff)
    sign = jnp.where(row % 2 == 0, 1.0, -1.0)
    h_nyq = jnp.sum(sign * hsum, axis=0, keepdims=True)
    o_ref[0, 0] = h_re
    o_ref[0, 1] = h_im
    o_ref[0, 2] = jnp.where(row == 0, h_nyq, h_re)


def _hyena_filters(seq, w1, b1, w2, b2, w3):
    z, window = _filter_embedding(seq)
    cos, sin = _dft_tables(seq)
    n_out = HY_ORDER * 2 * HY_WIDTH
    w1p = jnp.pad(w1, ((0, 0), (0, LANES - HY_EMB), (0, LANES - HY_FFN)))
    b1p = jnp.pad(b1, ((0, 0), (0, LANES - HY_FFN))).reshape(DEPTH, 1, LANES)
    w2p = jnp.pad(w2, ((0, 0), (0, LANES - HY_FFN), (0, LANES - HY_FFN)))
    b2p = jnp.pad(b2, ((0, 0), (0, LANES - HY_FFN))).reshape(DEPTH, 1, LANES)
    w3p = jnp.pad(w3, ((0, 0), (0, LANES - HY_FFN), (0, 0)))
    per_layer = lambda shape: pl.BlockSpec((1,) + shape, lambda l: (l,) + (0,) * len(shape))
    return pl.pallas_call(
        _filter_kernel,
        out_shape=jax.ShapeDtypeStruct((DEPTH, 3, seq, HY_ORDER * HY_WIDTH), F32),
        grid=(DEPTH,),
        in_specs=[
            _const_spec((seq, LANES)), _const_spec((seq, HY_WIDTH)),
            _const_spec((seq, seq)), _const_spec((seq, seq)),
            per_layer((LANES, LANES)), per_layer((1, LANES)),
            per_layer((LANES, LANES)), per_layer((1, LANES)),
            per_layer((LANES, n_out)),
        ],
        out_specs=pl.BlockSpec((1, 3, seq, HY_ORDER * HY_WIDTH), lambda l: (l, 0, 0, 0)),
        compiler_params=_params(("parallel",)),
        name=f"hyena_filters_{seq}",
    )(z, window, jnp.asarray(cos, F32), jnp.asarray(sin, F32), w1p, b1p, w2p, b2p, w3p)


def _hyena_kernel(group, u_ref, cw_ref, cb_ref, d_ref, filt_ref, fwd_ref, inv_ref, y_ref):
    seq = u_ref.shape[0] // group
    cw = cw_ref[...]
    row = _iota((seq, 1), 0)
    ucs = []
    for s in range(group):
        u = u_ref[s * seq:(s + 1) * seq, :]
        ucs.append(cb_ref[...] + cw[0:1] * _shift_rows(u, -1, row) + cw[1:2] * u + cw[2:3] * _shift_rows(u, 1, row))
    side = lambda c0: jnp.concatenate([uc[:, c0:c0 + HY_WIDTH] for uc in ucs], axis=1)
    tile = lambda a: jnp.concatenate([a] * group, axis=1)
    gates = (side(HY_WIDTH), side(2 * HY_WIDTH))
    d = d_ref[...]
    z = side(0)
    for o in range(HY_ORDER):
        c0 = o * HY_WIDTH
        spec = _mm(fwd_ref[...], z.astype(BF16))
        s_re, s_im = spec[:seq], spec[seq:]
        a = tile(filt_ref[0, :, c0:c0 + HY_WIDTH])
        b = tile(filt_ref[1, :, c0:c0 + HY_WIDTH])
        c = tile(filt_ref[2, :, c0:c0 + HY_WIDTH])
        prod = jnp.concatenate([s_re * a - s_im * b, s_re * b + s_im * c], axis=0)
        conv = _mm(inv_ref[...], prod.astype(BF16))
        z = gates[o] * (conv + tile(d[o:o + 1]) * z)
    for s in range(group):
        y_ref[s * seq:(s + 1) * seq, :] = z[:, s * HY_WIDTH:(s + 1) * HY_WIDTH]


def _mixer_call(body, name, proj_all, tok0, batch, seq, consts, batch_ins, state_shapes, scratch, group=1):
    proj_width = proj_all.shape[1]
    rows = group * seq
    blk0 = tok0 // rows

    def batch_spec(shape):
        nd = len(shape)
        return pl.BlockSpec((group,) + tuple(shape[1:]), lambda b: (b,) + (0,) * (nd - 1))

    def layer_spec(shape, layer):
        nd = len(shape)
        return pl.BlockSpec((group, None) + tuple(shape[2:]), lambda b: (b, layer) + (0,) * (nd - 2))

    in_specs = [pl.BlockSpec((rows, proj_width), lambda b: (blk0 + b, 0))]
    in_specs += [_const_spec(a.shape) for a in consts]
    in_specs += [layer_spec(a.shape, layer) for a, layer in batch_ins]
    batch_ins = [a for a, _ in batch_ins]
    out_shape = [jax.ShapeDtypeStruct((batch * seq, MIX_WIDTH), F32)]
    out_specs = [pl.BlockSpec((rows, MIX_WIDTH), lambda b: (b, 0))]
    for shape in state_shapes:
        out_shape.append(jax.ShapeDtypeStruct(shape, F32))
        out_specs.append(batch_spec(shape))
    return pl.pallas_call(
        body, out_shape=out_shape, grid=(batch // group,), in_specs=in_specs, out_specs=out_specs,
        scratch_shapes=scratch,
        compiler_params=_params(("parallel",)),
        name=f"{name}_{seq}",
    )(proj_all, *consts, *batch_ins)


def _hyena(u_all, tok0, batch, seq, conv_w, conv_b, d_bias, filt_l, tables):
    consts = [conv_w, conv_b, d_bias, filt_l, *tables]
    group = HY_GROUP_COLS // HY_WIDTH
    return _mixer_call(functools.partial(_hyena_kernel, group), "hyena", u_all, tok0, batch, seq, consts,
                       [], [], [], group=group)[0]


def _rglru_kernel(has_s0, emit_state, rg_ref, cw_ref, cb_ref, wg_ref, bg_ref, sp_ref, *refs):
    h0_ref = refs[0] if has_s0 else None
    y_ref = refs[1 if has_s0 else 0]
    st_ref = refs[-1] if emit_state else None
    seq = rg_ref.shape[0]
    w = RG_WIDTH
    xr = rg_ref[:, 0:w]
    gate = rg_ref[:, w:2 * w]
    row = _iota((seq, 1), 0)
    cw = cw_ref[...]
    xc = (cb_ref[...] + cw[0:1] * _shift_rows(xr, -2, row) + cw[1:2] * _shift_rows(xr, -1, row)
          + cw[2:3] * xr + cw[3:4] * _shift_rows(xr, 1, row))
    g = _sigmoid(_mm3(xc, wg_ref[...]) + bg_ref[...])
    sp = sp_ref[...]
    c = RG_CHUNK
    n = seq // c
    pos = row % c
    hs = []
    for d in range(2):
        forward = d == 0
        r = g[:, d * w:(d + 1) * w]
        i = g[:, (2 + d) * w:(3 + d) * w]
        log_a = -RG_C * r * sp[d:d + 1]
        a = jnp.exp(log_a)
        b = jnp.sqrt(jnp.tanh(-log_a) * (1.0 + a * a)) * (i * xc)
        step = 1
        while step < c:
            keep = (pos >= step) if forward else (pos < c - step)
            shift = step if forward else seq - step
            a_s = pltpu.roll(a, shift, axis=0)
            b_s = pltpu.roll(b, shift, axis=0)
            b = jnp.where(keep, a * b_s + b, b)
            a = jnp.where(keep, a * a_s, a)
            step *= 2
        a3 = a.reshape(n, c, w)
        b3 = b.reshape(n, c, w)
        edge = c - 1 if forward else 0
        a_end = a3[:, edge:edge + 1, :]
        b_end = b3[:, edge:edge + 1, :]
        h = h0_ref[0, d:d + 1, :] if has_s0 else jnp.zeros((1, w), F32)
        h_in = [None] * n
        for ci in (range(n) if forward else range(n - 1, -1, -1)):
            h_in[ci] = h
            h = b_end[ci] + a_end[ci] * h
        hs.append((b3 + a3 * jnp.stack(h_in, axis=0)).reshape(seq, w))
    y_ref[...] = (hs[0] + hs[1]) * _gelu_tanh(gate)
    if emit_state:
        st_ref[0, 0:1, :] = hs[0][seq - 1:seq]
        st_ref[0, 1:2, :] = hs[1][0:1]


def _rglru(rg_all, tok0, batch, seq, conv_w, conv_b, w_gates, b_gates, softplus_neg_lam, h0, emit_state):
    has_s0 = h0 is not None
    consts = [conv_w, conv_b, w_gates, b_gates, softplus_neg_lam]
    return _mixer_call(functools.partial(_rglru_kernel, has_s0, emit_state), "rglru", rg_all, tok0, batch, seq,
                       consts, [h0] if has_s0 else [],
                       [(batch, 2, RG_WIDTH)] if emit_state else [], [])


def _hgrn_kernel(has_s0, emit_state, hg_ref, lb_ref, gain_ref, *refs):
    s0_ref = refs[0] if has_s0 else None
    y_ref = refs[1 if has_s0 else 0]
    st_ref = refs[-1] if emit_state else None
    seq = hg_ref.shape[0]
    w = HG_WIDTH
    c = HG_CHUNK
    n = seq // c
    mid = c // 2
    chunks = lambda a: a.reshape(n, c, w)
    pos = _iota((seq, 1), 0) % c
    stack_mask = _head_mask(c)[None]
    bd_mask = _block_diag_mask()
    pair_shape = (c, N_HEADS * c)
    t_idx = _iota(pair_shape, 0)
    s_idx = _iota(pair_shape, 1) % c

    def stack_heads(a3):
        return jnp.where(stack_mask, jnp.concatenate([a3] * N_HEADS, axis=1), 0.0).astype(BF16)

    q3 = chunks(_silu(hg_ref[:, 0:w]))
    v3 = chunks(hg_ref[:, 3 * w:4 * w])
    v_stack = stack_heads(v3)
    v_t = jnp.swapaxes(v3, 1, 2).astype(BF16)
    lbv = lb_ref[...]
    o_sum = None
    finals = []
    for d in range(2):
        forward = d == 0
        f_pre = hg_ref[:, (1 + d) * w:(2 + d) * w]
        lo = lbv[d:d + 1]
        a1 = jnp.log(lo)
        a2 = jnp.log(1.0 - lo) + _log_sigmoid(f_pre)
        g = jnp.maximum(a1, a2) + jnp.log(1.0 + jnp.exp(-jnp.abs(a1 - a2)))
        k3 = chunks((1.0 - lo) * _sigmoid(-f_pre))
        step = 1
        while step < c:
            if forward:
                g = jnp.where(pos >= step, g + pltpu.roll(g, step, axis=0), g)
            else:
                g = jnp.where(pos < c - step, g + pltpu.roll(g, seq - step, axis=0), g)
            step *= 2
        g3 = chunks(g)
        g_ref = g3[:, mid:mid + 1, :]
        g_tot = g3[:, c - 1:c, :] if forward else g3[:, 0:1, :]
        q_in = (q3 * jnp.exp(g3)).astype(BF16)
        q_sc = (q3 * jnp.exp(g3 - g_ref)).astype(BF16)
        k_stack = stack_heads(k3 * jnp.exp(g_ref - g3))
        k_out = (k3 * jnp.exp(g_tot - g3)).astype(BF16)
        pair = jnp.einsum("ntl,nrl->ntr", q_sc, k_stack, preferred_element_type=F32)
        causal = (t_idx >= s_idx) if forward else (t_idx <= s_idx)
        pair = jnp.where(causal[None], pair, 0.0).astype(BF16)
        o_intra = jnp.einsum("ntr,nrv->ntv", pair, v_stack, preferred_element_type=F32)
        upd = jnp.einsum("nvs,nsk->nvk", v_t, k_out, preferred_element_type=F32)
        decay = jnp.exp(g_tot)
        st = _load_state(s0_ref, d) if has_s0 else jnp.zeros((w, w), F32)
        o_inter = [None] * n
        for ci in (range(n) if forward else range(n - 1, -1, -1)):
            o_inter[ci] = _mm(q_in[ci], st.astype(BF16), _NT)
            st = st * decay[ci] + jnp.where(bd_mask, upd[ci], 0.0)
        o_dir = o_intra + jnp.stack(o_inter, axis=0)
        o_sum = o_dir if o_sum is None else o_sum + o_dir
        finals.append(st)

    ones_bd = jnp.where(bd_mask, 1.0 / HEAD_DIM, 0.0).astype(BF16)
    o = _head_rmsnorm(o_sum.reshape(seq, w), ones_bd) * gain_ref[...]
    y_ref[...] = o * _silu(hg_ref[:, 4 * w:5 * w])
    if emit_state:
        _store_state(st_ref, 0, finals[0])
        _store_state(st_ref, 1, finals[1])


def _state_shape(batch):
    return (batch, 2, N_HEADS, HEAD_DIM, HEAD_DIM)


def _hgrn(hg_all, tok0, batch, seq, lb_l, gain_l, s0, emit_state):
    has_s0 = s0 is not None
    return _mixer_call(functools.partial(_hgrn_kernel, has_s0, emit_state), "hgrn2", hg_all, tok0, batch, seq,
                       [lb_l, gain_l], [s0] if has_s0 else [],
                       [_state_shape(batch)] if emit_state else [], [])


def _ret_kernel(use_rope, has_s0, emit_state, ret_ref, cos_ref, sin_ref, dl_ref, dlp_ref, *refs):
    s0_ref = refs[0] if has_s0 else None
    y_ref = refs[1 if has_s0 else 0]
    st_ref = refs[-1] if emit_state else None
    seq = ret_ref.shape[0]
    w = RET_WIDTH
    c = min(RET_CHUNK, seq)
    n = seq // c
    chunks = lambda a: a.reshape(n, c, w)
    q = ret_ref[:, 0:w]
    k = ret_ref[:, w:2 * w] * (HEAD_DIM ** -0.5)
    if use_rope:
        lane = _iota((seq, w), 1)
        even = (lane % 2) == 0
        cos = cos_ref[...]
        sin = sin_ref[...]

        def rope(x):
            nxt = pltpu.roll(x, w - 1, axis=1)
            prv = pltpu.roll(x, 1, axis=1)
            return x * cos + jnp.where(even, nxt, prv) * sin

        q = rope(q)
        k = rope(k)
    q3, k3, v3 = chunks(q), chunks(k), chunks(ret_ref[:, 2 * w:3 * w])

    lg = _log_sigmoid(dl_ref[...])
    lgp = _log_sigmoid(dlp_ref[...])
    pair_shape = (c, N_HEADS * c)
    t_idx = _iota(pair_shape, 0)
    s_idx = _iota(pair_shape, 1) % c
    dist = (t_idx - s_idx).astype(F32)
    decay = jnp.exp(jnp.where(dist >= 0, dist * lgp[0:1], -dist * lgp[1:2]))
    decay = decay + jnp.where(dist == 0, 1.0, 0.0)
    stack_mask = _head_mask(c)[None]
    bd_mask = _block_diag_mask()
    pos = _iota((c, 1), 0).astype(F32)
    fc = float(c)

    def stack_heads(a3):
        return jnp.where(stack_mask, jnp.concatenate([a3] * N_HEADS, axis=1), 0.0).astype(BF16)

    pair = jnp.einsum("ntl,nrl->ntr", q3.astype(BF16), stack_heads(k3), preferred_element_type=F32)
    pair = (pair * decay[None]).astype(BF16)
    o_sum = jnp.einsum("ntr,nrv->ntv", pair, stack_heads(v3), preferred_element_type=F32)
    v_t = jnp.swapaxes(v3, 1, 2).astype(BF16)
    finals = []
    for d in range(2):
        forward = d == 0
        lgd = lg[d:d + 1]
        q_in = (q3 * jnp.exp(((pos + 1.0) if forward else (fc - pos)) * lgd)).astype(BF16)
        k_out = (k3 * jnp.exp(((fc - 1.0 - pos) if forward else pos) * lgd)).astype(BF16)
        upd = jnp.einsum("nvs,nsk->nvk", v_t, k_out, preferred_element_type=F32)
        chunk_decay = jnp.exp(fc * lgd)
        st = _load_state(s0_ref, d) if has_s0 else jnp.zeros((w, w), F32)
        o_inter = [None] * n
        for ci in (range(n) if forward else range(n - 1, -1, -1)):
            o_inter[ci] = _mm(q_in[ci], st.astype(BF16), _NT)
            st = st * chunk_decay + jnp.where(bd_mask, upd[ci], 0.0)
        o_sum = o_sum + jnp.stack(o_inter, axis=0)
        finals.append(st)

    ones_bd = jnp.where(bd_mask, 1.0 / HEAD_DIM, 0.0).astype(BF16)
    o = _head_rmsnorm(o_sum.reshape(seq, w), ones_bd)
    y_ref[...] = _silu(ret_ref[:, 3 * w:4 * w]) * o
    if emit_state:
        _store_state(st_ref, 0, finals[0])
        _store_state(st_ref, 1, finals[1])


def _rope_tables(seq):
    rows = seq // GRID_W
    row = np.repeat(np.arange(rows), GRID_W).astype(np.float64)
    col = (np.arange(seq) % GRID_W).astype(np.float64)
    n_freq = HEAD_DIM // 4
    inv_freq = ROPE_BASE ** (-np.arange(n_freq, dtype=np.float64) / n_freq)
    ang = np.concatenate([row[:, None] * inv_freq, col[:, None] * inv_freq], axis=-1)
    ang = np.repeat(ang, 2, axis=1)
    cos = np.tile(np.cos(ang), (1, N_HEADS))
    sin = np.tile(np.sin(ang) * np.where(np.arange(HEAD_DIM) % 2 == 0, -1.0, 1.0)[None, :], (1, N_HEADS))
    return jnp.asarray(cos, F32), jnp.asarray(sin, F32)


def _retention(ret_all, tok0, batch, seq, use_rope, decay_l, s0, emit_state):
    has_s0 = s0 is not None
    w = RET_WIDTH
    c = min(RET_CHUNK, seq)
    cos, sin = _rope_tables(seq)
    dl = jnp.repeat(decay_l, HEAD_DIM, axis=-1)
    dlp = jnp.repeat(decay_l, c, axis=-1)
    return _mixer_call(functools.partial(_ret_kernel, use_rope, has_s0, emit_state), "retention", ret_all, tok0,
                       batch, seq, [cos, sin, dl, dlp], [s0] if has_s0 else [],
                       [_state_shape(batch)] if emit_state else [], [])


def _route(h2, wrt_ref, rb_ref):
    logits = _mm3(wrt_ref[...], h2, _NT)
    tm = logits.shape[1]
    m = jnp.max(logits, axis=0, keepdims=True)
    e = jnp.exp(logits - m)
    probs = e / jnp.sum(e, axis=0, keepdims=True)
    sel = probs + rb_ref[...]
    lane = _iota((N_EXPERTS, tm), 0).astype(F32)
    group = (_iota((N_EXPERTS, tm), 0) // E_PER_GROUP).astype(F32)
    neg = -jnp.inf

    def first_argmax(vals):
        mx = jnp.max(vals, axis=0, keepdims=True)
        idx = jnp.min(jnp.where(vals == mx, lane, float(N_EXPERTS)), axis=0, keepdims=True)
        return mx, idx

    best_score = None
    best = None
    for g in range(N_GROUPS):
        vals = jnp.where(group == float(g), sel, neg)
        m1, i1 = first_argmax(vals)
        m2, _ = first_argmax(jnp.where(lane == i1, neg, vals))
        score = m1 + m2
        if g == 0:
            best_score, best = score, jnp.zeros_like(score)
        else:
            take = score > best_score
            best = jnp.where(take, float(g), best)
            best_score = jnp.where(take, score, best_score)
    vals = jnp.where(group == best, sel, neg)
    _, i1 = first_argmax(vals)
    _, i2 = first_argmax(jnp.where(lane == i1, neg, vals))
    chosen = (lane == i1) | (lane == i2)
    picked = jnp.where(chosen, probs, 0.0)
    return picked / jnp.sum(picked, axis=0, keepdims=True), best


INFO_ROWS = 8
INFO_GROUP = 4
INFO_RANK = 5
H2_WIDTH = D_MODEL + LANES


def _route_kernel(n0_tiles, xp_ref, xs_ref, *refs):
    y_refs = refs[:8]
    mod_ref, wo_ref, n2_ref, wrt_ref, rb_ref, x1_ref, h2_ref, info_ref, cend_ref, carry_s = refs[8:]
    i = pl.program_id(0)
    first = i < n0_tiles
    d = D_MODEL
    tb = TB_MOE
    mod = mod_ref[0, 0]

    @pl.when(i == 0)
    def _():
        carry_s[...] = jnp.zeros_like(carry_s)

    tm = TM_ROUTE
    mixed = None
    for j in range(4):
        y = jnp.where(first, y_refs[2 * j][...], y_refs[2 * j + 1][...])
        part = _mm(y.astype(BF16), wo_ref[j * MIX_WIDTH:(j + 1) * MIX_WIDTH, :])
        mixed = part if mixed is None else mixed + part
    x = jnp.where(first, xp_ref[...], xs_ref[...])
    x1 = x + mod[:, 2 * d:3 * d] * mixed
    x1_ref[...] = x1
    h2 = x1 * lax.rsqrt(jnp.mean(x1 * x1, axis=-1, keepdims=True) + EPS) * n2_ref[...]
    h2 = h2 * (1.0 + mod[:, 4 * d:5 * d]) + mod[:, 3 * d:4 * d]
    h2_ref[:, 0:d] = h2.astype(BF16)

    gates, best = _route(h2, wrt_ref, rb_ref)
    expert = _iota((N_EXPERTS, tm), 0).astype(F32)
    g4 = [jnp.sum(jnp.where(expert == E_PER_GROUP * best + j, gates, 0.0), axis=0, keepdims=True)
          for j in range(E_PER_GROUP)]
    onehot = jnp.where(expert == best, 1.0, 0.0)
    upper = (_iota((tm, tm), 0) <= _iota((tm, tm), 1)).astype(BF16)
    carry = carry_s[...]
    incl = _mm(onehot.astype(BF16), upper) + carry[:, 0:1]
    rank = jnp.sum(onehot * incl, axis=0, keepdims=True) - 1.0
    info_ref[...] = jnp.concatenate(g4 + [best, rank, jnp.zeros((INFO_ROWS - 6, tm), F32)], axis=0)
    for h in range(tm // tb):
        end = (h + 1) * tb
        cend_ref[h] = jnp.broadcast_to(incl[:, end - 1:end], (N_EXPERTS, LANES))
    carry_s[...] = jnp.broadcast_to(incl[:, tm - 1:tm], (N_EXPERTS, LANES))

    pieces = _split3(jnp.concatenate(g4 + [jnp.zeros((N_EXPERTS - E_PER_GROUP, tm), F32)], axis=0))
    pieces = jnp.concatenate(list(pieces) + [jnp.zeros((LANES - 3 * N_EXPERTS, tm), BF16)], axis=0)
    eye = (_iota((tm, tm), 0) == _iota((tm, tm), 1)).astype(BF16)
    h2_ref[:, d:d + LANES] = _mm(eye, pieces, _NT).astype(BF16)


def _route_call(xp, xs, ys, mod_l, w_out_l, norm2_l, w_router, router_bias, dec_seq):
    n_prompt_tok = xp.shape[0]
    t_all = n_prompt_tok + xs.shape[0]
    tm = TM_ROUTE
    sub = tm // TB_MOE
    n0_tiles = n_prompt_tok // tm
    row = _mod_row_map(tm, n_prompt_tok, dec_seq)
    tok = lambda width: pl.BlockSpec((tm, width), lambda i: (i, 0))
    y_specs, y_args = [], []
    for yp, ysm in ys:
        y_specs += _pair_specs(tm, MIX_WIDTH, n0_tiles)
        y_args += [yp, ysm]
    return pl.pallas_call(
        functools.partial(_route_kernel, n0_tiles),
        out_shape=[jax.ShapeDtypeStruct((t_all, D_MODEL), F32), jax.ShapeDtypeStruct((t_all, H2_WIDTH), BF16),
                   jax.ShapeDtypeStruct((INFO_ROWS, t_all), F32),
                   jax.ShapeDtypeStruct((t_all // TB_MOE, N_EXPERTS, LANES), F32)],
        grid=(t_all // tm,),
        in_specs=_pair_specs(tm, D_MODEL, n0_tiles) + y_specs + [
            pl.BlockSpec((1, 1, 1, N_MOD * D_MODEL), lambda i: (0, row(i), 0, 0)),
            _const_spec((D_MODEL, D_MODEL)), _const_spec((1, D_MODEL)),
            _const_spec((N_EXPERTS, D_MODEL)), _const_spec((N_EXPERTS, 1)),
        ],
        out_specs=[tok(D_MODEL), tok(H2_WIDTH), pl.BlockSpec((INFO_ROWS, tm), lambda i: (0, i)),
                   pl.BlockSpec((sub, N_EXPERTS, LANES), lambda i: (i, 0, 0))],
        scratch_shapes=[pltpu.VMEM((N_EXPERTS, LANES), F32)],
        compiler_params=_params(("arbitrary",)),
        name="route",
    )(xp, xs, *y_args, mod_l, w_out_l, norm2_l, w_router.T, router_bias.reshape(N_EXPERTS, 1))


def _moe_plan(cend, info):
    ts = TS_MOE
    tb = TR_MOE
    n_blocks = cend.shape[0]
    t_all = n_blocks * TB_MOE
    n_tiles = t_all // ts + N_GROUPS
    sub_per_tile = ts // tb
    n_sub = n_tiles * sub_per_tile
    cend = cend[:, :N_GROUPS, 0].astype(jnp.int32)
    cum = jnp.concatenate([jnp.zeros((1, N_GROUPS), jnp.int32), cend[:-1]], axis=0)
    cnt = cend - cum
    grp = info[INFO_GROUP].astype(jnp.int32)
    rank = info[INFO_RANK].astype(jnp.int32)
    tot = cend[-1]
    padded = ((tot + ts - 1) // ts) * ts
    off = jnp.cumsum(padded) - padded
    pos = off[grp] + rank

    sub_start = jnp.arange(n_sub, dtype=jnp.int32) * tb
    in_g = (sub_start[:, None] >= off[None, :]) & (sub_start[:, None] < (off + padded)[None, :])
    g_of = jnp.argmax(in_g, axis=1).astype(jnp.int32)
    r0 = sub_start - off[g_of]
    r1 = jnp.minimum(r0 + tb, tot[g_of])
    live = jnp.any(in_g, axis=1) & (r1 > r0)
    cend_g = cend[:, g_of]
    blo = jnp.where(live, jnp.sum(cend_g <= r0[None, :], axis=0), 0).astype(jnp.int32)
    bhi = jnp.where(live, jnp.sum(cend_g <= (r1 - 1)[None, :], axis=0), -1).astype(jnp.int32)

    used = jnp.any(in_g, axis=1)[::sub_per_tile]
    tile_group = g_of[::sub_per_tile]
    eidx = E_PER_GROUP * tile_group[:, None] + jnp.arange(E_PER_GROUP, dtype=jnp.int32)[None, :]
    n_used = jnp.sum(used.astype(jnp.int32))
    last = eidx[jnp.maximum(n_used - 1, 0), E_PER_GROUP - 1]
    eidx = jnp.where(used[:, None], eidx, last).reshape(-1).astype(jnp.int32)

    start = off[None, :] + cum
    end = off[None, :] + cend
    has = cnt > 0
    slo = jnp.where(has, start // tb, 0).reshape(-1).astype(jnp.int32)
    shi = jnp.where(has, (end - 1) // tb, -1).reshape(-1).astype(jnp.int32)
    return dict(pos=pos, eidx=eidx, used=used.astype(jnp.int32), blo=blo, bhi=bhi, slo=slo, shi=shi,
                n_tiles=n_tiles)


def _expert_kernel(eidx_ref, used_ref, blo_ref, bhi_ref, h2_ref, pos_ref, wg_ref, wu_ref, wd_ref,
                   ys_ref, xs_s, acc_s, ax_s):
    i = pl.program_id(0)
    j = pl.program_id(1)
    used = used_ref[i] > 0
    tb, tr = TB_MOE, TR_MOE
    sub_per_tile = TS_MOE // tr

    @pl.when(used & (j == 0))
    def _():
        r_idx = _iota((tr, tb), 0)
        for part in range(sub_per_tile):
            s = i * sub_per_tile + part
            base = s * tr
            ax_s[...] = jnp.zeros_like(ax_s)

            def gather(b, carry):
                sel = jnp.where(pos_ref[b] - base == r_idx, 1.0, 0.0).astype(BF16)
                tok = pl.ds(pl.multiple_of(b * tb, tb), tb)
                ax_s[...] += _mm(sel, h2_ref[tok, :])
                return carry

            lax.fori_loop(blo_ref[s], bhi_ref[s] + 1, gather, 0)
            xs_s[part * tr:(part + 1) * tr, :] = ax_s[...].astype(BF16)
        acc_s[...] = jnp.zeros_like(acc_s)

    @pl.when(used)
    def _():
        x = xs_s[:, 0:D_MODEL]
        pieces = xs_s[:, D_MODEL:H2_WIDTH].astype(F32)
        ge = jnp.sum(jnp.where(_iota(pieces.shape, 1) % N_EXPERTS == j, pieces, 0.0), axis=-1, keepdims=True)
        hh = _silu(_mm(x, wg_ref[0, 0].astype(BF16))) * _mm(x, wu_ref[0, 0].astype(BF16)) * ge
        acc_s[...] += _mm(hh.astype(BF16), wd_ref[0, 0].astype(BF16))

    @pl.when(j == E_PER_GROUP - 1)
    def _():
        ys_ref[...] = jnp.where(used, acc_s[...], 0.0).astype(BF16)


def _expert_call(layer, plan, h2, w_gate, w_up, w_down):
    t_all = h2.shape[0]
    tb, ts = TB_MOE, TS_MOE
    n_tiles = plan["n_tiles"]
    pos_rows = plan["pos"].reshape(t_all // tb, 1, tb)
    weight = lambda shape: pl.BlockSpec((1, 1) + shape, lambda i, j, eidx, *_: (layer, eidx[i * E_PER_GROUP + j], 0, 0))
    grid_spec = pltpu.PrefetchScalarGridSpec(
        num_scalar_prefetch=4,
        grid=(n_tiles, E_PER_GROUP),
        in_specs=[
            _const_spec((t_all, H2_WIDTH)), _const_spec(pos_rows.shape),
            weight((D_MODEL, D_EXPERT)), weight((D_MODEL, D_EXPERT)), weight((D_EXPERT, D_MODEL)),
        ],
        out_specs=pl.BlockSpec((ts, D_MODEL), lambda i, j, *_: (i, 0)),
        scratch_shapes=[pltpu.VMEM((ts, H2_WIDTH), BF16), pltpu.VMEM((ts, D_MODEL), F32),
                        pltpu.VMEM((TR_MOE, H2_WIDTH), F32)],
    )
    return pl.pallas_call(
        _expert_kernel,
        out_shape=jax.ShapeDtypeStruct((n_tiles * ts, D_MODEL), BF16),
        grid_spec=grid_spec,
        compiler_params=_params(("arbitrary", "arbitrary")),
        name=f"experts_{layer}",
    )(plan["eidx"], plan["used"], plan["blo"], plan["bhi"], h2, pos_rows, w_gate, w_up, w_down)


def _combine_kernel(final, n0_tiles, slo_ref, shi_ref, ys_ref, pos_ref, x1_ref, mod_ref, nf_ref,
                    op_ref, os_ref, acc_s):
    b = pl.program_id(0)
    d = D_MODEL
    tb, tr = TB_MOE, TR_MOE
    acc_s[...] = jnp.zeros_like(acc_s)
    c_idx = _iota((tb, tr), 1)
    pos = pos_ref[...]
    for g in range(N_GROUPS):
        def scatter(s, carry):
            sel = jnp.where(pos - s * tr == c_idx, 1.0, 0.0).astype(BF16)
            acc_s[...] += _mm(sel, ys_ref[pl.ds(pl.multiple_of(s * tr, tr), tr), :])
            return carry

        lax.fori_loop(slo_ref[b * N_GROUPS + g], shi_ref[b * N_GROUPS + g] + 1, scatter, 0)

    def result():
        x2 = x1_ref[...] + mod_ref[0, 0][:, 5 * d:6 * d] * acc_s[...]
        if final:
            x2 = x2 * lax.rsqrt(jnp.mean(x2 * x2, axis=-1, keepdims=True) + EPS) * nf_ref[...]
        return x2

    @pl.when(b < n0_tiles)
    def _():
        op_ref[...] = result()

    @pl.when(b >= n0_tiles)
    def _():
        os_ref[...] = result()


def _combine_call(final, plan, ys_sorted, x1, mod_l, norm_final, n_prompt_tok, dec_seq):
    t_all = x1.shape[0]
    tb = TB_MOE
    n0_tiles = n_prompt_tok // tb
    row = _mod_row_map(tb, n_prompt_tok, dec_seq)
    grid_spec = pltpu.PrefetchScalarGridSpec(
        num_scalar_prefetch=2,
        grid=(t_all // tb,),
        in_specs=[
            _const_spec(ys_sorted.shape),
            pl.BlockSpec((tb, 1), lambda b, *_: (b, 0)),
            pl.BlockSpec((tb, D_MODEL), lambda b, *_: (b, 0)),
            pl.BlockSpec((1, 1, 1, N_MOD * D_MODEL), lambda b, *_: (0, row(b), 0, 0)),
            _const_spec((1, D_MODEL)),
        ],
        out_specs=_pair_specs(tb, D_MODEL, n0_tiles),
        scratch_shapes=[pltpu.VMEM((tb, D_MODEL), F32)],
    )
    return pl.pallas_call(
        functools.partial(_combine_kernel, final, n0_tiles),
        out_shape=[jax.ShapeDtypeStruct((n_prompt_tok, D_MODEL), F32),
                   jax.ShapeDtypeStruct((t_all - n_prompt_tok, D_MODEL), F32)],
        grid_spec=grid_spec,
        compiler_params=_params(("arbitrary",)),
        name="combine",
    )(plan["slo"], plan["shi"], ys_sorted, plan["pos"].reshape(t_all, 1), x1, mod_l, norm_final)


def _rg_gate_weights(wa, ba, wx, bx):
    eye = jnp.eye(RG_HEADS, dtype=wa.dtype)

    def dense(wd):
        return jnp.einsum("hij,hg->higj", wd, eye).reshape(RG_WIDTH, RG_WIDTH)

    w = jnp.concatenate([dense(wa[0]), dense(wa[1]), dense(wx[0]), dense(wx[1])], axis=1)
    b = jnp.concatenate([ba[0], ba[1], bx[0], bx[1]], axis=0).reshape(1, 4 * RG_WIDTH)
    return w, b


def kernel(x_prompt, x_sample, state_rglru, state_hgrn, state_ret, c, c_ctx, norm1, norm2, norm_final, w_ada, b_ada, w_in, w_out, hy_conv_w, hy_conv_b, hy_w1, hy_b1, hy_w2, hy_b2, hy_w3, hy_d, rg_conv_w, rg_conv_b, rg_wa, rg_ba, rg_wx, rg_bx, rg_lambda, hg_lb, hg_norm, ret_decay, w_router, router_bias, w_gate, w_up, w_down):
    batch, seq, d = x_prompt.shape
    dec_batch, dec_seq, _ = x_sample.shape
    assert d == D_MODEL and dec_batch + 1 <= COND_ROWS
    n_prompt_tok = batch * seq

    lb_cum = jnp.cumsum(jax.nn.softmax(hg_lb.astype(F32), axis=0), axis=0)
    lb_all = lb_cum - lb_cum[0:1]

    cond = jnp.zeros((COND_ROWS, d), F32).at[0].set(c_ctx).at[1:1 + dec_batch].set(c)
    mod = _modulation(cond, w_ada, b_ada).reshape(DEPTH, COND_ROWS, 1, N_MOD * d)

    w_in_b = w_in.astype(BF16)
    w_out_b = w_out.astype(BF16)

    passes = (
        dict(tok0=0, batch=batch, seq=seq, rope=False),
        dict(tok0=n_prompt_tok, batch=dec_batch, seq=dec_seq, rope=True),
    )
    filters = {p["seq"]: _hyena_filters(p["seq"], hy_w1, hy_b1, hy_w2, hy_b2, hy_w3) for p in passes}
    tables = {p["seq"]: _hyena_tables(p["seq"]) for p in passes}

    xp = x_prompt.reshape(-1, d)
    xs = x_sample.reshape(-1, d)
    new_rg, new_hg, new_ret = [], [], []
    for l in range(DEPTH):
        hy_all, rg_all, hg_all, ret_all = _in_projection(
            xp, xs, mod[l:l + 1], norm1[l].reshape(1, d), w_in_b[l], dec_seq)
        wg, bg = _rg_gate_weights(rg_wa[l], rg_ba[l], rg_wx[l], rg_bx[l])
        sp = jax.nn.softplus(-rg_lambda[l])
        ys = [[], [], [], []]
        for pi, p in enumerate(passes):
            first = pi == 0
            geom = (p["tok0"], p["batch"], p["seq"])
            y_hy = _hyena(hy_all, *geom, hy_conv_w[l], hy_conv_b[l].reshape(1, -1), hy_d[l],
                          filters[p["seq"]][l], tables[p["seq"]])
            y_rg, *st_rg = _rglru(rg_all, *geom, rg_conv_w[l], rg_conv_b[l].reshape(1, -1), wg, bg, sp,
                                  None if first else (state_rglru, l), first)
            y_hg, *st_hg = _hgrn(hg_all, *geom, lb_all[l], hg_norm[l].reshape(1, -1),
                                 None if first else (state_hgrn, l), first)
            y_ret, *st_ret = _retention(ret_all, *geom, p["rope"], ret_decay[l],
                                        None if first else (state_ret, l), first)
            for lst, y in zip(ys, (y_hy, y_rg, y_hg, y_ret)):
                lst.append(y)
            if first:
                new_rg.append(st_rg[0])
                new_hg.append(st_hg[0])
                new_ret.append(st_ret[0])
        x1, h2, info, cend = _route_call(xp, xs, ys, mod[l:l + 1], w_out_b[l], norm2[l].reshape(1, d),
                                         w_router, router_bias, dec_seq)
        plan = _moe_plan(cend, info)
        ys_sorted = _expert_call(l, plan, h2, w_gate, w_up, w_down)
        xp, xs = _combine_call(l == DEPTH - 1, plan, ys_sorted, x1, mod[l:l + 1], norm_final.reshape(1, d),
                               n_prompt_tok, dec_seq)

    return (xp.reshape(batch, seq, d), xs.reshape(dec_batch, dec_seq, d), jnp.stack(new_rg, axis=1),
            jnp.stack(new_hg, axis=1), jnp.stack(new_ret, axis=1))
```

```python
import functools
import math

import numpy as np
import jax
import jax.numpy as jnp
from jax import lax
from jax.experimental import pallas as pl
from jax.experimental.pallas import tpu as pltpu

F32 = jnp.float32
BF16 = jnp.bfloat16

D_MODEL = 1024
DEPTH = 2
GRID_W = 64
HY_WIDTH = 256
RG_WIDTH = 256
HG_WIDTH = 256
RET_WIDTH = 256
MIX_WIDTH = 256
HY_ORDER = 2
HY_EMB = 33
HY_BANDS = 16
HY_FFN = 64
HY_DECAY_TARGET = 1e-2
HY_DECAY_SHORT = 0.3
HY_DECAY_LONG = 1.5
HY_GROUP_COLS = 512
RG_HEADS = 8
RG_HEAD_DIM = 32
RG_C = 8.0
RG_CHUNK = 16
N_HEADS = 4
HEAD_DIM = 64
HG_CHUNK = 64
RET_CHUNK = 256
ROPE_BASE = 10000.0
N_EXPERTS = 16
N_GROUPS = 4
E_PER_GROUP = 4
D_EXPERT = 512
N_MOD = 6
EPS = 1e-6
PROJ_HY = 3 * HY_WIDTH
PROJ_RG = 2 * RG_WIDTH
PROJ_HG = 5 * HG_WIDTH
PROJ_RET = 4 * RET_WIDTH
PROJ_WIDTH = PROJ_HY + PROJ_RG + PROJ_HG + PROJ_RET
COND_ROWS = 16
LANES = 128
VMEM_LIMIT = 56 * 1024 * 1024
TM_PROJ = 512
TB_MOE = 512
TR_MOE = 256
TM_ROUTE = TB_MOE
TS_MOE = 1024
TN_MOD = 1536

_NN = (((1,), (0,)), ((), ()))
_NT = (((1,), (1,)), ((), ()))


def _mm(a, b, dn=_NN):
    return lax.dot_general(a, b, dn, preferred_element_type=F32)


def _split2(x):
    hi = x.astype(BF16)
    lo = (x - hi.astype(F32)).astype(BF16)
    return hi, lo


def _split3(x):
    hi = x.astype(BF16)
    r = x - hi.astype(F32)
    mid = r.astype(BF16)
    lo = (r - mid.astype(F32)).astype(BF16)
    return hi, mid, lo


def _mm3(a, b, dn=_NN):
    ah, al = _split2(a)
    bh, bl = _split2(b)
    return _mm(ah, bh, dn) + (_mm(ah, bl, dn) + _mm(al, bh, dn))


def _mm_exact_rhs(a, b_bf16):
    a1, a2, a3 = _split3(a)
    return _mm(a1, b_bf16) + (_mm(a2, b_bf16) + _mm(a3, b_bf16))


def _sigmoid(x):
    return 1.0 / (1.0 + jnp.exp(-x))


def _silu(x):
    return x * _sigmoid(x)


def _log_sigmoid(x):
    return jnp.minimum(x, 0.0) - jnp.log(1.0 + jnp.exp(-jnp.abs(x)))


def _gelu_tanh(x):
    return 0.5 * x * (1.0 + jnp.tanh(math.sqrt(2.0 / math.pi) * (x + 0.044715 * (x * x * x))))


def _iota(shape, dim):
    return lax.broadcasted_iota(jnp.int32, shape, dim)


def _shift_rows(u, k, row):
    n = u.shape[0]
    if k == 0:
        return u
    r = pltpu.roll(u, (-k) % n, axis=0)
    if k < 0:
        return jnp.where(row >= -k, r, 0.0)
    return jnp.where(row < n - k, r, 0.0)


def _head_mask(n_rows_per_head):
    shape = (N_HEADS * n_rows_per_head, MIX_WIDTH)
    return (_iota(shape, 0) // n_rows_per_head) == (_iota(shape, 1) // HEAD_DIM)


def _block_diag_mask():
    shape = (MIX_WIDTH, MIX_WIDTH)
    return (_iota(shape, 0) // HEAD_DIM) == (_iota(shape, 1) // HEAD_DIM)


def _head_rmsnorm(o, ones_bd):
    ms = _mm_exact_rhs(o * o, ones_bd)
    return o * lax.rsqrt(ms + EPS)


def _load_state(s0_ref, d):
    zero = jnp.zeros((HEAD_DIM, HEAD_DIM), F32)
    rows = []
    for h in range(N_HEADS):
        blk = s0_ref[0, d, h].T
        rows.append(jnp.concatenate([blk if g == h else zero for g in range(N_HEADS)], axis=1))
    return jnp.concatenate(rows, axis=0)


def _store_state(st_ref, d, st):
    for h in range(N_HEADS):
        lo, hi = h * HEAD_DIM, (h + 1) * HEAD_DIM
        st_ref[0, d, h] = st[lo:hi, lo:hi].T


def _params(sem, vmem=VMEM_LIMIT):
    return pltpu.CompilerParams(dimension_semantics=sem, vmem_limit_bytes=vmem)


def _const_spec(shape):
    nd = len(shape)
    return pl.BlockSpec(shape, lambda *_: (0,) * nd, pipeline_mode=pl.Buffered(1))


def _const_layer_spec(shape, layer):
    nd = len(shape)
    return pl.BlockSpec((None,) + tuple(shape[1:]), lambda *_: (layer,) + (0,) * (nd - 1),
                        pipeline_mode=pl.Buffered(1))


def _mod_kernel(cond_ref, w_ref, b_ref, o_ref):
    o_ref[0] = _mm3(_silu(cond_ref[...]), w_ref[0]) + b_ref[0]


def _modulation(cond, w_ada, b_ada):
    tn = TN_MOD
    n_mod = N_MOD * D_MODEL
    return pl.pallas_call(
        _mod_kernel,
        out_shape=jax.ShapeDtypeStruct((DEPTH, COND_ROWS, n_mod), F32),
        grid=(DEPTH, n_mod // tn),
        in_specs=[
            pl.BlockSpec((COND_ROWS, D_MODEL), lambda l, j: (0, 0)),
            pl.BlockSpec((1, D_MODEL, tn), lambda l, j: (l, 0, j)),
            pl.BlockSpec((1, 1, tn), lambda l, j: (l, 0, j)),
        ],
        out_specs=pl.BlockSpec((1, COND_ROWS, tn), lambda l, j: (l, 0, j)),
        compiler_params=_params(("parallel", "parallel")),
        name="modulation",
    )(cond, w_ada, b_ada.reshape(DEPTH, 1, n_mod))


def _pair_specs(tm, width, n0_tiles):
    return [pl.BlockSpec((tm, width), lambda i, *_: (jnp.minimum(i, n0_tiles - 1), 0)),
            pl.BlockSpec((tm, width), lambda i, *_: (jnp.maximum(i - n0_tiles, 0), 0))]


def _proj_kernel(n0_tiles, xp_ref, xs_ref, mod_ref, n1_ref, w_ref, hy_ref, rg_ref, hg_ref, ret_ref):
    x = jnp.where(pl.program_id(0) < n0_tiles, xp_ref[...], xs_ref[...])
    mod = mod_ref[0, 0]
    sh1 = mod[:, 0:D_MODEL]
    sc1 = mod[:, D_MODEL:2 * D_MODEL]
    h = x * lax.rsqrt(jnp.mean(x * x, axis=-1, keepdims=True) + EPS) * n1_ref[...]
    h = (h * (1.0 + sc1) + sh1).astype(BF16)
    c0 = 0
    for ref, width in ((hy_ref, PROJ_HY), (rg_ref, PROJ_RG), (hg_ref, PROJ_HG), (ret_ref, PROJ_RET)):
        ref[...] = _mm(h, w_ref[:, c0:c0 + width])
        c0 += width


def _mod_row_map(tm, n_prompt_tok, dec_seq):
    n_prompt_tiles = n_prompt_tok // tm

    def row(i):
        return jnp.where(i < n_prompt_tiles, 0, 1 + (i * tm - n_prompt_tok) // dec_seq)

    return row


def _in_projection(layer, xp, xs, mod_l, norm1_l, w_in, dec_seq):
    n_prompt_tok = xp.shape[0]
    t_all = n_prompt_tok + xs.shape[0]
    tm = TM_PROJ
    n0_tiles = n_prompt_tok // tm
    row = _mod_row_map(tm, n_prompt_tok, dec_seq)
    widths = (PROJ_HY, PROJ_RG, PROJ_HG, PROJ_RET)
    return pl.pallas_call(
        functools.partial(_proj_kernel, n0_tiles),
        out_shape=[jax.ShapeDtypeStruct((t_all, w), F32) for w in widths],
        grid=(t_all // tm,),
        in_specs=_pair_specs(tm, D_MODEL, n0_tiles) + [
            pl.BlockSpec((1, 1, 1, N_MOD * D_MODEL), lambda i: (0, row(i), 0, 0)),
            _const_spec((1, D_MODEL)),
            _const_layer_spec(w_in.shape, layer),
        ],
        out_specs=[pl.BlockSpec((tm, w), lambda i: (i, 0)) for w in widths],
        compiler_params=_params(("parallel",)),
        name="in_projection",
    )(xp, xs, mod_l, norm1_l, w_in)


def _dft_tables(seq):
    k = np.arange(seq, dtype=np.int64)
    m = (k[:, None] * k[None, :]) % (2 * seq)
    ang = np.pi * m.astype(np.float64) / seq
    return np.cos(ang), np.sin(ang)


def _hyena_tables(seq):
    cos, sin = _dft_tables(seq)
    sign = np.where(np.arange(seq) % 2 == 0, 1.0, -1.0)
    fwd = np.concatenate([cos, sign[None, :], sin[1:]], axis=0)
    wk = np.full((seq,), 2.0)
    wk[0] = 1.0
    inv_cos = (cos * wk[None, :]) / (2.0 * seq)
    inv_nyq = sign[:, None] / (2.0 * seq)
    inv_sin = 2.0 * sin[:, 1:] / (2.0 * seq)
    inv = np.concatenate([inv_cos, inv_nyq, inv_sin], axis=1)

    return jnp.asarray(fwd, F32).astype(BF16), jnp.asarray(inv, F32).astype(BF16)


def _filter_embedding(seq):
    t = np.arange(seq, dtype=np.float64)
    t_norm = t / max(seq - 1, 1)
    bands = np.linspace(1e-4, HY_BANDS - 1, HY_BANDS)
    ang = (2.0 * np.pi / seq) * t[:, None] * bands[None, :]
    z = np.concatenate([t_norm[:, None], np.cos(ang), np.sin(ang)], axis=-1)
    z = np.pad(z, ((0, 0), (0, LANES - HY_EMB)))
    deltas = np.abs(np.linspace(math.log(HY_DECAY_TARGET) / HY_DECAY_LONG,
                                math.log(HY_DECAY_TARGET) / HY_DECAY_SHORT, HY_WIDTH))
    window = np.exp(-t_norm[:, None] * deltas[None, :])
    return jnp.asarray(z, F32), jnp.asarray(window, F32)


def _filter_kernel(z_ref, win_ref, cos_ref, sin_ref, w1_ref, b1_ref, w2_ref, b2_ref, w3_ref, o_ref):
    seq = z_ref.shape[0]
    h = jnp.sin(_mm3(z_ref[...], w1_ref[0]) + b1_ref[0])
    h = jnp.sin(_mm3(h, w2_ref[0]) + b2_ref[0])
    h = _mm3(h, w3_ref[0])
    win = win_ref[...]
    row = _iota((seq, 1), 0)
    sums, diffs = [], []
    for o in range(HY_ORDER):
        c0 = o * 2 * HY_WIDTH
        hf = h[:, c0:c0 + HY_WIDTH] * win
        hb = h[:, c0 + HY_WIDTH:c0 + 2 * HY_WIDTH] * win
        ssq = jnp.sum(hf * hf + hb * hb, axis=0, keepdims=True)
        inv = lax.rsqrt(ssq + EPS)
        hf = hf * inv
        hb = jnp.where(row == 0, 0.0, hb * inv)
        sums.append(hf + hb)
        diffs.append(hf - hb)
    hsum = jnp.concatenate(sums, axis=1)
    hdiff = jnp.concatenate(diffs, axis=1)
    h_re = _mm(cos_ref[...], hsum.astype(BF16))
    h_im = _mm(sin_ref[...], hdiff.astype(BF16))
    sign = jnp.where(row % 2 == 0, 1.0, -1.0)
    h_nyq = jnp.sum(sign * hsum, axis=0, keepdims=True)
    o_ref[0, 0] = h_re
    o_ref[0, 1] = h_im
    o_ref[0, 2] = jnp.where(row == 0, h_nyq, h_re)


def _hyena_filters(seq, w1, b1, w2, b2, w3):
    z, window = _filter_embedding(seq)
    cos, sin = _dft_tables(seq)
    n_out = HY_ORDER * 2 * HY_WIDTH
    w1p = jnp.pad(w1, ((0, 0), (0, LANES - HY_EMB), (0, LANES - HY_FFN)))
    b1p = jnp.pad(b1, ((0, 0), (0, LANES - HY_FFN))).reshape(DEPTH, 1, LANES)
    w2p = jnp.pad(w2, ((0, 0), (0, LANES - HY_FFN), (0, LANES - HY_FFN)))
    b2p = jnp.pad(b2, ((0, 0), (0, LANES - HY_FFN))).reshape(DEPTH, 1, LANES)
    w3p = jnp.pad(w3, ((0, 0), (0, LANES - HY_FFN), (0, 0)))
    per_layer = lambda shape: pl.BlockSpec((1,) + shape, lambda l: (l,) + (0,) * len(shape))
    return pl.pallas_call(
        _filter_kernel,
        out_shape=jax.ShapeDtypeStruct((DEPTH, 3, seq, HY_ORDER * HY_WIDTH), F32),
        grid=(DEPTH,),
        in_specs=[
            _const_spec((seq, LANES)), _const_spec((seq, HY_WIDTH)),
            _const_spec((seq, seq)), _const_spec((seq, seq)),
            per_layer((LANES, LANES)), per_layer((1, LANES)),
            per_layer((LANES, LANES)), per_layer((1, LANES)),
            per_layer((LANES, n_out)),
        ],
        out_specs=pl.BlockSpec((1, 3, seq, HY_ORDER * HY_WIDTH), lambda l: (l, 0, 0, 0)),
        compiler_params=_params(("parallel",)),
        name=f"hyena_filters_{seq}",
    )(z, window, jnp.asarray(cos, F32).astype(BF16), jnp.asarray(sin, F32).astype(BF16), w1p, b1p, w2p, b2p, w3p)


def _hyena_kernel(group, u_ref, cw_ref, cb_ref, d_ref, filt_ref, fwd_ref, inv_ref, y_ref):
    seq = u_ref.shape[0] // group
    cw = cw_ref[...]
    row = _iota((seq, 1), 0)
    ucs = []
    for s in range(group):
        u = u_ref[s * seq:(s + 1) * seq, :]
        ucs.append(cb_ref[...] + cw[0:1] * _shift_rows(u, -1, row) + cw[1:2] * u + cw[2:3] * _shift_rows(u, 1, row))
    side = lambda c0: jnp.concatenate([uc[:, c0:c0 + HY_WIDTH] for uc in ucs], axis=1)
    tile = lambda a: jnp.concatenate([a] * group, axis=1)
    gates = (side(HY_WIDTH), side(2 * HY_WIDTH))
    d = d_ref[...]
    z = side(0)
    for o in range(HY_ORDER):
        c0 = o * HY_WIDTH
        spec = _mm(fwd_ref[...], z.astype(BF16))
        s_re, s_im = spec[:seq], spec[seq:]
        a = tile(filt_ref[0, :, c0:c0 + HY_WIDTH])
        b = tile(filt_ref[1, :, c0:c0 + HY_WIDTH])
        c = tile(filt_ref[2, :, c0:c0 + HY_WIDTH])
        prod = jnp.concatenate([s_re * a - s_im * b, s_re * b + s_im * c], axis=0)
        conv = _mm(inv_ref[...], prod.astype(BF16))
        z = gates[o] * (conv + tile(d[o:o + 1]) * z)
    for s in range(group):
        y_ref[s * seq:(s + 1) * seq, :] = z[:, s * HY_WIDTH:(s + 1) * HY_WIDTH]


def _mixer_call(body, name, proj_all, tok0, batch, seq, consts, batch_ins, state_shapes, scratch, group=1):
    proj_width = proj_all.shape[1]
    rows = group * seq
    blk0 = tok0 // rows

    def batch_spec(shape):
        nd = len(shape)
        return pl.BlockSpec((group,) + tuple(shape[1:]), lambda b: (b,) + (0,) * (nd - 1))

    def layer_spec(shape, layer):
        nd = len(shape)
        return pl.BlockSpec((group, None) + tuple(shape[2:]), lambda b: (b, layer) + (0,) * (nd - 2))

    in_specs = [pl.BlockSpec((rows, proj_width), lambda b: (blk0 + b, 0))]
    in_specs += [_const_layer_spec(a[0].shape, a[1]) if isinstance(a, tuple) else _const_spec(a.shape)
                 for a in consts]
    consts = [a[0] if isinstance(a, tuple) else a for a in consts]
    in_specs += [layer_spec(a.shape, layer) for a, layer in batch_ins]
    batch_ins = [a for a, _ in batch_ins]
    out_shape = [jax.ShapeDtypeStruct((batch * seq, MIX_WIDTH), F32)]
    out_specs = [pl.BlockSpec((rows, MIX_WIDTH), lambda b: (b, 0))]
    for shape in state_shapes:
        out_shape.append(jax.ShapeDtypeStruct(shape, F32))
        out_specs.append(batch_spec(shape))
    return pl.pallas_call(
        body, out_shape=out_shape, grid=(batch // group,), in_specs=in_specs, out_specs=out_specs,
        scratch_shapes=scratch,
        compiler_params=_params(("parallel",)),
        name=f"{name}_{seq}",
    )(proj_all, *consts, *batch_ins)


def _hyena(u_all, tok0, batch, seq, conv_w, conv_b, d_bias, filt_l, tables):
    consts = [conv_w, conv_b, d_bias, filt_l, *tables]
    group = HY_GROUP_COLS // HY_WIDTH
    return _mixer_call(functools.partial(_hyena_kernel, group), "hyena", u_all, tok0, batch, seq, consts,
                       [], [], [], group=group)[0]


def _rglru_kernel(has_s0, emit_state, rg_ref, cw_ref, cb_ref, wg_ref, bg_ref, sp_ref, *refs):
    h0_ref = refs[0] if has_s0 else None
    y_ref = refs[1 if has_s0 else 0]
    st_ref = refs[-1] if emit_state else None
    seq = rg_ref.shape[0]
    w = RG_WIDTH
    xr = rg_ref[:, 0:w]
    gate = rg_ref[:, w:2 * w]
    row = _iota((seq, 1), 0)
    cw = cw_ref[...]
    xc = (cb_ref[...] + cw[0:1] * _shift_rows(xr, -2, row) + cw[1:2] * _shift_rows(xr, -1, row)
          + cw[2:3] * xr + cw[3:4] * _shift_rows(xr, 1, row))
    g = _sigmoid(_mm(xc.astype(BF16), wg_ref[...].astype(BF16)) + bg_ref[...])
    sp = sp_ref[...]
    c = RG_CHUNK
    n = seq // c
    pos = row % c
    hs = []
    for d in range(2):
        forward = d == 0
        r = g[:, d * w:(d + 1) * w]
        i = g[:, (2 + d) * w:(3 + d) * w]
        log_a = -RG_C * r * sp[d:d + 1]
        a = jnp.exp(log_a)
        b = jnp.sqrt(jnp.tanh(-log_a) * (1.0 + a * a)) * (i * xc)
        step = 1
        while step < c:
            keep = (pos >= step) if forward else (pos < c - step)
            shift = step if forward else seq - step
            a_s = pltpu.roll(a, shift, axis=0)
            b_s = pltpu.roll(b, shift, axis=0)
            b = jnp.where(keep, a * b_s + b, b)
            a = jnp.where(keep, a * a_s, a)
            step *= 2
        a3 = a.reshape(n, c, w)
        b3 = b.reshape(n, c, w)
        edge = c - 1 if forward else 0
        a_end = a3[:, edge:edge + 1, :]
        b_end = b3[:, edge:edge + 1, :]
        h = h0_ref[0, d:d + 1, :] if has_s0 else jnp.zeros((1, w), F32)
        h_in = [None] * n
        for ci in (range(n) if forward else range(n - 1, -1, -1)):
            h_in[ci] = h
            h = b_end[ci] + a_end[ci] * h
        hs.append((b3 + a3 * jnp.stack(h_in, axis=0)).reshape(seq, w))
    y_ref[...] = (hs[0] + hs[1]) * _gelu_tanh(gate)
    if emit_state:
        st_ref[0, 0:1, :] = hs[0][seq - 1:seq]
        st_ref[0, 1:2, :] = hs[1][0:1]


def _rglru(rg_all, tok0, batch, seq, conv_w, conv_b, w_gates, b_gates, softplus_neg_lam, h0, emit_state):
    has_s0 = h0 is not None
    consts = [conv_w, conv_b, w_gates, b_gates, softplus_neg_lam]
    return _mixer_call(functools.partial(_rglru_kernel, has_s0, emit_state), "rglru", rg_all, tok0, batch, seq,
                       consts, [h0] if has_s0 else [],
                       [(batch, 2, RG_WIDTH)] if emit_state else [], [])


def _hgrn_kernel(has_s0, emit_state, hg_ref, lb_ref, gain_ref, *refs):
    s0_ref = refs[0] if has_s0 else None
    y_ref = refs[1 if has_s0 else 0]
    st_ref = refs[-1] if emit_state else None
    seq = hg_ref.shape[0]
    w = HG_WIDTH
    c = HG_CHUNK
    n = seq // c
    mid = c // 2
    chunks = lambda a: a.reshape(n, c, w)
    pos = _iota((seq, 1), 0) % c
    stack_mask = _head_mask(c)[None]
    bd_mask = _block_diag_mask()
    pair_shape = (c, N_HEADS * c)
    t_idx = _iota(pair_shape, 0)
    s_idx = _iota(pair_shape, 1) % c

    def stack_heads(a3):
        return jnp.where(stack_mask, jnp.concatenate([a3] * N_HEADS, axis=1), 0.0).astype(BF16)

    q3 = chunks(_silu(hg_ref[:, 0:w]))
    v3 = chunks(hg_ref[:, 3 * w:4 * w])
    v_stack = stack_heads(v3)
    v_t = jnp.swapaxes(v3, 1, 2).astype(BF16)
    lbv = lb_ref[...]
    o_sum = None
    finals = []
    for d in range(2):
        forward = d == 0
        f_pre = hg_ref[:, (1 + d) * w:(2 + d) * w]
        lo = lbv[d:d + 1]
        a1 = jnp.log(lo)
        a2 = jnp.log(1.0 - lo) + _log_sigmoid(f_pre)
        g = jnp.maximum(a1, a2) + jnp.log(1.0 + jnp.exp(-jnp.abs(a1 - a2)))
        k3 = chunks((1.0 - lo) * _sigmoid(-f_pre))
        step = 1
        while step < c:
            if forward:
                g = jnp.where(pos >= step, g + pltpu.roll(g, step, axis=0), g)
            else:
                g = jnp.where(pos < c - step, g + pltpu.roll(g, seq - step, axis=0), g)
            step *= 2
        g3 = chunks(g)
        g_ref = g3[:, mid:mid + 1, :]
        g_tot = g3[:, c - 1:c, :] if forward else g3[:, 0:1, :]
        q_in = (q3 * jnp.exp(g3)).astype(BF16)
        q_sc = (q3 * jnp.exp(g3 - g_ref)).astype(BF16)
        k_stack = stack_heads(k3 * jnp.exp(g_ref - g3))
        k_out = (k3 * jnp.exp(g_tot - g3)).astype(BF16)
        pair = jnp.einsum("ntl,nrl->ntr", q_sc, k_stack, preferred_element_type=F32)
        causal = (t_idx >= s_idx) if forward else (t_idx <= s_idx)
        pair = jnp.where(causal[None], pair, 0.0).astype(BF16)
        o_intra = jnp.einsum("ntr,nrv->ntv", pair, v_stack, preferred_element_type=F32)
        upd = jnp.einsum("nvs,nsk->nvk", v_t, k_out, preferred_element_type=F32)
        decay = jnp.exp(g_tot)
        st = _load_state(s0_ref, d) if has_s0 else jnp.zeros((w, w), F32)
        o_inter = [None] * n
        for ci in (range(n) if forward else range(n - 1, -1, -1)):
            o_inter[ci] = _mm(q_in[ci], st.astype(BF16), _NT)
            st = st * decay[ci] + jnp.where(bd_mask, upd[ci], 0.0)
        o_dir = o_intra + jnp.stack(o_inter, axis=0)
        o_sum = o_dir if o_sum is None else o_sum + o_dir
        finals.append(st)

    ones_bd = jnp.where(bd_mask, 1.0 / HEAD_DIM, 0.0).astype(BF16)
    o = _head_rmsnorm(o_sum.reshape(seq, w), ones_bd) * gain_ref[...]
    y_ref[...] = o * _silu(hg_ref[:, 4 * w:5 * w])
    if emit_state:
        _store_state(st_ref, 0, finals[0])
        _store_state(st_ref, 1, finals[1])


def _state_shape(batch):
    return (batch, 2, N_HEADS, HEAD_DIM, HEAD_DIM)


def _hgrn(hg_all, tok0, batch, seq, lb_l, gain_l, s0, emit_state):
    has_s0 = s0 is not None
    return _mixer_call(functools.partial(_hgrn_kernel, has_s0, emit_state), "hgrn2", hg_all, tok0, batch, seq,
                       [lb_l, gain_l], [s0] if has_s0 else [],
                       [_state_shape(batch)] if emit_state else [], [])


def _ret_kernel(use_rope, has_s0, emit_state, ret_ref, cos_ref, sin_ref, dl_ref, dlp_ref, *refs):
    s0_ref = refs[0] if has_s0 else None
    y_ref = refs[1 if has_s0 else 0]
    st_ref = refs[-1] if emit_state else None
    seq = ret_ref.shape[0]
    w = RET_WIDTH
    c = min(RET_CHUNK, seq)
    n = seq // c
    chunks = lambda a: a.reshape(n, c, w)
    q = ret_ref[:, 0:w]
    k = ret_ref[:, w:2 * w] * (HEAD_DIM ** -0.5)
    if use_rope:
        lane = _iota((seq, w), 1)
        even = (lane % 2) == 0
        cos = cos_ref[...]
        sin = sin_ref[...]

        def rope(x):
            nxt = pltpu.roll(x, w - 1, axis=1)
            prv = pltpu.roll(x, 1, axis=1)
            return x * cos + jnp.where(even, nxt, prv) * sin

        q = rope(q)
        k = rope(k)
    q3, k3, v3 = chunks(q), chunks(k), chunks(ret_ref[:, 2 * w:3 * w])

    lg = _log_sigmoid(dl_ref[...])
    lgp = _log_sigmoid(dlp_ref[...])
    pair_shape = (c, N_HEADS * c)
    t_idx = _iota(pair_shape, 0)
    s_idx = _iota(pair_shape, 1) % c
    dist = (t_idx - s_idx).astype(F32)
    decay = jnp.exp(jnp.where(dist >= 0, dist * lgp[0:1], -dist * lgp[1:2]))
    decay = decay + jnp.where(dist == 0, 1.0, 0.0)
    stack_mask = _head_mask(c)[None]
    bd_mask = _block_diag_mask()
    pos = _iota((c, 1), 0).astype(F32)
    fc = float(c)

    def stack_heads(a3):
        return jnp.where(stack_mask, jnp.concatenate([a3] * N_HEADS, axis=1), 0.0).astype(BF16)

    pair = jnp.einsum("ntl,nrl->ntr", q3.astype(BF16), stack_heads(k3), preferred_element_type=F32)
    pair = (pair * decay[None]).astype(BF16)
    o_sum = jnp.einsum("ntr,nrv->ntv", pair, stack_heads(v3), preferred_element_type=F32)
    v_t = jnp.swapaxes(v3, 1, 2).astype(BF16)
    finals = []
    for d in range(2):
        forward = d == 0
        lgd = lg[d:d + 1]
        q_in = (q3 * jnp.exp(((pos + 1.0) if forward else (fc - pos)) * lgd)).astype(BF16)
        k_out = (k3 * jnp.exp(((fc - 1.0 - pos) if forward else pos) * lgd)).astype(BF16)
        upd = jnp.einsum("nvs,nsk->nvk", v_t, k_out, preferred_element_type=F32)
        chunk_decay = jnp.exp(fc * lgd)
        st = _load_state(s0_ref, d) if has_s0 else jnp.zeros((w, w), F32)
        o_inter = [None] * n
        for ci in (range(n) if forward else range(n - 1, -1, -1)):
            o_inter[ci] = _mm(q_in[ci], st.astype(BF16), _NT)
            st = st * chunk_decay + jnp.where(bd_mask, upd[ci], 0.0)
        o_sum = o_sum + jnp.stack(o_inter, axis=0)
        finals.append(st)

    ones_bd = jnp.where(bd_mask, 1.0 / HEAD_DIM, 0.0).astype(BF16)
    o = _head_rmsnorm(o_sum.reshape(seq, w), ones_bd)
    y_ref[...] = _silu(ret_ref[:, 3 * w:4 * w]) * o
    if emit_state:
        _store_state(st_ref, 0, finals[0])
        _store_state(st_ref, 1, finals[1])


def _rope_tables(seq):
    rows = seq // GRID_W
    row = np.repeat(np.arange(rows), GRID_W).astype(np.float64)
    col = (np.arange(seq) % GRID_W).astype(np.float64)
    n_freq = HEAD_DIM // 4
    inv_freq = ROPE_BASE ** (-np.arange(n_freq, dtype=np.float64) / n_freq)
    ang = np.concatenate([row[:, None] * inv_freq, col[:, None] * inv_freq], axis=-1)
    ang = np.repeat(ang, 2, axis=1)
    cos = np.tile(np.cos(ang), (1, N_HEADS))
    sin = np.tile(np.sin(ang) * np.where(np.arange(HEAD_DIM) % 2 == 0, -1.0, 1.0)[None, :], (1, N_HEADS))
    return jnp.asarray(cos, F32), jnp.asarray(sin, F32)


def _retention(ret_all, tok0, batch, seq, use_rope, decay_l, s0, emit_state):
    has_s0 = s0 is not None
    w = RET_WIDTH
    c = min(RET_CHUNK, seq)
    cos, sin = _rope_tables(seq)
    dl = jnp.repeat(decay_l, HEAD_DIM, axis=-1)
    dlp = jnp.repeat(decay_l, c, axis=-1)
    return _mixer_call(functools.partial(_ret_kernel, use_rope, has_s0, emit_state), "retention", ret_all, tok0,
                       batch, seq, [cos, sin, dl, dlp], [s0] if has_s0 else [],
                       [_state_shape(batch)] if emit_state else [], [])


def _route(h2, wrt_ref, rb_ref):
    logits = _mm3(wrt_ref[...], h2, _NT)
    tm = logits.shape[1]
    m = jnp.max(logits, axis=0, keepdims=True)
    e = jnp.exp(logits - m)
    probs = e / jnp.sum(e, axis=0, keepdims=True)
    sel = probs + rb_ref[...]
    lane = _iota((N_EXPERTS, tm), 0).astype(F32)
    group = (_iota((N_EXPERTS, tm), 0) // E_PER_GROUP).astype(F32)
    neg = -jnp.inf

    def first_argmax(vals):
        mx = jnp.max(vals, axis=0, keepdims=True)
        idx = jnp.min(jnp.where(vals == mx, lane, float(N_EXPERTS)), axis=0, keepdims=True)
        return mx, idx

    best_score = None
    best = None
    for g in range(N_GROUPS):
        vals = jnp.where(group == float(g), sel, neg)
        m1, i1 = first_argmax(vals)
        m2, _ = first_argmax(jnp.where(lane == i1, neg, vals))
        score = m1 + m2
        if g == 0:
            best_score, best = score, jnp.zeros_like(score)
        else:
            take = score > best_score
            best = jnp.where(take, float(g), best)
            best_score = jnp.where(take, score, best_score)
    vals = jnp.where(group == best, sel, neg)
    _, i1 = first_argmax(vals)
    _, i2 = first_argmax(jnp.where(lane == i1, neg, vals))
    chosen = (lane == i1) | (lane == i2)
    picked = jnp.where(chosen, probs, 0.0)
    return picked / jnp.sum(picked, axis=0, keepdims=True), best


INFO_ROWS = 8
INFO_GROUP = 4
INFO_RANK = 5
H2_WIDTH = D_MODEL + LANES


def _route_kernel(n0_tiles, xp_ref, xs_ref, *refs):
    y_refs = refs[:8]
    mod_ref, wo_ref, n2_ref, wrt_ref, rb_ref, x1_ref, h2_ref, info_ref, cend_ref, carry_s = refs[8:]
    i = pl.program_id(0)
    first = i < n0_tiles
    d = D_MODEL
    tb = TB_MOE
    mod = mod_ref[0, 0]

    @pl.when(i == 0)
    def _():
        carry_s[...] = jnp.zeros_like(carry_s)

    tm = TM_ROUTE
    mixed = None
    for j in range(4):
        y = jnp.where(first, y_refs[2 * j][...], y_refs[2 * j + 1][...])
        part = _mm(y.astype(BF16), wo_ref[j * MIX_WIDTH:(j + 1) * MIX_WIDTH, :])
        mixed = part if mixed is None else mixed + part
    x = jnp.where(first, xp_ref[...], xs_ref[...])
    x1 = x + mod[:, 2 * d:3 * d] * mixed
    x1_ref[...] = x1
    h2 = x1 * lax.rsqrt(jnp.mean(x1 * x1, axis=-1, keepdims=True) + EPS) * n2_ref[...]
    h2 = h2 * (1.0 + mod[:, 4 * d:5 * d]) + mod[:, 3 * d:4 * d]
    h2_ref[:, 0:d] = h2.astype(BF16)

    gates, best = _route(h2, wrt_ref, rb_ref)
    expert = _iota((N_EXPERTS, tm), 0).astype(F32)
    g4 = [jnp.sum(jnp.where(expert == E_PER_GROUP * best + j, gates, 0.0), axis=0, keepdims=True)
          for j in range(E_PER_GROUP)]
    onehot = jnp.where(expert == best, 1.0, 0.0)
    upper = (_iota((tm, tm), 0) <= _iota((tm, tm), 1)).astype(BF16)
    carry = carry_s[...]
    incl = _mm(onehot.astype(BF16), upper) + carry[:, 0:1]
    rank = jnp.sum(onehot * incl, axis=0, keepdims=True) - 1.0
    info_ref[...] = jnp.concatenate(g4 + [best, rank, jnp.zeros((INFO_ROWS - 6, tm), F32)], axis=0)
    for h in range(tm // tb):
        end = (h + 1) * tb
        cend_ref[h] = jnp.broadcast_to(incl[:, end - 1:end], (N_EXPERTS, LANES))
    carry_s[...] = jnp.broadcast_to(incl[:, tm - 1:tm], (N_EXPERTS, LANES))

    pieces = _split3(jnp.concatenate(g4 + [jnp.zeros((N_EXPERTS - E_PER_GROUP, tm), F32)], axis=0))
    pieces = jnp.concatenate(list(pieces) + [jnp.zeros((LANES - 3 * N_EXPERTS, tm), BF16)], axis=0)
    eye = (_iota((tm, tm), 0) == _iota((tm, tm), 1)).astype(BF16)
    h2_ref[:, d:d + LANES] = _mm(eye, pieces, _NT).astype(BF16)


def _route_call(layer, xp, xs, ys, mod_l, w_out, norm2_l, w_router, router_bias, dec_seq):
    n_prompt_tok = xp.shape[0]
    t_all = n_prompt_tok + xs.shape[0]
    tm = TM_ROUTE
    sub = tm // TB_MOE
    n0_tiles = n_prompt_tok // tm
    row = _mod_row_map(tm, n_prompt_tok, dec_seq)
    tok = lambda width: pl.BlockSpec((tm, width), lambda i: (i, 0))
    y_specs, y_args = [], []
    for yp, ysm in ys:
        y_specs += _pair_specs(tm, MIX_WIDTH, n0_tiles)
        y_args += [yp, ysm]
    return pl.pallas_call(
        functools.partial(_route_kernel, n0_tiles),
        out_shape=[jax.ShapeDtypeStruct((t_all, D_MODEL), F32), jax.ShapeDtypeStruct((t_all, H2_WIDTH), BF16),
                   jax.ShapeDtypeStruct((INFO_ROWS, t_all), F32),
                   jax.ShapeDtypeStruct((t_all // TB_MOE, N_EXPERTS, LANES), F32)],
        grid=(t_all // tm,),
        in_specs=_pair_specs(tm, D_MODEL, n0_tiles) + y_specs + [
            pl.BlockSpec((1, 1, 1, N_MOD * D_MODEL), lambda i: (0, row(i), 0, 0)),
            _const_layer_spec(w_out.shape, layer), _const_spec((1, D_MODEL)),
            _const_spec((N_EXPERTS, D_MODEL)), _const_spec((N_EXPERTS, 1)),
        ],
        out_specs=[tok(D_MODEL), tok(H2_WIDTH), pl.BlockSpec((INFO_ROWS, tm), lambda i: (0, i)),
                   pl.BlockSpec((sub, N_EXPERTS, LANES), lambda i: (i, 0, 0))],
        scratch_shapes=[pltpu.VMEM((N_EXPERTS, LANES), F32)],
        compiler_params=_params(("arbitrary",)),
        name="route",
    )(xp, xs, *y_args, mod_l, w_out, norm2_l, w_router.T, router_bias.reshape(N_EXPERTS, 1))


def _moe_plan(cend, info):
    ts = TS_MOE
    tb = TR_MOE
    n_blocks = cend.shape[0]
    t_all = n_blocks * TB_MOE
    n_tiles = t_all // ts + N_GROUPS
    sub_per_tile = ts // tb
    n_sub = n_tiles * sub_per_tile
    cend = cend[:, :N_GROUPS, 0].astype(jnp.int32)
    cum = jnp.concatenate([jnp.zeros((1, N_GROUPS), jnp.int32), cend[:-1]], axis=0)
    cnt = cend - cum
    grp = info[INFO_GROUP].astype(jnp.int32)
    rank = info[INFO_RANK].astype(jnp.int32)
    tot = cend[-1]
    padded = ((tot + ts - 1) // ts) * ts
    off = jnp.cumsum(padded) - padded
    pos = rank + sum(jnp.where(grp == g, off[g], 0) for g in range(N_GROUPS))

    sub_start = jnp.arange(n_sub, dtype=jnp.int32) * tb
    in_g = (sub_start[:, None] >= off[None, :]) & (sub_start[:, None] < (off + padded)[None, :])
    g_of = jnp.argmax(in_g, axis=1).astype(jnp.int32)
    r0 = sub_start - off[g_of]
    r1 = jnp.minimum(r0 + tb, tot[g_of])
    live = jnp.any(in_g, axis=1) & (r1 > r0)
    cend_g = cend[:, g_of]
    blo = jnp.where(live, jnp.sum(cend_g <= r0[None, :], axis=0), 0).astype(jnp.int32)
    bhi = jnp.where(live, jnp.sum(cend_g <= (r1 - 1)[None, :], axis=0), -1).astype(jnp.int32)

    used = jnp.any(in_g, axis=1)[::sub_per_tile]
    tile_group = g_of[::sub_per_tile]
    eidx = E_PER_GROUP * tile_group[:, None] + jnp.arange(E_PER_GROUP, dtype=jnp.int32)[None, :]
    n_used = jnp.sum(used.astype(jnp.int32))
    last = eidx[jnp.maximum(n_used - 1, 0), E_PER_GROUP - 1]
    eidx = jnp.where(used[:, None], eidx, last).reshape(-1).astype(jnp.int32)

    start = off[None, :] + cum
    end = off[None, :] + cend
    has = cnt > 0
    slo = jnp.where(has, start // tb, 0).reshape(-1).astype(jnp.int32)
    shi = jnp.where(has, (end - 1) // tb, -1).reshape(-1).astype(jnp.int32)
    return dict(pos=pos, eidx=eidx, used=used.astype(jnp.int32), blo=blo, bhi=bhi, slo=slo, shi=shi,
                n_tiles=n_tiles)


def _expert_kernel(eidx_ref, used_ref, blo_ref, bhi_ref, h2_ref, pos_ref, wg_ref, wu_ref, wd_ref,
                   ys_ref, xs_s, acc_s, ax_s):
    i = pl.program_id(0)
    j = pl.program_id(1)
    used = used_ref[i] > 0
    tb, tr = TB_MOE, TR_MOE
    sub_per_tile = TS_MOE // tr

    @pl.when(used & (j == 0))
    def _():
        r_idx = _iota((tr, tb), 0)
        for part in range(sub_per_tile):
            s = i * sub_per_tile + part
            base = s * tr
            ax_s[...] = jnp.zeros_like(ax_s)

            def gather(b, carry):
                sel = jnp.where(pos_ref[b] - base == r_idx, 1.0, 0.0).astype(BF16)
                tok = pl.ds(pl.multiple_of(b * tb, tb), tb)
                ax_s[...] += _mm(sel, h2_ref[tok, :])
                return carry

            lax.fori_loop(blo_ref[s], bhi_ref[s] + 1, gather, 0)
            xs_s[part * tr:(part + 1) * tr, :] = ax_s[...].astype(BF16)
        acc_s[...] = jnp.zeros_like(acc_s)

    @pl.when(used)
    def _():
        x = xs_s[:, 0:D_MODEL]
        pieces = xs_s[:, D_MODEL:H2_WIDTH].astype(F32)
        ge = jnp.sum(jnp.where(_iota(pieces.shape, 1) % N_EXPERTS == j, pieces, 0.0), axis=-1, keepdims=True)
        hh = _silu(_mm(x, wg_ref[0, 0].astype(BF16))) * _mm(x, wu_ref[0, 0].astype(BF16)) * ge
        acc_s[...] += _mm(hh.astype(BF16), wd_ref[0, 0].astype(BF16))

    @pl.when(j == E_PER_GROUP - 1)
    def _():
        ys_ref[...] = jnp.where(used, acc_s[...], 0.0).astype(BF16)


def _expert_call(layer, plan, h2, w_gate, w_up, w_down):
    t_all = h2.shape[0]
    tb, ts = TB_MOE, TS_MOE
    n_tiles = plan["n_tiles"]
    pos_rows = plan["pos"].reshape(t_all // tb, 1, tb)
    weight = lambda shape: pl.BlockSpec((1, 1) + shape, lambda i, j, eidx, *_: (layer, eidx[i * E_PER_GROUP + j], 0, 0))
    grid_spec = pltpu.PrefetchScalarGridSpec(
        num_scalar_prefetch=4,
        grid=(n_tiles, E_PER_GROUP),
        in_specs=[
            _const_spec((t_all, H2_WIDTH)), _const_spec(pos_rows.shape),
            weight((D_MODEL, D_EXPERT)), weight((D_MODEL, D_EXPERT)), weight((D_EXPERT, D_MODEL)),
        ],
        out_specs=pl.BlockSpec((ts, D_MODEL), lambda i, j, *_: (i, 0)),
        scratch_shapes=[pltpu.VMEM((ts, H2_WIDTH), BF16), pltpu.VMEM((ts, D_MODEL), F32),
                        pltpu.VMEM((TR_MOE, H2_WIDTH), F32)],
    )
    return pl.pallas_call(
        _expert_kernel,
        out_shape=jax.ShapeDtypeStruct((n_tiles * ts, D_MODEL), BF16),
        grid_spec=grid_spec,
        compiler_params=_params(("arbitrary", "arbitrary")),
        name=f"experts_{layer}",
    )(plan["eidx"], plan["used"], plan["blo"], plan["bhi"], h2, pos_rows, w_gate, w_up, w_down)


def _combine_kernel(final, n0_tiles, slo_ref, shi_ref, ys_ref, pos_ref, x1_ref, mod_ref, nf_ref,
                    op_ref, os_ref, acc_s):
    b = pl.program_id(0)
    d = D_MODEL
    tb, tr = TB_MOE, TR_MOE
    acc_s[...] = jnp.zeros_like(acc_s)
    c_idx = _iota((tb, tr), 1)
    pos = pos_ref[...]
    for g in range(N_GROUPS):
        def scatter(s, carry):
            sel = jnp.where(pos - s * tr == c_idx, 1.0, 0.0).astype(BF16)
            acc_s[...] += _mm(sel, ys_ref[pl.ds(pl.multiple_of(s * tr, tr), tr), :])
            return carry

        lax.fori_loop(slo_ref[b * N_GROUPS + g], shi_ref[b * N_GROUPS + g] + 1, scatter, 0)

    def result():
        x2 = x1_ref[...] + mod_ref[0, 0][:, 5 * d:6 * d] * acc_s[...]
        if final:
            x2 = x2 * lax.rsqrt(jnp.mean(x2 * x2, axis=-1, keepdims=True) + EPS) * nf_ref[...]
        return x2

    @pl.when(b < n0_tiles)
    def _():
        op_ref[...] = result()

    @pl.when(b >= n0_tiles)
    def _():
        os_ref[...] = result()


def _combine_call(final, plan, ys_sorted, x1, mod_l, norm_final, n_prompt_tok, dec_seq):
    t_all = x1.shape[0]
    tb = TB_MOE
    n0_tiles = n_prompt_tok // tb
    row = _mod_row_map(tb, n_prompt_tok, dec_seq)
    grid_spec = pltpu.PrefetchScalarGridSpec(
        num_scalar_prefetch=2,
        grid=(t_all // tb,),
        in_specs=[
            _const_spec(ys_sorted.shape),
            pl.BlockSpec((tb, 1), lambda b, *_: (b, 0)),
            pl.BlockSpec((tb, D_MODEL), lambda b, *_: (b, 0)),
            pl.BlockSpec((1, 1, 1, N_MOD * D_MODEL), lambda b, *_: (0, row(b), 0, 0)),
            _const_spec((1, D_MODEL)),
        ],
        out_specs=_pair_specs(tb, D_MODEL, n0_tiles),
        scratch_shapes=[pltpu.VMEM((tb, D_MODEL), F32)],
    )
    return pl.pallas_call(
        functools.partial(_combine_kernel, final, n0_tiles),
        out_shape=[jax.ShapeDtypeStruct((n_prompt_tok, D_MODEL), F32),
                   jax.ShapeDtypeStruct((t_all - n_prompt_tok, D_MODEL), F32)],
        grid_spec=grid_spec,
        compiler_params=_params(("arbitrary",)),
        name="combine",
    )(plan["slo"], plan["shi"], ys_sorted, plan["pos"].reshape(t_all, 1), x1, mod_l, norm_final)


def _rg_gate_weights(wa, ba, wx, bx):
    eye = jnp.eye(RG_HEADS, dtype=wa.dtype)

    def dense(wd):
        return jnp.einsum("hij,hg->higj", wd, eye).reshape(RG_WIDTH, RG_WIDTH)

    w = jnp.concatenate([dense(wa[0]), dense(wa[1]), dense(wx[0]), dense(wx[1])], axis=1)
    b = jnp.concatenate([ba[0], ba[1], bx[0], bx[1]], axis=0).reshape(1, 4 * RG_WIDTH)
    return w, b


def kernel(x_prompt, x_sample, state_rglru, state_hgrn, state_ret, c, c_ctx, norm1, norm2, norm_final, w_ada, b_ada, w_in, w_out, hy_conv_w, hy_conv_b, hy_w1, hy_b1, hy_w2, hy_b2, hy_w3, hy_d, rg_conv_w, rg_conv_b, rg_wa, rg_ba, rg_wx, rg_bx, rg_lambda, hg_lb, hg_norm, ret_decay, w_router, router_bias, w_gate, w_up, w_down):
    batch, seq, d = x_prompt.shape
    dec_batch, dec_seq, _ = x_sample.shape
    assert d == D_MODEL and dec_batch + 1 <= COND_ROWS
    n_prompt_tok = batch * seq

    lb_cum = jnp.cumsum(jax.nn.softmax(hg_lb.astype(F32), axis=0), axis=0)
    lb_all = lb_cum - lb_cum[0:1]

    cond = jnp.zeros((COND_ROWS, d), F32).at[0].set(c_ctx).at[1:1 + dec_batch].set(c)
    mod = _modulation(cond, w_ada, b_ada).reshape(DEPTH, COND_ROWS, 1, N_MOD * d)

    w_in_b = w_in.astype(BF16)
    w_out_b = w_out.astype(BF16)

    passes = (
        dict(tok0=0, batch=batch, seq=seq, rope=False),
        dict(tok0=n_prompt_tok, batch=dec_batch, seq=dec_seq, rope=True),
    )
    filters = {p["seq"]: _hyena_filters(p["seq"], hy_w1, hy_b1, hy_w2, hy_b2, hy_w3) for p in passes}
    tables = {p["seq"]: _hyena_tables(p["seq"]) for p in passes}

    xp = x_prompt.reshape(-1, d)
    xs = x_sample.reshape(-1, d)
    new_rg, new_hg, new_ret = [], [], []
    for l in range(DEPTH):
        hy_all, rg_all, hg_all, ret_all = _in_projection(
            l, xp, xs, mod[l:l + 1], norm1[l].reshape(1, d), w_in_b, dec_seq)
        wg, bg = _rg_gate_weights(rg_wa[l], rg_ba[l], rg_wx[l], rg_bx[l])
        sp = jax.nn.softplus(-rg_lambda[l])
        ys = [[], [], [], []]
        for pi, p in enumerate(passes):
            first = pi == 0
            geom = (p["tok0"], p["batch"], p["seq"])
            y_hy = _hyena(hy_all, *geom, hy_conv_w[l], hy_conv_b[l].reshape(1, -1), hy_d[l],
                          (filters[p["seq"]], l), tables[p["seq"]])
            y_rg, *st_rg = _rglru(rg_all, *geom, rg_conv_w[l], rg_conv_b[l].reshape(1, -1), wg, bg, sp,
                                  None if first else (state_rglru, l), first)
            y_hg, *st_hg = _hgrn(hg_all, *geom, lb_all[l], hg_norm[l].reshape(1, -1),
                                 None if first else (state_hgrn, l), first)
            y_ret, *st_ret = _retention(ret_all, *geom, p["rope"], ret_decay[l],
                                        None if first else (state_ret, l), first)
            for lst, y in zip(ys, (y_hy, y_rg, y_hg, y_ret)):
                lst.append(y)
            if first:
                new_rg.append(st_rg[0])
                new_hg.append(st_hg[0])
                new_ret.append(st_ret[0])
        x1, h2, info, cend = _route_call(l, xp, xs, ys, mod[l:l + 1], w_out_b, norm2[l].reshape(1, d),
                                         w_router, router_bias, dec_seq)
        plan = _moe_plan(cend, info)
        ys_sorted = _expert_call(l, plan, h2, w_gate, w_up, w_down)
        xp, xs = _combine_call(l == DEPTH - 1, plan, ys_sorted, x1, mod[l:l + 1], norm_final.reshape(1, d),
                               n_prompt_tok, dec_seq)

    return (xp.reshape(batch, seq, d), xs.reshape(dec_batch, dec_seq, d), jnp.stack(new_rg, axis=1),
            jnp.stack(new_hg, axis=1), jnp.stack(new_ret, axis=1))
```

```python
import functools
import math

import numpy as np
import jax
import jax.numpy as jnp
from jax import lax
from jax.experimental import pallas as pl
from jax.experimental.pallas import tpu as pltpu

F32 = jnp.float32
BF16 = jnp.bfloat16

D_MODEL = 1024
DEPTH = 2
GRID_W = 64
HY_WIDTH = 256
RG_WIDTH = 256
HG_WIDTH = 256
RET_WIDTH = 256
MIX_WIDTH = 256
HY_ORDER = 2
HY_EMB = 33
HY_BANDS = 16
HY_FFN = 64
HY_DECAY_TARGET = 1e-2
HY_DECAY_SHORT = 0.3
HY_DECAY_LONG = 1.5
HY_GROUP_COLS = 512
RG_HEADS = 8
RG_HEAD_DIM = 32
RG_C = 8.0
RG_CHUNK = 16
N_HEADS = 4
HEAD_DIM = 64
HG_CHUNK = 64
RET_CHUNK = 256
ROPE_BASE = 10000.0
N_EXPERTS = 16
N_GROUPS = 4
E_PER_GROUP = 4
D_EXPERT = 512
N_MOD = 6
EPS = 1e-6
PROJ_HY = 3 * HY_WIDTH
PROJ_RG = 2 * RG_WIDTH
PROJ_HG = 5 * HG_WIDTH
PROJ_RET = 4 * RET_WIDTH
PROJ_WIDTH = PROJ_HY + PROJ_RG + PROJ_HG + PROJ_RET
COND_ROWS = 16
LANES = 128
VMEM_LIMIT = 56 * 1024 * 1024
TM_PROJ = 512
TB_MOE = 512
TR_MOE = 256
TM_ROUTE = TB_MOE
TS_MOE = 1024
TN_MOD = 1536

_NN = (((1,), (0,)), ((), ()))
_NT = (((1,), (1,)), ((), ()))


def _mm(a, b, dn=_NN):
    return lax.dot_general(a, b, dn, preferred_element_type=F32)


def _split2(x):
    hi = x.astype(BF16)
    lo = (x - hi.astype(F32)).astype(BF16)
    return hi, lo


def _split3(x):
    hi = x.astype(BF16)
    r = x - hi.astype(F32)
    mid = r.astype(BF16)
    lo = (r - mid.astype(F32)).astype(BF16)
    return hi, mid, lo


def _mm3(a, b, dn=_NN):
    ah, al = _split2(a)
    bh, bl = _split2(b)
    return _mm(ah, bh, dn) + (_mm(ah, bl, dn) + _mm(al, bh, dn))


def _mm_exact_rhs(a, b_bf16):
    a1, a2, a3 = _split3(a)
    return _mm(a1, b_bf16) + (_mm(a2, b_bf16) + _mm(a3, b_bf16))


def _sigmoid(x):
    return 1.0 / (1.0 + jnp.exp(-x))


def _silu(x):
    return x * _sigmoid(x)


def _log_sigmoid(x):
    return jnp.minimum(x, 0.0) - jnp.log(1.0 + jnp.exp(-jnp.abs(x)))


def _gelu_tanh(x):
    return 0.5 * x * (1.0 + jnp.tanh(math.sqrt(2.0 / math.pi) * (x + 0.044715 * (x * x * x))))


def _iota(shape, dim):
    return lax.broadcasted_iota(jnp.int32, shape, dim)


def _shift_rows(u, k, row):
    n = u.shape[0]
    if k == 0:
        return u
    r = pltpu.roll(u, (-k) % n, axis=0)
    if k < 0:
        return jnp.where(row >= -k, r, 0.0)
    return jnp.where(row < n - k, r, 0.0)


def _head_mask(n_rows_per_head):
    shape = (N_HEADS * n_rows_per_head, MIX_WIDTH)
    return (_iota(shape, 0) // n_rows_per_head) == (_iota(shape, 1) // HEAD_DIM)


def _block_diag_mask():
    shape = (MIX_WIDTH, MIX_WIDTH)
    return (_iota(shape, 0) // HEAD_DIM) == (_iota(shape, 1) // HEAD_DIM)


def _head_rmsnorm(o, ones_bd):
    ms = _mm_exact_rhs(o * o, ones_bd)
    return o * lax.rsqrt(ms + EPS)


def _load_state(s0_ref, d):
    zero = jnp.zeros((HEAD_DIM, HEAD_DIM), F32)
    rows = []
    for h in range(N_HEADS):
        blk = s0_ref[0, d, h].T
        rows.append(jnp.concatenate([blk if g == h else zero for g in range(N_HEADS)], axis=1))
    return jnp.concatenate(rows, axis=0)


def _store_state(st_ref, d, st):
    for h in range(N_HEADS):
        lo, hi = h * HEAD_DIM, (h + 1) * HEAD_DIM
        st_ref[0, d, h] = st[lo:hi, lo:hi].T


def _params(sem, vmem=VMEM_LIMIT):
    return pltpu.CompilerParams(dimension_semantics=sem, vmem_limit_bytes=vmem)


def _const_spec(shape):
    nd = len(shape)
    return pl.BlockSpec(shape, lambda *_: (0,) * nd, pipeline_mode=pl.Buffered(1))


def _const_layer_spec(shape, layer):
    nd = len(shape)
    return pl.BlockSpec((None,) + tuple(shape[1:]), lambda *_: (layer,) + (0,) * (nd - 1),
                        pipeline_mode=pl.Buffered(1))


def _mod_kernel(cond_ref, w_ref, b_ref, o_ref):
    o_ref[0] = _mm3(_silu(cond_ref[...]), w_ref[0]) + b_ref[0]


def _modulation(cond, w_ada, b_ada):
    tn = TN_MOD
    n_mod = N_MOD * D_MODEL
    return pl.pallas_call(
        _mod_kernel,
        out_shape=jax.ShapeDtypeStruct((DEPTH, COND_ROWS, n_mod), F32),
        grid=(DEPTH, n_mod // tn),
        in_specs=[
            pl.BlockSpec((COND_ROWS, D_MODEL), lambda l, j: (0, 0)),
            pl.BlockSpec((1, D_MODEL, tn), lambda l, j: (l, 0, j)),
            pl.BlockSpec((1, 1, tn), lambda l, j: (l, 0, j)),
        ],
        out_specs=pl.BlockSpec((1, COND_ROWS, tn), lambda l, j: (l, 0, j)),
        compiler_params=_params(("parallel", "parallel")),
        name="modulation",
    )(cond, w_ada, b_ada.reshape(DEPTH, 1, n_mod))


def _pair_specs(tm, width, n0_tiles):
    return [pl.BlockSpec((tm, width), lambda i, *_: (jnp.minimum(i, n0_tiles - 1), 0)),
            pl.BlockSpec((tm, width), lambda i, *_: (jnp.maximum(i - n0_tiles, 0), 0))]


def _proj_kernel(n0_tiles, xp_ref, xs_ref, mod_ref, n1_ref, w_ref, hy_ref, rg_ref, hg_ref, ret_ref):
    x = jnp.where(pl.program_id(0) < n0_tiles, xp_ref[...], xs_ref[...])
    mod = mod_ref[0, 0]
    sh1 = mod[:, 0:D_MODEL]
    sc1 = mod[:, D_MODEL:2 * D_MODEL]
    h = x * lax.rsqrt(jnp.mean(x * x, axis=-1, keepdims=True) + EPS) * n1_ref[...]
    h = (h * (1.0 + sc1) + sh1).astype(BF16)
    c0 = 0
    for ref, width in ((hy_ref, PROJ_HY), (rg_ref, PROJ_RG), (hg_ref, PROJ_HG), (ret_ref, PROJ_RET)):
        ref[...] = _mm(h, w_ref[:, c0:c0 + width])
        c0 += width


def _mod_row_map(tm, n_prompt_tok, dec_seq):
    n_prompt_tiles = n_prompt_tok // tm

    def row(i):
        return jnp.where(i < n_prompt_tiles, 0, 1 + (i * tm - n_prompt_tok) // dec_seq)

    return row


def _in_projection(layer, xp, xs, mod_l, norm1_l, w_in, dec_seq):
    n_prompt_tok = xp.shape[0]
    t_all = n_prompt_tok + xs.shape[0]
    tm = TM_PROJ
    n0_tiles = n_prompt_tok // tm
    row = _mod_row_map(tm, n_prompt_tok, dec_seq)
    widths = (PROJ_HY, PROJ_RG, PROJ_HG, PROJ_RET)
    return pl.pallas_call(
        functools.partial(_proj_kernel, n0_tiles),
        out_shape=[jax.ShapeDtypeStruct((t_all, w), F32) for w in widths],
        grid=(t_all // tm,),
        in_specs=_pair_specs(tm, D_MODEL, n0_tiles) + [
            pl.BlockSpec((1, 1, 1, N_MOD * D_MODEL), lambda i: (0, row(i), 0, 0)),
            _const_spec((1, D_MODEL)),
            _const_layer_spec(w_in.shape, layer),
        ],
        out_specs=[pl.BlockSpec((tm, w), lambda i: (i, 0)) for w in widths],
        compiler_params=_params(("parallel",)),
        name="in_projection",
    )(xp, xs, mod_l, norm1_l, w_in)


def _dft_tables(seq):
    k = np.arange(seq, dtype=np.int64)
    m = (k[:, None] * k[None, :]) % (2 * seq)
    ang = np.pi * m.astype(np.float64) / seq
    return np.cos(ang), np.sin(ang)


def _hyena_tables(seq):
    cos, sin = _dft_tables(seq)
    sign = np.where(np.arange(seq) % 2 == 0, 1.0, -1.0)
    fwd = np.concatenate([cos, sign[None, :], sin[1:]], axis=0)
    wk = np.full((seq,), 2.0)
    wk[0] = 1.0
    inv_cos = (cos * wk[None, :]) / (2.0 * seq)
    inv_nyq = sign[:, None] / (2.0 * seq)
    inv_sin = 2.0 * sin[:, 1:] / (2.0 * seq)
    inv = np.concatenate([inv_cos, inv_nyq, inv_sin], axis=1)

    return jnp.asarray(fwd, F32).astype(BF16), jnp.asarray(inv, F32).astype(BF16)


def _filter_embedding(seq):
    t = np.arange(seq, dtype=np.float64)
    t_norm = t / max(seq - 1, 1)
    bands = np.linspace(1e-4, HY_BANDS - 1, HY_BANDS)
    ang = (2.0 * np.pi / seq) * t[:, None] * bands[None, :]
    z = np.concatenate([t_norm[:, None], np.cos(ang), np.sin(ang)], axis=-1)
    z = np.pad(z, ((0, 0), (0, LANES - HY_EMB)))
    deltas = np.abs(np.linspace(math.log(HY_DECAY_TARGET) / HY_DECAY_LONG,
                                math.log(HY_DECAY_TARGET) / HY_DECAY_SHORT, HY_WIDTH))
    window = np.exp(-t_norm[:, None] * deltas[None, :])
    return jnp.asarray(z, F32), jnp.asarray(window, F32)


def _filter_kernel(z_ref, win_ref, cos_ref, sin_ref, w1_ref, b1_ref, w2_ref, b2_ref, w3_ref, o_ref):
    seq = z_ref.shape[0]
    h = jnp.sin(_mm3(z_ref[...], w1_ref[0]) + b1_ref[0])
    h = jnp.sin(_mm3(h, w2_ref[0]) + b2_ref[0])
    h = _mm3(h, w3_ref[0])
    win = win_ref[...]
    row = _iota((seq, 1), 0)
    sums, diffs = [], []
    for o in range(HY_ORDER):
        c0 = o * 2 * HY_WIDTH
        hf = h[:, c0:c0 + HY_WIDTH] * win
        hb = h[:, c0 + HY_WIDTH:c0 + 2 * HY_WIDTH] * win
        ssq = jnp.sum(hf * hf + hb * hb, axis=0, keepdims=True)
        inv = lax.rsqrt(ssq + EPS)
        hf = hf * inv
        hb = jnp.where(row == 0, 0.0, hb * inv)
        sums.append(hf + hb)
        diffs.append(hf - hb)
    hsum = jnp.concatenate(sums, axis=1)
    hdiff = jnp.concatenate(diffs, axis=1)
    h_re = _mm(cos_ref[...], hsum.astype(BF16))
    h_im = _mm(sin_ref[...], hdiff.astype(BF16))
    sign = jnp.where(row % 2 == 0, 1.0, -1.0)
    h_nyq = jnp.sum(sign * hsum, axis=0, keepdims=True)
    o_ref[0, 0] = h_re
    o_ref[0, 1] = h_im
    o_ref[0, 2] = jnp.where(row == 0, h_nyq, h_re)


def _hyena_filters(seq, w1, b1, w2, b2, w3):
    z, window = _filter_embedding(seq)
    cos, sin = _dft_tables(seq)
    n_out = HY_ORDER * 2 * HY_WIDTH
    w1p = jnp.pad(w1, ((0, 0), (0, LANES - HY_EMB), (0, LANES - HY_FFN)))
    b1p = jnp.pad(b1, ((0, 0), (0, LANES - HY_FFN))).reshape(DEPTH, 1, LANES)
    w2p = jnp.pad(w2, ((0, 0), (0, LANES - HY_FFN), (0, LANES - HY_FFN)))
    b2p = jnp.pad(b2, ((0, 0), (0, LANES - HY_FFN))).reshape(DEPTH, 1, LANES)
    w3p = jnp.pad(w3, ((0, 0), (0, LANES - HY_FFN), (0, 0)))
    per_layer = lambda shape: pl.BlockSpec((1,) + shape, lambda l: (l,) + (0,) * len(shape))
    return pl.pallas_call(
        _filter_kernel,
        out_shape=jax.ShapeDtypeStruct((DEPTH, 3, seq, HY_ORDER * HY_WIDTH), F32),
        grid=(DEPTH,),
        in_specs=[
            _const_spec((seq, LANES)), _const_spec((seq, HY_WIDTH)),
            _const_spec((seq, seq)), _const_spec((seq, seq)),
            per_layer((LANES, LANES)), per_layer((1, LANES)),
            per_layer((LANES, LANES)), per_layer((1, LANES)),
            per_layer((LANES, n_out)),
        ],
        out_specs=pl.BlockSpec((1, 3, seq, HY_ORDER * HY_WIDTH), lambda l: (l, 0, 0, 0)),
        compiler_params=_params(("parallel",)),
        name=f"hyena_filters_{seq}",
    )(z, window, jnp.asarray(cos, F32).astype(BF16), jnp.asarray(sin, F32).astype(BF16), w1p, b1p, w2p, b2p, w3p)


def _hyena_kernel(group, u_ref, cw_ref, cb_ref, d_ref, filt_ref, fwd_ref, inv_ref, y_ref):
    seq = u_ref.shape[0] // group
    cw = cw_ref[...]
    row = _iota((seq, 1), 0)
    ucs = []
    for s in range(group):
        u = u_ref[s * seq:(s + 1) * seq, :]
        ucs.append(cb_ref[...] + cw[0:1] * _shift_rows(u, -1, row) + cw[1:2] * u + cw[2:3] * _shift_rows(u, 1, row))
    side = lambda c0: jnp.concatenate([uc[:, c0:c0 + HY_WIDTH] for uc in ucs], axis=1)
    tile = lambda a: jnp.concatenate([a] * group, axis=1)
    gates = (side(HY_WIDTH), side(2 * HY_WIDTH))
    d = d_ref[...]
    z = side(0)
    for o in range(HY_ORDER):
        c0 = o * HY_WIDTH
        spec = _mm(fwd_ref[...], z.astype(BF16))
        s_re, s_im = spec[:seq], spec[seq:]
        a = tile(filt_ref[0, :, c0:c0 + HY_WIDTH])
        b = tile(filt_ref[1, :, c0:c0 + HY_WIDTH])
        c = tile(filt_ref[2, :, c0:c0 + HY_WIDTH])
        prod = jnp.concatenate([s_re * a - s_im * b, s_re * b + s_im * c], axis=0)
        conv = _mm(inv_ref[...], prod.astype(BF16))
        z = gates[o] * (conv + tile(d[o:o + 1]) * z)
    for s in range(group):
        y_ref[s * seq:(s + 1) * seq, :] = z[:, s * HY_WIDTH:(s + 1) * HY_WIDTH]


def _mixer_call(body, name, proj_all, tok0, batch, seq, consts, batch_ins, state_shapes, scratch, group=1):
    proj_width = proj_all.shape[1]
    rows = group * seq
    blk0 = tok0 // rows

    def batch_spec(shape):
        nd = len(shape)
        return pl.BlockSpec((group,) + tuple(shape[1:]), lambda b: (b,) + (0,) * (nd - 1))

    def layer_spec(shape, layer):
        nd = len(shape)
        return pl.BlockSpec((group, None) + tuple(shape[2:]), lambda b: (b, layer) + (0,) * (nd - 2))

    in_specs = [pl.BlockSpec((rows, proj_width), lambda b: (blk0 + b, 0))]
    in_specs += [_const_layer_spec(a[0].shape, a[1]) if isinstance(a, tuple) else _const_spec(a.shape)
                 for a in consts]
    consts = [a[0] if isinstance(a, tuple) else a for a in consts]
    in_specs += [layer_spec(a.shape, layer) for a, layer in batch_ins]
    batch_ins = [a for a, _ in batch_ins]
    out_shape = [jax.ShapeDtypeStruct((batch * seq, MIX_WIDTH), F32)]
    out_specs = [pl.BlockSpec((rows, MIX_WIDTH), lambda b: (b, 0))]
    for shape in state_shapes:
        out_shape.append(jax.ShapeDtypeStruct(shape, F32))
        out_specs.append(batch_spec(shape))
    return pl.pallas_call(
        body, out_shape=out_shape, grid=(batch // group,), in_specs=in_specs, out_specs=out_specs,
        scratch_shapes=scratch,
        compiler_params=_params(("parallel",)),
        name=f"{name}_{seq}",
    )(proj_all, *consts, *batch_ins)


def _hyena(u_all, tok0, batch, seq, conv_w, conv_b, d_bias, filt_l, tables):
    consts = [conv_w, conv_b, d_bias, filt_l, *tables]
    group = HY_GROUP_COLS // HY_WIDTH
    return _mixer_call(functools.partial(_hyena_kernel, group), "hyena", u_all, tok0, batch, seq, consts,
                       [], [], [], group=group)[0]


def _rglru_kernel(has_s0, emit_state, rg_ref, cw_ref, cb_ref, wg_ref, bg_ref, sp_ref, *refs):
    h0_ref = refs[0] if has_s0 else None
    y_ref = refs[1 if has_s0 else 0]
    st_ref = refs[-1] if emit_state else None
    seq = rg_ref.shape[0]
    w = RG_WIDTH
    xr = rg_ref[:, 0:w]
    gate = rg_ref[:, w:2 * w]
    row = _iota((seq, 1), 0)
    cw = cw_ref[...]
    xc = (cb_ref[...] + cw[0:1] * _shift_rows(xr, -2, row) + cw[1:2] * _shift_rows(xr, -1, row)
          + cw[2:3] * xr + cw[3:4] * _shift_rows(xr, 1, row))
    g = _sigmoid(_mm(xc.astype(BF16), wg_ref[...].astype(BF16)) + bg_ref[...])
    sp = sp_ref[...]
    c = RG_CHUNK
    n = seq // c
    pos = row % c
    hs = []
    for d in range(2):
        forward = d == 0
        r = g[:, d * w:(d + 1) * w]
        i = g[:, (2 + d) * w:(3 + d) * w]
        log_a = -RG_C * r * sp[d:d + 1]
        a = jnp.exp(log_a)
        b = jnp.sqrt(jnp.tanh(-log_a) * (1.0 + a * a)) * (i * xc)
        step = 1
        while step < c:
            keep = (pos >= step) if forward else (pos < c - step)
            shift = step if forward else seq - step
            a_s = pltpu.roll(a, shift, axis=0)
            b_s = pltpu.roll(b, shift, axis=0)
            b = jnp.where(keep, a * b_s + b, b)
            a = jnp.where(keep, a * a_s, a)
            step *= 2
        a3 = a.reshape(n, c, w)
        b3 = b.reshape(n, c, w)
        edge = c - 1 if forward else 0
        a_end = a3[:, edge:edge + 1, :]
        b_end = b3[:, edge:edge + 1, :]
        h = h0_ref[0, d:d + 1, :] if has_s0 else jnp.zeros((1, w), F32)
        h_in = [None] * n
        for ci in (range(n) if forward else range(n - 1, -1, -1)):
            h_in[ci] = h
            h = b_end[ci] + a_end[ci] * h
        hs.append((b3 + a3 * jnp.stack(h_in, axis=0)).reshape(seq, w))
    y_ref[...] = (hs[0] + hs[1]) * _gelu_tanh(gate)
    if emit_state:
        st_ref[0, 0:1, :] = hs[0][seq - 1:seq]
        st_ref[0, 1:2, :] = hs[1][0:1]


def _rglru(rg_all, tok0, batch, seq, conv_w, conv_b, w_gates, b_gates, softplus_neg_lam, h0, emit_state):
    has_s0 = h0 is not None
    consts = [conv_w, conv_b, w_gates, b_gates, softplus_neg_lam]
    return _mixer_call(functools.partial(_rglru_kernel, has_s0, emit_state), "rglru", rg_all, tok0, batch, seq,
                       consts, [h0] if has_s0 else [],
                       [(batch, 2, RG_WIDTH)] if emit_state else [], [])


def _hgrn_kernel(has_s0, emit_state, hg_ref, lb_ref, gain_ref, *refs):
    s0_ref = refs[0] if has_s0 else None
    y_ref = refs[1 if has_s0 else 0]
    st_ref = refs[-1] if emit_state else None
    seq = hg_ref.shape[0]
    w = HG_WIDTH
    c = HG_CHUNK
    n = seq // c
    mid = c // 2
    chunks = lambda a: a.reshape(n, c, w)
    pos = _iota((seq, 1), 0) % c
    stack_mask = _head_mask(c)[None]
    bd_mask = _block_diag_mask()
    pair_shape = (c, N_HEADS * c)
    t_idx = _iota(pair_shape, 0)
    s_idx = _iota(pair_shape, 1) % c

    def stack_heads(a3):
        return jnp.where(stack_mask, jnp.concatenate([a3] * N_HEADS, axis=1), 0.0).astype(BF16)

    q3 = chunks(_silu(hg_ref[:, 0:w]))
    v3 = chunks(hg_ref[:, 3 * w:4 * w])
    v_stack = stack_heads(v3)
    v_t = jnp.swapaxes(v3, 1, 2).astype(BF16)
    lbv = lb_ref[...]
    o_sum = None
    finals = []
    for d in range(2):
        forward = d == 0
        f_pre = hg_ref[:, (1 + d) * w:(2 + d) * w]
        lo = lbv[d:d + 1]
        a1 = jnp.log(lo)
        a2 = jnp.log(1.0 - lo) + _log_sigmoid(f_pre)
        g = jnp.maximum(a1, a2) + jnp.log(1.0 + jnp.exp(-jnp.abs(a1 - a2)))
        k3 = chunks((1.0 - lo) * _sigmoid(-f_pre))
        step = 1
        while step < c:
            if forward:
                g = jnp.where(pos >= step, g + pltpu.roll(g, step, axis=0), g)
            else:
                g = jnp.where(pos < c - step, g + pltpu.roll(g, seq - step, axis=0), g)
            step *= 2
        g3 = chunks(g)
        g_ref = g3[:, mid:mid + 1, :]
        g_tot = g3[:, c - 1:c, :] if forward else g3[:, 0:1, :]
        q_in = (q3 * jnp.exp(g3)).astype(BF16)
        q_sc = (q3 * jnp.exp(g3 - g_ref)).astype(BF16)
        k_stack = stack_heads(k3 * jnp.exp(g_ref - g3))
        k_out = (k3 * jnp.exp(g_tot - g3)).astype(BF16)
        pair = jnp.einsum("ntl,nrl->ntr", q_sc, k_stack, preferred_element_type=F32)
        causal = (t_idx >= s_idx) if forward else (t_idx <= s_idx)
        pair = jnp.where(causal[None], pair, 0.0).astype(BF16)
        o_intra = jnp.einsum("ntr,nrv->ntv", pair, v_stack, preferred_element_type=F32)
        upd = jnp.einsum("nvs,nsk->nvk", v_t, k_out, preferred_element_type=F32)
        decay = jnp.exp(g_tot)
        st = _load_state(s0_ref, d) if has_s0 else jnp.zeros((w, w), F32)
        o_inter = [None] * n
        for ci in (range(n) if forward else range(n - 1, -1, -1)):
            o_inter[ci] = _mm(q_in[ci], st.astype(BF16), _NT)
            st = st * decay[ci] + jnp.where(bd_mask, upd[ci], 0.0)
        o_dir = o_intra + jnp.stack(o_inter, axis=0)
        o_sum = o_dir if o_sum is None else o_sum + o_dir
        finals.append(st)

    ones_bd = jnp.where(bd_mask, 1.0 / HEAD_DIM, 0.0).astype(BF16)
    o = _head_rmsnorm(o_sum.reshape(seq, w), ones_bd) * gain_ref[...]
    y_ref[...] = o * _silu(hg_ref[:, 4 * w:5 * w])
    if emit_state:
        _store_state(st_ref, 0, finals[0])
        _store_state(st_ref, 1, finals[1])


def _state_shape(batch):
    return (batch, 2, N_HEADS, HEAD_DIM, HEAD_DIM)


def _hgrn(hg_all, tok0, batch, seq, lb_l, gain_l, s0, emit_state):
    has_s0 = s0 is not None
    return _mixer_call(functools.partial(_hgrn_kernel, has_s0, emit_state), "hgrn2", hg_all, tok0, batch, seq,
                       [lb_l, gain_l], [s0] if has_s0 else [],
                       [_state_shape(batch)] if emit_state else [], [])


def _ret_kernel(use_rope, has_s0, emit_state, ret_ref, cos_ref, sin_ref, dl_ref, dlp_ref, *refs):
    s0_ref = refs[0] if has_s0 else None
    y_ref = refs[1 if has_s0 else 0]
    st_ref = refs[-1] if emit_state else None
    seq = ret_ref.shape[0]
    w = RET_WIDTH
    c = min(RET_CHUNK, seq)
    n = seq // c
    chunks = lambda a: a.reshape(n, c, w)
    q = ret_ref[:, 0:w]
    k = ret_ref[:, w:2 * w] * (HEAD_DIM ** -0.5)
    if use_rope:
        lane = _iota((seq, w), 1)
        even = (lane % 2) == 0
        cos = cos_ref[...]
        sin = sin_ref[...]

        def rope(x):
            nxt = pltpu.roll(x, w - 1, axis=1)
            prv = pltpu.roll(x, 1, axis=1)
            return x * cos + jnp.where(even, nxt, prv) * sin

        q = rope(q)
        k = rope(k)
    q3, k3, v3 = chunks(q), chunks(k), chunks(ret_ref[:, 2 * w:3 * w])

    lg = _log_sigmoid(dl_ref[...])
    lgp = _log_sigmoid(dlp_ref[...])
    pair_shape = (c, N_HEADS * c)
    t_idx = _iota(pair_shape, 0)
    s_idx = _iota(pair_shape, 1) % c
    dist = (t_idx - s_idx).astype(F32)
    decay = jnp.exp(jnp.where(dist >= 0, dist * lgp[0:1], -dist * lgp[1:2]))
    decay = decay + jnp.where(dist == 0, 1.0, 0.0)
    stack_mask = _head_mask(c)[None]
    bd_mask = _block_diag_mask()
    pos = _iota((c, 1), 0).astype(F32)
    fc = float(c)

    def stack_heads(a3):
        return jnp.where(stack_mask, jnp.concatenate([a3] * N_HEADS, axis=1), 0.0).astype(BF16)

    pair = jnp.einsum("ntl,nrl->ntr", q3.astype(BF16), stack_heads(k3), preferred_element_type=F32)
    pair = (pair * decay[None]).astype(BF16)
    o_sum = jnp.einsum("ntr,nrv->ntv", pair, stack_heads(v3), preferred_element_type=F32)
    v_t = jnp.swapaxes(v3, 1, 2).astype(BF16)
    finals = []
    for d in range(2):
        forward = d == 0
        lgd = lg[d:d + 1]
        q_in = (q3 * jnp.exp(((pos + 1.0) if forward else (fc - pos)) * lgd)).astype(BF16)
        k_out = (k3 * jnp.exp(((fc - 1.0 - pos) if forward else pos) * lgd)).astype(BF16)
        upd = jnp.einsum("nvs,nsk->nvk", v_t, k_out, preferred_element_type=F32)
        chunk_decay = jnp.exp(fc * lgd)
        st = _load_state(s0_ref, d) if has_s0 else jnp.zeros((w, w), F32)
        o_inter = [None] * n
        for ci in (range(n) if forward else range(n - 1, -1, -1)):
            o_inter[ci] = _mm(q_in[ci], st.astype(BF16), _NT)
            st = st * chunk_decay + jnp.where(bd_mask, upd[ci], 0.0)
        o_sum = o_sum + jnp.stack(o_inter, axis=0)
        finals.append(st)

    ones_bd = jnp.where(bd_mask, 1.0 / HEAD_DIM, 0.0).astype(BF16)
    o = _head_rmsnorm(o_sum.reshape(seq, w), ones_bd)
    y_ref[...] = _silu(ret_ref[:, 3 * w:4 * w]) * o
    if emit_state:
        _store_state(st_ref, 0, finals[0])
        _store_state(st_ref, 1, finals[1])


def _rope_tables(seq):
    rows = seq // GRID_W
    row = np.repeat(np.arange(rows), GRID_W).astype(np.float64)
    col = (np.arange(seq) % GRID_W).astype(np.float64)
    n_freq = HEAD_DIM // 4
    inv_freq = ROPE_BASE ** (-np.arange(n_freq, dtype=np.float64) / n_freq)
    ang = np.concatenate([row[:, None] * inv_freq, col[:, None] * inv_freq], axis=-1)
    ang = np.repeat(ang, 2, axis=1)
    cos = np.tile(np.cos(ang), (1, N_HEADS))
    sin = np.tile(np.sin(ang) * np.where(np.arange(HEAD_DIM) % 2 == 0, -1.0, 1.0)[None, :], (1, N_HEADS))
    return jnp.asarray(cos, F32), jnp.asarray(sin, F32)


def _retention(ret_all, tok0, batch, seq, use_rope, decay_l, s0, emit_state):
    has_s0 = s0 is not None
    w = RET_WIDTH
    c = min(RET_CHUNK, seq)
    cos, sin = _rope_tables(seq)
    dl = jnp.repeat(decay_l, HEAD_DIM, axis=-1)
    dlp = jnp.repeat(decay_l, c, axis=-1)
    return _mixer_call(functools.partial(_ret_kernel, use_rope, has_s0, emit_state), "retention", ret_all, tok0,
                       batch, seq, [cos, sin, dl, dlp], [s0] if has_s0 else [],
                       [_state_shape(batch)] if emit_state else [], [])


def _route(h2, wrt_ref, rb_ref):
    logits = _mm3(wrt_ref[...], h2, _NT)
    tm = logits.shape[1]
    m = jnp.max(logits, axis=0, keepdims=True)
    e = jnp.exp(logits - m)
    probs = e / jnp.sum(e, axis=0, keepdims=True)
    sel = probs + rb_ref[...]
    lane = _iota((N_EXPERTS, tm), 0).astype(F32)
    group = (_iota((N_EXPERTS, tm), 0) // E_PER_GROUP).astype(F32)
    neg = -jnp.inf

    def first_argmax(vals):
        mx = jnp.max(vals, axis=0, keepdims=True)
        idx = jnp.min(jnp.where(vals == mx, lane, float(N_EXPERTS)), axis=0, keepdims=True)
        return mx, idx

    best_score = None
    best = None
    for g in range(N_GROUPS):
        vals = jnp.where(group == float(g), sel, neg)
        m1, i1 = first_argmax(vals)
        m2, _ = first_argmax(jnp.where(lane == i1, neg, vals))
        score = m1 + m2
        if g == 0:
            best_score, best = score, jnp.zeros_like(score)
        else:
            take = score > best_score
            best = jnp.where(take, float(g), best)
            best_score = jnp.where(take, score, best_score)
    vals = jnp.where(group == best, sel, neg)
    _, i1 = first_argmax(vals)
    _, i2 = first_argmax(jnp.where(lane == i1, neg, vals))
    chosen = (lane == i1) | (lane == i2)
    picked = jnp.where(chosen, probs, 0.0)
    return picked / jnp.sum(picked, axis=0, keepdims=True), best


INFO_ROWS = 8
INFO_GROUP = 4
INFO_RANK = 5
H2_WIDTH = D_MODEL + LANES


def _route_kernel(n0_tiles, xp_ref, xs_ref, *refs):
    y_refs = refs[:8]
    mod_ref, wo_ref, n2_ref, wrt_ref, rb_ref, x1_ref, h2_ref, info_ref, cend_ref, carry_s = refs[8:]
    i = pl.program_id(0)
    first = i < n0_tiles
    d = D_MODEL
    tb = TB_MOE
    mod = mod_ref[0, 0]

    @pl.when(i == 0)
    def _():
        carry_s[...] = jnp.zeros_like(carry_s)

    tm = TM_ROUTE
    ys = [jnp.where(first, y_refs[2 * j][...], y_refs[2 * j + 1][...]).astype(BF16) for j in range(4)]
    mixed = _mm(jnp.concatenate(ys, axis=1), wo_ref[...])
    x = jnp.where(first, xp_ref[...], xs_ref[...])
    x1 = x + mod[:, 2 * d:3 * d] * mixed
    x1_ref[...] = x1
    h2 = x1 * lax.rsqrt(jnp.mean(x1 * x1, axis=-1, keepdims=True) + EPS) * n2_ref[...]
    h2 = h2 * (1.0 + mod[:, 4 * d:5 * d]) + mod[:, 3 * d:4 * d]
    h2_ref[:, 0:d] = h2.astype(BF16)

    gates, best = _route(h2, wrt_ref, rb_ref)
    expert = _iota((N_EXPERTS, tm), 0).astype(F32)
    g4 = [jnp.sum(jnp.where(expert == E_PER_GROUP * best + j, gates, 0.0), axis=0, keepdims=True)
          for j in range(E_PER_GROUP)]
    onehot = jnp.where(expert == best, 1.0, 0.0)
    upper = (_iota((tm, tm), 0) <= _iota((tm, tm), 1)).astype(BF16)
    carry = carry_s[...]
    incl = _mm(onehot.astype(BF16), upper) + carry[:, 0:1]
    rank = jnp.sum(onehot * incl, axis=0, keepdims=True) - 1.0
    info_ref[...] = jnp.concatenate(g4 + [best, rank, jnp.zeros((INFO_ROWS - 6, tm), F32)], axis=0)
    for h in range(tm // tb):
        end = (h + 1) * tb
        cend_ref[h] = jnp.broadcast_to(incl[:, end - 1:end], (N_EXPERTS, LANES))
    carry_s[...] = jnp.broadcast_to(incl[:, tm - 1:tm], (N_EXPERTS, LANES))

    pieces = _split3(jnp.concatenate(g4 + [jnp.zeros((N_EXPERTS - E_PER_GROUP, tm), F32)], axis=0))
    pieces = jnp.concatenate(list(pieces) + [jnp.zeros((LANES - 3 * N_EXPERTS, tm), BF16)], axis=0)
    eye = (_iota((tm, tm), 0) == _iota((tm, tm), 1)).astype(BF16)
    h2_ref[:, d:d + LANES] = _mm(eye, pieces, _NT).astype(BF16)


def _route_call(layer, xp, xs, ys, mod_l, w_out, norm2_l, w_router, router_bias, dec_seq):
    n_prompt_tok = xp.shape[0]
    t_all = n_prompt_tok + xs.shape[0]
    tm = TM_ROUTE
    sub = tm // TB_MOE
    n0_tiles = n_prompt_tok // tm
    row = _mod_row_map(tm, n_prompt_tok, dec_seq)
    tok = lambda width: pl.BlockSpec((tm, width), lambda i: (i, 0))
    y_specs, y_args = [], []
    for yp, ysm in ys:
        y_specs += _pair_specs(tm, MIX_WIDTH, n0_tiles)
        y_args += [yp, ysm]
    return pl.pallas_call(
        functools.partial(_route_kernel, n0_tiles),
        out_shape=[jax.ShapeDtypeStruct((t_all, D_MODEL), F32), jax.ShapeDtypeStruct((t_all, H2_WIDTH), BF16),
                   jax.ShapeDtypeStruct((INFO_ROWS, t_all), F32),
                   jax.ShapeDtypeStruct((t_all // TB_MOE, N_EXPERTS, LANES), F32)],
        grid=(t_all // tm,),
        in_specs=_pair_specs(tm, D_MODEL, n0_tiles) + y_specs + [
            pl.BlockSpec((1, 1, 1, N_MOD * D_MODEL), lambda i: (0, row(i), 0, 0)),
            _const_layer_spec(w_out.shape, layer), _const_spec((1, D_MODEL)),
            _const_spec((N_EXPERTS, D_MODEL)), _const_spec((N_EXPERTS, 1)),
        ],
        out_specs=[tok(D_MODEL), tok(H2_WIDTH), pl.BlockSpec((INFO_ROWS, tm), lambda i: (0, i)),
                   pl.BlockSpec((sub, N_EXPERTS, LANES), lambda i: (i, 0, 0))],
        scratch_shapes=[pltpu.VMEM((N_EXPERTS, LANES), F32)],
        compiler_params=_params(("arbitrary",)),
        name="route",
    )(xp, xs, *y_args, mod_l, w_out, norm2_l, w_router.T, router_bias.reshape(N_EXPERTS, 1))


def _moe_plan(cend, info):
    ts = TS_MOE
    tb = TR_MOE
    n_blocks = cend.shape[0]
    t_all = n_blocks * TB_MOE
    n_tiles = t_all // ts + N_GROUPS
    sub_per_tile = ts // tb
    n_sub = n_tiles * sub_per_tile
    cend = cend[:, :N_GROUPS, 0].astype(jnp.int32)
    cum = jnp.concatenate([jnp.zeros((1, N_GROUPS), jnp.int32), cend[:-1]], axis=0)
    cnt = cend - cum
    grp = info[INFO_GROUP].astype(jnp.int32)
    rank = info[INFO_RANK].astype(jnp.int32)
    tot = cend[-1]
    padded = ((tot + ts - 1) // ts) * ts
    off = jnp.cumsum(padded) - padded
    pos = rank + sum(jnp.where(grp == g, off[g], 0) for g in range(N_GROUPS))

    sub_start = jnp.arange(n_sub, dtype=jnp.int32) * tb
    in_g = (sub_start[:, None] >= off[None, :]) & (sub_start[:, None] < (off + padded)[None, :])
    g_of = jnp.argmax(in_g, axis=1).astype(jnp.int32)
    r0 = sub_start - off[g_of]
    r1 = jnp.minimum(r0 + tb, tot[g_of])
    live = jnp.any(in_g, axis=1) & (r1 > r0)
    cend_g = cend[:, g_of]
    blo = jnp.where(live, jnp.sum(cend_g <= r0[None, :], axis=0), 0).astype(jnp.int32)
    bhi = jnp.where(live, jnp.sum(cend_g <= (r1 - 1)[None, :], axis=0), -1).astype(jnp.int32)

    used = jnp.any(in_g, axis=1)[::sub_per_tile]
    tile_group = g_of[::sub_per_tile]
    eidx = E_PER_GROUP * tile_group[:, None] + jnp.arange(E_PER_GROUP, dtype=jnp.int32)[None, :]
    n_used = jnp.sum(used.astype(jnp.int32))
    last = eidx[jnp.maximum(n_used - 1, 0), E_PER_GROUP - 1]
    eidx = jnp.where(used[:, None], eidx, last).reshape(-1).astype(jnp.int32)

    start = off[None, :] + cum
    end = off[None, :] + cend
    has = cnt > 0
    slo = jnp.where(has, start // tb, 0).reshape(-1).astype(jnp.int32)
    shi = jnp.where(has, (end - 1) // tb, -1).reshape(-1).astype(jnp.int32)
    tile_start = sub_start[::sub_per_tile]
    rows = jnp.where(used, jnp.clip(tot[tile_group] - (tile_start - off[tile_group]), 0, ts), 0)
    return dict(pos=pos, eidx=eidx, rows=rows.astype(jnp.int32), blo=blo, bhi=bhi, slo=slo, shi=shi,
                n_tiles=n_tiles)


def _expert_kernel(eidx_ref, rows_ref, blo_ref, bhi_ref, h2_ref, pos_ref, wg_ref, wu_ref, wd_ref,
                   ys_ref, xs_s, acc_s, ax_s):
    i = pl.program_id(0)
    j = pl.program_id(1)
    n_rows = rows_ref[i]
    used = n_rows > 0
    tb, tr = TB_MOE, TR_MOE
    sub_per_tile = TS_MOE // tr

    @pl.when(used & (j == 0))
    def _():
        r_idx = _iota((tr, tb), 0)
        for part in range(sub_per_tile):
            s = i * sub_per_tile + part
            base = s * tr
            ax_s[...] = jnp.zeros_like(ax_s)

            def gather(b, carry):
                sel = jnp.where(pos_ref[b] - base == r_idx, 1.0, 0.0).astype(BF16)
                tok = pl.ds(pl.multiple_of(b * tb, tb), tb)
                ax_s[...] += _mm(sel, h2_ref[tok, :])
                return carry

            lax.fori_loop(blo_ref[s], bhi_ref[s] + 1, gather, 0)
            xs_s[part * tr:(part + 1) * tr, :] = ax_s[...].astype(BF16)
        acc_s[...] = jnp.zeros_like(acc_s)

    half_rows = TS_MOE // 2
    for part in range(2):
        rows = slice(part * half_rows, (part + 1) * half_rows)

        @pl.when(n_rows > part * half_rows)
        def _():
            x = xs_s[rows, 0:D_MODEL]
            pieces = xs_s[rows, D_MODEL:H2_WIDTH].astype(F32)
            ge = jnp.sum(jnp.where(_iota(pieces.shape, 1) % N_EXPERTS == j, pieces, 0.0), axis=-1, keepdims=True)
            hh = _silu(_mm(x, wg_ref[0, 0].astype(BF16))) * _mm(x, wu_ref[0, 0].astype(BF16)) * ge
            acc_s[rows, :] += _mm(hh.astype(BF16), wd_ref[0, 0].astype(BF16))

    @pl.when(j == E_PER_GROUP - 1)
    def _():
        ys_ref[...] = jnp.where(used, acc_s[...], 0.0).astype(BF16)


def _expert_call(layer, plan, h2, w_gate, w_up, w_down):
    t_all = h2.shape[0]
    tb, ts = TB_MOE, TS_MOE
    n_tiles = plan["n_tiles"]
    pos_rows = plan["pos"].reshape(t_all // tb, 1, tb)
    weight = lambda shape: pl.BlockSpec((1, 1) + shape, lambda i, j, eidx, *_: (layer, eidx[i * E_PER_GROUP + j], 0, 0))
    grid_spec = pltpu.PrefetchScalarGridSpec(
        num_scalar_prefetch=4,
        grid=(n_tiles, E_PER_GROUP),
        in_specs=[
            _const_spec((t_all, H2_WIDTH)), _const_spec(pos_rows.shape),
            weight((D_MODEL, D_EXPERT)), weight((D_MODEL, D_EXPERT)), weight((D_EXPERT, D_MODEL)),
        ],
        out_specs=pl.BlockSpec((ts, D_MODEL), lambda i, j, *_: (i, 0)),
        scratch_shapes=[pltpu.VMEM((ts, H2_WIDTH), BF16), pltpu.VMEM((ts, D_MODEL), F32),
                        pltpu.VMEM((TR_MOE, H2_WIDTH), F32)],
    )
    return pl.pallas_call(
        _expert_kernel,
        out_shape=jax.ShapeDtypeStruct((n_tiles * ts, D_MODEL), BF16),
        grid_spec=grid_spec,
        compiler_params=_params(("arbitrary", "arbitrary")),
        name=f"experts_{layer}",
    )(plan["eidx"], plan["rows"], plan["blo"], plan["bhi"], h2, pos_rows, w_gate, w_up, w_down)


def _combine_kernel(final, n0_tiles, slo_ref, shi_ref, ys_ref, pos_ref, x1_ref, mod_ref, nf_ref,
                    op_ref, os_ref, acc_s):
    b = pl.program_id(0)
    d = D_MODEL
    tb, tr = TB_MOE, TR_MOE
    acc_s[...] = jnp.zeros_like(acc_s)
    c_idx = _iota((tb, tr), 1)
    pos = pos_ref[...]
    for g in range(N_GROUPS):
        def scatter(s, carry):
            sel = jnp.where(pos - s * tr == c_idx, 1.0, 0.0).astype(BF16)
            acc_s[...] += _mm(sel, ys_ref[pl.ds(pl.multiple_of(s * tr, tr), tr), :])
            return carry

        lax.fori_loop(slo_ref[b * N_GROUPS + g], shi_ref[b * N_GROUPS + g] + 1, scatter, 0)

    def result():
        x2 = x1_ref[...] + mod_ref[0, 0][:, 5 * d:6 * d] * acc_s[...]
        if final:
            x2 = x2 * lax.rsqrt(jnp.mean(x2 * x2, axis=-1, keepdims=True) + EPS) * nf_ref[...]
        return x2

    @pl.when(b < n0_tiles)
    def _():
        op_ref[...] = result()

    @pl.when(b >= n0_tiles)
    def _():
        os_ref[...] = result()


def _combine_call(final, plan, ys_sorted, x1, mod_l, norm_final, n_prompt_tok, dec_seq):
    t_all = x1.shape[0]
    tb = TB_MOE
    n0_tiles = n_prompt_tok // tb
    row = _mod_row_map(tb, n_prompt_tok, dec_seq)
    grid_spec = pltpu.PrefetchScalarGridSpec(
        num_scalar_prefetch=2,
        grid=(t_all // tb,),
        in_specs=[
            _const_spec(ys_sorted.shape),
            pl.BlockSpec((tb, 1), lambda b, *_: (b, 0)),
            pl.BlockSpec((tb, D_MODEL), lambda b, *_: (b, 0)),
            pl.BlockSpec((1, 1, 1, N_MOD * D_MODEL), lambda b, *_: (0, row(b), 0, 0)),
            _const_spec((1, D_MODEL)),
        ],
        out_specs=_pair_specs(tb, D_MODEL, n0_tiles),
        scratch_shapes=[pltpu.VMEM((tb, D_MODEL), F32)],
    )
    return pl.pallas_call(
        functools.partial(_combine_kernel, final, n0_tiles),
        out_shape=[jax.ShapeDtypeStruct((n_prompt_tok, D_MODEL), F32),
                   jax.ShapeDtypeStruct((t_all - n_prompt_tok, D_MODEL), F32)],
        grid_spec=grid_spec,
        compiler_params=_params(("arbitrary",)),
        name="combine",
    )(plan["slo"], plan["shi"], ys_sorted, plan["pos"].reshape(t_all, 1), x1, mod_l, norm_final)


def _rg_gate_weights(wa, ba, wx, bx):
    eye = jnp.eye(RG_HEADS, dtype=wa.dtype)

    def dense(wd):
        return jnp.einsum("hij,hg->higj", wd, eye).reshape(RG_WIDTH, RG_WIDTH)

    w = jnp.concatenate([dense(wa[0]), dense(wa[1]), dense(wx[0]), dense(wx[1])], axis=1)
    b = jnp.concatenate([ba[0], ba[1], bx[0], bx[1]], axis=0).reshape(1, 4 * RG_WIDTH)
    return w, b


def kernel(x_prompt, x_sample, state_rglru, state_hgrn, state_ret, c, c_ctx, norm1, norm2, norm_final, w_ada, b_ada, w_in, w_out, hy_conv_w, hy_conv_b, hy_w1, hy_b1, hy_w2, hy_b2, hy_w3, hy_d, rg_conv_w, rg_conv_b, rg_wa, rg_ba, rg_wx, rg_bx, rg_lambda, hg_lb, hg_norm, ret_decay, w_router, router_bias, w_gate, w_up, w_down):
    batch, seq, d = x_prompt.shape
    dec_batch, dec_seq, _ = x_sample.shape
    assert d == D_MODEL and dec_batch + 1 <= COND_ROWS
    n_prompt_tok = batch * seq

    lb_cum = jnp.cumsum(jax.nn.softmax(hg_lb.astype(F32), axis=0), axis=0)
    lb_all = lb_cum - lb_cum[0:1]

    cond = jnp.zeros((COND_ROWS, d), F32).at[0].set(c_ctx).at[1:1 + dec_batch].set(c)
    mod = _modulation(cond, w_ada, b_ada).reshape(DEPTH, COND_ROWS, 1, N_MOD * d)

    w_in_b = w_in.astype(BF16)
    w_out_b = w_out.astype(BF16)

    passes = (
        dict(tok0=0, batch=batch, seq=seq, rope=False),
        dict(tok0=n_prompt_tok, batch=dec_batch, seq=dec_seq, rope=True),
    )
    filters = {p["seq"]: _hyena_filters(p["seq"], hy_w1, hy_b1, hy_w2, hy_b2, hy_w3) for p in passes}
    tables = {p["seq"]: _hyena_tables(p["seq"]) for p in passes}

    xp = x_prompt.reshape(-1, d)
    xs = x_sample.reshape(-1, d)
    new_rg, new_hg, new_ret = [], [], []
    for l in range(DEPTH):
        hy_all, rg_all, hg_all, ret_all = _in_projection(
            l, xp, xs, mod[l:l + 1], norm1[l].reshape(1, d), w_in_b, dec_seq)
        wg, bg = _rg_gate_weights(rg_wa[l], rg_ba[l], rg_wx[l], rg_bx[l])
        sp = jax.nn.softplus(-rg_lambda[l])
        ys = [[], [], [], []]
        for pi, p in enumerate(passes):
            first = pi == 0
            geom = (p["tok0"], p["batch"], p["seq"])
            y_hy = _hyena(hy_all, *geom, hy_conv_w[l], hy_conv_b[l].reshape(1, -1), hy_d[l],
                          (filters[p["seq"]], l), tables[p["seq"]])
            y_rg, *st_rg = _rglru(rg_all, *geom, rg_conv_w[l], rg_conv_b[l].reshape(1, -1), wg, bg, sp,
                                  None if first else (state_rglru, l), first)
            y_hg, *st_hg = _hgrn(hg_all, *geom, lb_all[l], hg_norm[l].reshape(1, -1),
                                 None if first else (state_hgrn, l), first)
            y_ret, *st_ret = _retention(ret_all, *geom, p["rope"], ret_decay[l],
                                        None if first else (state_ret, l), first)
            for lst, y in zip(ys, (y_hy, y_rg, y_hg, y_ret)):
                lst.append(y)
            if first:
                new_rg.append(st_rg[0])
                new_hg.append(st_hg[0])
                new_ret.append(st_ret[0])
        x1, h2, info, cend = _route_call(l, xp, xs, ys, mod[l:l + 1], w_out_b, norm2[l].reshape(1, d),
                                         w_router, router_bias, dec_seq)
        plan = _moe_plan(cend, info)
        ys_sorted = _expert_call(l, plan, h2, w_gate, w_up, w_down)
        xp, xs = _combine_call(l == DEPTH - 1, plan, ys_sorted, x1, mod[l:l + 1], norm_final.reshape(1, d),
                               n_prompt_tok, dec_seq)

    return (xp.reshape(batch, seq, d), xs.reshape(dec_batch, dec_seq, d), jnp.stack(new_rg, axis=1),
            jnp.stack(new_hg, axis=1), jnp.stack(new_ret, axis=1))
```

```python
import functools
import math

import numpy as np
import jax
import jax.numpy as jnp
from jax import lax
from jax.experimental import pallas as pl
from jax.experimental.pallas import tpu as pltpu

F32 = jnp.float32
BF16 = jnp.bfloat16

D_MODEL = 1024
DEPTH = 2
GRID_W = 64
HY_WIDTH = 256
RG_WIDTH = 256
HG_WIDTH = 256
RET_WIDTH = 256
MIX_WIDTH = 256
HY_ORDER = 2
HY_EMB = 33
HY_BANDS = 16
HY_FFN = 64
HY_DECAY_TARGET = 1e-2
HY_DECAY_SHORT = 0.3
HY_DECAY_LONG = 1.5
HY_GROUP_COLS = 512
RG_HEADS = 8
RG_HEAD_DIM = 32
RG_C = 8.0
RG_CHUNK = 16
N_HEADS = 4
HEAD_DIM = 64
HG_CHUNK = 64
RET_CHUNK = 256
ROPE_BASE = 10000.0
N_EXPERTS = 16
N_GROUPS = 4
E_PER_GROUP = 4
D_EXPERT = 512
N_MOD = 6
EPS = 1e-6
PROJ_HY = 3 * HY_WIDTH
PROJ_RG = 2 * RG_WIDTH
PROJ_HG = 5 * HG_WIDTH
PROJ_RET = 4 * RET_WIDTH
PROJ_WIDTH = PROJ_HY + PROJ_RG + PROJ_HG + PROJ_RET
COND_ROWS = 16
LANES = 128
VMEM_LIMIT = 56 * 1024 * 1024
TM_PROJ = 512
TB_MOE = 512
TR_MOE = 256
TM_ROUTE = 2 * TB_MOE
TS_MOE = 1024
TN_MOD = 1536

_NN = (((1,), (0,)), ((), ()))
_NT = (((1,), (1,)), ((), ()))


def _mm(a, b, dn=_NN):
    return lax.dot_general(a, b, dn, preferred_element_type=F32)


def _split2(x):
    hi = x.astype(BF16)
    lo = (x - hi.astype(F32)).astype(BF16)
    return hi, lo


def _split3(x):
    hi = x.astype(BF16)
    r = x - hi.astype(F32)
    mid = r.astype(BF16)
    lo = (r - mid.astype(F32)).astype(BF16)
    return hi, mid, lo


def _mm3(a, b, dn=_NN):
    ah, al = _split2(a)
    bh, bl = _split2(b)
    return _mm(ah, bh, dn) + (_mm(ah, bl, dn) + _mm(al, bh, dn))


def _mm_exact_rhs(a, b_bf16):
    a1, a2, a3 = _split3(a)
    return _mm(a1, b_bf16) + (_mm(a2, b_bf16) + _mm(a3, b_bf16))


def _sigmoid(x):
    return 1.0 / (1.0 + jnp.exp(-x))


def _silu(x):
    return x * _sigmoid(x)


def _log_sigmoid(x):
    return jnp.minimum(x, 0.0) - jnp.log(1.0 + jnp.exp(-jnp.abs(x)))


def _gelu_tanh(x):
    return 0.5 * x * (1.0 + jnp.tanh(math.sqrt(2.0 / math.pi) * (x + 0.044715 * (x * x * x))))


def _iota(shape, dim):
    return lax.broadcasted_iota(jnp.int32, shape, dim)


def _shift_rows(u, k, row):
    n = u.shape[0]
    if k == 0:
        return u
    r = pltpu.roll(u, (-k) % n, axis=0)
    if k < 0:
        return jnp.where(row >= -k, r, 0.0)
    return jnp.where(row < n - k, r, 0.0)


def _head_mask(n_rows_per_head):
    shape = (N_HEADS * n_rows_per_head, MIX_WIDTH)
    return (_iota(shape, 0) // n_rows_per_head) == (_iota(shape, 1) // HEAD_DIM)


def _block_diag_mask():
    shape = (MIX_WIDTH, MIX_WIDTH)
    return (_iota(shape, 0) // HEAD_DIM) == (_iota(shape, 1) // HEAD_DIM)


def _head_rmsnorm(o, ones_bd):
    ms = _mm_exact_rhs(o * o, ones_bd)
    return o * lax.rsqrt(ms + EPS)


def _load_state(s0_ref, d):
    zero = jnp.zeros((HEAD_DIM, HEAD_DIM), F32)
    rows = []
    for h in range(N_HEADS):
        blk = s0_ref[0, d, h].T
        rows.append(jnp.concatenate([blk if g == h else zero for g in range(N_HEADS)], axis=1))
    return jnp.concatenate(rows, axis=0)


def _store_state(st_ref, d, st):
    for h in range(N_HEADS):
        lo, hi = h * HEAD_DIM, (h + 1) * HEAD_DIM
        st_ref[0, d, h] = st[lo:hi, lo:hi].T


def _params(sem, vmem=VMEM_LIMIT):
    return pltpu.CompilerParams(dimension_semantics=sem, vmem_limit_bytes=vmem)


def _const_spec(shape):
    nd = len(shape)
    return pl.BlockSpec(shape, lambda *_: (0,) * nd, pipeline_mode=pl.Buffered(1))


def _const_layer_spec(shape, layer):
    nd = len(shape)
    return pl.BlockSpec((None,) + tuple(shape[1:]), lambda *_: (layer,) + (0,) * (nd - 1),
                        pipeline_mode=pl.Buffered(1))


def _mod_kernel(cond_ref, w_ref, b_ref, o_ref):
    o_ref[0] = _mm3(_silu(cond_ref[...]), w_ref[0]) + b_ref[0]


def _modulation(cond, w_ada, b_ada):
    tn = TN_MOD
    n_mod = N_MOD * D_MODEL
    return pl.pallas_call(
        _mod_kernel,
        out_shape=jax.ShapeDtypeStruct((DEPTH, COND_ROWS, n_mod), F32),
        grid=(DEPTH, n_mod // tn),
        in_specs=[
            pl.BlockSpec((COND_ROWS, D_MODEL), lambda l, j: (0, 0)),
            pl.BlockSpec((1, D_MODEL, tn), lambda l, j: (l, 0, j)),
            pl.BlockSpec((1, 1, tn), lambda l, j: (l, 0, j)),
        ],
        out_specs=pl.BlockSpec((1, COND_ROWS, tn), lambda l, j: (l, 0, j)),
        compiler_params=_params(("parallel", "parallel")),
        name="modulation",
    )(cond, w_ada, b_ada.reshape(DEPTH, 1, n_mod))


def _pair_specs(tm, width, n0_tiles):
    return [pl.BlockSpec((tm, width), lambda i, *_: (jnp.minimum(i, n0_tiles - 1), 0)),
            pl.BlockSpec((tm, width), lambda i, *_: (jnp.maximum(i - n0_tiles, 0), 0))]


def _proj_kernel(n0_tiles, xp_ref, xs_ref, mod_ref, n1_ref, w_ref, hy_ref, rg_ref, hg_ref, ret_ref):
    x = jnp.where(pl.program_id(0) < n0_tiles, xp_ref[...], xs_ref[...])
    mod = mod_ref[0, 0]
    sh1 = mod[:, 0:D_MODEL]
    sc1 = mod[:, D_MODEL:2 * D_MODEL]
    h = x * lax.rsqrt(jnp.mean(x * x, axis=-1, keepdims=True) + EPS) * n1_ref[...]
    h = (h * (1.0 + sc1) + sh1).astype(BF16)
    c0 = 0
    for ref, width in ((hy_ref, PROJ_HY), (rg_ref, PROJ_RG), (hg_ref, PROJ_HG), (ret_ref, PROJ_RET)):
        ref[...] = _mm(h, w_ref[:, c0:c0 + width])
        c0 += width


def _mod_row_map(tm, n_prompt_tok, dec_seq):
    n_prompt_tiles = n_prompt_tok // tm

    def row(i):
        return jnp.where(i < n_prompt_tiles, 0, 1 + (i * tm - n_prompt_tok) // dec_seq)

    return row


def _in_projection(layer, xp, xs, mod_l, norm1_l, w_in, dec_seq):
    n_prompt_tok = xp.shape[0]
    t_all = n_prompt_tok + xs.shape[0]
    tm = TM_PROJ
    n0_tiles = n_prompt_tok // tm
    row = _mod_row_map(tm, n_prompt_tok, dec_seq)
    widths = (PROJ_HY, PROJ_RG, PROJ_HG, PROJ_RET)
    return pl.pallas_call(
        functools.partial(_proj_kernel, n0_tiles),
        out_shape=[jax.ShapeDtypeStruct((t_all, w), F32) for w in widths],
        grid=(t_all // tm,),
        in_specs=_pair_specs(tm, D_MODEL, n0_tiles) + [
            pl.BlockSpec((1, 1, 1, N_MOD * D_MODEL), lambda i: (0, row(i), 0, 0)),
            _const_spec((1, D_MODEL)),
            _const_layer_spec(w_in.shape, layer),
        ],
        out_specs=[pl.BlockSpec((tm, w), lambda i: (i, 0)) for w in widths],
        compiler_params=_params(("parallel",)),
        name="in_projection",
    )(xp, xs, mod_l, norm1_l, w_in)


def _dft_tables(seq):
    k = np.arange(seq, dtype=np.int64)
    m = (k[:, None] * k[None, :]) % (2 * seq)
    ang = np.pi * m.astype(np.float64) / seq
    return np.cos(ang), np.sin(ang)


def _hyena_tables(seq):
    cos, sin = _dft_tables(seq)
    sign = np.where(np.arange(seq) % 2 == 0, 1.0, -1.0)
    fwd = np.concatenate([cos, sign[None, :], sin[1:]], axis=0)
    wk = np.full((seq,), 2.0)
    wk[0] = 1.0
    inv_cos = (cos * wk[None, :]) / (2.0 * seq)
    inv_nyq = sign[:, None] / (2.0 * seq)
    inv_sin = 2.0 * sin[:, 1:] / (2.0 * seq)
    inv = np.concatenate([inv_cos, inv_nyq, inv_sin], axis=1)

    return jnp.asarray(fwd, F32).astype(BF16), jnp.asarray(inv, F32).astype(BF16)


def _filter_embedding(seq):
    t = np.arange(seq, dtype=np.float64)
    t_norm = t / max(seq - 1, 1)
    bands = np.linspace(1e-4, HY_BANDS - 1, HY_BANDS)
    ang = (2.0 * np.pi / seq) * t[:, None] * bands[None, :]
    z = np.concatenate([t_norm[:, None], np.cos(ang), np.sin(ang)], axis=-1)
    z = np.pad(z, ((0, 0), (0, LANES - HY_EMB)))
    deltas = np.abs(np.linspace(math.log(HY_DECAY_TARGET) / HY_DECAY_LONG,
                                math.log(HY_DECAY_TARGET) / HY_DECAY_SHORT, HY_WIDTH))
    window = np.exp(-t_norm[:, None] * deltas[None, :])
    return jnp.asarray(z, F32), jnp.asarray(window, F32)


def _filter_kernel(z_ref, win_ref, cos_ref, sin_ref, w1_ref, b1_ref, w2_ref, b2_ref, w3_ref, o_ref):
    seq = z_ref.shape[0]
    h = jnp.sin(_mm3(z_ref[...], w1_ref[0]) + b1_ref[0])
    h = jnp.sin(_mm3(h, w2_ref[0]) + b2_ref[0])
    h = _mm3(h, w3_ref[0])
    win = win_ref[...]
    row = _iota((seq, 1), 0)
    sums, diffs = [], []
    for o in range(HY_ORDER):
        c0 = o * 2 * HY_WIDTH
        hf = h[:, c0:c0 + HY_WIDTH] * win
        hb = h[:, c0 + HY_WIDTH:c0 + 2 * HY_WIDTH] * win
        ssq = jnp.sum(hf * hf + hb * hb, axis=0, keepdims=True)
        inv = lax.rsqrt(ssq + EPS)
        hf = hf * inv
        hb = jnp.where(row == 0, 0.0, hb * inv)
        sums.append(hf + hb)
        diffs.append(hf - hb)
    hsum = jnp.concatenate(sums, axis=1)
    hdiff = jnp.concatenate(diffs, axis=1)
    h_re = _mm(cos_ref[...], hsum.astype(BF16))
    h_im = _mm(sin_ref[...], hdiff.astype(BF16))
    sign = jnp.where(row % 2 == 0, 1.0, -1.0)
    h_nyq = jnp.sum(sign * hsum, axis=0, keepdims=True)
    o_ref[0, 0] = h_re
    o_ref[0, 1] = h_im
    o_ref[0, 2] = jnp.where(row == 0, h_nyq, h_re)


def _hyena_filters(seq, w1, b1, w2, b2, w3):
    z, window = _filter_embedding(seq)
    cos, sin = _dft_tables(seq)
    n_out = HY_ORDER * 2 * HY_WIDTH
    w1p = jnp.pad(w1, ((0, 0), (0, LANES - HY_EMB), (0, LANES - HY_FFN)))
    b1p = jnp.pad(b1, ((0, 0), (0, LANES - HY_FFN))).reshape(DEPTH, 1, LANES)
    w2p = jnp.pad(w2, ((0, 0), (0, LANES - HY_FFN), (0, LANES - HY_FFN)))
    b2p = jnp.pad(b2, ((0, 0), (0, LANES - HY_FFN))).reshape(DEPTH, 1, LANES)
    w3p = jnp.pad(w3, ((0, 0), (0, LANES - HY_FFN), (0, 0)))
    per_layer = lambda shape: pl.BlockSpec((1,) + shape, lambda l: (l,) + (0,) * len(shape))
    return pl.pallas_call(
        _filter_kernel,
        out_shape=jax.ShapeDtypeStruct((DEPTH, 3, seq, HY_ORDER * HY_WIDTH), F32),
        grid=(DEPTH,),
        in_specs=[
            _const_spec((seq, LANES)), _const_spec((seq, HY_WIDTH)),
            _const_spec((seq, seq)), _const_spec((seq, seq)),
            per_layer((LANES, LANES)), per_layer((1, LANES)),
            per_layer((LANES, LANES)), per_layer((1, LANES)),
            per_layer((LANES, n_out)),
        ],
        out_specs=pl.BlockSpec((1, 3, seq, HY_ORDER * HY_WIDTH), lambda l: (l, 0, 0, 0)),
        compiler_params=_params(("parallel",)),
        name=f"hyena_filters_{seq}",
    )(z, window, jnp.asarray(cos, F32).astype(BF16), jnp.asarray(sin, F32).astype(BF16), w1p, b1p, w2p, b2p, w3p)


def _hyena_kernel(group, u_ref, cw_ref, cb_ref, d_ref, filt_ref, fwd_ref, inv_ref, y_ref):
    seq = u_ref.shape[0] // group
    cw = cw_ref[...]
    row = _iota((seq, 1), 0)
    ucs = []
    for s in range(group):
        u = u_ref[s * seq:(s + 1) * seq, :]
        ucs.append(cb_ref[...] + cw[0:1] * _shift_rows(u, -1, row) + cw[1:2] * u + cw[2:3] * _shift_rows(u, 1, row))
    side = lambda c0: jnp.concatenate([uc[:, c0:c0 + HY_WIDTH] for uc in ucs], axis=1)
    tile = lambda a: jnp.concatenate([a] * group, axis=1)
    gates = (side(HY_WIDTH), side(2 * HY_WIDTH))
    d = d_ref[...]
    z = side(0)
    for o in range(HY_ORDER):
        c0 = o * HY_WIDTH
        spec = _mm(fwd_ref[...], z.astype(BF16))
        s_re, s_im = spec[:seq], spec[seq:]
        a = tile(filt_ref[0, :, c0:c0 + HY_WIDTH])
        b = tile(filt_ref[1, :, c0:c0 + HY_WIDTH])
        c = tile(filt_ref[2, :, c0:c0 + HY_WIDTH])
        prod = jnp.concatenate([s_re * a - s_im * b, s_re * b + s_im * c], axis=0)
        conv = _mm(inv_ref[...], prod.astype(BF16))
        z = gates[o] * (conv + tile(d[o:o + 1]) * z)
    for s in range(group):
        y_ref[s * seq:(s + 1) * seq, :] = z[:, s * HY_WIDTH:(s + 1) * HY_WIDTH]


def _mixer_call(body, name, proj_all, tok0, batch, seq, consts, batch_ins, state_shapes, scratch, group=1):
    proj_width = proj_all.shape[1]
    rows = group * seq
    blk0 = tok0 // rows

    def batch_spec(shape):
        nd = len(shape)
        return pl.BlockSpec((group,) + tuple(shape[1:]), lambda b: (b,) + (0,) * (nd - 1))

    def layer_spec(shape, layer):
        nd = len(shape)
        return pl.BlockSpec((group, None) + tuple(shape[2:]), lambda b: (b, layer) + (0,) * (nd - 2))

    in_specs = [pl.BlockSpec((rows, proj_width), lambda b: (blk0 + b, 0))]
    in_specs += [_const_layer_spec(a[0].shape, a[1]) if isinstance(a, tuple) else _const_spec(a.shape)
                 for a in consts]
    consts = [a[0] if isinstance(a, tuple) else a for a in consts]
    in_specs += [layer_spec(a.shape, layer) for a, layer in batch_ins]
    batch_ins = [a for a, _ in batch_ins]
    out_shape = [jax.ShapeDtypeStruct((batch * seq, MIX_WIDTH), F32)]
    out_specs = [pl.BlockSpec((rows, MIX_WIDTH), lambda b: (b, 0))]
    for shape in state_shapes:
        out_shape.append(jax.ShapeDtypeStruct(shape, F32))
        out_specs.append(batch_spec(shape))
    return pl.pallas_call(
        body, out_shape=out_shape, grid=(batch // group,), in_specs=in_specs, out_specs=out_specs,
        scratch_shapes=scratch,
        compiler_params=_params(("parallel",)),
        name=f"{name}_{seq}",
    )(proj_all, *consts, *batch_ins)


def _hyena(u_all, tok0, batch, seq, conv_w, conv_b, d_bias, filt_l, tables):
    consts = [conv_w, conv_b, d_bias, filt_l, *tables]
    group = HY_GROUP_COLS // HY_WIDTH
    return _mixer_call(functools.partial(_hyena_kernel, group), "hyena", u_all, tok0, batch, seq, consts,
                       [], [], [], group=group)[0]


def _rglru_kernel(has_s0, emit_state, rg_ref, cw_ref, cb_ref, wg_ref, bg_ref, sp_ref, *refs):
    h0_ref = refs[0] if has_s0 else None
    y_ref = refs[1 if has_s0 else 0]
    st_ref = refs[-1] if emit_state else None
    seq = rg_ref.shape[0]
    w = RG_WIDTH
    xr = rg_ref[:, 0:w]
    gate = rg_ref[:, w:2 * w]
    row = _iota((seq, 1), 0)
    cw = cw_ref[...]
    xc = (cb_ref[...] + cw[0:1] * _shift_rows(xr, -2, row) + cw[1:2] * _shift_rows(xr, -1, row)
          + cw[2:3] * xr + cw[3:4] * _shift_rows(xr, 1, row))
    g = _sigmoid(_mm(xc.astype(BF16), wg_ref[...].astype(BF16)) + bg_ref[...])
    sp = sp_ref[...]
    c = RG_CHUNK
    n = seq // c
    pos = row % c
    hs = []
    for d in range(2):
        forward = d == 0
        r = g[:, d * w:(d + 1) * w]
        i = g[:, (2 + d) * w:(3 + d) * w]
        log_a = -RG_C * r * sp[d:d + 1]
        a = jnp.exp(log_a)
        b = jnp.sqrt(jnp.tanh(-log_a) * (1.0 + a * a)) * (i * xc)
        step = 1
        while step < c:
            keep = (pos >= step) if forward else (pos < c - step)
            shift = step if forward else seq - step
            a_s = pltpu.roll(a, shift, axis=0)
            b_s = pltpu.roll(b, shift, axis=0)
            b = jnp.where(keep, a * b_s + b, b)
            a = jnp.where(keep, a * a_s, a)
            step *= 2
        a3 = a.reshape(n, c, w)
        b3 = b.reshape(n, c, w)
        edge = c - 1 if forward else 0
        a_end = a3[:, edge:edge + 1, :]
        b_end = b3[:, edge:edge + 1, :]
        h = h0_ref[0, d:d + 1, :] if has_s0 else jnp.zeros((1, w), F32)
        h_in = [None] * n
        for ci in (range(n) if forward else range(n - 1, -1, -1)):
            h_in[ci] = h
            h = b_end[ci] + a_end[ci] * h
        hs.append((b3 + a3 * jnp.stack(h_in, axis=0)).reshape(seq, w))
    y_ref[...] = (hs[0] + hs[1]) * _gelu_tanh(gate)
    if emit_state:
        st_ref[0, 0:1, :] = hs[0][seq - 1:seq]
        st_ref[0, 1:2, :] = hs[1][0:1]


def _rglru(rg_all, tok0, batch, seq, conv_w, conv_b, w_gates, b_gates, softplus_neg_lam, h0, emit_state):
    has_s0 = h0 is not None
    consts = [conv_w, conv_b, w_gates, b_gates, softplus_neg_lam]
    return _mixer_call(functools.partial(_rglru_kernel, has_s0, emit_state), "rglru", rg_all, tok0, batch, seq,
                       consts, [h0] if has_s0 else [],
                       [(batch, 2, RG_WIDTH)] if emit_state else [], [])


def _hgrn_kernel(has_s0, emit_state, hg_ref, lb_ref, gain_ref, *refs):
    s0_ref = refs[0] if has_s0 else None
    y_ref = refs[1 if has_s0 else 0]
    st_ref = refs[-1] if emit_state else None
    seq = hg_ref.shape[0]
    w = HG_WIDTH
    c = HG_CHUNK
    n = seq // c
    mid = c // 2
    chunks = lambda a: a.reshape(n, c, w)
    pos = _iota((seq, 1), 0) % c
    stack_mask = _head_mask(c)[None]
    bd_mask = _block_diag_mask()
    pair_shape = (c, N_HEADS * c)
    t_idx = _iota(pair_shape, 0)
    s_idx = _iota(pair_shape, 1) % c

    def stack_heads(a3):
        return jnp.where(stack_mask, jnp.concatenate([a3] * N_HEADS, axis=1), 0.0).astype(BF16)

    q3 = chunks(_silu(hg_ref[:, 0:w]))
    v3 = chunks(hg_ref[:, 3 * w:4 * w])
    v_stack = stack_heads(v3)
    v_t = jnp.swapaxes(v3, 1, 2).astype(BF16)
    lbv = lb_ref[...]
    o_sum = None
    finals = []
    for d in range(2):
        forward = d == 0
        f_pre = hg_ref[:, (1 + d) * w:(2 + d) * w]
        lo = lbv[d:d + 1]
        a1 = jnp.log(lo)
        a2 = jnp.log(1.0 - lo) + _log_sigmoid(f_pre)
        g = jnp.maximum(a1, a2) + jnp.log(1.0 + jnp.exp(-jnp.abs(a1 - a2)))
        k3 = chunks((1.0 - lo) * _sigmoid(-f_pre))
        step = 1
        while step < c:
            if forward:
                g = jnp.where(pos >= step, g + pltpu.roll(g, step, axis=0), g)
            else:
                g = jnp.where(pos < c - step, g + pltpu.roll(g, seq - step, axis=0), g)
            step *= 2
        g3 = chunks(g)
        g_ref = g3[:, mid:mid + 1, :]
        g_tot = g3[:, c - 1:c, :] if forward else g3[:, 0:1, :]
        q_in = (q3 * jnp.exp(g3)).astype(BF16)
        q_sc = (q3 * jnp.exp(g3 - g_ref)).astype(BF16)
        k_stack = stack_heads(k3 * jnp.exp(g_ref - g3))
        k_out = (k3 * jnp.exp(g_tot - g3)).astype(BF16)
        pair = jnp.einsum("ntl,nrl->ntr", q_sc, k_stack, preferred_element_type=F32)
        causal = (t_idx >= s_idx) if forward else (t_idx <= s_idx)
        pair = jnp.where(causal[None], pair, 0.0).astype(BF16)
        o_intra = jnp.einsum("ntr,nrv->ntv", pair, v_stack, preferred_element_type=F32)
        upd = jnp.einsum("nvs,nsk->nvk", v_t, k_out, preferred_element_type=F32)
        decay = jnp.exp(g_tot)
        st = _load_state(s0_ref, d) if has_s0 else jnp.zeros((w, w), F32)
        o_inter = [None] * n
        for ci in (range(n) if forward else range(n - 1, -1, -1)):
            o_inter[ci] = _mm(q_in[ci], st.astype(BF16), _NT)
            st = st * decay[ci] + jnp.where(bd_mask, upd[ci], 0.0)
        o_dir = o_intra + jnp.stack(o_inter, axis=0)
        o_sum = o_dir if o_sum is None else o_sum + o_dir
        finals.append(st)

    ones_bd = jnp.where(bd_mask, 1.0 / HEAD_DIM, 0.0).astype(BF16)
    o = _head_rmsnorm(o_sum.reshape(seq, w), ones_bd) * gain_ref[...]
    y_ref[...] = o * _silu(hg_ref[:, 4 * w:5 * w])
    if emit_state:
        _store_state(st_ref, 0, finals[0])
        _store_state(st_ref, 1, finals[1])


def _state_shape(batch):
    return (batch, 2, N_HEADS, HEAD_DIM, HEAD_DIM)


def _hgrn(hg_all, tok0, batch, seq, lb_l, gain_l, s0, emit_state):
    has_s0 = s0 is not None
    return _mixer_call(functools.partial(_hgrn_kernel, has_s0, emit_state), "hgrn2", hg_all, tok0, batch, seq,
                       [lb_l, gain_l], [s0] if has_s0 else [],
                       [_state_shape(batch)] if emit_state else [], [])


def _ret_kernel(use_rope, has_s0, emit_state, ret_ref, cos_ref, sin_ref, dl_ref, dlp_ref, *refs):
    s0_ref = refs[0] if has_s0 else None
    y_ref = refs[1 if has_s0 else 0]
    st_ref = refs[-1] if emit_state else None
    seq = ret_ref.shape[0]
    w = RET_WIDTH
    c = min(RET_CHUNK, seq)
    n = seq // c
    chunks = lambda a: a.reshape(n, c, w)
    q = ret_ref[:, 0:w]
    k = ret_ref[:, w:2 * w] * (HEAD_DIM ** -0.5)
    if use_rope:
        lane = _iota((seq, w), 1)
        even = (lane % 2) == 0
        cos = cos_ref[...]
        sin = sin_ref[...]

        def rope(x):
            nxt = pltpu.roll(x, w - 1, axis=1)
            prv = pltpu.roll(x, 1, axis=1)
            return x * cos + jnp.where(even, nxt, prv) * sin

        q = rope(q)
        k = rope(k)
    q3, k3, v3 = chunks(q), chunks(k), chunks(ret_ref[:, 2 * w:3 * w])

    lg = _log_sigmoid(dl_ref[...])
    lgp = _log_sigmoid(dlp_ref[...])
    pair_shape = (c, N_HEADS * c)
    t_idx = _iota(pair_shape, 0)
    s_idx = _iota(pair_shape, 1) % c
    dist = (t_idx - s_idx).astype(F32)
    decay = jnp.exp(jnp.where(dist >= 0, dist * lgp[0:1], -dist * lgp[1:2]))
    decay = decay + jnp.where(dist == 0, 1.0, 0.0)
    stack_mask = _head_mask(c)[None]
    bd_mask = _block_diag_mask()
    pos = _iota((c, 1), 0).astype(F32)
    fc = float(c)

    def stack_heads(a3):
        return jnp.where(stack_mask, jnp.concatenate([a3] * N_HEADS, axis=1), 0.0).astype(BF16)

    pair = jnp.einsum("ntl,nrl->ntr", q3.astype(BF16), stack_heads(k3), preferred_element_type=F32)
    pair = (pair * decay[None]).astype(BF16)
    o_sum = jnp.einsum("ntr,nrv->ntv", pair, stack_heads(v3), preferred_element_type=F32)
    v_t = jnp.swapaxes(v3, 1, 2).astype(BF16)
    finals = []
    for d in range(2):
        forward = d == 0
        lgd = lg[d:d + 1]
        q_in = (q3 * jnp.exp(((pos + 1.0) if forward else (fc - pos)) * lgd)).astype(BF16)
        k_out = (k3 * jnp.exp(((fc - 1.0 - pos) if forward else pos) * lgd)).astype(BF16)
        upd = jnp.einsum("nvs,nsk->nvk", v_t, k_out, preferred_element_type=F32)
        chunk_decay = jnp.exp(fc * lgd)
        st = _load_state(s0_ref, d) if has_s0 else jnp.zeros((w, w), F32)
        o_inter = [None] * n
        for ci in (range(n) if forward else range(n - 1, -1, -1)):
            o_inter[ci] = _mm(q_in[ci], st.astype(BF16), _NT)
            st = st * chunk_decay + jnp.where(bd_mask, upd[ci], 0.0)
        o_sum = o_sum + jnp.stack(o_inter, axis=0)
        finals.append(st)

    ones_bd = jnp.where(bd_mask, 1.0 / HEAD_DIM, 0.0).astype(BF16)
    o = _head_rmsnorm(o_sum.reshape(seq, w), ones_bd)
    y_ref[...] = _silu(ret_ref[:, 3 * w:4 * w]) * o
    if emit_state:
        _store_state(st_ref, 0, finals[0])
        _store_state(st_ref, 1, finals[1])


def _rope_tables(seq):
    rows = seq // GRID_W
    row = np.repeat(np.arange(rows), GRID_W).astype(np.float64)
    col = (np.arange(seq) % GRID_W).astype(np.float64)
    n_freq = HEAD_DIM // 4
    inv_freq = ROPE_BASE ** (-np.arange(n_freq, dtype=np.float64) / n_freq)
    ang = np.concatenate([row[:, None] * inv_freq, col[:, None] * inv_freq], axis=-1)
    ang = np.repeat(ang, 2, axis=1)
    cos = np.tile(np.cos(ang), (1, N_HEADS))
    sin = np.tile(np.sin(ang) * np.where(np.arange(HEAD_DIM) % 2 == 0, -1.0, 1.0)[None, :], (1, N_HEADS))
    return jnp.asarray(cos, F32), jnp.asarray(sin, F32)


def _retention(ret_all, tok0, batch, seq, use_rope, decay_l, s0, emit_state):
    has_s0 = s0 is not None
    w = RET_WIDTH
    c = min(RET_CHUNK, seq)
    cos, sin = _rope_tables(seq)
    dl = jnp.repeat(decay_l, HEAD_DIM, axis=-1)
    dlp = jnp.repeat(decay_l, c, axis=-1)
    return _mixer_call(functools.partial(_ret_kernel, use_rope, has_s0, emit_state), "retention", ret_all, tok0,
                       batch, seq, [cos, sin, dl, dlp], [s0] if has_s0 else [],
                       [_state_shape(batch)] if emit_state else [], [])


def _route(h2s, wrt_ref, rb_ref):
    tm = h2s[0].shape[0]
    lane = _iota((N_EXPERTS, tm), 0).astype(F32)
    group = (_iota((N_EXPERTS, tm), 0) // E_PER_GROUP).astype(F32)
    neg = -jnp.inf
    each = lambda f, *cols: [f(*args) for args in zip(*cols)]

    def first_argmax(vals):
        mx = jnp.max(vals, axis=0, keepdims=True)
        idx = jnp.min(jnp.where(vals == mx, lane, float(N_EXPERTS)), axis=0, keepdims=True)
        return mx, idx

    def softmax(logits):
        e = jnp.exp(logits - jnp.max(logits, axis=0, keepdims=True))
        return e / jnp.sum(e, axis=0, keepdims=True)

    logits = each(lambda h2: _mm3(wrt_ref[...], h2, _NT), h2s)
    probs = each(softmax, logits)
    sel = each(lambda p: p + rb_ref[...], probs)

    def group_score(s, g):
        vals = jnp.where(group == float(g), s, neg)
        m1, i1 = first_argmax(vals)
        m2, _ = first_argmax(jnp.where(lane == i1, neg, vals))
        return m1 + m2

    best_score = each(lambda s: group_score(s, 0), sel)
    best = each(jnp.zeros_like, best_score)
    for g in range(1, N_GROUPS):
        score = each(lambda s: group_score(s, g), sel)
        take = each(lambda sc, bs: sc > bs, score, best_score)
        best = each(lambda t, b: jnp.where(t, float(g), b), take, best)
        best_score = each(lambda t, sc, bs: jnp.where(t, sc, bs), take, score, best_score)

    def gates_of(s, p, b):
        vals = jnp.where(group == b, s, neg)
        _, i1 = first_argmax(vals)
        _, i2 = first_argmax(jnp.where(lane == i1, neg, vals))
        picked = jnp.where((lane == i1) | (lane == i2), p, 0.0)
        return picked / jnp.sum(picked, axis=0, keepdims=True)

    return each(gates_of, sel, probs, best), best


INFO_ROWS = 8
INFO_GROUP = 4
INFO_RANK = 5
H2_WIDTH = D_MODEL + LANES


def _route_kernel(n0_tiles, xp_ref, xs_ref, *refs):
    y_refs = refs[:8]
    mod_ref, wo_ref, n2_ref, wrt_ref, rb_ref, x1_ref, h2_ref, info_ref, cend_ref, carry_s = refs[8:]
    i = pl.program_id(0)
    first = i < n0_tiles
    d = D_MODEL
    tb = TB_MOE
    mod = mod_ref[0, 0]

    @pl.when(i == 0)
    def _():
        carry_s[...] = jnp.zeros_like(carry_s)

    n_sub = TM_ROUTE // tb
    each = lambda f, *cols: [f(*args) for args in zip(*cols)]
    rows = [slice(h * tb, (h + 1) * tb) for h in range(n_sub)]
    expert = _iota((N_EXPERTS, tb), 0).astype(F32)
    upper = (_iota((tb, tb), 0) <= _iota((tb, tb), 1)).astype(BF16)
    eye = (_iota((tb, tb), 0) == _iota((tb, tb), 1)).astype(BF16)

    def mixed_of(r):
        ys = [jnp.where(first, y_refs[2 * j][r, :], y_refs[2 * j + 1][r, :]).astype(BF16) for j in range(4)]
        return _mm(jnp.concatenate(ys, axis=1), wo_ref[...])

    mixed = each(mixed_of, rows)
    x1 = each(lambda r, m: jnp.where(first, xp_ref[r, :], xs_ref[r, :]) + mod[:, 2 * d:3 * d] * m, rows, mixed)

    def h2_of(v):
        h2 = v * lax.rsqrt(jnp.mean(v * v, axis=-1, keepdims=True) + EPS) * n2_ref[...]
        return h2 * (1.0 + mod[:, 4 * d:5 * d]) + mod[:, 3 * d:4 * d]

    h2 = each(h2_of, x1)
    gates, best = _route(h2, wrt_ref, rb_ref)
    g4 = each(lambda g, b: [jnp.sum(jnp.where(expert == E_PER_GROUP * b + j, g, 0.0), axis=0, keepdims=True)
                            for j in range(E_PER_GROUP)], gates, best)
    onehot = each(lambda b: jnp.where(expert == b, 1.0, 0.0), best)
    counts = each(lambda o: _mm(o.astype(BF16), upper), onehot)

    def pieces_of(g):
        p = _split3(jnp.concatenate(g + [jnp.zeros((N_EXPERTS - E_PER_GROUP, tb), F32)], axis=0))
        return jnp.concatenate(list(p) + [jnp.zeros((LANES - 3 * N_EXPERTS, tb), BF16)], axis=0)

    ext = each(lambda g: _mm(eye, pieces_of(g), _NT).astype(BF16), g4)

    carry = carry_s[...]
    for h in range(n_sub):
        incl = counts[h] + carry[:, 0:1]
        rank = jnp.sum(onehot[h] * incl, axis=0, keepdims=True) - 1.0
        carry = jnp.broadcast_to(incl[:, tb - 1:tb], (N_EXPERTS, LANES))
        x1_ref[rows[h], :] = x1[h]
        h2_ref[rows[h], 0:d] = h2[h].astype(BF16)
        h2_ref[rows[h], d:d + LANES] = ext[h]
        info_ref[:, rows[h]] = jnp.concatenate(g4[h] + [best[h], rank, jnp.zeros((INFO_ROWS - 6, tb), F32)], axis=0)
        cend_ref[h] = carry
    carry_s[...] = carry


def _route_call(layer, xp, xs, ys, mod_l, w_out, norm2_l, w_router, router_bias, dec_seq):
    n_prompt_tok = xp.shape[0]
    t_all = n_prompt_tok + xs.shape[0]
    tm = TM_ROUTE
    sub = tm // TB_MOE
    n0_tiles = n_prompt_tok // tm
    row = _mod_row_map(tm, n_prompt_tok, dec_seq)
    tok = lambda width: pl.BlockSpec((tm, width), lambda i: (i, 0))
    y_specs, y_args = [], []
    for yp, ysm in ys:
        y_specs += _pair_specs(tm, MIX_WIDTH, n0_tiles)
        y_args += [yp, ysm]
    return pl.pallas_call(
        functools.partial(_route_kernel, n0_tiles),
        out_shape=[jax.ShapeDtypeStruct((t_all, D_MODEL), F32), jax.ShapeDtypeStruct((t_all, H2_WIDTH), BF16),
                   jax.ShapeDtypeStruct((INFO_ROWS, t_all), F32),
                   jax.ShapeDtypeStruct((t_all // TB_MOE, N_EXPERTS, LANES), F32)],
        grid=(t_all // tm,),
        in_specs=_pair_specs(tm, D_MODEL, n0_tiles) + y_specs + [
            pl.BlockSpec((1, 1, 1, N_MOD * D_MODEL), lambda i: (0, row(i), 0, 0)),
            _const_layer_spec(w_out.shape, layer), _const_spec((1, D_MODEL)),
            _const_spec((N_EXPERTS, D_MODEL)), _const_spec((N_EXPERTS, 1)),
        ],
        out_specs=[tok(D_MODEL), tok(H2_WIDTH), pl.BlockSpec((INFO_ROWS, tm), lambda i: (0, i)),
                   pl.BlockSpec((sub, N_EXPERTS, LANES), lambda i: (i, 0, 0))],
        scratch_shapes=[pltpu.VMEM((N_EXPERTS, LANES), F32)],
        compiler_params=_params(("arbitrary",)),
        name="route",
    )(xp, xs, *y_args, mod_l, w_out, norm2_l, w_router.T, router_bias.reshape(N_EXPERTS, 1))


def _moe_plan(cend, info):
    ts = TS_MOE
    tb = TR_MOE
    n_blocks = cend.shape[0]
    t_all = n_blocks * TB_MOE
    n_tiles = t_all // ts + N_GROUPS
    sub_per_tile = ts // tb
    n_sub = n_tiles * sub_per_tile
    cend = cend[:, :N_GROUPS, 0].astype(jnp.int32)
    cum = jnp.concatenate([jnp.zeros((1, N_GROUPS), jnp.int32), cend[:-1]], axis=0)
    cnt = cend - cum
    grp = info[INFO_GROUP].astype(jnp.int32)
    rank = info[INFO_RANK].astype(jnp.int32)
    tot = cend[-1]
    padded = ((tot + ts - 1) // ts) * ts
    off = jnp.cumsum(padded) - padded
    pos = rank + sum(jnp.where(grp == g, off[g], 0) for g in range(N_GROUPS))

    sub_start = jnp.arange(n_sub, dtype=jnp.int32) * tb
    in_g = (sub_start[:, None] >= off[None, :]) & (sub_start[:, None] < (off + padded)[None, :])
    g_of = jnp.argmax(in_g, axis=1).astype(jnp.int32)
    r0 = sub_start - off[g_of]
    r1 = jnp.minimum(r0 + tb, tot[g_of])
    live = jnp.any(in_g, axis=1) & (r1 > r0)
    cend_g = cend[:, g_of]
    blo = jnp.where(live, jnp.sum(cend_g <= r0[None, :], axis=0), 0).astype(jnp.int32)
    bhi = jnp.where(live, jnp.sum(cend_g <= (r1 - 1)[None, :], axis=0), -1).astype(jnp.int32)

    used = jnp.any(in_g, axis=1)[::sub_per_tile]
    tile_group = g_of[::sub_per_tile]
    eidx = E_PER_GROUP * tile_group[:, None] + jnp.arange(E_PER_GROUP, dtype=jnp.int32)[None, :]
    n_used = jnp.sum(used.astype(jnp.int32))
    last = eidx[jnp.maximum(n_used - 1, 0), E_PER_GROUP - 1]
    eidx = jnp.where(used[:, None], eidx, last).reshape(-1).astype(jnp.int32)

    start = off[None, :] + cum
    end = off[None, :] + cend
    has = cnt > 0
    slo = jnp.where(has, start // tb, 0).reshape(-1).astype(jnp.int32)
    shi = jnp.where(has, (end - 1) // tb, -1).reshape(-1).astype(jnp.int32)
    tile_start = sub_start[::sub_per_tile]
    rows = jnp.where(used, jnp.clip(tot[tile_group] - (tile_start - off[tile_group]), 0, ts), 0)
    return dict(pos=pos, eidx=eidx, rows=rows.astype(jnp.int32), blo=blo, bhi=bhi, slo=slo, shi=shi,
                n_tiles=n_tiles)


def _expert_kernel(eidx_ref, rows_ref, blo_ref, bhi_ref, h2_ref, pos_ref, wg_ref, wu_ref, wd_ref,
                   ys_ref, xs_s, acc_s, ax_s):
    i = pl.program_id(0)
    j = pl.program_id(1)
    n_rows = rows_ref[i]
    used = n_rows > 0
    tb, tr = TB_MOE, TR_MOE
    sub_per_tile = TS_MOE // tr

    @pl.when(used & (j == 0))
    def _():
        r_idx = _iota((tr, tb), 0)
        for part in range(sub_per_tile):
            s = i * sub_per_tile + part
            base = s * tr
            ax_s[...] = jnp.zeros_like(ax_s)

            def gather(b, carry):
                sel = jnp.where(pos_ref[b] - base == r_idx, 1.0, 0.0).astype(BF16)
                tok = pl.ds(pl.multiple_of(b * tb, tb), tb)
                ax_s[...] += _mm(sel, h2_ref[tok, :])
                return carry

            lax.fori_loop(blo_ref[s], bhi_ref[s] + 1, gather, 0)
            xs_s[part * tr:(part + 1) * tr, :] = ax_s[...].astype(BF16)
        acc_s[...] = jnp.zeros_like(acc_s)

    half_rows = TS_MOE // 2
    for part in range(2):
        rows = slice(part * half_rows, (part + 1) * half_rows)

        @pl.when(n_rows > part * half_rows)
        def _():
            x = xs_s[rows, 0:D_MODEL]
            pieces = xs_s[rows, D_MODEL:H2_WIDTH].astype(F32)
            ge = jnp.sum(jnp.where(_iota(pieces.shape, 1) % N_EXPERTS == j, pieces, 0.0), axis=-1, keepdims=True)
            hh = _silu(_mm(x, wg_ref[0, 0].astype(BF16))) * _mm(x, wu_ref[0, 0].astype(BF16)) * ge
            acc_s[rows, :] += _mm(hh.astype(BF16), wd_ref[0, 0].astype(BF16))

    @pl.when(j == E_PER_GROUP - 1)
    def _():
        ys_ref[...] = jnp.where(used, acc_s[...], 0.0).astype(BF16)


def _expert_call(layer, plan, h2, w_gate, w_up, w_down):
    t_all = h2.shape[0]
    tb, ts = TB_MOE, TS_MOE
    n_tiles = plan["n_tiles"]
    pos_rows = plan["pos"].reshape(t_all // tb, 1, tb)
    weight = lambda shape: pl.BlockSpec((1, 1) + shape, lambda i, j, eidx, *_: (layer, eidx[i * E_PER_GROUP + j], 0, 0))
    grid_spec = pltpu.PrefetchScalarGridSpec(
        num_scalar_prefetch=4,
        grid=(n_tiles, E_PER_GROUP),
        in_specs=[
            _const_spec((t_all, H2_WIDTH)), _const_spec(pos_rows.shape),
            weight((D_MODEL, D_EXPERT)), weight((D_MODEL, D_EXPERT)), weight((D_EXPERT, D_MODEL)),
        ],
        out_specs=pl.BlockSpec((ts, D_MODEL), lambda i, j, *_: (i, 0)),
        scratch_shapes=[pltpu.VMEM((ts, H2_WIDTH), BF16), pltpu.VMEM((ts, D_MODEL), F32),
                        pltpu.VMEM((TR_MOE, H2_WIDTH), F32)],
    )
    return pl.pallas_call(
        _expert_kernel,
        out_shape=jax.ShapeDtypeStruct((n_tiles * ts, D_MODEL), BF16),
        grid_spec=grid_spec,
        compiler_params=_params(("arbitrary", "arbitrary")),
        name=f"experts_{layer}",
    )(plan["eidx"], plan["rows"], plan["blo"], plan["bhi"], h2, pos_rows, w_gate, w_up, w_down)


def _combine_kernel(final, n0_tiles, slo_ref, shi_ref, ys_ref, pos_ref, x1_ref, mod_ref, nf_ref,
                    op_ref, os_ref, acc_s):
    b = pl.program_id(0)
    d = D_MODEL
    tb, tr = TB_MOE, TR_MOE
    acc_s[...] = jnp.zeros_like(acc_s)
    c_idx = _iota((tb, tr), 1)
    pos = pos_ref[...]
    for g in range(N_GROUPS):
        def scatter(s, carry):
            sel = jnp.where(pos - s * tr == c_idx, 1.0, 0.0).astype(BF16)
            acc_s[...] += _mm(sel, ys_ref[pl.ds(pl.multiple_of(s * tr, tr), tr), :])
            return carry

        lax.fori_loop(slo_ref[b * N_GROUPS + g], shi_ref[b * N_GROUPS + g] + 1, scatter, 0)

    def result():
        x2 = x1_ref[...] + mod_ref[0, 0][:, 5 * d:6 * d] * acc_s[...]
        if final:
            x2 = x2 * lax.rsqrt(jnp.mean(x2 * x2, axis=-1, keepdims=True) + EPS) * nf_ref[...]
        return x2

    @pl.when(b < n0_tiles)
    def _():
        op_ref[...] = result()

    @pl.when(b >= n0_tiles)
    def _():
        os_ref[...] = result()


def _combine_call(final, plan, ys_sorted, x1, mod_l, norm_final, n_prompt_tok, dec_seq):
    t_all = x1.shape[0]
    tb = TB_MOE
    n0_tiles = n_prompt_tok // tb
    row = _mod_row_map(tb, n_prompt_tok, dec_seq)
    grid_spec = pltpu.PrefetchScalarGridSpec(
        num_scalar_prefetch=2,
        grid=(t_all // tb,),
        in_specs=[
            _const_spec(ys_sorted.shape),
            pl.BlockSpec((tb, 1), lambda b, *_: (b, 0)),
            pl.BlockSpec((tb, D_MODEL), lambda b, *_: (b, 0)),
            pl.BlockSpec((1, 1, 1, N_MOD * D_MODEL), lambda b, *_: (0, row(b), 0, 0)),
            _const_spec((1, D_MODEL)),
        ],
        out_specs=_pair_specs(tb, D_MODEL, n0_tiles),
        scratch_shapes=[pltpu.VMEM((tb, D_MODEL), F32)],
    )
    return pl.pallas_call(
        functools.partial(_combine_kernel, final, n0_tiles),
        out_shape=[jax.ShapeDtypeStruct((n_prompt_tok, D_MODEL), F32),
                   jax.ShapeDtypeStruct((t_all - n_prompt_tok, D_MODEL), F32)],
        grid_spec=grid_spec,
        compiler_params=_params(("arbitrary",)),
        name="combine",
    )(plan["slo"], plan["shi"], ys_sorted, plan["pos"].reshape(t_all, 1), x1, mod_l, norm_final)


def _rg_gate_weights(wa, ba, wx, bx):
    eye = jnp.eye(RG_HEADS, dtype=wa.dtype)

    def dense(wd):
        return jnp.einsum("hij,hg->higj", wd, eye).reshape(RG_WIDTH, RG_WIDTH)

    w = jnp.concatenate([dense(wa[0]), dense(wa[1]), dense(wx[0]), dense(wx[1])], axis=1)
    b = jnp.concatenate([ba[0], ba[1], bx[0], bx[1]], axis=0).reshape(1, 4 * RG_WIDTH)
    return w, b


def kernel(x_prompt, x_sample, state_rglru, state_hgrn, state_ret, c, c_ctx, norm1, norm2, norm_final, w_ada, b_ada, w_in, w_out, hy_conv_w, hy_conv_b, hy_w1, hy_b1, hy_w2, hy_b2, hy_w3, hy_d, rg_conv_w, rg_conv_b, rg_wa, rg_ba, rg_wx, rg_bx, rg_lambda, hg_lb, hg_norm, ret_decay, w_router, router_bias, w_gate, w_up, w_down):
    batch, seq, d = x_prompt.shape
    dec_batch, dec_seq, _ = x_sample.shape
    assert d == D_MODEL and dec_batch + 1 <= COND_ROWS
    n_prompt_tok = batch * seq

    lb_cum = jnp.cumsum(jax.nn.softmax(hg_lb.astype(F32), axis=0), axis=0)
    lb_all = lb_cum - lb_cum[0:1]

    cond = jnp.zeros((COND_ROWS, d), F32).at[0].set(c_ctx).at[1:1 + dec_batch].set(c)
    mod = _modulation(cond, w_ada, b_ada).reshape(DEPTH, COND_ROWS, 1, N_MOD * d)

    w_in_b = w_in.astype(BF16)
    w_out_b = w_out.astype(BF16)

    passes = (
        dict(tok0=0, batch=batch, seq=seq, rope=False),
        dict(tok0=n_prompt_tok, batch=dec_batch, seq=dec_seq, rope=True),
    )
    filters = {p["seq"]: _hyena_filters(p["seq"], hy_w1, hy_b1, hy_w2, hy_b2, hy_w3) for p in passes}
    tables = {p["seq"]: _hyena_tables(p["seq"]) for p in passes}

    xp = x_prompt.reshape(-1, d)
    xs = x_sample.reshape(-1, d)
    new_rg, new_hg, new_ret = [], [], []
    for l in range(DEPTH):
        hy_all, rg_all, hg_all, ret_all = _in_projection(
            l, xp, xs, mod[l:l + 1], norm1[l].reshape(1, d), w_in_b, dec_seq)
        wg, bg = _rg_gate_weights(rg_wa[l], rg_ba[l], rg_wx[l], rg_bx[l])
        sp = jax.nn.softplus(-rg_lambda[l])
        ys = [[], [], [], []]
        for pi, p in enumerate(passes):
            first = pi == 0
            geom = (p["tok0"], p["batch"], p["seq"])
            y_hy = _hyena(hy_all, *geom, hy_conv_w[l], hy_conv_b[l].reshape(1, -1), hy_d[l],
                          (filters[p["seq"]], l), tables[p["seq"]])
            y_rg, *st_rg = _rglru(rg_all, *geom, rg_conv_w[l], rg_conv_b[l].reshape(1, -1), wg, bg, sp,
                                  None if first else (state_rglru, l), first)
            y_hg, *st_hg = _hgrn(hg_all, *geom, lb_all[l], hg_norm[l].reshape(1, -1),
                                 None if first else (state_hgrn, l), first)
            y_ret, *st_ret = _retention(ret_all, *geom, p["rope"], ret_decay[l],
                                        None if first else (state_ret, l), first)
            for lst, y in zip(ys, (y_hy, y_rg, y_hg, y_ret)):
                lst.append(y)
            if first:
                new_rg.append(st_rg[0])
                new_hg.append(st_hg[0])
                new_ret.append(st_ret[0])
        x1, h2, info, cend = _route_call(l, xp, xs, ys, mod[l:l + 1], w_out_b, norm2[l].reshape(1, d),
                                         w_router, router_bias, dec_seq)
        plan = _moe_plan(cend, info)
        ys_sorted = _expert_call(l, plan, h2, w_gate, w_up, w_down)
        xp, xs = _combine_call(l == DEPTH - 1, plan, ys_sorted, x1, mod[l:l + 1], norm_final.reshape(1, d),
                               n_prompt_tok, dec_seq)

    return (xp.reshape(batch, seq, d), xs.reshape(dec_batch, dec_seq, d), jnp.stack(new_rg, axis=1),
            jnp.stack(new_hg, axis=1), jnp.stack(new_ret, axis=1))
```

```python
import functools
import math

import numpy as np
import jax
import jax.numpy as jnp
from jax import lax
from jax.experimental import pallas as pl
from jax.experimental.pallas import tpu as pltpu

F32 = jnp.float32
BF16 = jnp.bfloat16

D_MODEL = 1024
DEPTH = 2
GRID_W = 64
HY_WIDTH = 256
RG_WIDTH = 256
HG_WIDTH = 256
RET_WIDTH = 256
MIX_WIDTH = 256
HY_ORDER = 2
HY_EMB = 33
HY_BANDS = 16
HY_FFN = 64
HY_DECAY_TARGET = 1e-2
HY_DECAY_SHORT = 0.3
HY_DECAY_LONG = 1.5
HY_GROUP_COLS = 512
RG_HEADS = 8
RG_HEAD_DIM = 32
RG_C = 8.0
RG_CHUNK = 16
N_HEADS = 4
HEAD_DIM = 64
HG_CHUNK = 64
RET_CHUNK = 256
ROPE_BASE = 10000.0
N_EXPERTS = 16
N_GROUPS = 4
E_PER_GROUP = 4
D_EXPERT = 512
N_MOD = 6
EPS = 1e-6
PROJ_HY = 3 * HY_WIDTH
PROJ_RG = 2 * RG_WIDTH
PROJ_HG = 5 * HG_WIDTH
PROJ_RET = 4 * RET_WIDTH
PROJ_WIDTH = PROJ_HY + PROJ_RG + PROJ_HG + PROJ_RET
COND_ROWS = 16
LANES = 128
VMEM_LIMIT = 56 * 1024 * 1024
TM_PROJ = 512
TB_MOE = 512
TR_MOE = 256
TM_ROUTE = 2 * TB_MOE
TS_MOE = 1024
TN_MOD = 1536

_NN = (((1,), (0,)), ((), ()))
_NT = (((1,), (1,)), ((), ()))


def _mm(a, b, dn=_NN):
    return lax.dot_general(a, b, dn, preferred_element_type=F32)


def _split2(x):
    hi = x.astype(BF16)
    lo = (x - hi.astype(F32)).astype(BF16)
    return hi, lo


def _split3(x):
    hi = x.astype(BF16)
    r = x - hi.astype(F32)
    mid = r.astype(BF16)
    lo = (r - mid.astype(F32)).astype(BF16)
    return hi, mid, lo


def _mm3(a, b, dn=_NN):
    ah, al = _split2(a)
    bh, bl = _split2(b)
    return _mm(ah, bh, dn) + (_mm(ah, bl, dn) + _mm(al, bh, dn))


def _mm_exact_rhs(a, b_bf16):
    a1, a2, a3 = _split3(a)
    return _mm(a1, b_bf16) + (_mm(a2, b_bf16) + _mm(a3, b_bf16))


def _sigmoid(x):
    return 1.0 / (1.0 + jnp.exp(-x))


def _silu(x):
    return x * _sigmoid(x)


def _log_sigmoid(x):
    return jnp.minimum(x, 0.0) - jnp.log(1.0 + jnp.exp(-jnp.abs(x)))


def _gelu_tanh(x):
    return 0.5 * x * (1.0 + jnp.tanh(math.sqrt(2.0 / math.pi) * (x + 0.044715 * (x * x * x))))


def _iota(shape, dim):
    return lax.broadcasted_iota(jnp.int32, shape, dim)


def _shift_rows(u, k, row):
    n = u.shape[0]
    if k == 0:
        return u
    r = pltpu.roll(u, (-k) % n, axis=0)
    if k < 0:
        return jnp.where(row >= -k, r, 0.0)
    return jnp.where(row < n - k, r, 0.0)


def _head_mask(n_rows_per_head):
    shape = (N_HEADS * n_rows_per_head, MIX_WIDTH)
    return (_iota(shape, 0) // n_rows_per_head) == (_iota(shape, 1) // HEAD_DIM)


def _block_diag_mask():
    shape = (MIX_WIDTH, MIX_WIDTH)
    return (_iota(shape, 0) // HEAD_DIM) == (_iota(shape, 1) // HEAD_DIM)


def _head_rmsnorm(o, ones_bd):
    ms = _mm_exact_rhs(o * o, ones_bd)
    return o * lax.rsqrt(ms + EPS)


def _load_state(s0_ref, d):
    zero = jnp.zeros((HEAD_DIM, HEAD_DIM), F32)
    rows = []
    for h in range(N_HEADS):
        blk = s0_ref[0, d, h].T
        rows.append(jnp.concatenate([blk if g == h else zero for g in range(N_HEADS)], axis=1))
    return jnp.concatenate(rows, axis=0)


def _store_state(st_ref, d, st):
    for h in range(N_HEADS):
        lo, hi = h * HEAD_DIM, (h + 1) * HEAD_DIM
        st_ref[0, d, h] = st[lo:hi, lo:hi].T


def _params(sem, vmem=VMEM_LIMIT):
    return pltpu.CompilerParams(dimension_semantics=sem, vmem_limit_bytes=vmem)


def _const_spec(shape):
    nd = len(shape)
    return pl.BlockSpec(shape, lambda *_: (0,) * nd, pipeline_mode=pl.Buffered(1))


def _const_layer_spec(shape, layer):
    nd = len(shape)
    return pl.BlockSpec((None,) + tuple(shape[1:]), lambda *_: (layer,) + (0,) * (nd - 1),
                        pipeline_mode=pl.Buffered(1))


def _mod_kernel(cond_ref, w_ref, b_ref, o_ref):
    o_ref[0] = _mm3(_silu(cond_ref[...]), w_ref[0]) + b_ref[0]


def _modulation(cond, w_ada, b_ada):
    tn = TN_MOD
    n_mod = N_MOD * D_MODEL
    return pl.pallas_call(
        _mod_kernel,
        out_shape=jax.ShapeDtypeStruct((DEPTH, COND_ROWS, n_mod), F32),
        grid=(DEPTH, n_mod // tn),
        in_specs=[
            pl.BlockSpec((COND_ROWS, D_MODEL), lambda l, j: (0, 0)),
            pl.BlockSpec((1, D_MODEL, tn), lambda l, j: (l, 0, j)),
            pl.BlockSpec((1, 1, tn), lambda l, j: (l, 0, j)),
        ],
        out_specs=pl.BlockSpec((1, COND_ROWS, tn), lambda l, j: (l, 0, j)),
        compiler_params=_params(("parallel", "parallel")),
        name="modulation",
    )(cond, w_ada, b_ada.reshape(DEPTH, 1, n_mod))


def _pair_specs(tm, width, n0_tiles):
    return [pl.BlockSpec((tm, width), lambda i, *_: (jnp.minimum(i, n0_tiles - 1), 0)),
            pl.BlockSpec((tm, width), lambda i, *_: (jnp.maximum(i - n0_tiles, 0), 0))]


def _proj_kernel(n0_tiles, xp_ref, xs_ref, mod_ref, n1_ref, w_ref, hy_ref, rg_ref, hg_ref, ret_ref, wb_s):
    @pl.when(pl.program_id(0) == 0)
    def _():
        wb_s[...] = w_ref[...].astype(BF16)

    x = jnp.where(pl.program_id(0) < n0_tiles, xp_ref[...], xs_ref[...])
    mod = mod_ref[0, 0]
    sh1 = mod[:, 0:D_MODEL]
    sc1 = mod[:, D_MODEL:2 * D_MODEL]
    h = x * lax.rsqrt(jnp.mean(x * x, axis=-1, keepdims=True) + EPS) * n1_ref[...]
    h = (h * (1.0 + sc1) + sh1).astype(BF16)
    c0 = 0
    for ref, width in ((hy_ref, PROJ_HY), (rg_ref, PROJ_RG), (hg_ref, PROJ_HG), (ret_ref, PROJ_RET)):
        ref[...] = _mm(h, wb_s[:, c0:c0 + width])
        c0 += width


def _mod_row_map(tm, n_prompt_tok, dec_seq):
    n_prompt_tiles = n_prompt_tok // tm

    def row(i):
        return jnp.where(i < n_prompt_tiles, 0, 1 + (i * tm - n_prompt_tok) // dec_seq)

    return row


def _in_projection(layer, xp, xs, mod_l, norm1_l, w_in, dec_seq):
    n_prompt_tok = xp.shape[0]
    t_all = n_prompt_tok + xs.shape[0]
    tm = TM_PROJ
    n0_tiles = n_prompt_tok // tm
    row = _mod_row_map(tm, n_prompt_tok, dec_seq)
    widths = (PROJ_HY, PROJ_RG, PROJ_HG, PROJ_RET)
    return pl.pallas_call(
        functools.partial(_proj_kernel, n0_tiles),
        out_shape=[jax.ShapeDtypeStruct((t_all, w), F32) for w in widths],
        grid=(t_all // tm,),
        in_specs=_pair_specs(tm, D_MODEL, n0_tiles) + [
            pl.BlockSpec((1, 1, 1, N_MOD * D_MODEL), lambda i: (0, row(i), 0, 0)),
            _const_spec((1, D_MODEL)),
            _const_layer_spec(w_in.shape, layer),
        ],
        out_specs=[pl.BlockSpec((tm, w), lambda i: (i, 0)) for w in widths],
        scratch_shapes=[pltpu.VMEM((D_MODEL, PROJ_WIDTH), BF16)],
        compiler_params=_params(("arbitrary",)),
        name="in_projection",
    )(xp, xs, mod_l, norm1_l, w_in)


def _dft_tables(seq):
    k = np.arange(seq, dtype=np.int64)
    m = (k[:, None] * k[None, :]) % (2 * seq)
    ang = np.pi * m.astype(np.float64) / seq
    return np.cos(ang), np.sin(ang)


def _hyena_tables(seq):
    cos, sin = _dft_tables(seq)
    sign = np.where(np.arange(seq) % 2 == 0, 1.0, -1.0)
    fwd = np.concatenate([cos, sign[None, :], sin[1:]], axis=0)
    wk = np.full((seq,), 2.0)
    wk[0] = 1.0
    inv_cos = (cos * wk[None, :]) / (2.0 * seq)
    inv_nyq = sign[:, None] / (2.0 * seq)
    inv_sin = 2.0 * sin[:, 1:] / (2.0 * seq)
    inv = np.concatenate([inv_cos, inv_nyq, inv_sin], axis=1)

    return jnp.asarray(fwd, F32).astype(BF16), jnp.asarray(inv, F32).astype(BF16)


def _filter_embedding(seq):
    t = np.arange(seq, dtype=np.float64)
    t_norm = t / max(seq - 1, 1)
    bands = np.linspace(1e-4, HY_BANDS - 1, HY_BANDS)
    ang = (2.0 * np.pi / seq) * t[:, None] * bands[None, :]
    z = np.concatenate([t_norm[:, None], np.cos(ang), np.sin(ang)], axis=-1)
    z = np.pad(z, ((0, 0), (0, LANES - HY_EMB)))
    deltas = np.abs(np.linspace(math.log(HY_DECAY_TARGET) / HY_DECAY_LONG,
                                math.log(HY_DECAY_TARGET) / HY_DECAY_SHORT, HY_WIDTH))
    window = np.exp(-t_norm[:, None] * deltas[None, :])
    return jnp.asarray(z, F32), jnp.asarray(window, F32)


def _filter_kernel(z_ref, win_ref, cos_ref, sin_ref, w1_ref, b1_ref, w2_ref, b2_ref, w3_ref, o_ref):
    seq = z_ref.shape[0]
    h = jnp.sin(_mm3(z_ref[...], w1_ref[0]) + b1_ref[0])
    h = jnp.sin(_mm3(h, w2_ref[0]) + b2_ref[0])
    h = _mm3(h, w3_ref[0])
    win = win_ref[...]
    row = _iota((seq, 1), 0)
    sums, diffs = [], []
    for o in range(HY_ORDER):
        c0 = o * 2 * HY_WIDTH
        hf = h[:, c0:c0 + HY_WIDTH] * win
        hb = h[:, c0 + HY_WIDTH:c0 + 2 * HY_WIDTH] * win
        ssq = jnp.sum(hf * hf + hb * hb, axis=0, keepdims=True)
        inv = lax.rsqrt(ssq + EPS)
        hf = hf * inv
        hb = jnp.where(row == 0, 0.0, hb * inv)
        sums.append(hf + hb)
        diffs.append(hf - hb)
    hsum = jnp.concatenate(sums, axis=1)
    hdiff = jnp.concatenate(diffs, axis=1)
    h_re = _mm(cos_ref[...], hsum.astype(BF16))
    h_im = _mm(sin_ref[...], hdiff.astype(BF16))
    sign = jnp.where(row % 2 == 0, 1.0, -1.0)
    h_nyq = jnp.sum(sign * hsum, axis=0, keepdims=True)
    o_ref[0, 0] = h_re
    o_ref[0, 1] = h_im
    o_ref[0, 2] = jnp.where(row == 0, h_nyq, h_re)


def _hyena_filters(seq, w1, b1, w2, b2, w3):
    z, window = _filter_embedding(seq)
    cos, sin = _dft_tables(seq)
    n_out = HY_ORDER * 2 * HY_WIDTH
    w1p = jnp.pad(w1, ((0, 0), (0, LANES - HY_EMB), (0, LANES - HY_FFN)))
    b1p = jnp.pad(b1, ((0, 0), (0, LANES - HY_FFN))).reshape(DEPTH, 1, LANES)
    w2p = jnp.pad(w2, ((0, 0), (0, LANES - HY_FFN), (0, LANES - HY_FFN)))
    b2p = jnp.pad(b2, ((0, 0), (0, LANES - HY_FFN))).reshape(DEPTH, 1, LANES)
    w3p = jnp.pad(w3, ((0, 0), (0, LANES - HY_FFN), (0, 0)))
    per_layer = lambda shape: pl.BlockSpec((1,) + shape, lambda l: (l,) + (0,) * len(shape))
    return pl.pallas_call(
        _filter_kernel,
        out_shape=jax.ShapeDtypeStruct((DEPTH, 3, seq, HY_ORDER * HY_WIDTH), F32),
        grid=(DEPTH,),
        in_specs=[
            _const_spec((seq, LANES)), _const_spec((seq, HY_WIDTH)),
            _const_spec((seq, seq)), _const_spec((seq, seq)),
            per_layer((LANES, LANES)), per_layer((1, LANES)),
            per_layer((LANES, LANES)), per_layer((1, LANES)),
            per_layer((LANES, n_out)),
        ],
        out_specs=pl.BlockSpec((1, 3, seq, HY_ORDER * HY_WIDTH), lambda l: (l, 0, 0, 0)),
        compiler_params=_params(("parallel",)),
        name=f"hyena_filters_{seq}",
    )(z, window, jnp.asarray(cos, F32).astype(BF16), jnp.asarray(sin, F32).astype(BF16), w1p, b1p, w2p, b2p, w3p)


def _hyena_kernel(group, u_ref, cw_ref, cb_ref, d_ref, filt_ref, fwd_ref, inv_ref, y_ref):
    seq = u_ref.shape[0] // group
    cw = cw_ref[...]
    row = _iota((seq, 1), 0)
    ucs = []
    for s in range(group):
        u = u_ref[s * seq:(s + 1) * seq, :]
        ucs.append(cb_ref[...] + cw[0:1] * _shift_rows(u, -1, row) + cw[1:2] * u + cw[2:3] * _shift_rows(u, 1, row))
    side = lambda c0: jnp.concatenate([uc[:, c0:c0 + HY_WIDTH] for uc in ucs], axis=1)
    tile = lambda a: jnp.concatenate([a] * group, axis=1)
    gates = (side(HY_WIDTH), side(2 * HY_WIDTH))
    d = d_ref[...]
    z = side(0)
    for o in range(HY_ORDER):
        c0 = o * HY_WIDTH
        spec = _mm(fwd_ref[...], z.astype(BF16))
        s_re, s_im = spec[:seq], spec[seq:]
        a = tile(filt_ref[0, :, c0:c0 + HY_WIDTH])
        b = tile(filt_ref[1, :, c0:c0 + HY_WIDTH])
        c = tile(filt_ref[2, :, c0:c0 + HY_WIDTH])
        prod = jnp.concatenate([s_re * a - s_im * b, s_re * b + s_im * c], axis=0)
        conv = _mm(inv_ref[...], prod.astype(BF16))
        z = gates[o] * (conv + tile(d[o:o + 1]) * z)
    for s in range(group):
        y_ref[s * seq:(s + 1) * seq, :] = z[:, s * HY_WIDTH:(s + 1) * HY_WIDTH]


def _mixer_call(body, name, proj_all, tok0, batch, seq, consts, batch_ins, state_shapes, scratch, group=1):
    proj_width = proj_all.shape[1]
    rows = group * seq
    blk0 = tok0 // rows

    def batch_spec(shape):
        nd = len(shape)
        return pl.BlockSpec((group,) + tuple(shape[1:]), lambda b: (b,) + (0,) * (nd - 1))

    def layer_spec(shape, layer):
        nd = len(shape)
        return pl.BlockSpec((group, None) + tuple(shape[2:]), lambda b: (b, layer) + (0,) * (nd - 2))

    in_specs = [pl.BlockSpec((rows, proj_width), lambda b: (blk0 + b, 0))]
    in_specs += [_const_layer_spec(a[0].shape, a[1]) if isinstance(a, tuple) else _const_spec(a.shape)
                 for a in consts]
    consts = [a[0] if isinstance(a, tuple) else a for a in consts]
    in_specs += [layer_spec(a.shape, layer) for a, layer in batch_ins]
    batch_ins = [a for a, _ in batch_ins]
    out_shape = [jax.ShapeDtypeStruct((batch * seq, MIX_WIDTH), F32)]
    out_specs = [pl.BlockSpec((rows, MIX_WIDTH), lambda b: (b, 0))]
    for shape in state_shapes:
        out_shape.append(jax.ShapeDtypeStruct(shape, F32))
        out_specs.append(batch_spec(shape))
    return pl.pallas_call(
        body, out_shape=out_shape, grid=(batch // group,), in_specs=in_specs, out_specs=out_specs,
        scratch_shapes=scratch,
        compiler_params=_params(("parallel",)),
        name=f"{name}_{seq}",
    )(proj_all, *consts, *batch_ins)


def _hyena(u_all, tok0, batch, seq, conv_w, conv_b, d_bias, filt_l, tables):
    consts = [conv_w, conv_b, d_bias, filt_l, *tables]
    group = HY_GROUP_COLS // HY_WIDTH
    return _mixer_call(functools.partial(_hyena_kernel, group), "hyena", u_all, tok0, batch, seq, consts,
                       [], [], [], group=group)[0]


def _rglru_kernel(has_s0, emit_state, rg_ref, cw_ref, cb_ref, wg_ref, bg_ref, sp_ref, *refs):
    h0_ref = refs[0] if has_s0 else None
    y_ref = refs[1 if has_s0 else 0]
    st_ref = refs[-1] if emit_state else None
    seq = rg_ref.shape[0]
    w = RG_WIDTH
    xr = rg_ref[:, 0:w]
    gate = rg_ref[:, w:2 * w]
    row = _iota((seq, 1), 0)
    cw = cw_ref[...]
    xc = (cb_ref[...] + cw[0:1] * _shift_rows(xr, -2, row) + cw[1:2] * _shift_rows(xr, -1, row)
          + cw[2:3] * xr + cw[3:4] * _shift_rows(xr, 1, row))
    g = _sigmoid(_mm(xc.astype(BF16), wg_ref[...].astype(BF16)) + bg_ref[...])
    sp = sp_ref[...]
    c = RG_CHUNK
    n = seq // c
    pos = row % c
    hs = []
    for d in range(2):
        forward = d == 0
        r = g[:, d * w:(d + 1) * w]
        i = g[:, (2 + d) * w:(3 + d) * w]
        log_a = -RG_C * r * sp[d:d + 1]
        a = jnp.exp(log_a)
        b = jnp.sqrt(jnp.tanh(-log_a) * (1.0 + a * a)) * (i * xc)
        step = 1
        while step < c:
            keep = (pos >= step) if forward else (pos < c - step)
            shift = step if forward else seq - step
            a_s = pltpu.roll(a, shift, axis=0)
            b_s = pltpu.roll(b, shift, axis=0)
            b = jnp.where(keep, a * b_s + b, b)
            a = jnp.where(keep, a * a_s, a)
            step *= 2
        a3 = a.reshape(n, c, w)
        b3 = b.reshape(n, c, w)
        edge = c - 1 if forward else 0
        a_end = a3[:, edge:edge + 1, :]
        b_end = b3[:, edge:edge + 1, :]
        h = h0_ref[0, d:d + 1, :] if has_s0 else jnp.zeros((1, w), F32)
        h_in = [None] * n
        for ci in (range(n) if forward else range(n - 1, -1, -1)):
            h_in[ci] = h
            h = b_end[ci] + a_end[ci] * h
        hs.append((b3 + a3 * jnp.stack(h_in, axis=0)).reshape(seq, w))
    y_ref[...] = (hs[0] + hs[1]) * _gelu_tanh(gate)
    if emit_state:
        st_ref[0, 0:1, :] = hs[0][seq - 1:seq]
        st_ref[0, 1:2, :] = hs[1][0:1]


def _rglru(rg_all, tok0, batch, seq, conv_w, conv_b, w_gates, b_gates, softplus_neg_lam, h0, emit_state):
    has_s0 = h0 is not None
    consts = [conv_w, conv_b, w_gates, b_gates, softplus_neg_lam]
    return _mixer_call(functools.partial(_rglru_kernel, has_s0, emit_state), "rglru", rg_all, tok0, batch, seq,
                       consts, [h0] if has_s0 else [],
                       [(batch, 2, RG_WIDTH)] if emit_state else [], [])


def _hgrn_kernel(has_s0, emit_state, hg_ref, lb_ref, gain_ref, *refs):
    s0_ref = refs[0] if has_s0 else None
    y_ref = refs[1 if has_s0 else 0]
    st_ref = refs[-1] if emit_state else None
    seq = hg_ref.shape[0]
    w = HG_WIDTH
    c = HG_CHUNK
    n = seq // c
    mid = c // 2
    chunks = lambda a: a.reshape(n, c, w)
    pos = _iota((seq, 1), 0) % c
    stack_mask = _head_mask(c)[None]
    bd_mask = _block_diag_mask()
    pair_shape = (c, N_HEADS * c)
    t_idx = _iota(pair_shape, 0)
    s_idx = _iota(pair_shape, 1) % c

    def stack_heads(a3):
        return jnp.where(stack_mask, jnp.concatenate([a3] * N_HEADS, axis=1), 0.0).astype(BF16)

    q3 = chunks(_silu(hg_ref[:, 0:w]))
    v3 = chunks(hg_ref[:, 3 * w:4 * w])
    v_stack = stack_heads(v3)
    v_t = jnp.swapaxes(v3, 1, 2).astype(BF16)
    lbv = lb_ref[...]
    o_sum = None
    finals = []
    for d in range(2):
        forward = d == 0
        f_pre = hg_ref[:, (1 + d) * w:(2 + d) * w]
        lo = lbv[d:d + 1]
        a1 = jnp.log(lo)
        a2 = jnp.log(1.0 - lo) + _log_sigmoid(f_pre)
        g = jnp.maximum(a1, a2) + jnp.log(1.0 + jnp.exp(-jnp.abs(a1 - a2)))
        k3 = chunks((1.0 - lo) * _sigmoid(-f_pre))
        step = 1
        while step < c:
            if forward:
                g = jnp.where(pos >= step, g + pltpu.roll(g, step, axis=0), g)
            else:
                g = jnp.where(pos < c - step, g + pltpu.roll(g, seq - step, axis=0), g)
            step *= 2
        g3 = chunks(g)
        g_ref = g3[:, mid:mid + 1, :]
        g_tot = g3[:, c - 1:c, :] if forward else g3[:, 0:1, :]
        q_in = (q3 * jnp.exp(g3)).astype(BF16)
        q_sc = (q3 * jnp.exp(g3 - g_ref)).astype(BF16)
        k_stack = stack_heads(k3 * jnp.exp(g_ref - g3))
        k_out = (k3 * jnp.exp(g_tot - g3)).astype(BF16)
        pair = jnp.einsum("ntl,nrl->ntr", q_sc, k_stack, preferred_element_type=F32)
        causal = (t_idx >= s_idx) if forward else (t_idx <= s_idx)
        pair = jnp.where(causal[None], pair, 0.0).astype(BF16)
        o_intra = jnp.einsum("ntr,nrv->ntv", pair, v_stack, preferred_element_type=F32)
        upd = jnp.einsum("nvs,nsk->nvk", v_t, k_out, preferred_element_type=F32)
        decay = jnp.exp(g_tot)
        st = _load_state(s0_ref, d) if has_s0 else jnp.zeros((w, w), F32)
        o_inter = [None] * n
        for ci in (range(n) if forward else range(n - 1, -1, -1)):
            o_inter[ci] = _mm(q_in[ci], st.astype(BF16), _NT)
            st = st * decay[ci] + jnp.where(bd_mask, upd[ci], 0.0)
        o_dir = o_intra + jnp.stack(o_inter, axis=0)
        o_sum = o_dir if o_sum is None else o_sum + o_dir
        finals.append(st)

    ones_bd = jnp.where(bd_mask, 1.0 / HEAD_DIM, 0.0).astype(BF16)
    o = _head_rmsnorm(o_sum.reshape(seq, w), ones_bd) * gain_ref[...]
    y_ref[...] = o * _silu(hg_ref[:, 4 * w:5 * w])
    if emit_state:
        _store_state(st_ref, 0, finals[0])
        _store_state(st_ref, 1, finals[1])


def _state_shape(batch):
    return (batch, 2, N_HEADS, HEAD_DIM, HEAD_DIM)


def _hgrn(hg_all, tok0, batch, seq, lb_l, gain_l, s0, emit_state):
    has_s0 = s0 is not None
    return _mixer_call(functools.partial(_hgrn_kernel, has_s0, emit_state), "hgrn2", hg_all, tok0, batch, seq,
                       [lb_l, gain_l], [s0] if has_s0 else [],
                       [_state_shape(batch)] if emit_state else [], [])


def _ret_kernel(use_rope, has_s0, emit_state, ret_ref, cos_ref, sin_ref, dl_ref, dlp_ref, *refs):
    s0_ref = refs[0] if has_s0 else None
    y_ref = refs[1 if has_s0 else 0]
    st_ref = refs[-1] if emit_state else None
    seq = ret_ref.shape[0]
    w = RET_WIDTH
    c = min(RET_CHUNK, seq)
    n = seq // c
    chunks = lambda a: a.reshape(n, c, w)
    q = ret_ref[:, 0:w]
    k = ret_ref[:, w:2 * w] * (HEAD_DIM ** -0.5)
    if use_rope:
        lane = _iota((seq, w), 1)
        even = (lane % 2) == 0
        cos = cos_ref[...]
        sin = sin_ref[...]

        def rope(x):
            nxt = pltpu.roll(x, w - 1, axis=1)
            prv = pltpu.roll(x, 1, axis=1)
            return x * cos + jnp.where(even, nxt, prv) * sin

        q = rope(q)
        k = rope(k)
    q3, k3, v3 = chunks(q), chunks(k), chunks(ret_ref[:, 2 * w:3 * w])

    lg = _log_sigmoid(dl_ref[...])
    lgp = _log_sigmoid(dlp_ref[...])
    pair_shape = (c, N_HEADS * c)
    t_idx = _iota(pair_shape, 0)
    s_idx = _iota(pair_shape, 1) % c
    dist = (t_idx - s_idx).astype(F32)
    decay = jnp.exp(jnp.where(dist >= 0, dist * lgp[0:1], -dist * lgp[1:2]))
    decay = decay + jnp.where(dist == 0, 1.0, 0.0)
    stack_mask = _head_mask(c)[None]
    bd_mask = _block_diag_mask()
    pos = _iota((c, 1), 0).astype(F32)
    fc = float(c)

    def stack_heads(a3):
        return jnp.where(stack_mask, jnp.concatenate([a3] * N_HEADS, axis=1), 0.0).astype(BF16)

    pair = jnp.einsum("ntl,nrl->ntr", q3.astype(BF16), stack_heads(k3), preferred_element_type=F32)
    pair = (pair * decay[None]).astype(BF16)
    o_sum = jnp.einsum("ntr,nrv->ntv", pair, stack_heads(v3), preferred_element_type=F32)
    v_t = jnp.swapaxes(v3, 1, 2).astype(BF16)
    finals = []
    for d in range(2):
        forward = d == 0
        lgd = lg[d:d + 1]
        q_in = (q3 * jnp.exp(((pos + 1.0) if forward else (fc - pos)) * lgd)).astype(BF16)
        k_out = (k3 * jnp.exp(((fc - 1.0 - pos) if forward else pos) * lgd)).astype(BF16)
        upd = jnp.einsum("nvs,nsk->nvk", v_t, k_out, preferred_element_type=F32)
        chunk_decay = jnp.exp(fc * lgd)
        st = _load_state(s0_ref, d) if has_s0 else jnp.zeros((w, w), F32)
        o_inter = [None] * n
        for ci in (range(n) if forward else range(n - 1, -1, -1)):
            o_inter[ci] = _mm(q_in[ci], st.astype(BF16), _NT)
            st = st * chunk_decay + jnp.where(bd_mask, upd[ci], 0.0)
        o_sum = o_sum + jnp.stack(o_inter, axis=0)
        finals.append(st)

    ones_bd = jnp.where(bd_mask, 1.0 / HEAD_DIM, 0.0).astype(BF16)
    o = _head_rmsnorm(o_sum.reshape(seq, w), ones_bd)
    y_ref[...] = _silu(ret_ref[:, 3 * w:4 * w]) * o
    if emit_state:
        _store_state(st_ref, 0, finals[0])
        _store_state(st_ref, 1, finals[1])


def _rope_tables(seq):
    rows = seq // GRID_W
    row = np.repeat(np.arange(rows), GRID_W).astype(np.float64)
    col = (np.arange(seq) % GRID_W).astype(np.float64)
    n_freq = HEAD_DIM // 4
    inv_freq = ROPE_BASE ** (-np.arange(n_freq, dtype=np.float64) / n_freq)
    ang = np.concatenate([row[:, None] * inv_freq, col[:, None] * inv_freq], axis=-1)
    ang = np.repeat(ang, 2, axis=1)
    cos = np.tile(np.cos(ang), (1, N_HEADS))
    sin = np.tile(np.sin(ang) * np.where(np.arange(HEAD_DIM) % 2 == 0, -1.0, 1.0)[None, :], (1, N_HEADS))
    return jnp.asarray(cos, F32), jnp.asarray(sin, F32)


def _retention(ret_all, tok0, batch, seq, use_rope, decay_l, s0, emit_state):
    has_s0 = s0 is not None
    w = RET_WIDTH
    c = min(RET_CHUNK, seq)
    cos, sin = _rope_tables(seq)
    dl = jnp.repeat(decay_l, HEAD_DIM, axis=-1)
    dlp = jnp.repeat(decay_l, c, axis=-1)
    return _mixer_call(functools.partial(_ret_kernel, use_rope, has_s0, emit_state), "retention", ret_all, tok0,
                       batch, seq, [cos, sin, dl, dlp], [s0] if has_s0 else [],
                       [_state_shape(batch)] if emit_state else [], [])


def _route(h2s, wrt_ref, rb_ref):
    tm = h2s[0].shape[0]
    lane = _iota((N_EXPERTS, tm), 0).astype(F32)
    group = (_iota((N_EXPERTS, tm), 0) // E_PER_GROUP).astype(F32)
    neg = -jnp.inf
    each = lambda f, *cols: [f(*args) for args in zip(*cols)]

    def first_argmax(vals):
        mx = jnp.max(vals, axis=0, keepdims=True)
        idx = jnp.min(jnp.where(vals == mx, lane, float(N_EXPERTS)), axis=0, keepdims=True)
        return mx, idx

    def softmax(logits):
        e = jnp.exp(logits - jnp.max(logits, axis=0, keepdims=True))
        return e / jnp.sum(e, axis=0, keepdims=True)

    logits = each(lambda h2: _mm3(wrt_ref[...], h2, _NT), h2s)
    probs = each(softmax, logits)
    sel = each(lambda p: p + rb_ref[...], probs)

    def group_score(s, g):
        vals = jnp.where(group == float(g), s, neg)
        m1, i1 = first_argmax(vals)
        m2, _ = first_argmax(jnp.where(lane == i1, neg, vals))
        return m1 + m2

    best_score = each(lambda s: group_score(s, 0), sel)
    best = each(jnp.zeros_like, best_score)
    for g in range(1, N_GROUPS):
        score = each(lambda s: group_score(s, g), sel)
        take = each(lambda sc, bs: sc > bs, score, best_score)
        best = each(lambda t, b: jnp.where(t, float(g), b), take, best)
        best_score = each(lambda t, sc, bs: jnp.where(t, sc, bs), take, score, best_score)

    def gates_of(s, p, b):
        vals = jnp.where(group == b, s, neg)
        _, i1 = first_argmax(vals)
        _, i2 = first_argmax(jnp.where(lane == i1, neg, vals))
        picked = jnp.where((lane == i1) | (lane == i2), p, 0.0)
        return picked / jnp.sum(picked, axis=0, keepdims=True)

    return each(gates_of, sel, probs, best), best


INFO_ROWS = 8
INFO_GROUP = 4
INFO_RANK = 5
H2_WIDTH = D_MODEL + LANES


def _route_kernel(n0_tiles, xp_ref, xs_ref, *refs):
    y_refs = refs[:8]
    mod_ref, wo_ref, n2_ref, wrt_ref, rb_ref, x1_ref, h2_ref, info_ref, cend_ref, carry_s, wob_s = refs[8:]
    i = pl.program_id(0)
    first = i < n0_tiles
    d = D_MODEL
    tb = TB_MOE
    mod = mod_ref[0, 0]

    @pl.when(i == 0)
    def _():
        carry_s[...] = jnp.zeros_like(carry_s)
        wob_s[...] = wo_ref[...].astype(BF16)

    n_sub = TM_ROUTE // tb
    each = lambda f, *cols: [f(*args) for args in zip(*cols)]
    rows = [slice(h * tb, (h + 1) * tb) for h in range(n_sub)]
    expert = _iota((N_EXPERTS, tb), 0).astype(F32)
    upper = (_iota((tb, tb), 0) <= _iota((tb, tb), 1)).astype(BF16)
    eye = (_iota((tb, tb), 0) == _iota((tb, tb), 1)).astype(BF16)

    def mixed_of(r):
        ys = [jnp.where(first, y_refs[2 * j][r, :], y_refs[2 * j + 1][r, :]).astype(BF16) for j in range(4)]
        return _mm(jnp.concatenate(ys, axis=1), wob_s[...])

    mixed = each(mixed_of, rows)
    x1 = each(lambda r, m: jnp.where(first, xp_ref[r, :], xs_ref[r, :]) + mod[:, 2 * d:3 * d] * m, rows, mixed)

    def h2_of(v):
        h2 = v * lax.rsqrt(jnp.mean(v * v, axis=-1, keepdims=True) + EPS) * n2_ref[...]
        return h2 * (1.0 + mod[:, 4 * d:5 * d]) + mod[:, 3 * d:4 * d]

    h2 = each(h2_of, x1)
    gates, best = _route(h2, wrt_ref, rb_ref)
    g4 = each(lambda g, b: [jnp.sum(jnp.where(expert == E_PER_GROUP * b + j, g, 0.0), axis=0, keepdims=True)
                            for j in range(E_PER_GROUP)], gates, best)
    onehot = each(lambda b: jnp.where(expert == b, 1.0, 0.0), best)
    counts = each(lambda o: _mm(o.astype(BF16), upper), onehot)

    def pieces_of(g):
        p = _split3(jnp.concatenate(g + [jnp.zeros((N_EXPERTS - E_PER_GROUP, tb), F32)], axis=0))
        return jnp.concatenate(list(p) + [jnp.zeros((LANES - 3 * N_EXPERTS, tb), BF16)], axis=0)

    ext = each(lambda g: _mm(eye, pieces_of(g), _NT).astype(BF16), g4)

    carry = carry_s[...]
    for h in range(n_sub):
        incl = counts[h] + carry[:, 0:1]
        rank = jnp.sum(onehot[h] * incl, axis=0, keepdims=True) - 1.0
        carry = jnp.broadcast_to(incl[:, tb - 1:tb], (N_EXPERTS, LANES))
        x1_ref[rows[h], :] = x1[h]
        h2_ref[rows[h], 0:d] = h2[h].astype(BF16)
        h2_ref[rows[h], d:d + LANES] = ext[h]
        info_ref[:, rows[h]] = jnp.concatenate(g4[h] + [best[h], rank, jnp.zeros((INFO_ROWS - 6, tb), F32)], axis=0)
        cend_ref[h] = carry
    carry_s[...] = carry


def _route_call(layer, xp, xs, ys, mod_l, w_out, norm2_l, w_router, router_bias, dec_seq):
    n_prompt_tok = xp.shape[0]
    t_all = n_prompt_tok + xs.shape[0]
    tm = TM_ROUTE
    sub = tm // TB_MOE
    n0_tiles = n_prompt_tok // tm
    row = _mod_row_map(tm, n_prompt_tok, dec_seq)
    tok = lambda width: pl.BlockSpec((tm, width), lambda i: (i, 0))
    y_specs, y_args = [], []
    for yp, ysm in ys:
        y_specs += _pair_specs(tm, MIX_WIDTH, n0_tiles)
        y_args += [yp, ysm]
    return pl.pallas_call(
        functools.partial(_route_kernel, n0_tiles),
        out_shape=[jax.ShapeDtypeStruct((t_all, D_MODEL), F32), jax.ShapeDtypeStruct((t_all, H2_WIDTH), BF16),
                   jax.ShapeDtypeStruct((INFO_ROWS, t_all), F32),
                   jax.ShapeDtypeStruct((t_all // TB_MOE, N_EXPERTS, LANES), F32)],
        grid=(t_all // tm,),
        in_specs=_pair_specs(tm, D_MODEL, n0_tiles) + y_specs + [
            pl.BlockSpec((1, 1, 1, N_MOD * D_MODEL), lambda i: (0, row(i), 0, 0)),
            _const_layer_spec(w_out.shape, layer), _const_spec((1, D_MODEL)),
            _const_spec((N_EXPERTS, D_MODEL)), _const_spec((N_EXPERTS, 1)),
        ],
        out_specs=[tok(D_MODEL), tok(H2_WIDTH), pl.BlockSpec((INFO_ROWS, tm), lambda i: (0, i)),
                   pl.BlockSpec((sub, N_EXPERTS, LANES), lambda i: (i, 0, 0))],
        scratch_shapes=[pltpu.VMEM((N_EXPERTS, LANES), F32), pltpu.VMEM((D_MODEL, D_MODEL), BF16)],
        compiler_params=_params(("arbitrary",)),
        name="route",
    )(xp, xs, *y_args, mod_l, w_out, norm2_l, w_router.T, router_bias.reshape(N_EXPERTS, 1))


def _moe_plan(cend, info):
    ts = TS_MOE
    tb = TR_MOE
    n_blocks = cend.shape[0]
    t_all = n_blocks * TB_MOE
    n_tiles = t_all // ts + N_GROUPS
    sub_per_tile = ts // tb
    n_sub = n_tiles * sub_per_tile
    cend = cend[:, :N_GROUPS, 0].astype(jnp.int32)
    cum = jnp.concatenate([jnp.zeros((1, N_GROUPS), jnp.int32), cend[:-1]], axis=0)
    cnt = cend - cum
    grp = info[INFO_GROUP].astype(jnp.int32)
    rank = info[INFO_RANK].astype(jnp.int32)
    tot = cend[-1]
    padded = ((tot + ts - 1) // ts) * ts
    off = jnp.cumsum(padded) - padded
    pos = rank + sum(jnp.where(grp == g, off[g], 0) for g in range(N_GROUPS))

    sub_start = jnp.arange(n_sub, dtype=jnp.int32) * tb
    in_g = (sub_start[:, None] >= off[None, :]) & (sub_start[:, None] < (off + padded)[None, :])
    g_of = jnp.argmax(in_g, axis=1).astype(jnp.int32)
    r0 = sub_start - off[g_of]
    r1 = jnp.minimum(r0 + tb, tot[g_of])
    live = jnp.any(in_g, axis=1) & (r1 > r0)
    cend_g = cend[:, g_of]
    blo = jnp.where(live, jnp.sum(cend_g <= r0[None, :], axis=0), 0).astype(jnp.int32)
    bhi = jnp.where(live, jnp.sum(cend_g <= (r1 - 1)[None, :], axis=0), -1).astype(jnp.int32)

    used = jnp.any(in_g, axis=1)[::sub_per_tile]
    tile_group = g_of[::sub_per_tile]
    eidx = E_PER_GROUP * tile_group[:, None] + jnp.arange(E_PER_GROUP, dtype=jnp.int32)[None, :]
    n_used = jnp.sum(used.astype(jnp.int32))
    last = eidx[jnp.maximum(n_used - 1, 0), E_PER_GROUP - 1]
    eidx = jnp.where(used[:, None], eidx, last).reshape(-1).astype(jnp.int32)

    start = off[None, :] + cum
    end = off[None, :] + cend
    has = cnt > 0
    slo = jnp.where(has, start // tb, 0).reshape(-1).astype(jnp.int32)
    shi = jnp.where(has, (end - 1) // tb, -1).reshape(-1).astype(jnp.int32)
    tile_start = sub_start[::sub_per_tile]
    rows = jnp.where(used, jnp.clip(tot[tile_group] - (tile_start - off[tile_group]), 0, ts), 0)
    return dict(pos=pos, eidx=eidx, rows=rows.astype(jnp.int32), blo=blo, bhi=bhi, slo=slo, shi=shi,
                n_tiles=n_tiles)


def _expert_kernel(eidx_ref, rows_ref, blo_ref, bhi_ref, h2_ref, pos_ref, wg_ref, wu_ref, wd_ref,
                   ys_ref, xs_s, acc_s, ax_s):
    i = pl.program_id(0)
    j = pl.program_id(1)
    n_rows = rows_ref[i]
    used = n_rows > 0
    tb, tr = TB_MOE, TR_MOE
    sub_per_tile = TS_MOE // tr

    @pl.when(used & (j == 0))
    def _():
        r_idx = _iota((tr, tb), 0)
        for part in range(sub_per_tile):
            s = i * sub_per_tile + part
            base = s * tr
            ax_s[...] = jnp.zeros_like(ax_s)

            def gather(b, carry):
                sel = jnp.where(pos_ref[b] - base == r_idx, 1.0, 0.0).astype(BF16)
                tok = pl.ds(pl.multiple_of(b * tb, tb), tb)
                ax_s[...] += _mm(sel, h2_ref[tok, :])
                return carry

            lax.fori_loop(blo_ref[s], bhi_ref[s] + 1, gather, 0)
            xs_s[part * tr:(part + 1) * tr, :] = ax_s[...].astype(BF16)
        acc_s[...] = jnp.zeros_like(acc_s)

    half_rows = TS_MOE // 2
    for part in range(2):
        rows = slice(part * half_rows, (part + 1) * half_rows)

        @pl.when(n_rows > part * half_rows)
        def _():
            x = xs_s[rows, 0:D_MODEL]
            pieces = xs_s[rows, D_MODEL:H2_WIDTH].astype(F32)
            ge = jnp.sum(jnp.where(_iota(pieces.shape, 1) % N_EXPERTS == j, pieces, 0.0), axis=-1, keepdims=True)
            hh = _silu(_mm(x, wg_ref[0, 0].astype(BF16))) * _mm(x, wu_ref[0, 0].astype(BF16)) * ge
            acc_s[rows, :] += _mm(hh.astype(BF16), wd_ref[0, 0].astype(BF16))

    @pl.when(j == E_PER_GROUP - 1)
    def _():
        ys_ref[...] = jnp.where(used, acc_s[...], 0.0).astype(BF16)


def _expert_call(layer, plan, h2, w_gate, w_up, w_down):
    t_all = h2.shape[0]
    tb, ts = TB_MOE, TS_MOE
    n_tiles = plan["n_tiles"]
    pos_rows = plan["pos"].reshape(t_all // tb, 1, tb)
    weight = lambda shape: pl.BlockSpec((1, 1) + shape, lambda i, j, eidx, *_: (layer, eidx[i * E_PER_GROUP + j], 0, 0))
    grid_spec = pltpu.PrefetchScalarGridSpec(
        num_scalar_prefetch=4,
        grid=(n_tiles, E_PER_GROUP),
        in_specs=[
            _const_spec((t_all, H2_WIDTH)), _const_spec(pos_rows.shape),
            weight((D_MODEL, D_EXPERT)), weight((D_MODEL, D_EXPERT)), weight((D_EXPERT, D_MODEL)),
        ],
        out_specs=pl.BlockSpec((ts, D_MODEL), lambda i, j, *_: (i, 0)),
        scratch_shapes=[pltpu.VMEM((ts, H2_WIDTH), BF16), pltpu.VMEM((ts, D_MODEL), F32),
                        pltpu.VMEM((TR_MOE, H2_WIDTH), F32)],
    )
    return pl.pallas_call(
        _expert_kernel,
        out_shape=jax.ShapeDtypeStruct((n_tiles * ts, D_MODEL), BF16),
        grid_spec=grid_spec,
        compiler_params=_params(("arbitrary", "arbitrary")),
        name=f"experts_{layer}",
    )(plan["eidx"], plan["rows"], plan["blo"], plan["bhi"], h2, pos_rows, w_gate, w_up, w_down)


def _combine_kernel(final, n0_tiles, slo_ref, shi_ref, ys_ref, pos_ref, x1_ref, mod_ref, nf_ref,
                    op_ref, os_ref, acc_s):
    b = pl.program_id(0)
    d = D_MODEL
    tb, tr = TB_MOE, TR_MOE
    acc_s[...] = jnp.zeros_like(acc_s)
    c_idx = _iota((tb, tr), 1)
    pos = pos_ref[...]
    for g in range(N_GROUPS):
        def scatter(s, carry):
            sel = jnp.where(pos - s * tr == c_idx, 1.0, 0.0).astype(BF16)
            acc_s[...] += _mm(sel, ys_ref[pl.ds(pl.multiple_of(s * tr, tr), tr), :])
            return carry

        lax.fori_loop(slo_ref[b * N_GROUPS + g], shi_ref[b * N_GROUPS + g] + 1, scatter, 0)

    def result():
        x2 = x1_ref[...] + mod_ref[0, 0][:, 5 * d:6 * d] * acc_s[...]
        if final:
            x2 = x2 * lax.rsqrt(jnp.mean(x2 * x2, axis=-1, keepdims=True) + EPS) * nf_ref[...]
        return x2

    @pl.when(b < n0_tiles)
    def _():
        op_ref[...] = result()

    @pl.when(b >= n0_tiles)
    def _():
        os_ref[...] = result()


def _combine_call(final, plan, ys_sorted, x1, mod_l, norm_final, n_prompt_tok, dec_seq):
    t_all = x1.shape[0]
    tb = TB_MOE
    n0_tiles = n_prompt_tok // tb
    row = _mod_row_map(tb, n_prompt_tok, dec_seq)
    grid_spec = pltpu.PrefetchScalarGridSpec(
        num_scalar_prefetch=2,
        grid=(t_all // tb,),
        in_specs=[
            _const_spec(ys_sorted.shape),
            pl.BlockSpec((tb, 1), lambda b, *_: (b, 0)),
            pl.BlockSpec((tb, D_MODEL), lambda b, *_: (b, 0)),
            pl.BlockSpec((1, 1, 1, N_MOD * D_MODEL), lambda b, *_: (0, row(b), 0, 0)),
            _const_spec((1, D_MODEL)),
        ],
        out_specs=_pair_specs(tb, D_MODEL, n0_tiles),
        scratch_shapes=[pltpu.VMEM((tb, D_MODEL), F32)],
    )
    return pl.pallas_call(
        functools.partial(_combine_kernel, final, n0_tiles),
        out_shape=[jax.ShapeDtypeStruct((n_prompt_tok, D_MODEL), F32),
                   jax.ShapeDtypeStruct((t_all - n_prompt_tok, D_MODEL), F32)],
        grid_spec=grid_spec,
        compiler_params=_params(("arbitrary",)),
        name="combine",
    )(plan["slo"], plan["shi"], ys_sorted, plan["pos"].reshape(t_all, 1), x1, mod_l, norm_final)


def _rg_gate_weights(wa, ba, wx, bx):
    eye = jnp.eye(RG_HEADS, dtype=wa.dtype)

    def dense(wd):
        return jnp.einsum("hij,hg->higj", wd, eye).reshape(RG_WIDTH, RG_WIDTH)

    w = jnp.concatenate([dense(wa[0]), dense(wa[1]), dense(wx[0]), dense(wx[1])], axis=1)
    b = jnp.concatenate([ba[0], ba[1], bx[0], bx[1]], axis=0).reshape(1, 4 * RG_WIDTH)
    return w, b


def kernel(x_prompt, x_sample, state_rglru, state_hgrn, state_ret, c, c_ctx, norm1, norm2, norm_final, w_ada, b_ada, w_in, w_out, hy_conv_w, hy_conv_b, hy_w1, hy_b1, hy_w2, hy_b2, hy_w3, hy_d, rg_conv_w, rg_conv_b, rg_wa, rg_ba, rg_wx, rg_bx, rg_lambda, hg_lb, hg_norm, ret_decay, w_router, router_bias, w_gate, w_up, w_down):
    batch, seq, d = x_prompt.shape
    dec_batch, dec_seq, _ = x_sample.shape
    assert d == D_MODEL and dec_batch + 1 <= COND_ROWS
    n_prompt_tok = batch * seq

    lb_cum = jnp.cumsum(jax.nn.softmax(hg_lb.astype(F32), axis=0), axis=0)
    lb_all = lb_cum - lb_cum[0:1]

    cond = jnp.zeros((COND_ROWS, d), F32).at[0].set(c_ctx).at[1:1 + dec_batch].set(c)
    mod = _modulation(cond, w_ada, b_ada).reshape(DEPTH, COND_ROWS, 1, N_MOD * d)


    passes = (
        dict(tok0=0, batch=batch, seq=seq, rope=False),
        dict(tok0=n_prompt_tok, batch=dec_batch, seq=dec_seq, rope=True),
    )
    filters = {p["seq"]: _hyena_filters(p["seq"], hy_w1, hy_b1, hy_w2, hy_b2, hy_w3) for p in passes}
    tables = {p["seq"]: _hyena_tables(p["seq"]) for p in passes}

    xp = x_prompt.reshape(-1, d)
    xs = x_sample.reshape(-1, d)
    new_rg, new_hg, new_ret = [], [], []
    for l in range(DEPTH):
        hy_all, rg_all, hg_all, ret_all = _in_projection(
            l, xp, xs, mod[l:l + 1], norm1[l].reshape(1, d), w_in, dec_seq)
        wg, bg = _rg_gate_weights(rg_wa[l], rg_ba[l], rg_wx[l], rg_bx[l])
        sp = jax.nn.softplus(-rg_lambda[l])
        ys = [[], [], [], []]
        for pi, p in enumerate(passes):
            first = pi == 0
            geom = (p["tok0"], p["batch"], p["seq"])
            y_hy = _hyena(hy_all, *geom, hy_conv_w[l], hy_conv_b[l].reshape(1, -1), hy_d[l],
                          (filters[p["seq"]], l), tables[p["seq"]])
            y_rg, *st_rg = _rglru(rg_all, *geom, rg_conv_w[l], rg_conv_b[l].reshape(1, -1), wg, bg, sp,
                                  None if first else (state_rglru, l), first)
            y_hg, *st_hg = _hgrn(hg_all, *geom, lb_all[l], hg_norm[l].reshape(1, -1),
                                 None if first else (state_hgrn, l), first)
            y_ret, *st_ret = _retention(ret_all, *geom, p["rope"], ret_decay[l],
                                        None if first else (state_ret, l), first)
            for lst, y in zip(ys, (y_hy, y_rg, y_hg, y_ret)):
                lst.append(y)
            if first:
                new_rg.append(st_rg[0])
                new_hg.append(st_hg[0])
                new_ret.append(st_ret[0])
        x1, h2, info, cend = _route_call(l, xp, xs, ys, mod[l:l + 1], w_out, norm2[l].reshape(1, d),
                                         w_router, router_bias, dec_seq)
        plan = _moe_plan(cend, info)
        ys_sorted = _expert_call(l, plan, h2, w_gate, w_up, w_down)
        xp, xs = _combine_call(l == DEPTH - 1, plan, ys_sorted, x1, mod[l:l + 1], norm_final.reshape(1, d),
                               n_prompt_tok, dec_seq)

    return (xp.reshape(batch, seq, d), xs.reshape(dec_batch, dec_seq, d), jnp.stack(new_rg, axis=1),
            jnp.stack(new_hg, axis=1), jnp.stack(new_ret, axis=1))
```

```python
import functools
import math

import numpy as np
import jax
import jax.numpy as jnp
from jax import lax
from jax.experimental import pallas as pl
from jax.experimental.pallas import tpu as pltpu

F32 = jnp.float32
BF16 = jnp.bfloat16

D_MODEL = 1024
DEPTH = 2
GRID_W = 64
HY_WIDTH = 256
RG_WIDTH = 256
HG_WIDTH = 256
RET_WIDTH = 256
MIX_WIDTH = 256
HY_ORDER = 2
HY_EMB = 33
HY_BANDS = 16
HY_FFN = 64
HY_DECAY_TARGET = 1e-2
HY_DECAY_SHORT = 0.3
HY_DECAY_LONG = 1.5
HY_GROUP_COLS = 512
RG_HEADS = 8
RG_HEAD_DIM = 32
RG_C = 8.0
RG_CHUNK = 16
N_HEADS = 4
HEAD_DIM = 64
HG_CHUNK = 64
RET_CHUNK = 128
ROPE_BASE = 10000.0
N_EXPERTS = 16
N_GROUPS = 4
E_PER_GROUP = 4
D_EXPERT = 512
N_MOD = 6
EPS = 1e-6
PROJ_HY = 3 * HY_WIDTH
PROJ_RG = 2 * RG_WIDTH
PROJ_HG = 5 * HG_WIDTH
PROJ_RET = 4 * RET_WIDTH
PROJ_WIDTH = PROJ_HY + PROJ_RG + PROJ_HG + PROJ_RET
COND_ROWS = 16
LANES = 128
VMEM_LIMIT = 56 * 1024 * 1024
TM_PROJ = 512
TB_MOE = 512
TR_MOE = 256
TM_ROUTE = 2 * TB_MOE
TS_MOE = 1024
TN_MOD = 1536

_NN = (((1,), (0,)), ((), ()))
_NT = (((1,), (1,)), ((), ()))


def _mm(a, b, dn=_NN):
    return lax.dot_general(a, b, dn, preferred_element_type=F32)


def _split2(x):
    hi = x.astype(BF16)
    lo = (x - hi.astype(F32)).astype(BF16)
    return hi, lo


def _split3(x):
    hi = x.astype(BF16)
    r = x - hi.astype(F32)
    mid = r.astype(BF16)
    lo = (r - mid.astype(F32)).astype(BF16)
    return hi, mid, lo


def _mm3(a, b, dn=_NN):
    ah, al = _split2(a)
    bh, bl = _split2(b)
    return _mm(ah, bh, dn) + (_mm(ah, bl, dn) + _mm(al, bh, dn))


def _mm_exact_rhs(a, b_bf16):
    a1, a2, a3 = _split3(a)
    return _mm(a1, b_bf16) + (_mm(a2, b_bf16) + _mm(a3, b_bf16))


def _sigmoid(x):
    return 1.0 / (1.0 + jnp.exp(-x))


def _silu(x):
    return x * _sigmoid(x)


def _log_sigmoid(x):
    return jnp.minimum(x, 0.0) - jnp.log(1.0 + jnp.exp(-jnp.abs(x)))


def _gelu_tanh(x):
    return 0.5 * x * (1.0 + jnp.tanh(math.sqrt(2.0 / math.pi) * (x + 0.044715 * (x * x * x))))


def _iota(shape, dim):
    return lax.broadcasted_iota(jnp.int32, shape, dim)


def _shift_rows(u, k, row):
    n = u.shape[0]
    if k == 0:
        return u
    r = pltpu.roll(u, (-k) % n, axis=0)
    if k < 0:
        return jnp.where(row >= -k, r, 0.0)
    return jnp.where(row < n - k, r, 0.0)


def _head_mask(n_rows_per_head):
    shape = (N_HEADS * n_rows_per_head, MIX_WIDTH)
    return (_iota(shape, 0) // n_rows_per_head) == (_iota(shape, 1) // HEAD_DIM)


def _block_diag_mask():
    shape = (MIX_WIDTH, MIX_WIDTH)
    return (_iota(shape, 0) // HEAD_DIM) == (_iota(shape, 1) // HEAD_DIM)


def _head_rmsnorm(o, ones_bd):
    ms = _mm_exact_rhs(o * o, ones_bd)
    return o * lax.rsqrt(ms + EPS)


def _load_state(s0_ref, d):
    zero = jnp.zeros((HEAD_DIM, HEAD_DIM), F32)
    rows = []
    for h in range(N_HEADS):
        blk = s0_ref[0, d, h].T
        rows.append(jnp.concatenate([blk if g == h else zero for g in range(N_HEADS)], axis=1))
    return jnp.concatenate(rows, axis=0)


def _store_state(st_ref, d, st):
    for h in range(N_HEADS):
        lo, hi = h * HEAD_DIM, (h + 1) * HEAD_DIM
        st_ref[0, d, h] = st[lo:hi, lo:hi].T


def _params(sem, vmem=VMEM_LIMIT):
    return pltpu.CompilerParams(dimension_semantics=sem, vmem_limit_bytes=vmem)


def _const_spec(shape):
    nd = len(shape)
    return pl.BlockSpec(shape, lambda *_: (0,) * nd, pipeline_mode=pl.Buffered(1))


def _const_layer_spec(shape, layer):
    nd = len(shape)
    return pl.BlockSpec((None,) + tuple(shape[1:]), lambda *_: (layer,) + (0,) * (nd - 1),
                        pipeline_mode=pl.Buffered(1))


def _mod_kernel(cond_ref, w_ref, b_ref, o_ref):
    o_ref[0] = _mm3(_silu(cond_ref[...]), w_ref[0]) + b_ref[0]


def _modulation(cond, w_ada, b_ada):
    tn = TN_MOD
    n_mod = N_MOD * D_MODEL
    return pl.pallas_call(
        _mod_kernel,
        out_shape=jax.ShapeDtypeStruct((DEPTH, COND_ROWS, n_mod), F32),
        grid=(DEPTH, n_mod // tn),
        in_specs=[
            pl.BlockSpec((COND_ROWS, D_MODEL), lambda l, j: (0, 0)),
            pl.BlockSpec((1, D_MODEL, tn), lambda l, j: (l, 0, j)),
            pl.BlockSpec((1, 1, tn), lambda l, j: (l, 0, j)),
        ],
        out_specs=pl.BlockSpec((1, COND_ROWS, tn), lambda l, j: (l, 0, j)),
        compiler_params=_params(("parallel", "parallel")),
        name="modulation",
    )(cond, w_ada, b_ada.reshape(DEPTH, 1, n_mod))


def _pair_specs(tm, width, n0_tiles):
    return [pl.BlockSpec((tm, width), lambda i, *_: (jnp.minimum(i, n0_tiles - 1), 0)),
            pl.BlockSpec((tm, width), lambda i, *_: (jnp.maximum(i - n0_tiles, 0), 0))]


def _proj_kernel(n0_tiles, xp_ref, xs_ref, mod_ref, n1_ref, w_ref, hy_ref, rg_ref, hg_ref, ret_ref, wb_s):
    @pl.when(pl.program_id(0) == 0)
    def _():
        wb_s[...] = w_ref[...].astype(BF16)

    x = jnp.where(pl.program_id(0) < n0_tiles, xp_ref[...], xs_ref[...])
    mod = mod_ref[0, 0]
    sh1 = mod[:, 0:D_MODEL]
    sc1 = mod[:, D_MODEL:2 * D_MODEL]
    h = x * lax.rsqrt(jnp.mean(x * x, axis=-1, keepdims=True) + EPS) * n1_ref[...]
    h = (h * (1.0 + sc1) + sh1).astype(BF16)
    c0 = 0
    for ref, width in ((hy_ref, PROJ_HY), (rg_ref, PROJ_RG), (hg_ref, PROJ_HG), (ret_ref, PROJ_RET)):
        ref[...] = _mm(h, wb_s[:, c0:c0 + width])
        c0 += width


def _mod_row_map(tm, n_prompt_tok, dec_seq):
    n_prompt_tiles = n_prompt_tok // tm

    def row(i):
        return jnp.where(i < n_prompt_tiles, 0, 1 + (i * tm - n_prompt_tok) // dec_seq)

    return row


def _in_projection(layer, xp, xs, mod_l, norm1_l, w_in, dec_seq):
    n_prompt_tok = xp.shape[0]
    t_all = n_prompt_tok + xs.shape[0]
    tm = TM_PROJ
    n0_tiles = n_prompt_tok // tm
    row = _mod_row_map(tm, n_prompt_tok, dec_seq)
    widths = (PROJ_HY, PROJ_RG, PROJ_HG, PROJ_RET)
    return pl.pallas_call(
        functools.partial(_proj_kernel, n0_tiles),
        out_shape=[jax.ShapeDtypeStruct((t_all, w), F32) for w in widths],
        grid=(t_all // tm,),
        in_specs=_pair_specs(tm, D_MODEL, n0_tiles) + [
            pl.BlockSpec((1, 1, 1, N_MOD * D_MODEL), lambda i: (0, row(i), 0, 0)),
            _const_spec((1, D_MODEL)),
            _const_layer_spec(w_in.shape, layer),
        ],
        out_specs=[pl.BlockSpec((tm, w), lambda i: (i, 0)) for w in widths],
        scratch_shapes=[pltpu.VMEM((D_MODEL, PROJ_WIDTH), BF16)],
        compiler_params=_params(("arbitrary",)),
        name="in_projection",
    )(xp, xs, mod_l, norm1_l, w_in)


def _dft_tables(seq):
    k = np.arange(seq, dtype=np.int64)
    m = (k[:, None] * k[None, :]) % (2 * seq)
    ang = np.pi * m.astype(np.float64) / seq
    return np.cos(ang), np.sin(ang)


def _hyena_tables(seq):
    cos, sin = _dft_tables(seq)
    sign = np.where(np.arange(seq) % 2 == 0, 1.0, -1.0)
    fwd = np.concatenate([cos, sign[None, :], sin[1:]], axis=0)
    wk = np.full((seq,), 2.0)
    wk[0] = 1.0
    inv_cos = (cos * wk[None, :]) / (2.0 * seq)
    inv_nyq = sign[:, None] / (2.0 * seq)
    inv_sin = 2.0 * sin[:, 1:] / (2.0 * seq)
    inv = np.concatenate([inv_cos, inv_nyq, inv_sin], axis=1)

    return jnp.asarray(fwd, F32).astype(BF16), jnp.asarray(inv, F32).astype(BF16)


def _filter_embedding(seq):
    t = np.arange(seq, dtype=np.float64)
    t_norm = t / max(seq - 1, 1)
    bands = np.linspace(1e-4, HY_BANDS - 1, HY_BANDS)
    ang = (2.0 * np.pi / seq) * t[:, None] * bands[None, :]
    z = np.concatenate([t_norm[:, None], np.cos(ang), np.sin(ang)], axis=-1)
    z = np.pad(z, ((0, 0), (0, LANES - HY_EMB)))
    deltas = np.abs(np.linspace(math.log(HY_DECAY_TARGET) / HY_DECAY_LONG,
                                math.log(HY_DECAY_TARGET) / HY_DECAY_SHORT, HY_WIDTH))
    window = np.exp(-t_norm[:, None] * deltas[None, :])
    return jnp.asarray(z, F32), jnp.asarray(window, F32)


def _filter_kernel(z_ref, win_ref, cos_ref, sin_ref, w1_ref, b1_ref, w2_ref, b2_ref, w3_ref, o_ref):
    seq = z_ref.shape[0]
    h = jnp.sin(_mm3(z_ref[...], w1_ref[0]) + b1_ref[0])
    h = jnp.sin(_mm3(h, w2_ref[0]) + b2_ref[0])
    h = _mm3(h, w3_ref[0])
    win = win_ref[...]
    row = _iota((seq, 1), 0)
    sums, diffs = [], []
    for o in range(HY_ORDER):
        c0 = o * 2 * HY_WIDTH
        hf = h[:, c0:c0 + HY_WIDTH] * win
        hb = h[:, c0 + HY_WIDTH:c0 + 2 * HY_WIDTH] * win
        ssq = jnp.sum(hf * hf + hb * hb, axis=0, keepdims=True)
        inv = lax.rsqrt(ssq + EPS)
        hf = hf * inv
        hb = jnp.where(row == 0, 0.0, hb * inv)
        sums.append(hf + hb)
        diffs.append(hf - hb)
    hsum = jnp.concatenate(sums, axis=1)
    hdiff = jnp.concatenate(diffs, axis=1)
    h_re = _mm(cos_ref[...], hsum.astype(BF16))
    h_im = _mm(sin_ref[...], hdiff.astype(BF16))
    sign = jnp.where(row % 2 == 0, 1.0, -1.0)
    h_nyq = jnp.sum(sign * hsum, axis=0, keepdims=True)
    o_ref[0, 0] = h_re
    o_ref[0, 1] = h_im
    o_ref[0, 2] = jnp.where(row == 0, h_nyq, h_re)


def _hyena_filters(seq, w1, b1, w2, b2, w3):
    z, window = _filter_embedding(seq)
    cos, sin = _dft_tables(seq)
    n_out = HY_ORDER * 2 * HY_WIDTH
    w1p = jnp.pad(w1, ((0, 0), (0, LANES - HY_EMB), (0, LANES - HY_FFN)))
    b1p = jnp.pad(b1, ((0, 0), (0, LANES - HY_FFN))).reshape(DEPTH, 1, LANES)
    w2p = jnp.pad(w2, ((0, 0), (0, LANES - HY_FFN), (0, LANES - HY_FFN)))
    b2p = jnp.pad(b2, ((0, 0), (0, LANES - HY_FFN))).reshape(DEPTH, 1, LANES)
    w3p = jnp.pad(w3, ((0, 0), (0, LANES - HY_FFN), (0, 0)))
    per_layer = lambda shape: pl.BlockSpec((1,) + shape, lambda l: (l,) + (0,) * len(shape))
    return pl.pallas_call(
        _filter_kernel,
        out_shape=jax.ShapeDtypeStruct((DEPTH, 3, seq, HY_ORDER * HY_WIDTH), F32),
        grid=(DEPTH,),
        in_specs=[
            _const_spec((seq, LANES)), _const_spec((seq, HY_WIDTH)),
            _const_spec((seq, seq)), _const_spec((seq, seq)),
            per_layer((LANES, LANES)), per_layer((1, LANES)),
            per_layer((LANES, LANES)), per_layer((1, LANES)),
            per_layer((LANES, n_out)),
        ],
        out_specs=pl.BlockSpec((1, 3, seq, HY_ORDER * HY_WIDTH), lambda l: (l, 0, 0, 0)),
        compiler_params=_params(("parallel",)),
        name=f"hyena_filters_{seq}",
    )(z, window, jnp.asarray(cos, F32).astype(BF16), jnp.asarray(sin, F32).astype(BF16), w1p, b1p, w2p, b2p, w3p)


def _hyena_kernel(group, u_ref, cw_ref, cb_ref, d_ref, filt_ref, fwd_ref, inv_ref, y_ref):
    seq = u_ref.shape[0] // group
    cw = cw_ref[...]
    row = _iota((seq, 1), 0)
    ucs = []
    for s in range(group):
        u = u_ref[s * seq:(s + 1) * seq, :]
        ucs.append(cb_ref[...] + cw[0:1] * _shift_rows(u, -1, row) + cw[1:2] * u + cw[2:3] * _shift_rows(u, 1, row))
    side = lambda c0: jnp.concatenate([uc[:, c0:c0 + HY_WIDTH] for uc in ucs], axis=1)
    tile = lambda a: jnp.concatenate([a] * group, axis=1)
    gates = (side(HY_WIDTH), side(2 * HY_WIDTH))
    d = d_ref[...]
    z = side(0)
    for o in range(HY_ORDER):
        c0 = o * HY_WIDTH
        spec = _mm(fwd_ref[...], z.astype(BF16))
        s_re, s_im = spec[:seq], spec[seq:]
        a = tile(filt_ref[0, :, c0:c0 + HY_WIDTH])
        b = tile(filt_ref[1, :, c0:c0 + HY_WIDTH])
        c = tile(filt_ref[2, :, c0:c0 + HY_WIDTH])
        prod = jnp.concatenate([s_re * a - s_im * b, s_re * b + s_im * c], axis=0)
        conv = _mm(inv_ref[...], prod.astype(BF16))
        z = gates[o] * (conv + tile(d[o:o + 1]) * z)
    for s in range(group):
        y_ref[s * seq:(s + 1) * seq, :] = z[:, s * HY_WIDTH:(s + 1) * HY_WIDTH]


def _mixer_call(body, name, proj_all, tok0, batch, seq, consts, batch_ins, state_shapes, scratch, group=1):
    proj_width = proj_all.shape[1]
    rows = group * seq
    blk0 = tok0 // rows

    def batch_spec(shape):
        nd = len(shape)
        return pl.BlockSpec((group,) + tuple(shape[1:]), lambda b: (b,) + (0,) * (nd - 1))

    def layer_spec(shape, layer):
        nd = len(shape)
        return pl.BlockSpec((group, None) + tuple(shape[2:]), lambda b: (b, layer) + (0,) * (nd - 2))

    in_specs = [pl.BlockSpec((rows, proj_width), lambda b: (blk0 + b, 0))]
    in_specs += [_const_layer_spec(a[0].shape, a[1]) if isinstance(a, tuple) else _const_spec(a.shape)
                 for a in consts]
    consts = [a[0] if isinstance(a, tuple) else a for a in consts]
    in_specs += [layer_spec(a.shape, layer) for a, layer in batch_ins]
    batch_ins = [a for a, _ in batch_ins]
    out_shape = [jax.ShapeDtypeStruct((batch * seq, MIX_WIDTH), F32)]
    out_specs = [pl.BlockSpec((rows, MIX_WIDTH), lambda b: (b, 0))]
    for shape in state_shapes:
        out_shape.append(jax.ShapeDtypeStruct(shape, F32))
        out_specs.append(batch_spec(shape))
    return pl.pallas_call(
        body, out_shape=out_shape, grid=(batch // group,), in_specs=in_specs, out_specs=out_specs,
        scratch_shapes=scratch,
        compiler_params=_params(("parallel",)),
        name=f"{name}_{seq}",
    )(proj_all, *consts, *batch_ins)


def _hyena(u_all, tok0, batch, seq, conv_w, conv_b, d_bias, filt_l, tables):
    consts = [conv_w, conv_b, d_bias, filt_l, *tables]
    group = HY_GROUP_COLS // HY_WIDTH
    return _mixer_call(functools.partial(_hyena_kernel, group), "hyena", u_all, tok0, batch, seq, consts,
                       [], [], [], group=group)[0]


def _rglru_kernel(has_s0, emit_state, rg_ref, cw_ref, cb_ref, wg_ref, bg_ref, sp_ref, *refs):
    h0_ref = refs[0] if has_s0 else None
    y_ref = refs[1 if has_s0 else 0]
    st_ref = refs[-1] if emit_state else None
    seq = rg_ref.shape[0]
    w = RG_WIDTH
    xr = rg_ref[:, 0:w]
    gate = rg_ref[:, w:2 * w]
    row = _iota((seq, 1), 0)
    cw = cw_ref[...]
    xc = (cb_ref[...] + cw[0:1] * _shift_rows(xr, -2, row) + cw[1:2] * _shift_rows(xr, -1, row)
          + cw[2:3] * xr + cw[3:4] * _shift_rows(xr, 1, row))
    g = _sigmoid(_mm(xc.astype(BF16), wg_ref[...].astype(BF16)) + bg_ref[...])
    sp = sp_ref[...]
    c = RG_CHUNK
    n = seq // c
    pos = row % c
    hs = []
    for d in range(2):
        forward = d == 0
        r = g[:, d * w:(d + 1) * w]
        i = g[:, (2 + d) * w:(3 + d) * w]
        log_a = -RG_C * r * sp[d:d + 1]
        a = jnp.exp(log_a)
        b = jnp.sqrt(jnp.tanh(-log_a) * (1.0 + a * a)) * (i * xc)
        step = 1
        while step < c:
            keep = (pos >= step) if forward else (pos < c - step)
            shift = step if forward else seq - step
            a_s = pltpu.roll(a, shift, axis=0)
            b_s = pltpu.roll(b, shift, axis=0)
            b = jnp.where(keep, a * b_s + b, b)
            a = jnp.where(keep, a * a_s, a)
            step *= 2
        a3 = a.reshape(n, c, w)
        b3 = b.reshape(n, c, w)
        edge = c - 1 if forward else 0
        a_end = a3[:, edge:edge + 1, :]
        b_end = b3[:, edge:edge + 1, :]
        h = h0_ref[0, d:d + 1, :] if has_s0 else jnp.zeros((1, w), F32)
        h_in = [None] * n
        for ci in (range(n) if forward else range(n - 1, -1, -1)):
            h_in[ci] = h
            h = b_end[ci] + a_end[ci] * h
        hs.append((b3 + a3 * jnp.stack(h_in, axis=0)).reshape(seq, w))
    y_ref[...] = (hs[0] + hs[1]) * _gelu_tanh(gate)
    if emit_state:
        st_ref[0, 0:1, :] = hs[0][seq - 1:seq]
        st_ref[0, 1:2, :] = hs[1][0:1]


def _rglru(rg_all, tok0, batch, seq, conv_w, conv_b, w_gates, b_gates, softplus_neg_lam, h0, emit_state):
    has_s0 = h0 is not None
    consts = [conv_w, conv_b, w_gates, b_gates, softplus_neg_lam]
    return _mixer_call(functools.partial(_rglru_kernel, has_s0, emit_state), "rglru", rg_all, tok0, batch, seq,
                       consts, [h0] if has_s0 else [],
                       [(batch, 2, RG_WIDTH)] if emit_state else [], [])


def _hgrn_kernel(has_s0, emit_state, hg_ref, lb_ref, gain_ref, *refs):
    s0_ref = refs[0] if has_s0 else None
    y_ref = refs[1 if has_s0 else 0]
    st_ref = refs[-1] if emit_state else None
    seq = hg_ref.shape[0]
    w = HG_WIDTH
    c = HG_CHUNK
    n = seq // c
    mid = c // 2
    chunks = lambda a: a.reshape(n, c, w)
    pos = _iota((seq, 1), 0) % c
    stack_mask = _head_mask(c)[None]
    bd_mask = _block_diag_mask()
    pair_shape = (c, N_HEADS * c)
    t_idx = _iota(pair_shape, 0)
    s_idx = _iota(pair_shape, 1) % c

    def stack_heads(a3):
        return jnp.where(stack_mask, jnp.concatenate([a3] * N_HEADS, axis=1), 0.0).astype(BF16)

    q3 = chunks(_silu(hg_ref[:, 0:w]))
    v3 = chunks(hg_ref[:, 3 * w:4 * w])
    v_stack = stack_heads(v3)
    v_t = jnp.swapaxes(v3, 1, 2).astype(BF16)
    lbv = lb_ref[...]
    o_sum = None
    finals = []
    for d in range(2):
        forward = d == 0
        f_pre = hg_ref[:, (1 + d) * w:(2 + d) * w]
        lo = lbv[d:d + 1]
        a1 = jnp.log(lo)
        a2 = jnp.log(1.0 - lo) + _log_sigmoid(f_pre)
        g = jnp.maximum(a1, a2) + jnp.log(1.0 + jnp.exp(-jnp.abs(a1 - a2)))
        k3 = chunks((1.0 - lo) * _sigmoid(-f_pre))
        step = 1
        while step < c:
            if forward:
                g = jnp.where(pos >= step, g + pltpu.roll(g, step, axis=0), g)
            else:
                g = jnp.where(pos < c - step, g + pltpu.roll(g, seq - step, axis=0), g)
            step *= 2
        g3 = chunks(g)
        g_ref = g3[:, mid:mid + 1, :]
        g_tot = g3[:, c - 1:c, :] if forward else g3[:, 0:1, :]
        q_in = (q3 * jnp.exp(g3)).astype(BF16)
        q_sc = (q3 * jnp.exp(g3 - g_ref)).astype(BF16)
        k_stack = stack_heads(k3 * jnp.exp(g_ref - g3))
        k_out = (k3 * jnp.exp(g_tot - g3)).astype(BF16)
        pair = jnp.einsum("ntl,nrl->ntr", q_sc, k_stack, preferred_element_type=F32)
        causal = (t_idx >= s_idx) if forward else (t_idx <= s_idx)
        pair = jnp.where(causal[None], pair, 0.0).astype(BF16)
        o_intra = jnp.einsum("ntr,nrv->ntv", pair, v_stack, preferred_element_type=F32)
        upd = jnp.einsum("nvs,nsk->nvk", v_t, k_out, preferred_element_type=F32)
        decay = jnp.exp(g_tot)
        st = _load_state(s0_ref, d) if has_s0 else jnp.zeros((w, w), F32)
        o_inter = [None] * n
        for ci in (range(n) if forward else range(n - 1, -1, -1)):
            o_inter[ci] = _mm(q_in[ci], st.astype(BF16), _NT)
            st = st * decay[ci] + jnp.where(bd_mask, upd[ci], 0.0)
        o_dir = o_intra + jnp.stack(o_inter, axis=0)
        o_sum = o_dir if o_sum is None else o_sum + o_dir
        finals.append(st)

    ones_bd = jnp.where(bd_mask, 1.0 / HEAD_DIM, 0.0).astype(BF16)
    o = _head_rmsnorm(o_sum.reshape(seq, w), ones_bd) * gain_ref[...]
    y_ref[...] = o * _silu(hg_ref[:, 4 * w:5 * w])
    if emit_state:
        _store_state(st_ref, 0, finals[0])
        _store_state(st_ref, 1, finals[1])


def _state_shape(batch):
    return (batch, 2, N_HEADS, HEAD_DIM, HEAD_DIM)


def _hgrn(hg_all, tok0, batch, seq, lb_l, gain_l, s0, emit_state):
    has_s0 = s0 is not None
    return _mixer_call(functools.partial(_hgrn_kernel, has_s0, emit_state), "hgrn2", hg_all, tok0, batch, seq,
                       [lb_l, gain_l], [s0] if has_s0 else [],
                       [_state_shape(batch)] if emit_state else [], [])


def _ret_kernel(use_rope, has_s0, emit_state, ret_ref, cos_ref, sin_ref, dl_ref, dlp_ref, *refs):
    s0_ref = refs[0] if has_s0 else None
    y_ref = refs[1 if has_s0 else 0]
    st_ref = refs[-1] if emit_state else None
    seq = ret_ref.shape[0]
    w = RET_WIDTH
    c = min(RET_CHUNK, seq)
    n = seq // c
    chunks = lambda a: a.reshape(n, c, w)
    q = ret_ref[:, 0:w]
    k = ret_ref[:, w:2 * w] * (HEAD_DIM ** -0.5)
    if use_rope:
        lane = _iota((seq, w), 1)
        even = (lane % 2) == 0
        cos = cos_ref[...]
        sin = sin_ref[...]

        def rope(x):
            nxt = pltpu.roll(x, w - 1, axis=1)
            prv = pltpu.roll(x, 1, axis=1)
            return x * cos + jnp.where(even, nxt, prv) * sin

        q = rope(q)
        k = rope(k)
    q3, k3, v3 = chunks(q), chunks(k), chunks(ret_ref[:, 2 * w:3 * w])

    lg = _log_sigmoid(dl_ref[...])
    lgp = _log_sigmoid(dlp_ref[...])
    pair_shape = (c, N_HEADS * c)
    t_idx = _iota(pair_shape, 0)
    s_idx = _iota(pair_shape, 1) % c
    dist = (t_idx - s_idx).astype(F32)
    decay = jnp.exp(jnp.where(dist >= 0, dist * lgp[0:1], -dist * lgp[1:2]))
    decay = decay + jnp.where(dist == 0, 1.0, 0.0)
    stack_mask = _head_mask(c)[None]
    bd_mask = _block_diag_mask()
    pos = _iota((c, 1), 0).astype(F32)
    fc = float(c)

    def stack_heads(a3):
        return jnp.where(stack_mask, jnp.concatenate([a3] * N_HEADS, axis=1), 0.0).astype(BF16)

    pair = jnp.einsum("ntl,nrl->ntr", q3.astype(BF16), stack_heads(k3), preferred_element_type=F32)
    pair = (pair * decay[None]).astype(BF16)
    o_sum = jnp.einsum("ntr,nrv->ntv", pair, stack_heads(v3), preferred_element_type=F32)
    v_t = jnp.swapaxes(v3, 1, 2).astype(BF16)
    finals = []
    for d in range(2):
        forward = d == 0
        lgd = lg[d:d + 1]
        q_in = (q3 * jnp.exp(((pos + 1.0) if forward else (fc - pos)) * lgd)).astype(BF16)
        k_out = (k3 * jnp.exp(((fc - 1.0 - pos) if forward else pos) * lgd)).astype(BF16)
        upd = jnp.einsum("nvs,nsk->nvk", v_t, k_out, preferred_element_type=F32)
        chunk_decay = jnp.exp(fc * lgd)
        st = _load_state(s0_ref, d) if has_s0 else jnp.zeros((w, w), F32)
        o_inter = [None] * n
        for ci in (range(n) if forward else range(n - 1, -1, -1)):
            o_inter[ci] = _mm(q_in[ci], st.astype(BF16), _NT)
            st = st * chunk_decay + jnp.where(bd_mask, upd[ci], 0.0)
        o_sum = o_sum + jnp.stack(o_inter, axis=0)
        finals.append(st)

    ones_bd = jnp.where(bd_mask, 1.0 / HEAD_DIM, 0.0).astype(BF16)
    o = _head_rmsnorm(o_sum.reshape(seq, w), ones_bd)
    y_ref[...] = _silu(ret_ref[:, 3 * w:4 * w]) * o
    if emit_state:
        _store_state(st_ref, 0, finals[0])
        _store_state(st_ref, 1, finals[1])


def _rope_tables(seq):
    rows = seq // GRID_W
    row = np.repeat(np.arange(rows), GRID_W).astype(np.float64)
    col = (np.arange(seq) % GRID_W).astype(np.float64)
    n_freq = HEAD_DIM // 4
    inv_freq = ROPE_BASE ** (-np.arange(n_freq, dtype=np.float64) / n_freq)
    ang = np.concatenate([row[:, None] * inv_freq, col[:, None] * inv_freq], axis=-1)
    ang = np.repeat(ang, 2, axis=1)
    cos = np.tile(np.cos(ang), (1, N_HEADS))
    sin = np.tile(np.sin(ang) * np.where(np.arange(HEAD_DIM) % 2 == 0, -1.0, 1.0)[None, :], (1, N_HEADS))
    return jnp.asarray(cos, F32), jnp.asarray(sin, F32)


def _retention(ret_all, tok0, batch, seq, use_rope, decay_l, s0, emit_state):
    has_s0 = s0 is not None
    w = RET_WIDTH
    c = min(RET_CHUNK, seq)
    cos, sin = _rope_tables(seq)
    dl = jnp.repeat(decay_l, HEAD_DIM, axis=-1)
    dlp = jnp.repeat(decay_l, c, axis=-1)
    return _mixer_call(functools.partial(_ret_kernel, use_rope, has_s0, emit_state), "retention", ret_all, tok0,
                       batch, seq, [cos, sin, dl, dlp], [s0] if has_s0 else [],
                       [_state_shape(batch)] if emit_state else [], [])


def _route(h2s, wrt_ref, rb_ref):
    tm = h2s[0].shape[0]
    lane = _iota((N_EXPERTS, tm), 0).astype(F32)
    group = (_iota((N_EXPERTS, tm), 0) // E_PER_GROUP).astype(F32)
    neg = -jnp.inf
    each = lambda f, *cols: [f(*args) for args in zip(*cols)]

    def first_argmax(vals):
        mx = jnp.max(vals, axis=0, keepdims=True)
        idx = jnp.min(jnp.where(vals == mx, lane, float(N_EXPERTS)), axis=0, keepdims=True)
        return mx, idx

    def softmax(logits):
        e = jnp.exp(logits - jnp.max(logits, axis=0, keepdims=True))
        return e / jnp.sum(e, axis=0, keepdims=True)

    logits = each(lambda h2: _mm3(wrt_ref[...], h2, _NT), h2s)
    probs = each(softmax, logits)
    sel = each(lambda p: p + rb_ref[...], probs)

    def group_score(s, g):
        vals = jnp.where(group == float(g), s, neg)
        m1, i1 = first_argmax(vals)
        m2, _ = first_argmax(jnp.where(lane == i1, neg, vals))
        return m1 + m2

    best_score = each(lambda s: group_score(s, 0), sel)
    best = each(jnp.zeros_like, best_score)
    for g in range(1, N_GROUPS):
        score = each(lambda s: group_score(s, g), sel)
        take = each(lambda sc, bs: sc > bs, score, best_score)
        best = each(lambda t, b: jnp.where(t, float(g), b), take, best)
        best_score = each(lambda t, sc, bs: jnp.where(t, sc, bs), take, score, best_score)

    def gates_of(s, p, b):
        vals = jnp.where(group == b, s, neg)
        _, i1 = first_argmax(vals)
        _, i2 = first_argmax(jnp.where(lane == i1, neg, vals))
        picked = jnp.where((lane == i1) | (lane == i2), p, 0.0)
        return picked / jnp.sum(picked, axis=0, keepdims=True)

    return each(gates_of, sel, probs, best), best


INFO_ROWS = 8
INFO_GROUP = 4
INFO_RANK = 5
H2_WIDTH = D_MODEL + LANES


def _route_kernel(n0_tiles, xp_ref, xs_ref, *refs):
    y_refs = refs[:8]
    mod_ref, wo_ref, n2_ref, wrt_ref, rb_ref, x1_ref, h2_ref, info_ref, cend_ref, carry_s, wob_s = refs[8:]
    i = pl.program_id(0)
    first = i < n0_tiles
    d = D_MODEL
    tb = TB_MOE
    mod = mod_ref[0, 0]

    @pl.when(i == 0)
    def _():
        carry_s[...] = jnp.zeros_like(carry_s)
        wob_s[...] = wo_ref[...].astype(BF16)

    n_sub = TM_ROUTE // tb
    each = lambda f, *cols: [f(*args) for args in zip(*cols)]
    rows = [slice(h * tb, (h + 1) * tb) for h in range(n_sub)]
    expert = _iota((N_EXPERTS, tb), 0).astype(F32)
    upper = (_iota((tb, tb), 0) <= _iota((tb, tb), 1)).astype(BF16)
    eye = (_iota((tb, tb), 0) == _iota((tb, tb), 1)).astype(BF16)

    def mixed_of(r):
        ys = [jnp.where(first, y_refs[2 * j][r, :], y_refs[2 * j + 1][r, :]).astype(BF16) for j in range(4)]
        return _mm(jnp.concatenate(ys, axis=1), wob_s[...])

    mixed = each(mixed_of, rows)
    x1 = each(lambda r, m: jnp.where(first, xp_ref[r, :], xs_ref[r, :]) + mod[:, 2 * d:3 * d] * m, rows, mixed)

    def h2_of(v):
        h2 = v * lax.rsqrt(jnp.mean(v * v, axis=-1, keepdims=True) + EPS) * n2_ref[...]
        return h2 * (1.0 + mod[:, 4 * d:5 * d]) + mod[:, 3 * d:4 * d]

    h2 = each(h2_of, x1)
    gates, best = _route(h2, wrt_ref, rb_ref)
    g4 = each(lambda g, b: [jnp.sum(jnp.where(expert == E_PER_GROUP * b + j, g, 0.0), axis=0, keepdims=True)
                            for j in range(E_PER_GROUP)], gates, best)
    onehot = each(lambda b: jnp.where(expert == b, 1.0, 0.0), best)
    counts = each(lambda o: _mm(o.astype(BF16), upper), onehot)

    def pieces_of(g):
        p = _split3(jnp.concatenate(g + [jnp.zeros((N_EXPERTS - E_PER_GROUP, tb), F32)], axis=0))
        return jnp.concatenate(list(p) + [jnp.zeros((LANES - 3 * N_EXPERTS, tb), BF16)], axis=0)

    ext = each(lambda g: _mm(eye, pieces_of(g), _NT).astype(BF16), g4)

    carry = carry_s[...]
    for h in range(n_sub):
        incl = counts[h] + carry[:, 0:1]
        rank = jnp.sum(onehot[h] * incl, axis=0, keepdims=True) - 1.0
        carry = jnp.broadcast_to(incl[:, tb - 1:tb], (N_EXPERTS, LANES))
        x1_ref[rows[h], :] = x1[h]
        h2_ref[rows[h], 0:d] = h2[h].astype(BF16)
        h2_ref[rows[h], d:d + LANES] = ext[h]
        info_ref[:, rows[h]] = jnp.concatenate(g4[h] + [best[h], rank, jnp.zeros((INFO_ROWS - 6, tb), F32)], axis=0)
        cend_ref[h] = carry
    carry_s[...] = carry


def _route_call(layer, xp, xs, ys, mod_l, w_out, norm2_l, w_router, router_bias, dec_seq):
    n_prompt_tok = xp.shape[0]
    t_all = n_prompt_tok + xs.shape[0]
    tm = TM_ROUTE
    sub = tm // TB_MOE
    n0_tiles = n_prompt_tok // tm
    row = _mod_row_map(tm, n_prompt_tok, dec_seq)
    tok = lambda width: pl.BlockSpec((tm, width), lambda i: (i, 0))
    y_specs, y_args = [], []
    for yp, ysm in ys:
        y_specs += _pair_specs(tm, MIX_WIDTH, n0_tiles)
        y_args += [yp, ysm]
    return pl.pallas_call(
        functools.partial(_route_kernel, n0_tiles),
        out_shape=[jax.ShapeDtypeStruct((t_all, D_MODEL), F32), jax.ShapeDtypeStruct((t_all, H2_WIDTH), BF16),
                   jax.ShapeDtypeStruct((INFO_ROWS, t_all), F32),
                   jax.ShapeDtypeStruct((t_all // TB_MOE, N_EXPERTS, LANES), F32)],
        grid=(t_all // tm,),
        in_specs=_pair_specs(tm, D_MODEL, n0_tiles) + y_specs + [
            pl.BlockSpec((1, 1, 1, N_MOD * D_MODEL), lambda i: (0, row(i), 0, 0)),
            _const_layer_spec(w_out.shape, layer), _const_spec((1, D_MODEL)),
            _const_spec((N_EXPERTS, D_MODEL)), _const_spec((N_EXPERTS, 1)),
        ],
        out_specs=[tok(D_MODEL), tok(H2_WIDTH), pl.BlockSpec((INFO_ROWS, tm), lambda i: (0, i)),
                   pl.BlockSpec((sub, N_EXPERTS, LANES), lambda i: (i, 0, 0))],
        scratch_shapes=[pltpu.VMEM((N_EXPERTS, LANES), F32), pltpu.VMEM((D_MODEL, D_MODEL), BF16)],
        compiler_params=_params(("arbitrary",)),
        name="route",
    )(xp, xs, *y_args, mod_l, w_out, norm2_l, w_router.T, router_bias.reshape(N_EXPERTS, 1))


def _moe_plan(cend, info):
    ts = TS_MOE
    tb = TR_MOE
    n_blocks = cend.shape[0]
    t_all = n_blocks * TB_MOE
    n_tiles = t_all // ts + N_GROUPS
    sub_per_tile = ts // tb
    n_sub = n_tiles * sub_per_tile
    cend = cend[:, :N_GROUPS, 0].astype(jnp.int32)
    cum = jnp.concatenate([jnp.zeros((1, N_GROUPS), jnp.int32), cend[:-1]], axis=0)
    cnt = cend - cum
    grp = info[INFO_GROUP].astype(jnp.int32)
    rank = info[INFO_RANK].astype(jnp.int32)
    tot = cend[-1]
    padded = ((tot + ts - 1) // ts) * ts
    off = jnp.cumsum(padded) - padded
    pos = rank + sum(jnp.where(grp == g, off[g], 0) for g in range(N_GROUPS))

    sub_start = jnp.arange(n_sub, dtype=jnp.int32) * tb
    in_g = (sub_start[:, None] >= off[None, :]) & (sub_start[:, None] < (off + padded)[None, :])
    g_of = jnp.argmax(in_g, axis=1).astype(jnp.int32)
    r0 = sub_start - off[g_of]
    r1 = jnp.minimum(r0 + tb, tot[g_of])
    live = jnp.any(in_g, axis=1) & (r1 > r0)
    cend_g = cend[:, g_of]
    blo = jnp.where(live, jnp.sum(cend_g <= r0[None, :], axis=0), 0).astype(jnp.int32)
    bhi = jnp.where(live, jnp.sum(cend_g <= (r1 - 1)[None, :], axis=0), -1).astype(jnp.int32)

    used = jnp.any(in_g, axis=1)[::sub_per_tile]
    tile_group = g_of[::sub_per_tile]
    eidx = E_PER_GROUP * tile_group[:, None] + jnp.arange(E_PER_GROUP, dtype=jnp.int32)[None, :]
    n_used = jnp.sum(used.astype(jnp.int32))
    last = eidx[jnp.maximum(n_used - 1, 0), E_PER_GROUP - 1]
    eidx = jnp.where(used[:, None], eidx, last).reshape(-1).astype(jnp.int32)

    start = off[None, :] + cum
    end = off[None, :] + cend
    has = cnt > 0
    slo = jnp.where(has, start // tb, 0).reshape(-1).astype(jnp.int32)
    shi = jnp.where(has, (end - 1) // tb, -1).reshape(-1).astype(jnp.int32)
    tile_start = sub_start[::sub_per_tile]
    rows = jnp.where(used, jnp.clip(tot[tile_group] - (tile_start - off[tile_group]), 0, ts), 0)
    return dict(pos=pos, eidx=eidx, rows=rows.astype(jnp.int32), blo=blo, bhi=bhi, slo=slo, shi=shi,
                n_tiles=n_tiles)


def _expert_kernel(eidx_ref, rows_ref, blo_ref, bhi_ref, h2_ref, pos_ref, wg_ref, wu_ref, wd_ref,
                   ys_ref, xs_s, acc_s, ax_s):
    i = pl.program_id(0)
    j = pl.program_id(1)
    n_rows = rows_ref[i]
    used = n_rows > 0
    tb, tr = TB_MOE, TR_MOE
    sub_per_tile = TS_MOE // tr

    @pl.when(used & (j == 0))
    def _():
        r_idx = _iota((tr, tb), 0)
        for part in range(sub_per_tile):
            s = i * sub_per_tile + part
            base = s * tr
            ax_s[...] = jnp.zeros_like(ax_s)

            def gather(b, carry):
                sel = jnp.where(pos_ref[b] - base == r_idx, 1.0, 0.0).astype(BF16)
                tok = pl.ds(pl.multiple_of(b * tb, tb), tb)
                ax_s[...] += _mm(sel, h2_ref[tok, :])
                return carry

            lax.fori_loop(blo_ref[s], bhi_ref[s] + 1, gather, 0)
            xs_s[part * tr:(part + 1) * tr, :] = ax_s[...].astype(BF16)
        acc_s[...] = jnp.zeros_like(acc_s)

    half_rows = TS_MOE // 2
    for part in range(2):
        rows = slice(part * half_rows, (part + 1) * half_rows)

        @pl.when(n_rows > part * half_rows)
        def _():
            x = xs_s[rows, 0:D_MODEL]
            pieces = xs_s[rows, D_MODEL:H2_WIDTH].astype(F32)
            ge = jnp.sum(jnp.where(_iota(pieces.shape, 1) % N_EXPERTS == j, pieces, 0.0), axis=-1, keepdims=True)
            hh = _silu(_mm(x, wg_ref[0, 0].astype(BF16))) * _mm(x, wu_ref[0, 0].astype(BF16)) * ge
            acc_s[rows, :] += _mm(hh.astype(BF16), wd_ref[0, 0].astype(BF16))

    @pl.when(j == E_PER_GROUP - 1)
    def _():
        ys_ref[...] = jnp.where(used, acc_s[...], 0.0).astype(BF16)


def _expert_call(layer, plan, h2, w_gate, w_up, w_down):
    t_all = h2.shape[0]
    tb, ts = TB_MOE, TS_MOE
    n_tiles = plan["n_tiles"]
    pos_rows = plan["pos"].reshape(t_all // tb, 1, tb)
    weight = lambda shape: pl.BlockSpec((1, 1) + shape, lambda i, j, eidx, *_: (layer, eidx[i * E_PER_GROUP + j], 0, 0))
    grid_spec = pltpu.PrefetchScalarGridSpec(
        num_scalar_prefetch=4,
        grid=(n_tiles, E_PER_GROUP),
        in_specs=[
            _const_spec((t_all, H2_WIDTH)), _const_spec(pos_rows.shape),
            weight((D_MODEL, D_EXPERT)), weight((D_MODEL, D_EXPERT)), weight((D_EXPERT, D_MODEL)),
        ],
        out_specs=pl.BlockSpec((ts, D_MODEL), lambda i, j, *_: (i, 0)),
        scratch_shapes=[pltpu.VMEM((ts, H2_WIDTH), BF16), pltpu.VMEM((ts, D_MODEL), F32),
                        pltpu.VMEM((TR_MOE, H2_WIDTH), F32)],
    )
    return pl.pallas_call(
        _expert_kernel,
        out_shape=jax.ShapeDtypeStruct((n_tiles * ts, D_MODEL), BF16),
        grid_spec=grid_spec,
        compiler_params=_params(("arbitrary", "arbitrary")),
        name=f"experts_{layer}",
    )(plan["eidx"], plan["rows"], plan["blo"], plan["bhi"], h2, pos_rows, w_gate, w_up, w_down)


def _combine_kernel(final, n0_tiles, slo_ref, shi_ref, ys_ref, pos_ref, x1_ref, mod_ref, nf_ref,
                    op_ref, os_ref, acc_s):
    b = pl.program_id(0)
    d = D_MODEL
    tb, tr = TB_MOE, TR_MOE
    acc_s[...] = jnp.zeros_like(acc_s)
    c_idx = _iota((tb, tr), 1)
    pos = pos_ref[...]
    for g in range(N_GROUPS):
        def scatter(s, carry):
            sel = jnp.where(pos - s * tr == c_idx, 1.0, 0.0).astype(BF16)
            acc_s[...] += _mm(sel, ys_ref[pl.ds(pl.multiple_of(s * tr, tr), tr), :])
            return carry

        lax.fori_loop(slo_ref[b * N_GROUPS + g], shi_ref[b * N_GROUPS + g] + 1, scatter, 0)

    def result():
        x2 = x1_ref[...] + mod_ref[0, 0][:, 5 * d:6 * d] * acc_s[...]
        if final:
            x2 = x2 * lax.rsqrt(jnp.mean(x2 * x2, axis=-1, keepdims=True) + EPS) * nf_ref[...]
        return x2

    @pl.when(b < n0_tiles)
    def _():
        op_ref[...] = result()

    @pl.when(b >= n0_tiles)
    def _():
        os_ref[...] = result()


def _combine_call(final, plan, ys_sorted, x1, mod_l, norm_final, n_prompt_tok, dec_seq):
    t_all = x1.shape[0]
    tb = TB_MOE
    n0_tiles = n_prompt_tok // tb
    row = _mod_row_map(tb, n_prompt_tok, dec_seq)
    grid_spec = pltpu.PrefetchScalarGridSpec(
        num_scalar_prefetch=2,
        grid=(t_all // tb,),
        in_specs=[
            _const_spec(ys_sorted.shape),
            pl.BlockSpec((tb, 1), lambda b, *_: (b, 0)),
            pl.BlockSpec((tb, D_MODEL), lambda b, *_: (b, 0)),
            pl.BlockSpec((1, 1, 1, N_MOD * D_MODEL), lambda b, *_: (0, row(b), 0, 0)),
            _const_spec((1, D_MODEL)),
        ],
        out_specs=_pair_specs(tb, D_MODEL, n0_tiles),
        scratch_shapes=[pltpu.VMEM((tb, D_MODEL), F32)],
    )
    return pl.pallas_call(
        functools.partial(_combine_kernel, final, n0_tiles),
        out_shape=[jax.ShapeDtypeStruct((n_prompt_tok, D_MODEL), F32),
                   jax.ShapeDtypeStruct((t_all - n_prompt_tok, D_MODEL), F32)],
        grid_spec=grid_spec,
        compiler_params=_params(("arbitrary",)),
        name="combine",
    )(plan["slo"], plan["shi"], ys_sorted, plan["pos"].reshape(t_all, 1), x1, mod_l, norm_final)


def _rg_gate_weights(wa, ba, wx, bx):
    eye = jnp.eye(RG_HEADS, dtype=wa.dtype)

    def dense(wd):
        return jnp.einsum("hij,hg->higj", wd, eye).reshape(RG_WIDTH, RG_WIDTH)

    w = jnp.concatenate([dense(wa[0]), dense(wa[1]), dense(wx[0]), dense(wx[1])], axis=1)
    b = jnp.concatenate([ba[0], ba[1], bx[0], bx[1]], axis=0).reshape(1, 4 * RG_WIDTH)
    return w, b


def kernel(x_prompt, x_sample, state_rglru, state_hgrn, state_ret, c, c_ctx, norm1, norm2, norm_final, w_ada, b_ada, w_in, w_out, hy_conv_w, hy_conv_b, hy_w1, hy_b1, hy_w2, hy_b2, hy_w3, hy_d, rg_conv_w, rg_conv_b, rg_wa, rg_ba, rg_wx, rg_bx, rg_lambda, hg_lb, hg_norm, ret_decay, w_router, router_bias, w_gate, w_up, w_down):
    batch, seq, d = x_prompt.shape
    dec_batch, dec_seq, _ = x_sample.shape
    assert d == D_MODEL and dec_batch + 1 <= COND_ROWS
    n_prompt_tok = batch * seq

    lb_cum = jnp.cumsum(jax.nn.softmax(hg_lb.astype(F32), axis=0), axis=0)
    lb_all = lb_cum - lb_cum[0:1]

    cond = jnp.zeros((COND_ROWS, d), F32).at[0].set(c_ctx).at[1:1 + dec_batch].set(c)
    mod = _modulation(cond, w_ada, b_ada).reshape(DEPTH, COND_ROWS, 1, N_MOD * d)


    passes = (
        dict(tok0=0, batch=batch, seq=seq, rope=False),
        dict(tok0=n_prompt_tok, batch=dec_batch, seq=dec_seq, rope=True),
    )
    filters = {p["seq"]: _hyena_filters(p["seq"], hy_w1, hy_b1, hy_w2, hy_b2, hy_w3) for p in passes}
    tables = {p["seq"]: _hyena_tables(p["seq"]) for p in passes}

    xp = x_prompt.reshape(-1, d)
    xs = x_sample.reshape(-1, d)
    new_rg, new_hg, new_ret = [], [], []
    for l in range(DEPTH):
        hy_all, rg_all, hg_all, ret_all = _in_projection(
            l, xp, xs, mod[l:l + 1], norm1[l].reshape(1, d), w_in, dec_seq)
        wg, bg = _rg_gate_weights(rg_wa[l], rg_ba[l], rg_wx[l], rg_bx[l])
        sp = jax.nn.softplus(-rg_lambda[l])
        ys = [[], [], [], []]
        for pi, p in enumerate(passes):
            first = pi == 0
            geom = (p["tok0"], p["batch"], p["seq"])
            y_hy = _hyena(hy_all, *geom, hy_conv_w[l], hy_conv_b[l].reshape(1, -1), hy_d[l],
                          (filters[p["seq"]], l), tables[p["seq"]])
            y_rg, *st_rg = _rglru(rg_all, *geom, rg_conv_w[l], rg_conv_b[l].reshape(1, -1), wg, bg, sp,
                                  None if first else (state_rglru, l), first)
            y_hg, *st_hg = _hgrn(hg_all, *geom, lb_all[l], hg_norm[l].reshape(1, -1),
                                 None if first else (state_hgrn, l), first)
            y_ret, *st_ret = _retention(ret_all, *geom, p["rope"], ret_decay[l],
                                        None if first else (state_ret, l), first)
            for lst, y in zip(ys, (y_hy, y_rg, y_hg, y_ret)):
                lst.append(y)
            if first:
                new_rg.append(st_rg[0])
                new_hg.append(st_hg[0])
                new_ret.append(st_ret[0])
        x1, h2, info, cend = _route_call(l, xp, xs, ys, mod[l:l + 1], w_out, norm2[l].reshape(1, d),
                                         w_router, router_bias, dec_seq)
        plan = _moe_plan(cend, info)
        ys_sorted = _expert_call(l, plan, h2, w_gate, w_up, w_down)
        xp, xs = _combine_call(l == DEPTH - 1, plan, ys_sorted, x1, mod[l:l + 1], norm_final.reshape(1, d),
                               n_prompt_tok, dec_seq)

    return (xp.reshape(batch, seq, d), xs.reshape(dec_batch, dec_seq, d), jnp.stack(new_rg, axis=1),
            jnp.stack(new_hg, axis=1), jnp.stack(new_ret, axis=1))
```

```python
import functools
import math

import numpy as np
import jax
import jax.numpy as jnp
from jax import lax
from jax.experimental import pallas as pl
from jax.experimental.pallas import tpu as pltpu

F32 = jnp.float32
BF16 = jnp.bfloat16

D_MODEL = 1024
DEPTH = 2
GRID_W = 64
HY_WIDTH = 256
RG_WIDTH = 256
HG_WIDTH = 256
RET_WIDTH = 256
MIX_WIDTH = 256
HY_ORDER = 2
HY_EMB = 33
HY_BANDS = 16
HY_FFN = 64
HY_DECAY_TARGET = 1e-2
HY_DECAY_SHORT = 0.3
HY_DECAY_LONG = 1.5
HY_GROUP_COLS = 512
RG_HEADS = 8
RG_HEAD_DIM = 32
RG_C = 8.0
RG_CHUNK = 16
N_HEADS = 4
HEAD_DIM = 64
HG_CHUNK = 64
RET_CHUNK = 128
ROPE_BASE = 10000.0
N_EXPERTS = 16
N_GROUPS = 4
E_PER_GROUP = 4
D_EXPERT = 512
N_MOD = 6
EPS = 1e-6
PROJ_HY = 3 * HY_WIDTH
PROJ_RG = 2 * RG_WIDTH
PROJ_HG = 5 * HG_WIDTH
PROJ_RET = 4 * RET_WIDTH
PROJ_WIDTH = PROJ_HY + PROJ_RG + PROJ_HG + PROJ_RET
COND_ROWS = 16
LANES = 128
VMEM_LIMIT = 56 * 1024 * 1024
TM_PROJ = 512
TB_MOE = 512
TR_MOE = 256
TD_MOE = 128
TM_ROUTE = 2 * TB_MOE
TS_MOE = 1024
TN_MOD = 1536

_NN = (((1,), (0,)), ((), ()))
_NT = (((1,), (1,)), ((), ()))


def _mm(a, b, dn=_NN):
    return lax.dot_general(a, b, dn, preferred_element_type=F32)


def _split2(x):
    hi = x.astype(BF16)
    lo = (x - hi.astype(F32)).astype(BF16)
    return hi, lo


def _split3(x):
    hi = x.astype(BF16)
    r = x - hi.astype(F32)
    mid = r.astype(BF16)
    lo = (r - mid.astype(F32)).astype(BF16)
    return hi, mid, lo


def _mm3(a, b, dn=_NN):
    ah, al = _split2(a)
    bh, bl = _split2(b)
    return _mm(ah, bh, dn) + (_mm(ah, bl, dn) + _mm(al, bh, dn))


def _mm_exact_rhs(a, b_bf16):
    a1, a2, a3 = _split3(a)
    return _mm(a1, b_bf16) + (_mm(a2, b_bf16) + _mm(a3, b_bf16))


def _sigmoid(x):
    return 1.0 / (1.0 + jnp.exp(-x))


def _silu(x):
    return x * _sigmoid(x)


def _log_sigmoid(x):
    return jnp.minimum(x, 0.0) - jnp.log(1.0 + jnp.exp(-jnp.abs(x)))


def _gelu_tanh(x):
    return 0.5 * x * (1.0 + jnp.tanh(math.sqrt(2.0 / math.pi) * (x + 0.044715 * (x * x * x))))


def _iota(shape, dim):
    return lax.broadcasted_iota(jnp.int32, shape, dim)


def _shift_rows(u, k, row):
    n = u.shape[0]
    if k == 0:
        return u
    r = pltpu.roll(u, (-k) % n, axis=0)
    if k < 0:
        return jnp.where(row >= -k, r, 0.0)
    return jnp.where(row < n - k, r, 0.0)


def _head_mask(n_rows_per_head):
    shape = (N_HEADS * n_rows_per_head, MIX_WIDTH)
    return (_iota(shape, 0) // n_rows_per_head) == (_iota(shape, 1) // HEAD_DIM)


def _block_diag_mask():
    shape = (MIX_WIDTH, MIX_WIDTH)
    return (_iota(shape, 0) // HEAD_DIM) == (_iota(shape, 1) // HEAD_DIM)


def _head_rmsnorm(o, ones_bd):
    ms = _mm_exact_rhs(o * o, ones_bd)
    return o * lax.rsqrt(ms + EPS)


def _load_state(s0_ref, d):
    zero = jnp.zeros((HEAD_DIM, HEAD_DIM), F32)
    rows = []
    for h in range(N_HEADS):
        blk = s0_ref[0, d, h].T
        rows.append(jnp.concatenate([blk if g == h else zero for g in range(N_HEADS)], axis=1))
    return jnp.concatenate(rows, axis=0)


def _store_state(st_ref, d, st):
    for h in range(N_HEADS):
        lo, hi = h * HEAD_DIM, (h + 1) * HEAD_DIM
        st_ref[0, d, h] = st[lo:hi, lo:hi].T


def _params(sem, vmem=VMEM_LIMIT):
    return pltpu.CompilerParams(dimension_semantics=sem, vmem_limit_bytes=vmem)


def _const_spec(shape):
    nd = len(shape)
    return pl.BlockSpec(shape, lambda *_: (0,) * nd, pipeline_mode=pl.Buffered(1))


def _const_layer_spec(shape, layer):
    nd = len(shape)
    return pl.BlockSpec((None,) + tuple(shape[1:]), lambda *_: (layer,) + (0,) * (nd - 1),
                        pipeline_mode=pl.Buffered(1))


def _mod_kernel(cond_ref, w_ref, b_ref, o_ref):
    o_ref[0] = _mm3(_silu(cond_ref[...]), w_ref[0]) + b_ref[0]


def _modulation(cond, w_ada, b_ada):
    tn = TN_MOD
    n_mod = N_MOD * D_MODEL
    return pl.pallas_call(
        _mod_kernel,
        out_shape=jax.ShapeDtypeStruct((DEPTH, COND_ROWS, n_mod), F32),
        grid=(DEPTH, n_mod // tn),
        in_specs=[
            pl.BlockSpec((COND_ROWS, D_MODEL), lambda l, j: (0, 0)),
            pl.BlockSpec((1, D_MODEL, tn), lambda l, j: (l, 0, j)),
            pl.BlockSpec((1, 1, tn), lambda l, j: (l, 0, j)),
        ],
        out_specs=pl.BlockSpec((1, COND_ROWS, tn), lambda l, j: (l, 0, j)),
        compiler_params=_params(("parallel", "parallel")),
        name="modulation",
    )(cond, w_ada, b_ada.reshape(DEPTH, 1, n_mod))


def _pair_specs(tm, width, n0_tiles):
    return [pl.BlockSpec((tm, width), lambda i, *_: (jnp.minimum(i, n0_tiles - 1), 0)),
            pl.BlockSpec((tm, width), lambda i, *_: (jnp.maximum(i - n0_tiles, 0), 0))]


def _proj_kernel(n0_tiles, xp_ref, xs_ref, mod_ref, n1_ref, w_ref, hy_ref, rg_ref, hg_ref, ret_ref, wb_s):
    @pl.when(pl.program_id(0) == 0)
    def _():
        wb_s[...] = w_ref[...].astype(BF16)

    x = jnp.where(pl.program_id(0) < n0_tiles, xp_ref[...], xs_ref[...])
    mod = mod_ref[0, 0]
    sh1 = mod[:, 0:D_MODEL]
    sc1 = mod[:, D_MODEL:2 * D_MODEL]
    h = x * lax.rsqrt(jnp.mean(x * x, axis=-1, keepdims=True) + EPS) * n1_ref[...]
    h = (h * (1.0 + sc1) + sh1).astype(BF16)
    c0 = 0
    for ref, width in ((hy_ref, PROJ_HY), (rg_ref, PROJ_RG), (hg_ref, PROJ_HG), (ret_ref, PROJ_RET)):
        ref[...] = _mm(h, wb_s[:, c0:c0 + width])
        c0 += width


def _mod_row_map(tm, n_prompt_tok, dec_seq):
    n_prompt_tiles = n_prompt_tok // tm

    def row(i):
        return jnp.where(i < n_prompt_tiles, 0, 1 + (i * tm - n_prompt_tok) // dec_seq)

    return row


def _in_projection(layer, xp, xs, mod_l, norm1_l, w_in, dec_seq):
    n_prompt_tok = xp.shape[0]
    t_all = n_prompt_tok + xs.shape[0]
    tm = TM_PROJ
    n0_tiles = n_prompt_tok // tm
    row = _mod_row_map(tm, n_prompt_tok, dec_seq)
    widths = (PROJ_HY, PROJ_RG, PROJ_HG, PROJ_RET)
    return pl.pallas_call(
        functools.partial(_proj_kernel, n0_tiles),
        out_shape=[jax.ShapeDtypeStruct((t_all, w), F32) for w in widths],
        grid=(t_all // tm,),
        in_specs=_pair_specs(tm, D_MODEL, n0_tiles) + [
            pl.BlockSpec((1, 1, 1, N_MOD * D_MODEL), lambda i: (0, row(i), 0, 0)),
            _const_spec((1, D_MODEL)),
            _const_layer_spec(w_in.shape, layer),
        ],
        out_specs=[pl.BlockSpec((tm, w), lambda i: (i, 0)) for w in widths],
        scratch_shapes=[pltpu.VMEM((D_MODEL, PROJ_WIDTH), BF16)],
        compiler_params=_params(("arbitrary",)),
        name="in_projection",
    )(xp, xs, mod_l, norm1_l, w_in)


def _dft_tables(seq):
    k = np.arange(seq, dtype=np.int64)
    m = (k[:, None] * k[None, :]) % (2 * seq)
    ang = np.pi * m.astype(np.float64) / seq
    return np.cos(ang), np.sin(ang)


def _hyena_tables(seq):
    cos, sin = _dft_tables(seq)
    sign = np.where(np.arange(seq) % 2 == 0, 1.0, -1.0)
    fwd = np.concatenate([cos, sign[None, :], sin[1:]], axis=0)
    wk = np.full((seq,), 2.0)
    wk[0] = 1.0
    inv_cos = (cos * wk[None, :]) / (2.0 * seq)
    inv_nyq = sign[:, None] / (2.0 * seq)
    inv_sin = 2.0 * sin[:, 1:] / (2.0 * seq)
    inv = np.concatenate([inv_cos, inv_nyq, inv_sin], axis=1)

    return jnp.asarray(fwd, F32).astype(BF16), jnp.asarray(inv, F32).astype(BF16)


def _filter_embedding(seq):
    t = np.arange(seq, dtype=np.float64)
    t_norm = t / max(seq - 1, 1)
    bands = np.linspace(1e-4, HY_BANDS - 1, HY_BANDS)
    ang = (2.0 * np.pi / seq) * t[:, None] * bands[None, :]
    z = np.concatenate([t_norm[:, None], np.cos(ang), np.sin(ang)], axis=-1)
    z = np.pad(z, ((0, 0), (0, LANES - HY_EMB)))
    deltas = np.abs(np.linspace(math.log(HY_DECAY_TARGET) / HY_DECAY_LONG,
                                math.log(HY_DECAY_TARGET) / HY_DECAY_SHORT, HY_WIDTH))
    window = np.exp(-t_norm[:, None] * deltas[None, :])
    return jnp.asarray(z, F32), jnp.asarray(window, F32)


def _filter_kernel(z_ref, win_ref, cos_ref, sin_ref, w1_ref, b1_ref, w2_ref, b2_ref, w3_ref, o_ref):
    seq = z_ref.shape[0]
    h = jnp.sin(_mm3(z_ref[...], w1_ref[0]) + b1_ref[0])
    h = jnp.sin(_mm3(h, w2_ref[0]) + b2_ref[0])
    h = _mm3(h, w3_ref[0])
    win = win_ref[...]
    row = _iota((seq, 1), 0)
    sums, diffs = [], []
    for o in range(HY_ORDER):
        c0 = o * 2 * HY_WIDTH
        hf = h[:, c0:c0 + HY_WIDTH] * win
        hb = h[:, c0 + HY_WIDTH:c0 + 2 * HY_WIDTH] * win
        ssq = jnp.sum(hf * hf + hb * hb, axis=0, keepdims=True)
        inv = lax.rsqrt(ssq + EPS)
        hf = hf * inv
        hb = jnp.where(row == 0, 0.0, hb * inv)
        sums.append(hf + hb)
        diffs.append(hf - hb)
    hsum = jnp.concatenate(sums, axis=1)
    hdiff = jnp.concatenate(diffs, axis=1)
    h_re = _mm(cos_ref[...], hsum.astype(BF16))
    h_im = _mm(sin_ref[...], hdiff.astype(BF16))
    sign = jnp.where(row % 2 == 0, 1.0, -1.0)
    h_nyq = jnp.sum(sign * hsum, axis=0, keepdims=True)
    o_ref[0, 0] = h_re
    o_ref[0, 1] = h_im
    o_ref[0, 2] = jnp.where(row == 0, h_nyq, h_re)


def _hyena_filters(seq, w1, b1, w2, b2, w3):
    z, window = _filter_embedding(seq)
    cos, sin = _dft_tables(seq)
    n_out = HY_ORDER * 2 * HY_WIDTH
    w1p = jnp.pad(w1, ((0, 0), (0, LANES - HY_EMB), (0, LANES - HY_FFN)))
    b1p = jnp.pad(b1, ((0, 0), (0, LANES - HY_FFN))).reshape(DEPTH, 1, LANES)
    w2p = jnp.pad(w2, ((0, 0), (0, LANES - HY_FFN), (0, LANES - HY_FFN)))
    b2p = jnp.pad(b2, ((0, 0), (0, LANES - HY_FFN))).reshape(DEPTH, 1, LANES)
    w3p = jnp.pad(w3, ((0, 0), (0, LANES - HY_FFN), (0, 0)))
    per_layer = lambda shape: pl.BlockSpec((1,) + shape, lambda l: (l,) + (0,) * len(shape))
    return pl.pallas_call(
        _filter_kernel,
        out_shape=jax.ShapeDtypeStruct((DEPTH, 3, seq, HY_ORDER * HY_WIDTH), F32),
        grid=(DEPTH,),
        in_specs=[
            _const_spec((seq, LANES)), _const_spec((seq, HY_WIDTH)),
            _const_spec((seq, seq)), _const_spec((seq, seq)),
            per_layer((LANES, LANES)), per_layer((1, LANES)),
            per_layer((LANES, LANES)), per_layer((1, LANES)),
            per_layer((LANES, n_out)),
        ],
        out_specs=pl.BlockSpec((1, 3, seq, HY_ORDER * HY_WIDTH), lambda l: (l, 0, 0, 0)),
        compiler_params=_params(("parallel",)),
        name=f"hyena_filters_{seq}",
    )(z, window, jnp.asarray(cos, F32).astype(BF16), jnp.asarray(sin, F32).astype(BF16), w1p, b1p, w2p, b2p, w3p)


def _hyena_kernel(group, u_ref, cw_ref, cb_ref, d_ref, filt_ref, fwd_ref, inv_ref, y_ref):
    seq = u_ref.shape[0] // group
    cw = cw_ref[...]
    row = _iota((seq, 1), 0)
    ucs = []
    for s in range(group):
        u = u_ref[s * seq:(s + 1) * seq, :]
        ucs.append(cb_ref[...] + cw[0:1] * _shift_rows(u, -1, row) + cw[1:2] * u + cw[2:3] * _shift_rows(u, 1, row))
    side = lambda c0: jnp.concatenate([uc[:, c0:c0 + HY_WIDTH] for uc in ucs], axis=1)
    tile = lambda a: jnp.concatenate([a] * group, axis=1)
    gates = (side(HY_WIDTH), side(2 * HY_WIDTH))
    d = d_ref[...]
    z = side(0)
    for o in range(HY_ORDER):
        c0 = o * HY_WIDTH
        spec = _mm(fwd_ref[...], z.astype(BF16))
        s_re, s_im = spec[:seq], spec[seq:]
        a = tile(filt_ref[0, :, c0:c0 + HY_WIDTH])
        b = tile(filt_ref[1, :, c0:c0 + HY_WIDTH])
        c = tile(filt_ref[2, :, c0:c0 + HY_WIDTH])
        prod = jnp.concatenate([s_re * a - s_im * b, s_re * b + s_im * c], axis=0)
        conv = _mm(inv_ref[...], prod.astype(BF16))
        z = gates[o] * (conv + tile(d[o:o + 1]) * z)
    for s in range(group):
        y_ref[s * seq:(s + 1) * seq, :] = z[:, s * HY_WIDTH:(s + 1) * HY_WIDTH]


def _mixer_call(body, name, proj_all, tok0, batch, seq, consts, batch_ins, state_shapes, scratch, group=1):
    proj_width = proj_all.shape[1]
    rows = group * seq
    blk0 = tok0 // rows

    def batch_spec(shape):
        nd = len(shape)
        return pl.BlockSpec((group,) + tuple(shape[1:]), lambda b: (b,) + (0,) * (nd - 1))

    def layer_spec(shape, layer):
        nd = len(shape)
        return pl.BlockSpec((group, None) + tuple(shape[2:]), lambda b: (b, layer) + (0,) * (nd - 2))

    in_specs = [pl.BlockSpec((rows, proj_width), lambda b: (blk0 + b, 0))]
    in_specs += [_const_layer_spec(a[0].shape, a[1]) if isinstance(a, tuple) else _const_spec(a.shape)
                 for a in consts]
    consts = [a[0] if isinstance(a, tuple) else a for a in consts]
    in_specs += [layer_spec(a.shape, layer) for a, layer in batch_ins]
    batch_ins = [a for a, _ in batch_ins]
    out_shape = [jax.ShapeDtypeStruct((batch * seq, MIX_WIDTH), F32)]
    out_specs = [pl.BlockSpec((rows, MIX_WIDTH), lambda b: (b, 0))]
    for shape in state_shapes:
        out_shape.append(jax.ShapeDtypeStruct(shape, F32))
        out_specs.append(batch_spec(shape))
    return pl.pallas_call(
        body, out_shape=out_shape, grid=(batch // group,), in_specs=in_specs, out_specs=out_specs,
        scratch_shapes=scratch,
        compiler_params=_params(("parallel",)),
        name=f"{name}_{seq}",
    )(proj_all, *consts, *batch_ins)


def _hyena(u_all, tok0, batch, seq, conv_w, conv_b, d_bias, filt_l, tables):
    consts = [conv_w, conv_b, d_bias, filt_l, *tables]
    group = HY_GROUP_COLS // HY_WIDTH
    return _mixer_call(functools.partial(_hyena_kernel, group), "hyena", u_all, tok0, batch, seq, consts,
                       [], [], [], group=group)[0]


def _rglru_kernel(has_s0, emit_state, rg_ref, cw_ref, cb_ref, wg_ref, bg_ref, sp_ref, *refs):
    h0_ref = refs[0] if has_s0 else None
    y_ref = refs[1 if has_s0 else 0]
    st_ref = refs[-1] if emit_state else None
    seq = rg_ref.shape[0]
    w = RG_WIDTH
    xr = rg_ref[:, 0:w]
    gate = rg_ref[:, w:2 * w]
    row = _iota((seq, 1), 0)
    cw = cw_ref[...]
    xc = (cb_ref[...] + cw[0:1] * _shift_rows(xr, -2, row) + cw[1:2] * _shift_rows(xr, -1, row)
          + cw[2:3] * xr + cw[3:4] * _shift_rows(xr, 1, row))
    g = _sigmoid(_mm(xc.astype(BF16), wg_ref[...].astype(BF16)) + bg_ref[...])
    sp = sp_ref[...]
    c = RG_CHUNK
    n = seq // c
    pos = row % c
    hs = []
    for d in range(2):
        forward = d == 0
        r = g[:, d * w:(d + 1) * w]
        i = g[:, (2 + d) * w:(3 + d) * w]
        log_a = -RG_C * r * sp[d:d + 1]
        a = jnp.exp(log_a)
        b = jnp.sqrt(jnp.tanh(-log_a) * (1.0 + a * a)) * (i * xc)
        step = 1
        while step < c:
            keep = (pos >= step) if forward else (pos < c - step)
            shift = step if forward else seq - step
            a_s = pltpu.roll(a, shift, axis=0)
            b_s = pltpu.roll(b, shift, axis=0)
            b = jnp.where(keep, a * b_s + b, b)
            a = jnp.where(keep, a * a_s, a)
            step *= 2
        a3 = a.reshape(n, c, w)
        b3 = b.reshape(n, c, w)
        edge = c - 1 if forward else 0
        a_end = a3[:, edge:edge + 1, :]
        b_end = b3[:, edge:edge + 1, :]
        h = h0_ref[0, d:d + 1, :] if has_s0 else jnp.zeros((1, w), F32)
        h_in = [None] * n
        for ci in (range(n) if forward else range(n - 1, -1, -1)):
            h_in[ci] = h
            h = b_end[ci] + a_end[ci] * h
        hs.append((b3 + a3 * jnp.stack(h_in, axis=0)).reshape(seq, w))
    y_ref[...] = (hs[0] + hs[1]) * _gelu_tanh(gate)
    if emit_state:
        st_ref[0, 0:1, :] = hs[0][seq - 1:seq]
        st_ref[0, 1:2, :] = hs[1][0:1]


def _rglru(rg_all, tok0, batch, seq, conv_w, conv_b, w_gates, b_gates, softplus_neg_lam, h0, emit_state):
    has_s0 = h0 is not None
    consts = [conv_w, conv_b, w_gates, b_gates, softplus_neg_lam]
    return _mixer_call(functools.partial(_rglru_kernel, has_s0, emit_state), "rglru", rg_all, tok0, batch, seq,
                       consts, [h0] if has_s0 else [],
                       [(batch, 2, RG_WIDTH)] if emit_state else [], [])


def _hgrn_kernel(has_s0, emit_state, hg_ref, lb_ref, gain_ref, *refs):
    s0_ref = refs[0] if has_s0 else None
    y_ref = refs[1 if has_s0 else 0]
    st_ref = refs[-1] if emit_state else None
    seq = hg_ref.shape[0]
    w = HG_WIDTH
    c = HG_CHUNK
    n = seq // c
    mid = c // 2
    chunks = lambda a: a.reshape(n, c, w)
    pos = _iota((seq, 1), 0) % c
    stack_mask = _head_mask(c)[None]
    bd_mask = _block_diag_mask()
    pair_shape = (c, N_HEADS * c)
    t_idx = _iota(pair_shape, 0)
    s_idx = _iota(pair_shape, 1) % c

    def stack_heads(a3):
        return jnp.where(stack_mask, jnp.concatenate([a3] * N_HEADS, axis=1), 0.0).astype(BF16)

    q3 = chunks(_silu(hg_ref[:, 0:w]))
    v3 = chunks(hg_ref[:, 3 * w:4 * w])
    v_stack = stack_heads(v3)
    v_t = jnp.swapaxes(v3, 1, 2).astype(BF16)
    lbv = lb_ref[...]
    o_sum = None
    finals = []
    for d in range(2):
        forward = d == 0
        f_pre = hg_ref[:, (1 + d) * w:(2 + d) * w]
        lo = lbv[d:d + 1]
        a1 = jnp.log(lo)
        a2 = jnp.log(1.0 - lo) + _log_sigmoid(f_pre)
        g = jnp.maximum(a1, a2) + jnp.log(1.0 + jnp.exp(-jnp.abs(a1 - a2)))
        k3 = chunks((1.0 - lo) * _sigmoid(-f_pre))
        step = 1
        while step < c:
            if forward:
                g = jnp.where(pos >= step, g + pltpu.roll(g, step, axis=0), g)
            else:
                g = jnp.where(pos < c - step, g + pltpu.roll(g, seq - step, axis=0), g)
            step *= 2
        g3 = chunks(g)
        g_ref = g3[:, mid:mid + 1, :]
        g_tot = g3[:, c - 1:c, :] if forward else g3[:, 0:1, :]
        q_in = (q3 * jnp.exp(g3)).astype(BF16)
        q_sc = (q3 * jnp.exp(g3 - g_ref)).astype(BF16)
        k_stack = stack_heads(k3 * jnp.exp(g_ref - g3))
        k_out = (k3 * jnp.exp(g_tot - g3)).astype(BF16)
        pair = jnp.einsum("ntl,nrl->ntr", q_sc, k_stack, preferred_element_type=F32)
        causal = (t_idx >= s_idx) if forward else (t_idx <= s_idx)
        pair = jnp.where(causal[None], pair, 0.0).astype(BF16)
        o_intra = jnp.einsum("ntr,nrv->ntv", pair, v_stack, preferred_element_type=F32)
        upd = jnp.einsum("nvs,nsk->nvk", v_t, k_out, preferred_element_type=F32)
        decay = jnp.exp(g_tot)
        st = _load_state(s0_ref, d) if has_s0 else jnp.zeros((w, w), F32)
        o_inter = [None] * n
        for ci in (range(n) if forward else range(n - 1, -1, -1)):
            o_inter[ci] = _mm(q_in[ci], st.astype(BF16), _NT)
            st = st * decay[ci] + jnp.where(bd_mask, upd[ci], 0.0)
        o_dir = o_intra + jnp.stack(o_inter, axis=0)
        o_sum = o_dir if o_sum is None else o_sum + o_dir
        finals.append(st)

    ones_bd = jnp.where(bd_mask, 1.0 / HEAD_DIM, 0.0).astype(BF16)
    o = _head_rmsnorm(o_sum.reshape(seq, w), ones_bd) * gain_ref[...]
    y_ref[...] = o * _silu(hg_ref[:, 4 * w:5 * w])
    if emit_state:
        _store_state(st_ref, 0, finals[0])
        _store_state(st_ref, 1, finals[1])


def _state_shape(batch):
    return (batch, 2, N_HEADS, HEAD_DIM, HEAD_DIM)


def _hgrn(hg_all, tok0, batch, seq, lb_l, gain_l, s0, emit_state):
    has_s0 = s0 is not None
    return _mixer_call(functools.partial(_hgrn_kernel, has_s0, emit_state), "hgrn2", hg_all, tok0, batch, seq,
                       [lb_l, gain_l], [s0] if has_s0 else [],
                       [_state_shape(batch)] if emit_state else [], [])


def _ret_kernel(use_rope, has_s0, emit_state, ret_ref, cos_ref, sin_ref, dl_ref, dlp_ref, *refs):
    s0_ref = refs[0] if has_s0 else None
    y_ref = refs[1 if has_s0 else 0]
    st_ref = refs[-1] if emit_state else None
    seq = ret_ref.shape[0]
    w = RET_WIDTH
    c = min(RET_CHUNK, seq)
    n = seq // c
    chunks = lambda a: a.reshape(n, c, w)
    q = ret_ref[:, 0:w]
    k = ret_ref[:, w:2 * w] * (HEAD_DIM ** -0.5)
    if use_rope:
        lane = _iota((seq, w), 1)
        even = (lane % 2) == 0
        cos = cos_ref[...]
        sin = sin_ref[...]

        def rope(x):
            nxt = pltpu.roll(x, w - 1, axis=1)
            prv = pltpu.roll(x, 1, axis=1)
            return x * cos + jnp.where(even, nxt, prv) * sin

        q = rope(q)
        k = rope(k)
    q3, k3, v3 = chunks(q), chunks(k), chunks(ret_ref[:, 2 * w:3 * w])

    lg = _log_sigmoid(dl_ref[...])
    lgp = _log_sigmoid(dlp_ref[...])
    pair_shape = (c, N_HEADS * c)
    t_idx = _iota(pair_shape, 0)
    s_idx = _iota(pair_shape, 1) % c
    dist = (t_idx - s_idx).astype(F32)
    decay = jnp.exp(jnp.where(dist >= 0, dist * lgp[0:1], -dist * lgp[1:2]))
    decay = decay + jnp.where(dist == 0, 1.0, 0.0)
    stack_mask = _head_mask(c)[None]
    bd_mask = _block_diag_mask()
    pos = _iota((c, 1), 0).astype(F32)
    fc = float(c)

    def stack_heads(a3):
        return jnp.where(stack_mask, jnp.concatenate([a3] * N_HEADS, axis=1), 0.0).astype(BF16)

    pair = jnp.einsum("ntl,nrl->ntr", q3.astype(BF16), stack_heads(k3), preferred_element_type=F32)
    pair = (pair * decay[None]).astype(BF16)
    o_sum = jnp.einsum("ntr,nrv->ntv", pair, stack_heads(v3), preferred_element_type=F32)
    v_t = jnp.swapaxes(v3, 1, 2).astype(BF16)
    finals = []
    for d in range(2):
        forward = d == 0
        lgd = lg[d:d + 1]
        q_in = (q3 * jnp.exp(((pos + 1.0) if forward else (fc - pos)) * lgd)).astype(BF16)
        k_out = (k3 * jnp.exp(((fc - 1.0 - pos) if forward else pos) * lgd)).astype(BF16)
        upd = jnp.einsum("nvs,nsk->nvk", v_t, k_out, preferred_element_type=F32)
        chunk_decay = jnp.exp(fc * lgd)
        st = _load_state(s0_ref, d) if has_s0 else jnp.zeros((w, w), F32)
        o_inter = [None] * n
        for ci in (range(n) if forward else range(n - 1, -1, -1)):
            o_inter[ci] = _mm(q_in[ci], st.astype(BF16), _NT)
            st = st * chunk_decay + jnp.where(bd_mask, upd[ci], 0.0)
        o_sum = o_sum + jnp.stack(o_inter, axis=0)
        finals.append(st)

    ones_bd = jnp.where(bd_mask, 1.0 / HEAD_DIM, 0.0).astype(BF16)
    o = _head_rmsnorm(o_sum.reshape(seq, w), ones_bd)
    y_ref[...] = _silu(ret_ref[:, 3 * w:4 * w]) * o
    if emit_state:
        _store_state(st_ref, 0, finals[0])
        _store_state(st_ref, 1, finals[1])


def _rope_tables(seq):
    rows = seq // GRID_W
    row = np.repeat(np.arange(rows), GRID_W).astype(np.float64)
    col = (np.arange(seq) % GRID_W).astype(np.float64)
    n_freq = HEAD_DIM // 4
    inv_freq = ROPE_BASE ** (-np.arange(n_freq, dtype=np.float64) / n_freq)
    ang = np.concatenate([row[:, None] * inv_freq, col[:, None] * inv_freq], axis=-1)
    ang = np.repeat(ang, 2, axis=1)
    cos = np.tile(np.cos(ang), (1, N_HEADS))
    sin = np.tile(np.sin(ang) * np.where(np.arange(HEAD_DIM) % 2 == 0, -1.0, 1.0)[None, :], (1, N_HEADS))
    return jnp.asarray(cos, F32), jnp.asarray(sin, F32)


def _retention(ret_all, tok0, batch, seq, use_rope, decay_l, s0, emit_state):
    has_s0 = s0 is not None
    w = RET_WIDTH
    c = min(RET_CHUNK, seq)
    cos, sin = _rope_tables(seq)
    dl = jnp.repeat(decay_l, HEAD_DIM, axis=-1)
    dlp = jnp.repeat(decay_l, c, axis=-1)
    return _mixer_call(functools.partial(_ret_kernel, use_rope, has_s0, emit_state), "retention", ret_all, tok0,
                       batch, seq, [cos, sin, dl, dlp], [s0] if has_s0 else [],
                       [_state_shape(batch)] if emit_state else [], [])


def _route(h2s, wrt_ref, rb_ref):
    tm = h2s[0].shape[0]
    lane = _iota((N_EXPERTS, tm), 0).astype(F32)
    group = (_iota((N_EXPERTS, tm), 0) // E_PER_GROUP).astype(F32)
    neg = -jnp.inf
    each = lambda f, *cols: [f(*args) for args in zip(*cols)]

    def first_argmax(vals):
        mx = jnp.max(vals, axis=0, keepdims=True)
        idx = jnp.min(jnp.where(vals == mx, lane, float(N_EXPERTS)), axis=0, keepdims=True)
        return mx, idx

    def softmax(logits):
        e = jnp.exp(logits - jnp.max(logits, axis=0, keepdims=True))
        return e / jnp.sum(e, axis=0, keepdims=True)

    logits = each(lambda h2: _mm3(wrt_ref[...], h2, _NT), h2s)
    probs = each(softmax, logits)
    sel = each(lambda p: p + rb_ref[...], probs)

    def group_score(s, g):
        vals = jnp.where(group == float(g), s, neg)
        m1, i1 = first_argmax(vals)
        m2, _ = first_argmax(jnp.where(lane == i1, neg, vals))
        return m1 + m2

    best_score = each(lambda s: group_score(s, 0), sel)
    best = each(jnp.zeros_like, best_score)
    for g in range(1, N_GROUPS):
        score = each(lambda s: group_score(s, g), sel)
        take = each(lambda sc, bs: sc > bs, score, best_score)
        best = each(lambda t, b: jnp.where(t, float(g), b), take, best)
        best_score = each(lambda t, sc, bs: jnp.where(t, sc, bs), take, score, best_score)

    def gates_of(s, p, b):
        vals = jnp.where(group == b, s, neg)
        _, i1 = first_argmax(vals)
        _, i2 = first_argmax(jnp.where(lane == i1, neg, vals))
        picked = jnp.where((lane == i1) | (lane == i2), p, 0.0)
        return picked / jnp.sum(picked, axis=0, keepdims=True)

    return each(gates_of, sel, probs, best), best


INFO_ROWS = 8
INFO_GROUP = 4
INFO_RANK = 5
H2_WIDTH = D_MODEL + LANES


def _route_kernel(n0_tiles, xp_ref, xs_ref, *refs):
    y_refs = refs[:8]
    mod_ref, wo_ref, n2_ref, wrt_ref, rb_ref, x1_ref, h2_ref, info_ref, cend_ref, carry_s, wob_s = refs[8:]
    i = pl.program_id(0)
    first = i < n0_tiles
    d = D_MODEL
    tb = TB_MOE
    mod = mod_ref[0, 0]

    @pl.when(i == 0)
    def _():
        carry_s[...] = jnp.zeros_like(carry_s)
        wob_s[...] = wo_ref[...].astype(BF16)

    n_sub = TM_ROUTE // tb
    each = lambda f, *cols: [f(*args) for args in zip(*cols)]
    rows = [slice(h * tb, (h + 1) * tb) for h in range(n_sub)]
    expert = _iota((N_EXPERTS, tb), 0).astype(F32)
    upper = (_iota((tb, tb), 0) <= _iota((tb, tb), 1)).astype(BF16)
    eye = (_iota((tb, tb), 0) == _iota((tb, tb), 1)).astype(BF16)

    def mixed_of(r):
        ys = [jnp.where(first, y_refs[2 * j][r, :], y_refs[2 * j + 1][r, :]).astype(BF16) for j in range(4)]
        return _mm(jnp.concatenate(ys, axis=1), wob_s[...])

    mixed = each(mixed_of, rows)
    x1 = each(lambda r, m: jnp.where(first, xp_ref[r, :], xs_ref[r, :]) + mod[:, 2 * d:3 * d] * m, rows, mixed)

    def h2_of(v):
        h2 = v * lax.rsqrt(jnp.mean(v * v, axis=-1, keepdims=True) + EPS) * n2_ref[...]
        return h2 * (1.0 + mod[:, 4 * d:5 * d]) + mod[:, 3 * d:4 * d]

    h2 = each(h2_of, x1)
    gates, best = _route(h2, wrt_ref, rb_ref)
    g4 = each(lambda g, b: [jnp.sum(jnp.where(expert == E_PER_GROUP * b + j, g, 0.0), axis=0, keepdims=True)
                            for j in range(E_PER_GROUP)], gates, best)
    onehot = each(lambda b: jnp.where(expert == b, 1.0, 0.0), best)
    counts = each(lambda o: _mm(o.astype(BF16), upper), onehot)

    def pieces_of(g):
        p = _split3(jnp.concatenate(g + [jnp.zeros((N_EXPERTS - E_PER_GROUP, tb), F32)], axis=0))
        return jnp.concatenate(list(p) + [jnp.zeros((LANES - 3 * N_EXPERTS, tb), BF16)], axis=0)

    ext = each(lambda g: _mm(eye, pieces_of(g), _NT).astype(BF16), g4)

    carry = carry_s[...]
    for h in range(n_sub):
        incl = counts[h] + carry[:, 0:1]
        rank = jnp.sum(onehot[h] * incl, axis=0, keepdims=True) - 1.0
        carry = jnp.broadcast_to(incl[:, tb - 1:tb], (N_EXPERTS, LANES))
        x1_ref[rows[h], :] = x1[h]
        h2_ref[rows[h], 0:d] = h2[h].astype(BF16)
        h2_ref[rows[h], d:d + LANES] = ext[h]
        info_ref[:, rows[h]] = jnp.concatenate(g4[h] + [best[h], rank, jnp.zeros((INFO_ROWS - 6, tb), F32)], axis=0)
        cend_ref[h] = carry
    carry_s[...] = carry


def _route_call(layer, xp, xs, ys, mod_l, w_out, norm2_l, w_router, router_bias, dec_seq):
    n_prompt_tok = xp.shape[0]
    t_all = n_prompt_tok + xs.shape[0]
    tm = TM_ROUTE
    sub = tm // TB_MOE
    n0_tiles = n_prompt_tok // tm
    row = _mod_row_map(tm, n_prompt_tok, dec_seq)
    tok = lambda width: pl.BlockSpec((tm, width), lambda i: (i, 0))
    y_specs, y_args = [], []
    for yp, ysm in ys:
        y_specs += _pair_specs(tm, MIX_WIDTH, n0_tiles)
        y_args += [yp, ysm]
    return pl.pallas_call(
        functools.partial(_route_kernel, n0_tiles),
        out_shape=[jax.ShapeDtypeStruct((t_all, D_MODEL), F32), jax.ShapeDtypeStruct((t_all, H2_WIDTH), BF16),
                   jax.ShapeDtypeStruct((INFO_ROWS, t_all), F32),
                   jax.ShapeDtypeStruct((t_all // TB_MOE, N_EXPERTS, LANES), F32)],
        grid=(t_all // tm,),
        in_specs=_pair_specs(tm, D_MODEL, n0_tiles) + y_specs + [
            pl.BlockSpec((1, 1, 1, N_MOD * D_MODEL), lambda i: (0, row(i), 0, 0)),
            _const_layer_spec(w_out.shape, layer), _const_spec((1, D_MODEL)),
            _const_spec((N_EXPERTS, D_MODEL)), _const_spec((N_EXPERTS, 1)),
        ],
        out_specs=[tok(D_MODEL), tok(H2_WIDTH), pl.BlockSpec((INFO_ROWS, tm), lambda i: (0, i)),
                   pl.BlockSpec((sub, N_EXPERTS, LANES), lambda i: (i, 0, 0))],
        scratch_shapes=[pltpu.VMEM((N_EXPERTS, LANES), F32), pltpu.VMEM((D_MODEL, D_MODEL), BF16)],
        compiler_params=_params(("arbitrary",)),
        name="route",
    )(xp, xs, *y_args, mod_l, w_out, norm2_l, w_router.T, router_bias.reshape(N_EXPERTS, 1))


def _moe_plan(cend, info):
    ts = TS_MOE
    tb = TD_MOE
    n_blocks = cend.shape[0]
    t_all = n_blocks * TB_MOE
    n_tiles = t_all // ts + N_GROUPS
    sub_per_tile = ts // tb
    n_sub = n_tiles * sub_per_tile
    cend = cend[:, :N_GROUPS, 0].astype(jnp.int32)
    cum = jnp.concatenate([jnp.zeros((1, N_GROUPS), jnp.int32), cend[:-1]], axis=0)
    cnt = cend - cum
    grp = info[INFO_GROUP].astype(jnp.int32)
    rank = info[INFO_RANK].astype(jnp.int32)
    tot = cend[-1]
    padded = ((tot + ts - 1) // ts) * ts
    off = jnp.cumsum(padded) - padded
    pos = rank + sum(jnp.where(grp == g, off[g], 0) for g in range(N_GROUPS))

    sub_start = jnp.arange(n_sub, dtype=jnp.int32) * tb
    in_g = (sub_start[:, None] >= off[None, :]) & (sub_start[:, None] < (off + padded)[None, :])
    g_of = jnp.argmax(in_g, axis=1).astype(jnp.int32)
    r0 = sub_start - off[g_of]
    r1 = jnp.minimum(r0 + tb, tot[g_of])
    live = jnp.any(in_g, axis=1) & (r1 > r0)
    cend_g = cend[:, g_of]
    blo = jnp.where(live, jnp.sum(cend_g <= r0[None, :], axis=0), 0).astype(jnp.int32)
    bhi = jnp.where(live, jnp.sum(cend_g <= (r1 - 1)[None, :], axis=0), -1).astype(jnp.int32)

    used = jnp.any(in_g, axis=1)[::sub_per_tile]
    tile_group = g_of[::sub_per_tile]
    eidx = E_PER_GROUP * tile_group[:, None] + jnp.arange(E_PER_GROUP, dtype=jnp.int32)[None, :]
    n_used = jnp.sum(used.astype(jnp.int32))
    last = eidx[jnp.maximum(n_used - 1, 0), E_PER_GROUP - 1]
    eidx = jnp.where(used[:, None], eidx, last).reshape(-1).astype(jnp.int32)

    start = off[None, :] + cum
    end = off[None, :] + cend
    has = cnt > 0
    slo = jnp.where(has, start // TR_MOE, 0).reshape(-1).astype(jnp.int32)
    shi = jnp.where(has, (end - 1) // TR_MOE, -1).reshape(-1).astype(jnp.int32)
    tile_start = sub_start[::sub_per_tile]
    rows = jnp.where(used, jnp.clip(tot[tile_group] - (tile_start - off[tile_group]), 0, ts), 0)
    return dict(pos=pos, eidx=eidx, rows=rows.astype(jnp.int32), blo=blo, bhi=bhi, slo=slo, shi=shi,
                n_tiles=n_tiles)


def _expert_kernel(eidx_ref, rows_ref, blo_ref, bhi_ref, h2_ref, pos_ref, wg_ref, wu_ref, wd_ref,
                   ys_ref, xs_s, acc_s, ax_s):
    i = pl.program_id(0)
    j = pl.program_id(1)
    n_rows = rows_ref[i]
    used = n_rows > 0
    tb, tr = TB_MOE, TD_MOE
    sub_per_tile = TS_MOE // tr

    @pl.when(used & (j == 0))
    def _():
        r_idx = _iota((tr, tb), 0)
        for part in range(sub_per_tile):
            s = i * sub_per_tile + part
            base = s * tr
            ax_s[...] = jnp.zeros_like(ax_s)

            def gather(b, carry):
                sel = jnp.where(pos_ref[b] - base == r_idx, 1.0, 0.0).astype(BF16)
                tok = pl.ds(pl.multiple_of(b * tb, tb), tb)
                ax_s[...] += _mm(sel, h2_ref[tok, :])
                return carry

            lax.fori_loop(blo_ref[s], bhi_ref[s] + 1, gather, 0)
            xs_s[part * tr:(part + 1) * tr, :] = ax_s[...].astype(BF16)
        acc_s[...] = jnp.zeros_like(acc_s)

    half_rows = TS_MOE // 2
    for part in range(2):
        rows = slice(part * half_rows, (part + 1) * half_rows)

        @pl.when(n_rows > part * half_rows)
        def _():
            x = xs_s[rows, 0:D_MODEL]
            pieces = xs_s[rows, D_MODEL:H2_WIDTH].astype(F32)
            ge = jnp.sum(jnp.where(_iota(pieces.shape, 1) % N_EXPERTS == j, pieces, 0.0), axis=-1, keepdims=True)
            hh = _silu(_mm(x, wg_ref[0, 0].astype(BF16))) * _mm(x, wu_ref[0, 0].astype(BF16)) * ge
            acc_s[rows, :] += _mm(hh.astype(BF16), wd_ref[0, 0].astype(BF16))

    @pl.when(j == E_PER_GROUP - 1)
    def _():
        ys_ref[...] = jnp.where(used, acc_s[...], 0.0).astype(BF16)


def _expert_call(layer, plan, h2, w_gate, w_up, w_down):
    t_all = h2.shape[0]
    tb, ts = TB_MOE, TS_MOE
    n_tiles = plan["n_tiles"]
    pos_rows = plan["pos"].reshape(t_all // tb, 1, tb)
    weight = lambda shape: pl.BlockSpec((1, 1) + shape, lambda i, j, eidx, *_: (layer, eidx[i * E_PER_GROUP + j], 0, 0))
    grid_spec = pltpu.PrefetchScalarGridSpec(
        num_scalar_prefetch=4,
        grid=(n_tiles, E_PER_GROUP),
        in_specs=[
            _const_spec((t_all, H2_WIDTH)), _const_spec(pos_rows.shape),
            weight((D_MODEL, D_EXPERT)), weight((D_MODEL, D_EXPERT)), weight((D_EXPERT, D_MODEL)),
        ],
        out_specs=pl.BlockSpec((ts, D_MODEL), lambda i, j, *_: (i, 0)),
        scratch_shapes=[pltpu.VMEM((ts, H2_WIDTH), BF16), pltpu.VMEM((ts, D_MODEL), F32),
                        pltpu.VMEM((TD_MOE, H2_WIDTH), F32)],
    )
    return pl.pallas_call(
        _expert_kernel,
        out_shape=jax.ShapeDtypeStruct((n_tiles * ts, D_MODEL), BF16),
        grid_spec=grid_spec,
        compiler_params=_params(("arbitrary", "arbitrary")),
        name=f"experts_{layer}",
    )(plan["eidx"], plan["rows"], plan["blo"], plan["bhi"], h2, pos_rows, w_gate, w_up, w_down)


def _combine_kernel(final, n0_tiles, slo_ref, shi_ref, ys_ref, pos_ref, x1_ref, mod_ref, nf_ref,
                    op_ref, os_ref, acc_s):
    b = pl.program_id(0)
    d = D_MODEL
    tb, tr = TB_MOE, TR_MOE
    acc_s[...] = jnp.zeros_like(acc_s)
    c_idx = _iota((tb, tr), 1)
    pos = pos_ref[...]
    for g in range(N_GROUPS):
        def scatter(s, carry):
            sel = jnp.where(pos - s * tr == c_idx, 1.0, 0.0).astype(BF16)
            acc_s[...] += _mm(sel, ys_ref[pl.ds(pl.multiple_of(s * tr, tr), tr), :])
            return carry

        lax.fori_loop(slo_ref[b * N_GROUPS + g], shi_ref[b * N_GROUPS + g] + 1, scatter, 0)

    def result():
        x2 = x1_ref[...] + mod_ref[0, 0][:, 5 * d:6 * d] * acc_s[...]
        if final:
            x2 = x2 * lax.rsqrt(jnp.mean(x2 * x2, axis=-1, keepdims=True) + EPS) * nf_ref[...]
        return x2

    @pl.when(b < n0_tiles)
    def _():
        op_ref[...] = result()

    @pl.when(b >= n0_tiles)
    def _():
        os_ref[...] = result()


def _combine_call(final, plan, ys_sorted, x1, mod_l, norm_final, n_prompt_tok, dec_seq):
    t_all = x1.shape[0]
    tb = TB_MOE
    n0_tiles = n_prompt_tok // tb
    row = _mod_row_map(tb, n_prompt_tok, dec_seq)
    grid_spec = pltpu.PrefetchScalarGridSpec(
        num_scalar_prefetch=2,
        grid=(t_all // tb,),
        in_specs=[
            _const_spec(ys_sorted.shape),
            pl.BlockSpec((tb, 1), lambda b, *_: (b, 0)),
            pl.BlockSpec((tb, D_MODEL), lambda b, *_: (b, 0)),
            pl.BlockSpec((1, 1, 1, N_MOD * D_MODEL), lambda b, *_: (0, row(b), 0, 0)),
            _const_spec((1, D_MODEL)),
        ],
        out_specs=_pair_specs(tb, D_MODEL, n0_tiles),
        scratch_shapes=[pltpu.VMEM((tb, D_MODEL), F32)],
    )
    return pl.pallas_call(
        functools.partial(_combine_kernel, final, n0_tiles),
        out_shape=[jax.ShapeDtypeStruct((n_prompt_tok, D_MODEL), F32),
                   jax.ShapeDtypeStruct((t_all - n_prompt_tok, D_MODEL), F32)],
        grid_spec=grid_spec,
        compiler_params=_params(("arbitrary",)),
        name="combine",
    )(plan["slo"], plan["shi"], ys_sorted, plan["pos"].reshape(t_all, 1), x1, mod_l, norm_final)


def _rg_gate_weights(wa, ba, wx, bx):
    eye = jnp.eye(RG_HEADS, dtype=wa.dtype)

    def dense(wd):
        return jnp.einsum("hij,hg->higj", wd, eye).reshape(RG_WIDTH, RG_WIDTH)

    w = jnp.concatenate([dense(wa[0]), dense(wa[1]), dense(wx[0]), dense(wx[1])], axis=1)
    b = jnp.concatenate([ba[0], ba[1], bx[0], bx[1]], axis=0).reshape(1, 4 * RG_WIDTH)
    return w, b


def kernel(x_prompt, x_sample, state_rglru, state_hgrn, state_ret, c, c_ctx, norm1, norm2, norm_final, w_ada, b_ada, w_in, w_out, hy_conv_w, hy_conv_b, hy_w1, hy_b1, hy_w2, hy_b2, hy_w3, hy_d, rg_conv_w, rg_conv_b, rg_wa, rg_ba, rg_wx, rg_bx, rg_lambda, hg_lb, hg_norm, ret_decay, w_router, router_bias, w_gate, w_up, w_down):
    batch, seq, d = x_prompt.shape
    dec_batch, dec_seq, _ = x_sample.shape
    assert d == D_MODEL and dec_batch + 1 <= COND_ROWS
    n_prompt_tok = batch * seq

    lb_cum = jnp.cumsum(jax.nn.softmax(hg_lb.astype(F32), axis=0), axis=0)
    lb_all = lb_cum - lb_cum[0:1]

    cond = jnp.zeros((COND_ROWS, d), F32).at[0].set(c_ctx).at[1:1 + dec_batch].set(c)
    mod = _modulation(cond, w_ada, b_ada).reshape(DEPTH, COND_ROWS, 1, N_MOD * d)


    passes = (
        dict(tok0=0, batch=batch, seq=seq, rope=False),
        dict(tok0=n_prompt_tok, batch=dec_batch, seq=dec_seq, rope=True),
    )
    filters = {p["seq"]: _hyena_filters(p["seq"], hy_w1, hy_b1, hy_w2, hy_b2, hy_w3) for p in passes}
    tables = {p["seq"]: _hyena_tables(p["seq"]) for p in passes}

    xp = x_prompt.reshape(-1, d)
    xs = x_sample.reshape(-1, d)
    new_rg, new_hg, new_ret = [], [], []
    for l in range(DEPTH):
        hy_all, rg_all, hg_all, ret_all = _in_projection(
            l, xp, xs, mod[l:l + 1], norm1[l].reshape(1, d), w_in, dec_seq)
        wg, bg = _rg_gate_weights(rg_wa[l], rg_ba[l], rg_wx[l], rg_bx[l])
        sp = jax.nn.softplus(-rg_lambda[l])
        ys = [[], [], [], []]
        for pi, p in enumerate(passes):
            first = pi == 0
            geom = (p["tok0"], p["batch"], p["seq"])
            y_hy = _hyena(hy_all, *geom, hy_conv_w[l], hy_conv_b[l].reshape(1, -1), hy_d[l],
                          (filters[p["seq"]], l), tables[p["seq"]])
            y_rg, *st_rg = _rglru(rg_all, *geom, rg_conv_w[l], rg_conv_b[l].reshape(1, -1), wg, bg, sp,
                                  None if first else (state_rglru, l), first)
            y_hg, *st_hg = _hgrn(hg_all, *geom, lb_all[l], hg_norm[l].reshape(1, -1),
                                 None if first else (state_hgrn, l), first)
            y_ret, *st_ret = _retention(ret_all, *geom, p["rope"], ret_decay[l],
                                        None if first else (state_ret, l), first)
            for lst, y in zip(ys, (y_hy, y_rg, y_hg, y_ret)):
                lst.append(y)
            if first:
                new_rg.append(st_rg[0])
                new_hg.append(st_hg[0])
                new_ret.append(st_ret[0])
        x1, h2, info, cend = _route_call(l, xp, xs, ys, mod[l:l + 1], w_out, norm2[l].reshape(1, d),
                                         w_router, router_bias, dec_seq)
        plan = _moe_plan(cend, info)
        ys_sorted = _expert_call(l, plan, h2, w_gate, w_up, w_down)
        xp, xs = _combine_call(l == DEPTH - 1, plan, ys_sorted, x1, mod[l:l + 1], norm_final.reshape(1, d),
                               n_prompt_tok, dec_seq)

    return (xp.reshape(batch, seq, d), xs.reshape(dec_batch, dec_seq, d), jnp.stack(new_rg, axis=1),
            jnp.stack(new_hg, axis=1), jnp.stack(new_ret, axis=1))
```
